```python
import math
import jax
import jax.numpy as jnp
from jax import lax
import numpy as np

D_MODEL = 1024
BATCH = 1
SEQ = 16384
DEPTH = 4

N_EVEN = (DEPTH + 1) // 2
N_ODD = DEPTH // 2
EPS = 1e-6
ROPE_THETA = 10000.0
Q_BLOCK = 128

GDN_HEADS = 4
GDN_DK = 128
GDN_DV = 128
GDN_CONV = 4
GDN_CHUNK = 64
GDN_QK = GDN_HEADS * GDN_DK
GDN_V = GDN_HEADS * GDN_DV
GDN_CONV_CH = 2 * GDN_QK + GDN_V
MOBA_HEADS = 4
MOBA_DH = 128
MOBA_BLOCK = 256
MOBA_TOPK = 3
MOBA_W = MOBA_HEADS * MOBA_DH
HY_SIZES = (GDN_CONV_CH, GDN_V, GDN_HEADS, GDN_HEADS, MOBA_W, MOBA_W, MOBA_W)
HY_IN = sum(HY_SIZES)
HY_SPLITS = tuple(int(i) for i in np.cumsum(HY_SIZES)[:-1])
HY_OUT = GDN_V + MOBA_W
MLA_HEADS = 8
MLA_Q_RANK = 512
MLA_KV_RANK = 256
MLA_NOPE = 128
MLA_ROPE = 64
MLA_V = 128
MLA_IN = MLA_Q_RANK + MLA_KV_RANK + MLA_ROPE
PEER_HEADS = 8
PEER_NKEYS = 128
PEER_EXPERTS = PEER_NKEYS * PEER_NKEYS
PEER_DKEY = 256
PEER_TOPK = 16
PEER_TOK_CHUNK = 128

kernel_name = 'hybrid_gdn_moba_mla_peer'


def rmsnorm(x, w):
    xf = x.astype(jnp.float32)
    y = xf * lax.rsqrt(jnp.mean(xf * xf, axis=-1, keepdims=True) + EPS)
    return (y * w.astype(jnp.float32)).astype(x.dtype)


def l2norm(x):
    xf = x.astype(jnp.float32)
    return xf * lax.rsqrt(jnp.sum(xf * xf, axis=-1, keepdims=True) + EPS)


def rope_tables(positions, dim):
    inv_freq = ROPE_THETA ** (-jnp.arange(0, dim, 2, dtype=jnp.float32) / dim)
    ang = positions.astype(jnp.float32)[..., None] * inv_freq
    return jnp.cos(ang), jnp.sin(ang)


def apply_rope(x, cos, sin):
    x1, x2 = jnp.split(x.astype(jnp.float32), 2, axis=-1)
    c = cos[:, :, None, :]
    s = sin[:, :, None, :]
    return jnp.concatenate([x1 * c - x2 * s, x2 * c + x1 * s], axis=-1)


def causal_conv_silu(x, w):
    k_w = w.shape[0]
    s = x.shape[1]
    xp = jnp.pad(x, ((0, 0), (k_w - 1, 0), (0, 0)))
    y = xp[:, 0:s] * w[0]
    for j in range(1, k_w):
        y = y + xp[:, j:j + s] * w[j]
    return jax.nn.silu(y)


def unit_lower_inverse(l_strict):
    n = l_strict.shape[-1]
    eye = jnp.eye(n, dtype=l_strict.dtype)
    neg = -l_strict
    inv = eye + neg
    power = neg
    p = 2
    while p < n:
        power = power @ power
        inv = inv @ (eye + power)
        p *= 2
    return inv


def gated_delta_rule(q, k, v, g, beta):
    bsz, s, nh, dk = q.shape
    dv = v.shape[-1]
    cs = GDN_CHUNK
    nc = s // cs

    def to_chunks(t):
        t = t.reshape((bsz, nc, cs, nh) + t.shape[3:])
        return jnp.moveaxis(t, 3, 1)

    q = to_chunks(q * dk ** -0.5)
    k = to_chunks(k)
    v = to_chunks(v)
    beta = to_chunks(beta)
    g = jnp.cumsum(to_chunks(g), axis=-1)
    causal = jnp.tril(jnp.ones((cs, cs), dtype=bool))
    strict = jnp.tril(jnp.ones((cs, cs), dtype=bool), -1)
    decay = jnp.exp(jnp.where(causal, g[..., :, None] - g[..., None, :], -jnp.inf))
    k_beta = k * beta[..., None]
    l_strict = jnp.where(strict, jnp.einsum('bhncd,bhnmd->bhncm', k_beta, k) * decay, 0.0)
    t_inv = unit_lower_inverse(l_strict)
    u = t_inv @ (v * beta[..., None])
    w = t_inv @ (k_beta * jnp.exp(g)[..., None])
    qk = jnp.einsum('bhncd,bhnmd->bhncm', q, k) * decay
    q_dec = q * jnp.exp(g)[..., None]
    g_last = g[..., -1]
    k_dec = k * jnp.exp(g_last[..., None] - g)[..., None]

    def step(state, inp):
        qd, kd, uc, wc, qkc, gl = inp
        v_new = uc - jnp.einsum('bhck,bhkv->bhcv', wc, state)
        out = jnp.einsum('bhck,bhkv->bhcv', qd, state) + jnp.einsum('bhcm,bhmv->bhcv', qkc, v_new)
        state = state * jnp.exp(gl)[..., None, None] + jnp.einsum('bhck,bhcv->bhkv', kd, v_new)
        return state, out

    xs = tuple(jnp.moveaxis(t, 2, 0) for t in (q_dec, k_dec, u, w, qk, g_last))
    state0 = jnp.zeros((bsz, nh, dk, dv), jnp.float32)
    _, o = lax.scan(step, state0, xs)
    return jnp.transpose(o, (1, 0, 3, 2, 4)).reshape(bsz, s, nh, dv)


def moba_attention(q, k, v):
    bsz, s, nh, d = q.shape
    nb = -(-s // MOBA_BLOCK)
    pad = nb * MOBA_BLOCK - s
    topk = min(MOBA_TOPK, nb)
    qh = jnp.moveaxis(q, 2, 1) * d ** -0.5

    def to_blocks(t):
        t = jnp.pad(jnp.moveaxis(t, 2, 1), ((0, 0), (0, 0), (0, pad), (0, 0)))
        return t.reshape(bsz, nh, nb, MOBA_BLOCK, d)

    k_blk = to_blocks(k)
    v_blk = to_blocks(v)
    k_mean = jnp.mean(k_blk, axis=3)
    b_idx = jnp.arange(bsz)[:, None, None, None]
    h_idx = jnp.arange(nh)[None, :, None, None]
    blk_ids = jnp.arange(nb)

    def one_block(ci):
        start = ci * Q_BLOCK
        own = start // MOBA_BLOCK
        qc = lax.dynamic_slice_in_dim(qh, start, Q_BLOCK, axis=2)
        qpos = start + jnp.arange(Q_BLOCK)
        gate = jnp.einsum('bhqd,bhnd->bhqn', qc, k_mean)
        gate = jnp.where(blk_ids < own, gate, -jnp.inf)
        _, sel = lax.top_k(gate, topk)
        valid = sel < own
        k_sel = k_blk[b_idx, h_idx, sel]
        v_sel = v_blk[b_idx, h_idx, sel]
        s_sel = jnp.einsum('bhqd,bhqtkd->bhqtk', qc, k_sel)
        s_sel = jnp.where(valid[..., None], s_sel, -jnp.inf).reshape(bsz, nh, Q_BLOCK, topk * MOBA_BLOCK)
        k_own = lax.dynamic_index_in_dim(k_blk, own, axis=2, keepdims=False)
        v_own = lax.dynamic_index_in_dim(v_blk, own, axis=2, keepdims=False)
        kpos = own * MOBA_BLOCK + jnp.arange(MOBA_BLOCK)
        s_own = jnp.einsum('bhqd,bhkd->bhqk', qc, k_own)
        s_own = jnp.where(kpos[None, :] <= qpos[:, None], s_own, -jnp.inf)
        p = jax.nn.softmax(jnp.concatenate([s_sel, s_own], axis=-1), axis=-1)
        p_sel = p[..., :topk * MOBA_BLOCK].reshape(bsz, nh, Q_BLOCK, topk, MOBA_BLOCK)
        p_own = p[..., topk * MOBA_BLOCK:]
        return (jnp.einsum('bhqtk,bhqtkd->bhqd', p_sel, v_sel)
                + jnp.einsum('bhqk,bhkd->bhqd', p_own, v_own))

    o = lax.map(one_block, jnp.arange(s // Q_BLOCK))
    return jnp.transpose(o, (1, 0, 3, 2, 4)).reshape(bsz, s, nh, d)


def hybrid_mixer(h, cos, sin, w_in, conv_w, a_log, dt_bias, o_norm, w_out):
    bsz, s, _ = h.shape
    f32 = jnp.float32
    qkv_a, z_a, b_a, a_a, q_b, k_b, v_b = jnp.split(h @ w_in, HY_SPLITS, axis=-1)
    qkv_a = causal_conv_silu(qkv_a, conv_w)
    q_a, k_a, v_a = jnp.split(qkv_a, [GDN_QK, 2 * GDN_QK], axis=-1)
    q_a = l2norm(q_a.reshape(bsz, s, GDN_HEADS, GDN_DK))
    k_a = l2norm(k_a.reshape(bsz, s, GDN_HEADS, GDN_DK))
    v_a = v_a.reshape(bsz, s, GDN_HEADS, GDN_DV).astype(f32)
    beta = jax.nn.sigmoid(b_a.astype(f32))
    g = -jnp.exp(a_log.astype(f32)) * jax.nn.softplus(a_a.astype(f32) + dt_bias.astype(f32))
    o_a = gated_delta_rule(q_a, k_a, v_a, g, beta)
    o_a = rmsnorm(o_a, o_norm) * jax.nn.silu(z_a.reshape(bsz, s, GDN_HEADS, GDN_DV).astype(f32))
    q_b = apply_rope(q_b.reshape(bsz, s, MOBA_HEADS, MOBA_DH), cos, sin)
    k_b = apply_rope(k_b.reshape(bsz, s, MOBA_HEADS, MOBA_DH), cos, sin)
    v_b = v_b.reshape(bsz, s, MOBA_HEADS, MOBA_DH).astype(f32)
    o_b = moba_attention(q_b, k_b, v_b)
    o = jnp.concatenate([o_a.reshape(bsz, s, GDN_V), o_b.reshape(bsz, s, MOBA_W)], axis=-1)
    return o.astype(h.dtype) @ w_out


def mla_mixer(h, cos, sin, w_in, q_norm, kv_norm, w_uq, w_ukv, w_out):
    bsz, s, _ = h.shape
    f32 = jnp.float32
    cq, ckv, k_rope = jnp.split(h @ w_in, [MLA_Q_RANK, MLA_Q_RANK + MLA_KV_RANK], axis=-1)
    q = (rmsnorm(cq, q_norm) @ w_uq).reshape(bsz, s, MLA_HEADS, MLA_NOPE + MLA_ROPE)
    kv = (rmsnorm(ckv, kv_norm) @ w_ukv).reshape(bsz, s, MLA_HEADS, MLA_NOPE + MLA_V)
    q_nope, q_rope = jnp.split(q.astype(f32), [MLA_NOPE], axis=-1)
    k_nope, val = jnp.split(kv.astype(f32), [MLA_NOPE], axis=-1)
    q_rope = apply_rope(q_rope, cos, sin)
    k_rope = apply_rope(k_rope[:, :, None, :], cos, sin)[:, :, 0, :]
    scale = (MLA_NOPE + MLA_ROPE) ** -0.5
    qn = jnp.moveaxis(q_nope, 2, 1) * scale
    qr = jnp.moveaxis(q_rope, 2, 1) * scale
    kn = jnp.moveaxis(k_nope, 2, 1)
    vh = jnp.moveaxis(val, 2, 1)
    kpos = jnp.arange(s)

    def one_block(bi):
        start = bi * Q_BLOCK
        qn_b = lax.dynamic_slice_in_dim(qn, start, Q_BLOCK, axis=2)
        qr_b = lax.dynamic_slice_in_dim(qr, start, Q_BLOCK, axis=2)
        scores = (jnp.einsum('bhqd,bhkd->bhqk', qn_b, kn)
                  + jnp.einsum('bhqr,bkr->bhqk', qr_b, k_rope))
        qpos = start + jnp.arange(Q_BLOCK)
        scores = jnp.where(kpos[None, :] <= qpos[:, None], scores, -jnp.inf)
        return jnp.einsum('bhqk,bhkd->bhqd', jax.nn.softmax(scores, axis=-1), vh)

    o = lax.map(one_block, jnp.arange(s // Q_BLOCK))
    o = jnp.transpose(o, (1, 0, 3, 2, 4)).reshape(bsz, s, MLA_HEADS * MLA_V)
    return o.astype(h.dtype) @ w_out


def peer_ffn(h, w_q, sub_keys, u, v):
    bsz, s, d = h.shape
    f32 = jnp.float32
    q = (h @ w_q).astype(f32).reshape(bsz, s, PEER_HEADS, 2, PEER_DKEY // 2)
    scores = jnp.einsum('bshpd,hpnd->bshpn', q, sub_keys.astype(f32))
    s1, i1 = lax.top_k(scores[..., 0, :], PEER_TOPK)
    s2, i2 = lax.top_k(scores[..., 1, :], PEER_TOPK)
    n_cand = PEER_TOPK * PEER_TOPK
    cand = (s1[..., :, None] + s2[..., None, :]).reshape(bsz, s, PEER_HEADS, n_cand)
    cand_idx = (i1[..., :, None] * PEER_NKEYS + i2[..., None, :]).reshape(bsz, s, PEER_HEADS, n_cand)
    top, pos = lax.top_k(cand, PEER_TOPK)
    expert = jnp.take_along_axis(cand_idx, pos, axis=-1)
    gates = jax.nn.softmax(top, axis=-1)
    n_tok = bsz * s
    n_sel = PEER_HEADS * PEER_TOPK
    n_chunk = n_tok // PEER_TOK_CHUNK
    h_c = h.reshape(n_chunk, PEER_TOK_CHUNK, d)
    e_c = expert.reshape(n_chunk, PEER_TOK_CHUNK, n_sel)
    g_c = gates.reshape(n_chunk, PEER_TOK_CHUNK, n_sel)

    def one_chunk(args):
        hc, ec, gc = args
        act = jax.nn.gelu(jnp.einsum('td,ted->te', hc, u[ec]).astype(f32), approximate=False)
        return jnp.einsum('te,ted->td', (gc * act).astype(v.dtype), v[ec])

    out = lax.map(one_chunk, (h_c, e_c, g_c))
    return out.reshape(bsz, s, d).astype(h.dtype)


def setup_inputs(seed: int = 0) -> dict:
    key = jax.random.key(seed)
    ks = jax.random.split(key, 24)
    f32 = jnp.float32
    D = D_MODEL

    def nrm(k, shape, scale):
        return jax.random.normal(k, shape, f32) * scale

    def gain(k, shape):
        return 1.0 + 0.02 * jax.random.normal(k, shape, f32)

    dt = jnp.exp(jax.random.uniform(ks[9], (N_EVEN, GDN_HEADS), f32, math.log(1e-3), math.log(1e-1)))
    return {
        'x': nrm(ks[0], (BATCH, SEQ, D), 1.0),
        'c': nrm(ks[1], (BATCH, D), 1.0),
        'positions': jnp.broadcast_to(jnp.arange(SEQ, dtype=jnp.int32), (BATCH, SEQ)),
        'mod_w': nrm(ks[2], (DEPTH, D, 6 * D), 0.5 * D ** -0.5),
        'mod_b': nrm(ks[3], (DEPTH, 6 * D), 0.02),
        'norm_mix': gain(ks[4], (DEPTH, D)),
        'norm_ffn': gain(ks[5], (DEPTH, D)),
        'hy_w_in': nrm(ks[6], (N_EVEN, D, HY_IN), D ** -0.5),
        'gdn_conv': nrm(ks[7], (N_EVEN, GDN_CONV, GDN_CONV_CH), GDN_CONV ** -0.5),
        'gdn_a_log': jnp.log(jax.random.uniform(ks[8], (N_EVEN, GDN_HEADS), f32, 1.0, 16.0)),
        'gdn_dt_bias': dt + jnp.log(-jnp.expm1(-dt)),
        'gdn_o_norm': gain(ks[10], (N_EVEN, GDN_DV)),
        'hy_w_out': nrm(ks[11], (N_EVEN, HY_OUT, D), HY_OUT ** -0.5),
        'mla_w_in': nrm(ks[12], (N_ODD, D, MLA_IN), D ** -0.5),
        'mla_q_norm': gain(ks[13], (N_ODD, MLA_Q_RANK)),
        'mla_kv_norm': gain(ks[14], (N_ODD, MLA_KV_RANK)),
        'mla_w_uq': nrm(ks[15], (N_ODD, MLA_Q_RANK, MLA_HEADS * (MLA_NOPE + MLA_ROPE)), MLA_Q_RANK ** -0.5),
        'mla_w_ukv': nrm(ks[16], (N_ODD, MLA_KV_RANK, MLA_HEADS * (MLA_NOPE + MLA_V)), MLA_KV_RANK ** -0.5),
        'mla_w_out': nrm(ks[17], (N_ODD, MLA_HEADS * MLA_V, D), (MLA_HEADS * MLA_V) ** -0.5),
        'peer_w_q': nrm(ks[18], (DEPTH, D, PEER_HEADS * PEER_DKEY), D ** -0.5),
        'peer_sub_keys': nrm(ks[19], (DEPTH, PEER_HEADS, 2, PEER_NKEYS, PEER_DKEY // 2), (PEER_DKEY // 2) ** -0.5),
        'peer_u': nrm(ks[20], (DEPTH, PEER_EXPERTS, D), D ** -0.5),
        'peer_v': nrm(ks[21], (DEPTH, PEER_EXPERTS, D), PEER_HEADS ** -0.5),
        'final_norm': gain(ks[22], (D,)),
    }


def reference(x, c, positions, mod_w, mod_b, norm_mix, norm_ffn, hy_w_in, gdn_conv, gdn_a_log,
              gdn_dt_bias, gdn_o_norm, hy_w_out, mla_w_in, mla_q_norm, mla_kv_norm, mla_w_uq,
              mla_w_ukv, mla_w_out, peer_w_q, peer_sub_keys, peer_u, peer_v, final_norm):
    cond = jax.nn.silu(c)
    cos_b, sin_b = rope_tables(positions, MOBA_DH)
    cos_c, sin_c = rope_tables(positions, MLA_ROPE)
    for layer in range(DEPTH):
        mod = cond @ mod_w[layer] + mod_b[layer]
        sh1, sc1, g1, sh2, sc2, g2 = jnp.split(mod[:, None, :], 6, axis=-1)
        h = rmsnorm(x, norm_mix[layer]) * (1.0 + sc1) + sh1
        i = layer // 2
        if layer % 2 == 0:
            y = hybrid_mixer(h, cos_b, sin_b, hy_w_in[i], gdn_conv[i], gdn_a_log[i], gdn_dt_bias[i],
                             gdn_o_norm[i], hy_w_out[i])
        else:
            y = mla_mixer(h, cos_c, sin_c, mla_w_in[i], mla_q_norm[i], mla_kv_norm[i], mla_w_uq[i],
                          mla_w_ukv[i], mla_w_out[i])
        x = x + g1 * y
        h = rmsnorm(x, norm_ffn[layer]) * (1.0 + sc2) + sh2
        x = x + g2 * peer_ffn(h, peer_w_q[layer], peer_sub_keys[layer], peer_u[layer], peer_v[layer])
    return rmsnorm(x, final_norm)
```

```python
import functools
import math

import jax
import jax.numpy as jnp
from jax import lax
from jax.experimental import pallas as pl
from jax.experimental.pallas import tpu as pltpu

F32 = jnp.float32
BF16 = jnp.bfloat16
NEG_INF = float("-inf")
HIGHEST = lax.Precision.HIGHEST

D_MODEL = 1024
EPS = 1e-6
ROPE_THETA = 10000.0

GDN_HEADS = 4
GDN_DK = 128
GDN_CHUNK = 64
GDN_QK = GDN_HEADS * GDN_DK
GDN_V = GDN_HEADS * 128
GDN_CONV_CH = 2 * GDN_QK + GDN_V
GDN_ROWS = 256

MOBA_HEADS = 4
MOBA_DH = 128
MOBA_BLOCK = 256
MOBA_TOPK = 3
MOBA_W = MOBA_HEADS * MOBA_DH

MLA_HEADS = 8
MLA_Q_RANK = 512
MLA_KV_RANK = 256
MLA_NOPE = 128
MLA_ROPE = 64
MLA_V = 128
MLA_QK = MLA_NOPE + MLA_ROPE
ATT_BLOCK = 256

PEER_HEADS = 8
PEER_NKEYS = 128
PEER_TOPK = 16
PEER_DSUB = 128
PEER_TOK = 512
PEER_EXP = 512
PEER_TOPK_TOK = 256

LANES = 128
VMEM_LIMIT = 56 * 1024 * 1024


def _params(*sem):
    return pltpu.CompilerParams(dimension_semantics=sem, vmem_limit_bytes=VMEM_LIMIT)


def _dot(a, b, precision=None):
    return jnp.dot(a, b, preferred_element_type=F32, precision=precision)


def _dot_nt(a, b):
    return lax.dot_general(a, b, (((1,), (1,)), ((), ())), preferred_element_type=F32)


def _sigmoid(x):
    return jax.nn.sigmoid(x)


def _norm_mod(x, nw, sc, sh):
    y = x * lax.rsqrt(jnp.mean(x * x, axis=-1, keepdims=True) + EPS) * nw
    return y * (1.0 + sc) + sh


def _full(shape):
    n = len(shape)
    return pl.BlockSpec(shape, lambda *_: (0,) * n)


def _mod_kernel(c_ref, w_ref, b_ref, o_ref):
    c = c_ref[...]
    cond = c * _sigmoid(c)
    o_ref[0] = jnp.sum(cond * w_ref[0], axis=0, keepdims=True) + b_ref[0]


def _modulation(c_col, mod_w, mod_b):
    depth, d, n = mod_w.shape
    tn = 1536
    return pl.pallas_call(
        _mod_kernel,
        grid=(depth, n // tn),
        in_specs=[pl.BlockSpec((d, 1), lambda l, j: (0, 0)),
                  pl.BlockSpec((1, d, tn), lambda l, j: (l, 0, j)),
                  pl.BlockSpec((1, 1, tn), lambda l, j: (l, 0, j))],
        out_specs=pl.BlockSpec((1, 1, tn), lambda l, j: (l, 0, j)),
        out_shape=jax.ShapeDtypeStruct((depth, 1, n), F32),
        compiler_params=_params("parallel", "parallel"),
        name="adaln_modulation",
    )(c_col, mod_w, mod_b.reshape(depth, 1, n))


def _rope_kernel(pos_ref, fb_ref, fc_ref, cb_ref, sb_ref, cc_ref, sc_ref):
    p = pos_ref[...].astype(F32)
    lane = lax.broadcasted_iota(jnp.int32, cb_ref.shape, 1)
    ab = p * fb_ref[...]
    sb = jnp.sin(ab)
    cb_ref[...] = jnp.cos(ab)
    sb_ref[...] = jnp.where(lane < MOBA_DH // 2, -sb, sb)
    ac = p * fc_ref[...]
    sc = jnp.sin(ac)
    cc_ref[...] = jnp.cos(ac)
    sc_ref[...] = jnp.where(lane % MLA_ROPE < MLA_ROPE // 2, -sc, sc)


def _rope_tables(pos_col):
    s = pos_col.shape[0]
    tm = min(s, 1024)
    fb = ROPE_THETA ** (-jnp.arange(0, MOBA_DH, 2, dtype=F32) / MOBA_DH)
    fc = ROPE_THETA ** (-jnp.arange(0, MLA_ROPE, 2, dtype=F32) / MLA_ROPE)
    fb = jnp.tile(fb, 2).reshape(1, LANES)
    fc = jnp.tile(fc, 4).reshape(1, LANES)
    tab = pl.BlockSpec((tm, LANES), lambda i: (i, 0))
    return pl.pallas_call(
        _rope_kernel,
        grid=(s // tm,),
        in_specs=[pl.BlockSpec((tm, 1), lambda i: (i, 0)), _full((1, LANES)), _full((1, LANES))],
        out_specs=[tab] * 4,
        out_shape=[jax.ShapeDtypeStruct((s, LANES), F32)] * 4,
        compiler_params=_params("parallel"),
        name="rope_tables",
    )(pos_col, fb, fc)


def _rope_head128(x, cos, sin):
    return x * cos + pltpu.roll(x, MOBA_DH // 2, 1) * sin


def _rope_heads64(x, cos, sin):
    lane = lax.broadcasted_iota(jnp.int32, x.shape, 1)
    first = lane % MLA_ROPE < MLA_ROPE // 2
    partner = jnp.where(first, pltpu.roll(x, LANES - MLA_ROPE // 2, 1), pltpu.roll(x, MLA_ROPE // 2, 1))
    return x * cos + partner * sin


def _hy_proj_kernel(x_ref, nw_ref, sc_ref, sh_ref, wa_ref, wz_ref, wba_ref, wb_ref, cos_ref, sin_ref,
                    qkva_ref, z_ref, ba_ref, qb_ref, kb_ref, vb_ref):
    h = _norm_mod(x_ref[...], nw_ref[...], sc_ref[...], sh_ref[...]).astype(BF16)
    qkva_ref[...] = _dot(h, wa_ref[...])
    z_ref[...] = _dot(h, wz_ref[...])
    ba_ref[...] = _dot(h, wba_ref[...])
    qkvb = _dot(h, wb_ref[...])
    cos = cos_ref[...]
    sin = sin_ref[...]
    scale = MOBA_DH ** -0.5
    for hh in range(MOBA_HEADS):
        lo = hh * MOBA_DH
        q = qkvb[:, lo:lo + MOBA_DH]
        k = qkvb[:, MOBA_W + lo:MOBA_W + lo + MOBA_DH]
        qb_ref[:, lo:lo + MOBA_DH] = (_rope_head128(q, cos, sin) * scale).astype(BF16)
        kb_ref[:, lo:lo + MOBA_DH] = _rope_head128(k, cos, sin).astype(BF16)
    vb_ref[...] = qkvb[:, 2 * MOBA_W:].astype(BF16)


def _hy_proj(x, nw, sc, sh, wa, wz, wba, wb, cos, sin):
    s, d = x.shape
    tm = min(s, 256)
    row = lambda n: pl.BlockSpec((tm, n), lambda i: (i, 0))
    vec = _full((1, d))
    return pl.pallas_call(
        _hy_proj_kernel,
        grid=(s // tm,),
        in_specs=[row(d), vec, vec, vec, _full(wa.shape), _full(wz.shape), _full(wba.shape),
                  _full(wb.shape), row(LANES), row(LANES)],
        out_specs=[row(GDN_CONV_CH), row(GDN_V), row(LANES), row(MOBA_W), row(MOBA_W), row(MOBA_W)],
        out_shape=[jax.ShapeDtypeStruct((s, GDN_CONV_CH), F32),
                   jax.ShapeDtypeStruct((s, GDN_V), F32),
                   jax.ShapeDtypeStruct((s, LANES), F32),
                   jax.ShapeDtypeStruct((s, MOBA_W), BF16),
                   jax.ShapeDtypeStruct((s, MOBA_W), BF16),
                   jax.ShapeDtypeStruct((s, MOBA_W), BF16)],
        compiler_params=_params("parallel"),
        name="hybrid_in_proj",
    )(x, nw, sc, sh, wa, wz, wba, wb, cos, sin)


def _unit_lower_inverse(l_strict, eye):
    neg = -l_strict
    inv = eye + neg
    power = neg
    p = 2
    while p < GDN_CHUNK:
        power = _dot(power, power, HIGHEST)
        inv = _dot(inv, eye + power, HIGHEST)
        p *= 2
    return inv


def _gdn_kernel(qkv_ref, ba_ref, z_ref, conv_ref, alog_ref, dtb_ref, onorm_ref, o_ref, ext_ref, state_ref):
    tb = GDN_ROWS
    cs = GDN_CHUNK

    @pl.when(pl.program_id(0) == 0)
    def _():
        ext_ref[0:8, :] = jnp.zeros((8, GDN_CONV_CH), F32)
        state_ref[...] = jnp.zeros_like(state_ref)

    ext_ref[8:8 + tb, :] = qkv_ref[...]
    cw = conv_ref[...]
    y = (ext_ref[5:5 + tb, :] * cw[0:1] + ext_ref[6:6 + tb, :] * cw[1:2]
         + ext_ref[7:7 + tb, :] * cw[2:3] + ext_ref[8:8 + tb, :] * cw[3:4])
    ext_ref[0:8, :] = ext_ref[tb:tb + 8, :]
    y = y * _sigmoid(y)

    ba = ba_ref[...]
    beta_all = _sigmoid(ba)
    t = ba + dtb_ref[...]
    softplus = jnp.maximum(t, 0.0) + jnp.log1p(jnp.exp(-jnp.abs(t)))
    g_all = -jnp.exp(alog_ref[...]) * softplus

    row = lax.broadcasted_iota(jnp.int32, (tb, tb), 0)
    col = lax.broadcasted_iota(jnp.int32, (tb, tb), 1)
    same = (row // cs) == (col // cs)
    causal = same & (col <= row)
    strict = same & (col < row)
    eye = (row == col).astype(F32)
    g_cum = _dot(causal.astype(F32), g_all, HIGHEST)
    g_tot = _dot(same.astype(F32), g_all, HIGHEST)
    g_cum_t = jnp.transpose(g_cum)
    z = z_ref[...]
    onorm = onorm_ref[...]

    for h in range(GDN_HEADS):
        lo = h * GDN_DK
        q = y[:, lo:lo + GDN_DK]
        k = y[:, GDN_QK + lo:GDN_QK + lo + GDN_DK]
        v = y[:, 2 * GDN_QK + lo:2 * GDN_QK + lo + GDN_DK]
        q = q * lax.rsqrt(jnp.sum(q * q, axis=-1, keepdims=True) + EPS) * (GDN_DK ** -0.5)
        k = k * lax.rsqrt(jnp.sum(k * k, axis=-1, keepdims=True) + EPS)
        beta = beta_all[:, h:h + 1]
        gcol = g_cum[:, GDN_HEADS + h:GDN_HEADS + h + 1]
        grow = g_cum_t[GDN_HEADS + h:GDN_HEADS + h + 1, :]
        gtot = g_tot[:, GDN_HEADS + h:GDN_HEADS + h + 1]
        decay = jnp.exp(jnp.where(causal, gcol - grow, NEG_INF))
        kb = k * beta
        k16 = k.astype(BF16)
        l_strict = jnp.where(strict, _dot_nt(kb.astype(BF16), k16) * decay, 0.0)
        t_inv = _unit_lower_inverse(l_strict, eye).astype(BF16)
        eg = jnp.exp(gcol)
        uw = _dot(t_inv, jnp.concatenate([v * beta, kb * eg], axis=1).astype(BF16))
        u = uw[:, :GDN_DK]
        w = uw[:, GDN_DK:].astype(BF16)
        qk = (_dot_nt(q.astype(BF16), k16) * decay).astype(BF16)
        qdec = (q * eg).astype(BF16)
        kdec = k * jnp.exp(gtot - gcol)

        state = state_ref[h]
        v_new = []
        o_state = []
        for c in range(tb // cs):
            r0 = c * cs
            s16 = state.astype(BF16)
            vn = u[r0:r0 + cs] - _dot(w[r0:r0 + cs], s16)
            o_state.append(_dot(qdec[r0:r0 + cs], s16))
            kd_t = jnp.transpose(kdec[r0:r0 + cs]).astype(BF16)
            state = state * jnp.exp(gtot[r0:r0 + 1]) + _dot(kd_t, vn.astype(BF16))
            v_new.append(vn)
        state_ref[h] = state
        o = jnp.concatenate(o_state, axis=0) + _dot(qk, jnp.concatenate(v_new, axis=0).astype(BF16))

        o = o * lax.rsqrt(jnp.mean(o * o, axis=-1, keepdims=True) + EPS) * onorm
        zh = z[:, lo:lo + GDN_DK]
        o_ref[:, lo:lo + GDN_DK] = (o * (zh * _sigmoid(zh))).astype(BF16)


def _gdn(qkva, ba, z, conv_w, alog, dtb, onorm):
    s = qkva.shape[0]
    tb = GDN_ROWS
    row = lambda n: pl.BlockSpec((tb, n), lambda i: (i, 0))
    return pl.pallas_call(
        _gdn_kernel,
        grid=(s // tb,),
        in_specs=[row(GDN_CONV_CH), row(LANES), row(GDN_V), _full(conv_w.shape),
                  _full((1, LANES)), _full((1, LANES)), _full((1, LANES))],
        out_specs=row(GDN_V),
        out_shape=jax.ShapeDtypeStruct((s, GDN_V), BF16),
        scratch_shapes=[pltpu.VMEM((tb + 8, GDN_CONV_CH), F32),
                        pltpu.VMEM((GDN_HEADS, GDN_DK, 128), F32)],
        compiler_params=_params("arbitrary"),
        name="gated_delta_rule",
    )(qkva, ba, z, conv_w, alog, dtb, onorm)


def _softmax_first_block(s, v, m_ref, l_ref, acc_ref):
    m = jnp.max(s, axis=1, keepdims=True)
    p = jnp.exp(s - m)
    m_ref[...] = m
    l_ref[...] = jnp.sum(p, axis=1, keepdims=True)
    acc_ref[...] = _dot(p.astype(BF16), v)


def _softmax_next_block(s, v, m_ref, l_ref, acc_ref):
    m_old = m_ref[...]
    m_new = jnp.maximum(m_old, jnp.max(s, axis=1, keepdims=True))
    alpha = jnp.exp(m_old - m_new)
    p = jnp.exp(s - m_new)
    l_ref[...] = alpha * l_ref[...] + jnp.sum(p, axis=1, keepdims=True)
    acc_ref[...] = alpha * acc_ref[...] + _dot(p.astype(BF16), v)
    m_ref[...] = m_new


def _moba_kernel(q_ref, k_ref, v_ref, o_ref, kmean_ref, sel_ref, m_ref, l_ref, acc_ref, *, n_blocks):
    i = pl.program_id(1)
    bs = MOBA_BLOCK

    @pl.when(i == 0)
    def _():
        kmean_ref[...] = jnp.zeros_like(kmean_ref)

        def mean_body(n, carry):
            blk = k_ref[pl.ds(pl.multiple_of(n * bs, bs), bs), :].astype(F32)
            kmean_ref[pl.ds(n, 1), :] = jnp.mean(blk, axis=0, keepdims=True)
            return carry

        lax.fori_loop(0, n_blocks, mean_body, 0)

    q = q_ref[...]
    gate = _dot_nt(q, kmean_ref[...].astype(BF16))
    blk_id = lax.broadcasted_iota(jnp.int32, gate.shape, 1)
    gate = jnp.where(blk_id < i, gate, NEG_INF)
    sel = jnp.zeros(gate.shape, F32)
    for _ in range(MOBA_TOPK):
        gmax = jnp.max(gate, axis=1, keepdims=True)
        first = jnp.min(jnp.where(gate == gmax, blk_id, LANES), axis=1, keepdims=True)
        hit = blk_id == first
        sel = jnp.where(hit & (gmax > NEG_INF), 1.0, sel)
        gate = jnp.where(hit, NEG_INF, gate)
    sel_ref[...] = sel

    own = pl.ds(pl.multiple_of(i * bs, bs), bs)
    s = _dot_nt(q, k_ref[own, :])
    row = lax.broadcasted_iota(jnp.int32, s.shape, 0)
    col = lax.broadcasted_iota(jnp.int32, s.shape, 1)
    s = jnp.where(col <= row, s, NEG_INF)
    _softmax_first_block(s, v_ref[own, :], m_ref, l_ref, acc_ref)

    def past_body(j, carry):
        blk = pl.ds(pl.multiple_of(j * bs, bs), bs)
        s = _dot_nt(q, k_ref[blk, :])
        lanes = lax.broadcasted_iota(jnp.int32, sel_ref.shape, 1)
        chosen = jnp.sum(jnp.where(lanes == j, sel_ref[...], 0.0), axis=1, keepdims=True)
        s = jnp.where(chosen > 0.0, s, NEG_INF)
        _softmax_next_block(s, v_ref[blk, :], m_ref, l_ref, acc_ref)
        return carry

    lax.fori_loop(0, i, past_body, 0)
    o_ref[...] = (acc_ref[...] / l_ref[...]).astype(o_ref.dtype)


def _moba(q, k, v):
    s = q.shape[0]
    bs = MOBA_BLOCK
    n_blocks = s // bs
    assert n_blocks <= LANES
    kv = pl.BlockSpec((s, MOBA_DH), lambda h, i: (0, h))
    qo = pl.BlockSpec((bs, MOBA_DH), lambda h, i: (i, h))
    return pl.pallas_call(
        functools.partial(_moba_kernel, n_blocks=n_blocks),
        grid=(MOBA_HEADS, n_blocks),
        in_specs=[qo, kv, kv],
        out_specs=qo,
        out_shape=jax.ShapeDtypeStruct((s, MOBA_W), BF16),
        scratch_shapes=[pltpu.VMEM((LANES, MOBA_DH), F32),
                        pltpu.VMEM((bs, LANES), F32),
                        pltpu.VMEM((bs, 1), F32),
                        pltpu.VMEM((bs, 1), F32),
                        pltpu.VMEM((bs, MOBA_DH), F32)],
        compiler_params=_params("parallel", "arbitrary"),
        name="moba_attention",
    )(q, k, v)


def _outproj_kernel(*refs, n_in):
    o_refs = refs[:n_in]
    w_refs = refs[n_in:2 * n_in]
    x_ref, g_ref, out_ref = refs[2 * n_in:]
    y = _dot(o_refs[0][...], w_refs[0][...])
    for o_ref, w_ref in zip(o_refs[1:], w_refs[1:]):
        y = y + _dot(o_ref[...], w_ref[...])
    out_ref[...] = x_ref[...] + g_ref[...] * y


def _outproj_residual(os_, ws, x, gate):
    s, d = x.shape
    tm = min(s, 512)
    row = lambda n: pl.BlockSpec((tm, n), lambda i: (i, 0))
    return pl.pallas_call(
        functools.partial(_outproj_kernel, n_in=len(os_)),
        grid=(s // tm,),
        in_specs=[row(o.shape[1]) for o in os_] + [_full(w.shape) for w in ws] + [row(d), _full((1, d))],
        out_specs=row(d),
        out_shape=jax.ShapeDtypeStruct((s, d), F32),
        compiler_params=_params("parallel"),
        name="out_proj_residual",
    )(*os_, *ws, x, gate)


def _mla_proj_kernel(x_ref, nw_ref, sc_ref, sh_ref, wcq_ref, wckv_ref, wkr_ref, qn_ref, kvn_ref,
                     wqn_ref, wqr_ref, wkn_ref, wv_ref, cos_ref, sin_ref, q_ref, k_ref, v_ref):
    h = _norm_mod(x_ref[...], nw_ref[...], sc_ref[...], sh_ref[...]).astype(BF16)
    cq = _dot(h, wcq_ref[...])
    ckv = _dot(h, wckv_ref[...])
    kr = _dot(h, wkr_ref[...])
    cq = (cq * lax.rsqrt(jnp.mean(cq * cq, axis=-1, keepdims=True) + EPS) * qn_ref[...]).astype(BF16)
    ckv = (ckv * lax.rsqrt(jnp.mean(ckv * ckv, axis=-1, keepdims=True) + EPS) * kvn_ref[...]).astype(BF16)
    cos = cos_ref[...]
    sin = sin_ref[...]
    scale = MLA_QK ** -0.5
    q_nope = _dot(cq, wqn_ref[...]) * scale
    q_rope = _dot(cq, wqr_ref[...])
    k_nope = _dot(ckv, wkn_ref[...])
    val = _dot(ckv, wv_ref[...])
    k_rope = _rope_heads64(kr, cos, sin)[:, :MLA_ROPE].astype(BF16)
    for pair in range(MLA_HEADS // 2):
        slab = _rope_heads64(q_rope[:, pair * LANES:(pair + 1) * LANES], cos, sin) * scale
        for sub in range(2):
            hh = 2 * pair + sub
            q_ref[hh, :, MLA_NOPE:] = slab[:, sub * MLA_ROPE:(sub + 1) * MLA_ROPE].astype(BF16)
    for hh in range(MLA_HEADS):
        q_ref[hh, :, :MLA_NOPE] = q_nope[:, hh * MLA_NOPE:(hh + 1) * MLA_NOPE].astype(BF16)
        k_ref[hh, :, :MLA_NOPE] = k_nope[:, hh * MLA_NOPE:(hh + 1) * MLA_NOPE].astype(BF16)
        k_ref[hh, :, MLA_NOPE:] = k_rope
        v_ref[hh] = val[:, hh * MLA_V:(hh + 1) * MLA_V].astype(BF16)


def _mla_proj(x, nw, sc, sh, wcq, wckv, wkr, qn, kvn, wqn, wqr, wkn, wv, cos, sin):
    s, d = x.shape
    tm = min(s, 256)
    row = lambda n: pl.BlockSpec((tm, n), lambda i: (i, 0))
    vec = _full((1, d))
    heads = lambda n: pl.BlockSpec((MLA_HEADS, tm, n), lambda i: (0, i, 0))
    return pl.pallas_call(
        _mla_proj_kernel,
        grid=(s // tm,),
        in_specs=[row(d), vec, vec, vec, _full(wcq.shape), _full(wckv.shape), _full(wkr.shape),
                  _full(qn.shape), _full(kvn.shape), _full(wqn.shape), _full(wqr.shape),
                  _full(wkn.shape), _full(wv.shape), row(LANES), row(LANES)],
        out_specs=[heads(MLA_QK), heads(MLA_QK), heads(MLA_V)],
        out_shape=[jax.ShapeDtypeStruct((MLA_HEADS, s, MLA_QK), BF16),
                   jax.ShapeDtypeStruct((MLA_HEADS, s, MLA_QK), BF16),
                   jax.ShapeDtypeStruct((MLA_HEADS, s, MLA_V), BF16)],
        compiler_params=_params("parallel"),
        name="mla_in_proj",
    )(x, nw, sc, sh, wcq, wckv, wkr, qn, kvn, wqn, wqr, wkn, wv, cos, sin)


def _mla_attn_kernel(q_ref, k_ref, v_ref, o_ref, m_ref, l_ref, acc_ref):
    i = pl.program_id(1)
    bs = ATT_BLOCK
    q = q_ref[0]
    own = pl.ds(pl.multiple_of(i * bs, bs), bs)
    s = _dot_nt(q, k_ref[0, own, :])
    row = lax.broadcasted_iota(jnp.int32, s.shape, 0)
    col = lax.broadcasted_iota(jnp.int32, s.shape, 1)
    s = jnp.where(col <= row, s, NEG_INF)
    _softmax_first_block(s, v_ref[0, own, :], m_ref, l_ref, acc_ref)

    def past_body(j, carry):
        blk = pl.ds(pl.multiple_of(j * bs, bs), bs)
        s = _dot_nt(q, k_ref[0, blk, :])
        _softmax_next_block(s, v_ref[0, blk, :], m_ref, l_ref, acc_ref)
        return carry

    lax.fori_loop(0, i, past_body, 0)
    o_ref[...] = (acc_ref[...] / l_ref[...]).astype(o_ref.dtype)


def _mla_attention(q, k, v):
    _, s, _ = q.shape
    bs = min(s, ATT_BLOCK)
    return pl.pallas_call(
        _mla_attn_kernel,
        grid=(MLA_HEADS, s // bs),
        in_specs=[pl.BlockSpec((1, bs, MLA_QK), lambda h, i: (h, i, 0)),
                  pl.BlockSpec((1, s, MLA_QK), lambda h, i: (h, 0, 0)),
                  pl.BlockSpec((1, s, MLA_V), lambda h, i: (h, 0, 0))],
        out_specs=pl.BlockSpec((bs, MLA_V), lambda h, i: (i, h)),
        out_shape=jax.ShapeDtypeStruct((s, MLA_HEADS * MLA_V), BF16),
        scratch_shapes=[pltpu.VMEM((bs, 1), F32), pltpu.VMEM((bs, 1), F32), pltpu.VMEM((bs, MLA_V), F32)],
        compiler_params=_params("parallel", "arbitrary"),
        name="mla_attention",
    )(q, k, v)


def _peer_proj_kernel(x_ref, nw_ref, sc_ref, sh_ref, wq_ref, keys_ref, h_ref, s_ref):
    h = _norm_mod(x_ref[...], nw_ref[...], sc_ref[...], sh_ref[...]).astype(BF16)
    h_ref[...] = h
    q = _dot(h, wq_ref[...]).astype(BF16)
    for hp in range(2 * PEER_HEADS):
        s_ref[hp] = _dot_nt(keys_ref[hp], q[:, hp * PEER_DSUB:(hp + 1) * PEER_DSUB])


def _peer_proj(x, nw, sc, sh, wq, keys):
    s, d = x.shape
    tm = min(s, 256)
    vec = _full((1, d))
    return pl.pallas_call(
        _peer_proj_kernel,
        grid=(s // tm,),
        in_specs=[pl.BlockSpec((tm, d), lambda i: (i, 0)), vec, vec, vec, _full(wq.shape), _full(keys.shape)],
        out_specs=[pl.BlockSpec((tm, d), lambda i: (i, 0)),
                   pl.BlockSpec((2 * PEER_HEADS, PEER_NKEYS, tm), lambda i: (0, 0, i))],
        out_shape=[jax.ShapeDtypeStruct((s, d), BF16),
                   jax.ShapeDtypeStruct((2 * PEER_HEADS, PEER_NKEYS, s), F32)],
        compiler_params=_params("parallel"),
        name="peer_query_scores",
    )(x, nw, sc, sh, wq, keys)


def _top_sorted(x, k):
    idx = lax.broadcasted_iota(jnp.int32, x.shape, 0)
    out = []
    for _ in range(k):
        m = jnp.max(x, axis=0, keepdims=True)
        first = jnp.min(jnp.where(x == m, idx, x.shape[0]), axis=0, keepdims=True)
        x = jnp.where(idx == first, NEG_INF, x)
        out.append(m)
    return out


_STAIRCASE = [(r1, r2) for r1 in range(PEER_TOPK) for r2 in range(PEER_TOPK) if (r1 + 1) * (r2 + 1) <= PEER_TOPK]


def _peer_topk_kernel(s_ref, a_ref, b_ref, thr_ref):
    def head_body(h, carry):
        s1 = s_ref[2 * h]
        s2 = s_ref[2 * h + 1]
        v1 = _top_sorted(s1, PEER_TOPK)
        v2 = _top_sorted(s2, PEER_TOPK)
        cand = jnp.concatenate([v1[r1] + v2[r2] for r1, r2 in _STAIRCASE], axis=0)
        top = _top_sorted(cand, PEER_TOPK)
        z = jnp.ones_like(top[0])
        for r in range(1, PEER_TOPK):
            z = z + jnp.exp(top[r] - top[0])
        a_ref[h] = jnp.exp(s1 - v1[0]) / z
        b_ref[h] = jnp.exp(s2 - v2[0])
        thr_ref[pl.ds(h, 1), :] = top[PEER_TOPK - 1]
        return carry

    lax.fori_loop(0, PEER_HEADS, head_body, 0)


def _peer_topk(scores):
    _, nk, s = scores.shape
    tt = min(s, PEER_TOPK_TOK)
    big = pl.BlockSpec((PEER_HEADS, nk, tt), lambda i: (0, 0, i))
    return pl.pallas_call(
        _peer_topk_kernel,
        grid=(s // tt,),
        in_specs=[pl.BlockSpec((2 * PEER_HEADS, nk, tt), lambda i: (0, 0, i))],
        out_specs=[big, big, pl.BlockSpec((PEER_HEADS, tt), lambda i: (0, i))],
        out_shape=[jax.ShapeDtypeStruct((PEER_HEADS, nk, s), F32),
                   jax.ShapeDtypeStruct((PEER_HEADS, nk, s), F32),
                   jax.ShapeDtypeStruct((PEER_HEADS, s), F32)],
        compiler_params=_params("parallel"),
        name="peer_topk_threshold",
    )(scores)


def _gelu(x):
    return 0.5 * x * (1.0 + lax.erf(x * (2.0 ** -0.5)))


def _peer_dense_kernel(h_ref, s_ref, a_ref, b_ref, thr_ref, u_ref, vt_ref, x_ref, g_ref, o_ref,
                       acc_ref, p_ref):
    j = pl.program_id(1)
    nk = PEER_NKEYS

    @pl.when(j == 0)
    def _():
        acc_ref[...] = jnp.zeros_like(acc_ref)

    act_t = _dot_nt(u_ref[...], h_ref[...])
    for blk in range(PEER_EXP // nk):
        i1 = j * (PEER_EXP // nk) + blk
        wgt = jnp.zeros((nk, PEER_TOK), F32)
        for h in range(PEER_HEADS):
            s1_row = s_ref[2 * h, pl.ds(i1, 1), :]
            a_row = a_ref[h, pl.ds(i1, 1), :]
            chosen = (s1_row + s_ref[2 * h + 1]) >= thr_ref[h:h + 1, :]
            wgt = wgt + jnp.where(chosen, b_ref[h], 0.0) * a_row
        p_ref[blk * nk:(blk + 1) * nk, :] = (wgt * _gelu(act_t[blk * nk:(blk + 1) * nk])).astype(BF16)
    acc_ref[...] += _dot(vt_ref[...], p_ref[...])

    @pl.when(j == pl.num_programs(1) - 1)
    def _():
        o_ref[...] = x_ref[...] + g_ref[...] * jnp.transpose(acc_ref[...])


def _peer_dense(h2, scores, a, b, thr, u16, vt16, x, gate):
    s, d = x.shape
    n_exp = u16.shape[0]
    tt = min(s, PEER_TOK)
    assert tt == PEER_TOK
    tok3 = lambda n: pl.BlockSpec((n, PEER_NKEYS, tt), lambda i, j: (0, 0, i))
    return pl.pallas_call(
        _peer_dense_kernel,
        grid=(s // tt, n_exp // PEER_EXP),
        in_specs=[pl.BlockSpec((tt, d), lambda i, j: (i, 0)),
                  tok3(2 * PEER_HEADS), tok3(PEER_HEADS), tok3(PEER_HEADS),
                  pl.BlockSpec((PEER_HEADS, tt), lambda i, j: (0, i)),
                  pl.BlockSpec((PEER_EXP, d), lambda i, j: (j, 0)),
                  pl.BlockSpec((d, PEER_EXP), lambda i, j: (0, j)),
                  pl.BlockSpec((tt, d), lambda i, j: (i, 0)),
                  pl.BlockSpec((1, d), lambda i, j: (0, 0))],
        out_specs=pl.BlockSpec((tt, d), lambda i, j: (i, 0)),
        out_shape=jax.ShapeDtypeStruct((s, d), F32),
        scratch_shapes=[pltpu.VMEM((d, tt), F32), pltpu.VMEM((PEER_EXP, tt), BF16)],
        compiler_params=_params("parallel", "arbitrary"),
        name="peer_dense_experts",
    )(h2, scores, a, b, thr, u16, vt16, x, gate)


def _final_norm_kernel(x_ref, w_ref, o_ref):
    x = x_ref[...]
    o_ref[...] = x * lax.rsqrt(jnp.mean(x * x, axis=-1, keepdims=True) + EPS) * w_ref[...]


def _final_norm(x, w):
    s, d = x.shape
    tm = min(s, 1024)
    return pl.pallas_call(
        _final_norm_kernel,
        grid=(s // tm,),
        in_specs=[pl.BlockSpec((tm, d), lambda i: (i, 0)), _full((1, d))],
        out_specs=pl.BlockSpec((tm, d), lambda i: (i, 0)),
        out_shape=jax.ShapeDtypeStruct((s, d), F32),
        compiler_params=_params("parallel"),
        name="final_rmsnorm",
    )(x, w)


def _lane_row(values, offset):
    return jnp.zeros((1, LANES), F32).at[0, offset:offset + values.shape[0]].set(values.astype(F32))


def _hybrid_layer(x, nw, sc, sh, gate, cos, sin, w_in, conv_w, a_log, dt_bias, o_norm, w_out):
    w16 = w_in.astype(BF16)
    c0 = GDN_CONV_CH
    c1 = c0 + GDN_V
    c2 = c1 + 2 * GDN_HEADS
    wba = jnp.pad(w16[:, c1:c2], ((0, 0), (0, LANES - 2 * GDN_HEADS)))
    qkva, z, ba, qb, kb, vb = _hy_proj(x, nw, sc, sh, w16[:, :c0], w16[:, c0:c1], wba, w16[:, c2:], cos, sin)
    o_a = _gdn(qkva, ba, z, conv_w, _lane_row(a_log, GDN_HEADS), _lane_row(dt_bias, GDN_HEADS),
               o_norm.reshape(1, -1))
    o_b = _moba(qb, kb, vb)
    wo16 = w_out.astype(BF16)
    return _outproj_residual([o_a, o_b], [wo16[:GDN_V], wo16[GDN_V:]], x, gate)


def _mla_layer(x, nw, sc, sh, gate, cos, sin, w_in, q_norm, kv_norm, w_uq, w_ukv, w_out):
    w16 = w_in.astype(BF16)
    r0 = MLA_Q_RANK
    r1 = r0 + MLA_KV_RANK
    wkr = jnp.pad(w16[:, r1:], ((0, 0), (0, LANES - MLA_ROPE)))
    wuq = w_uq.astype(BF16).reshape(MLA_Q_RANK, MLA_HEADS, MLA_QK)
    wqn = wuq[:, :, :MLA_NOPE].reshape(MLA_Q_RANK, MLA_HEADS * MLA_NOPE)
    wqr = wuq[:, :, MLA_NOPE:].reshape(MLA_Q_RANK, MLA_HEADS * MLA_ROPE)
    wukv = w_ukv.astype(BF16).reshape(MLA_KV_RANK, MLA_HEADS, MLA_NOPE + MLA_V)
    wkn = wukv[:, :, :MLA_NOPE].reshape(MLA_KV_RANK, MLA_HEADS * MLA_NOPE)
    wv = wukv[:, :, MLA_NOPE:].reshape(MLA_KV_RANK, MLA_HEADS * MLA_V)
    q, k, v = _mla_proj(x, nw, sc, sh, w16[:, :r0], w16[:, r0:r1], wkr, q_norm.reshape(1, -1),
                        kv_norm.reshape(1, -1), wqn, wqr, wkn, wv, cos, sin)
    o = _mla_attention(q, k, v)
    return _outproj_residual([o], [w_out.astype(BF16)], x, gate)


def _peer_layer(x, nw, sc, sh, gate, w_q, sub_keys, u, v):
    keys = sub_keys.astype(BF16).reshape(2 * PEER_HEADS, PEER_NKEYS, PEER_DSUB)
    h2, scores = _peer_proj(x, nw, sc, sh, w_q.astype(BF16), keys)
    a, b, thr = _peer_topk(scores)
    return _peer_dense(h2, scores, a, b, thr, u.astype(BF16), v.astype(BF16).T, x, gate)


def kernel(x, c, positions, mod_w, mod_b, norm_mix, norm_ffn, hy_w_in, gdn_conv, gdn_a_log, gdn_dt_bias, gdn_o_norm, hy_w_out, mla_w_in, mla_q_norm, mla_kv_norm, mla_w_uq, mla_w_ukv, mla_w_out, peer_w_q, peer_sub_keys, peer_u, peer_v, final_norm):
    bsz, s, d = x.shape
    assert bsz == 1 and d == D_MODEL
    depth = mod_w.shape[0]
    xs = x.reshape(s, d)
    mod = _modulation(c.reshape(d, 1), mod_w, mod_b)
    cos_b, sin_b, cos_c, sin_c = _rope_tables(positions.reshape(s, 1))
    for layer in range(depth):
        sh1, sc1, g1, sh2, sc2, g2 = (mod[layer, :, n * d:(n + 1) * d] for n in range(6))
        nw = norm_mix[layer].reshape(1, d)
        i = layer // 2
        if layer % 2 == 0:
            xs = _hybrid_layer(xs, nw, sc1, sh1, g1, cos_b, sin_b, hy_w_in[i], gdn_conv[i], gdn_a_log[i],
                               gdn_dt_bias[i], gdn_o_norm[i], hy_w_out[i])
        else:
            xs = _mla_layer(xs, nw, sc1, sh1, g1, cos_c, sin_c, mla_w_in[i], mla_q_norm[i], mla_kv_norm[i],
                            mla_w_uq[i], mla_w_ukv[i], mla_w_out[i])
        xs = _peer_layer(xs, norm_ffn[layer].reshape(1, d), sc2, sh2, g2, peer_w_q[layer],
                         peer_sub_keys[layer], peer_u[layer], peer_v[layer])
    return _final_norm(xs, final_norm.reshape(1, d)).reshape(bsz, s, d)
```

```python
import functools
import math

import jax
import jax.numpy as jnp
from jax import lax
from jax.experimental import pallas as pl
from jax.experimental.pallas import tpu as pltpu

F32 = jnp.float32
BF16 = jnp.bfloat16
NEG_INF = float("-inf")
HIGHEST = lax.Precision.HIGHEST

D_MODEL = 1024
EPS = 1e-6
ROPE_THETA = 10000.0

GDN_HEADS = 4
GDN_DK = 128
GDN_CHUNK = 64
GDN_QK = GDN_HEADS * GDN_DK
GDN_V = GDN_HEADS * 128
GDN_CONV_CH = 2 * GDN_QK + GDN_V
GDN_ROWS = 256

MOBA_HEADS = 4
MOBA_DH = 128
MOBA_BLOCK = 256
MOBA_TOPK = 3
MOBA_W = MOBA_HEADS * MOBA_DH

MLA_HEADS = 8
MLA_Q_RANK = 512
MLA_KV_RANK = 256
MLA_NOPE = 128
MLA_ROPE = 64
MLA_V = 128
MLA_QK = MLA_NOPE + MLA_ROPE
KEY_BLOCK = 256
MLA_Q_BLOCK = 512

PEER_HEADS = 8
PEER_NKEYS = 128
PEER_TOPK = 16
PEER_DSUB = 128
PEER_TOK = 512
PEER_EXP = 512
PEER_TOPK_TOK = 256

LANES = 128
VMEM_LIMIT = 56 * 1024 * 1024


def _params(*sem):
    return pltpu.CompilerParams(dimension_semantics=sem, vmem_limit_bytes=VMEM_LIMIT)


def _dot(a, b, precision=None):
    return jnp.dot(a, b, preferred_element_type=F32, precision=precision)


def _dot_nt(a, b):
    return lax.dot_general(a, b, (((1,), (1,)), ((), ())), preferred_element_type=F32)


def _sigmoid(x):
    return jax.nn.sigmoid(x)


def _norm_mod(x, nw, sc, sh):
    y = x * lax.rsqrt(jnp.mean(x * x, axis=-1, keepdims=True) + EPS) * nw
    return y * (1.0 + sc) + sh


def _full(shape):
    n = len(shape)
    return pl.BlockSpec(shape, lambda *_: (0,) * n)


def _mod_kernel(c_ref, w_ref, b_ref, o_ref):
    c = c_ref[...]
    cond = c * _sigmoid(c)
    o_ref[0] = jnp.sum(cond * w_ref[0], axis=0, keepdims=True) + b_ref[0]


def _modulation(c_col, mod_w, mod_b):
    depth, d, n = mod_w.shape
    tn = 1536
    return pl.pallas_call(
        _mod_kernel,
        grid=(depth, n // tn),
        in_specs=[pl.BlockSpec((d, 1), lambda l, j: (0, 0)),
                  pl.BlockSpec((1, d, tn), lambda l, j: (l, 0, j)),
                  pl.BlockSpec((1, 1, tn), lambda l, j: (l, 0, j))],
        out_specs=pl.BlockSpec((1, 1, tn), lambda l, j: (l, 0, j)),
        out_shape=jax.ShapeDtypeStruct((depth, 1, n), F32),
        compiler_params=_params("parallel", "parallel"),
        name="adaln_modulation",
    )(c_col, mod_w, mod_b.reshape(depth, 1, n))


def _rope_kernel(pos_ref, fb_ref, fc_ref, cb_ref, sb_ref, cc_ref, sc_ref):
    p = pos_ref[...].astype(F32)
    lane = lax.broadcasted_iota(jnp.int32, cb_ref.shape, 1)
    ab = p * fb_ref[...]
    sb = jnp.sin(ab)
    cb_ref[...] = jnp.cos(ab)
    sb_ref[...] = jnp.where(lane < MOBA_DH // 2, -sb, sb)
    ac = p * fc_ref[...]
    sc = jnp.sin(ac)
    cc_ref[...] = jnp.cos(ac)
    sc_ref[...] = jnp.where(lane % MLA_ROPE < MLA_ROPE // 2, -sc, sc)


def _rope_tables(pos_col):
    s = pos_col.shape[0]
    tm = min(s, 1024)
    fb = ROPE_THETA ** (-jnp.arange(0, MOBA_DH, 2, dtype=F32) / MOBA_DH)
    fc = ROPE_THETA ** (-jnp.arange(0, MLA_ROPE, 2, dtype=F32) / MLA_ROPE)
    fb = jnp.tile(fb, 2).reshape(1, LANES)
    fc = jnp.tile(fc, 4).reshape(1, LANES)
    tab = pl.BlockSpec((tm, LANES), lambda i: (i, 0))
    return pl.pallas_call(
        _rope_kernel,
        grid=(s // tm,),
        in_specs=[pl.BlockSpec((tm, 1), lambda i: (i, 0)), _full((1, LANES)), _full((1, LANES))],
        out_specs=[tab] * 4,
        out_shape=[jax.ShapeDtypeStruct((s, LANES), F32)] * 4,
        compiler_params=_params("parallel"),
        name="rope_tables",
    )(pos_col, fb, fc)


def _rope_head128(x, cos, sin):
    return x * cos + pltpu.roll(x, MOBA_DH // 2, 1) * sin


def _rope_heads64(x, cos, sin):
    lane = lax.broadcasted_iota(jnp.int32, x.shape, 1)
    first = lane % MLA_ROPE < MLA_ROPE // 2
    partner = jnp.where(first, pltpu.roll(x, LANES - MLA_ROPE // 2, 1), pltpu.roll(x, MLA_ROPE // 2, 1))
    return x * cos + partner * sin


def _hy_proj_kernel(x_ref, nw_ref, sc_ref, sh_ref, wa_ref, wz_ref, wba_ref, wb_ref, cos_ref, sin_ref,
                    qkva_ref, z_ref, ba_ref, qb_ref, kb_ref, vb_ref):
    h = _norm_mod(x_ref[...], nw_ref[...], sc_ref[...], sh_ref[...]).astype(BF16)
    qkva_ref[...] = _dot(h, wa_ref[...])
    z_ref[...] = _dot(h, wz_ref[...])
    ba_ref[...] = _dot(h, wba_ref[...])
    qkvb = _dot(h, wb_ref[...])
    cos = cos_ref[...]
    sin = sin_ref[...]
    scale = MOBA_DH ** -0.5
    for hh in range(MOBA_HEADS):
        lo = hh * MOBA_DH
        q = qkvb[:, lo:lo + MOBA_DH]
        k = qkvb[:, MOBA_W + lo:MOBA_W + lo + MOBA_DH]
        qb_ref[:, lo:lo + MOBA_DH] = (_rope_head128(q, cos, sin) * scale).astype(BF16)
        kb_ref[:, lo:lo + MOBA_DH] = _rope_head128(k, cos, sin).astype(BF16)
        v = qkvb[:, 2 * MOBA_W + lo:2 * MOBA_W + lo + MOBA_DH]
        vb_ref[hh, 0] = jnp.transpose(v).astype(BF16)


def _hy_proj(x, nw, sc, sh, wa, wz, wba, wb, cos, sin):
    s, d = x.shape
    tm = KEY_BLOCK
    row = lambda n: pl.BlockSpec((tm, n), lambda i: (i, 0))
    vec = _full((1, d))
    return pl.pallas_call(
        _hy_proj_kernel,
        grid=(s // tm,),
        in_specs=[row(d), vec, vec, vec, _full(wa.shape), _full(wz.shape), _full(wba.shape),
                  _full(wb.shape), row(LANES), row(LANES)],
        out_specs=[row(GDN_CONV_CH), row(GDN_V), row(LANES), row(MOBA_W), row(MOBA_W),
                   pl.BlockSpec((MOBA_HEADS, 1, MOBA_DH, tm), lambda i: (0, i, 0, 0))],
        out_shape=[jax.ShapeDtypeStruct((s, GDN_CONV_CH), F32),
                   jax.ShapeDtypeStruct((s, GDN_V), F32),
                   jax.ShapeDtypeStruct((s, LANES), F32),
                   jax.ShapeDtypeStruct((s, MOBA_W), BF16),
                   jax.ShapeDtypeStruct((s, MOBA_W), BF16),
                   jax.ShapeDtypeStruct((MOBA_HEADS, s // tm, MOBA_DH, tm), BF16)],
        compiler_params=_params("parallel"),
        name="hybrid_in_proj",
    )(x, nw, sc, sh, wa, wz, wba, wb, cos, sin)


def _unit_lower_inverse(l_strict, eye):
    neg = -l_strict
    inv = eye + neg
    power = neg
    p = 2
    while p < GDN_CHUNK:
        power = _dot(power, power, HIGHEST)
        inv = _dot(inv, eye + power, HIGHEST)
        p *= 2
    return inv


def _gdn_kernel(qkv_ref, ba_ref, z_ref, conv_ref, alog_ref, dtb_ref, onorm_ref, o_ref, ext_ref, state_ref):
    tb = GDN_ROWS
    cs = GDN_CHUNK

    @pl.when(pl.program_id(0) == 0)
    def _():
        ext_ref[0:8, :] = jnp.zeros((8, GDN_CONV_CH), F32)
        state_ref[...] = jnp.zeros_like(state_ref)

    ext_ref[8:8 + tb, :] = qkv_ref[...]
    cw = conv_ref[...]
    y = (ext_ref[5:5 + tb, :] * cw[0:1] + ext_ref[6:6 + tb, :] * cw[1:2]
         + ext_ref[7:7 + tb, :] * cw[2:3] + ext_ref[8:8 + tb, :] * cw[3:4])
    ext_ref[0:8, :] = ext_ref[tb:tb + 8, :]
    y = y * _sigmoid(y)

    ba = ba_ref[...]
    beta_all = _sigmoid(ba)
    t = ba + dtb_ref[...]
    softplus = jnp.maximum(t, 0.0) + jnp.log1p(jnp.exp(-jnp.abs(t)))
    g_all = -jnp.exp(alog_ref[...]) * softplus

    row = lax.broadcasted_iota(jnp.int32, (tb, tb), 0)
    col = lax.broadcasted_iota(jnp.int32, (tb, tb), 1)
    same = (row // cs) == (col // cs)
    causal = same & (col <= row)
    strict = same & (col < row)
    eye = (row == col).astype(F32)
    g_cum = _dot(causal.astype(F32), g_all, HIGHEST)
    g_tot = _dot(same.astype(F32), g_all, HIGHEST)
    g_cum_t = jnp.transpose(g_cum)
    z = z_ref[...]
    onorm = onorm_ref[...]

    for h in range(GDN_HEADS):
        lo = h * GDN_DK
        q = y[:, lo:lo + GDN_DK]
        k = y[:, GDN_QK + lo:GDN_QK + lo + GDN_DK]
        v = y[:, 2 * GDN_QK + lo:2 * GDN_QK + lo + GDN_DK]
        q = q * lax.rsqrt(jnp.sum(q * q, axis=-1, keepdims=True) + EPS) * (GDN_DK ** -0.5)
        k = k * lax.rsqrt(jnp.sum(k * k, axis=-1, keepdims=True) + EPS)
        beta = beta_all[:, h:h + 1]
        gcol = g_cum[:, GDN_HEADS + h:GDN_HEADS + h + 1]
        grow = g_cum_t[GDN_HEADS + h:GDN_HEADS + h + 1, :]
        gtot = g_tot[:, GDN_HEADS + h:GDN_HEADS + h + 1]
        decay = jnp.exp(jnp.where(causal, gcol - grow, NEG_INF))
        kb = k * beta
        k16 = k.astype(BF16)
        l_strict = jnp.where(strict, _dot_nt(kb.astype(BF16), k16) * decay, 0.0)
        t_inv = _unit_lower_inverse(l_strict, eye).astype(BF16)
        eg = jnp.exp(gcol)
        uw = _dot(t_inv, jnp.concatenate([v * beta, kb * eg], axis=1).astype(BF16))
        u = uw[:, :GDN_DK]
        w = uw[:, GDN_DK:].astype(BF16)
        qk = (_dot_nt(q.astype(BF16), k16) * decay).astype(BF16)
        qdec = (q * eg).astype(BF16)
        kdec = k * jnp.exp(gtot - gcol)

        state = state_ref[h]
        v_new = []
        o_state = []
        for c in range(tb // cs):
            r0 = c * cs
            s16 = state.astype(BF16)
            vn = u[r0:r0 + cs] - _dot(w[r0:r0 + cs], s16)
            o_state.append(_dot(qdec[r0:r0 + cs], s16))
            kd_t = jnp.transpose(kdec[r0:r0 + cs]).astype(BF16)
            state = state * jnp.exp(gtot[r0:r0 + 1]) + _dot(kd_t, vn.astype(BF16))
            v_new.append(vn)
        state_ref[h] = state
        o = jnp.concatenate(o_state, axis=0) + _dot(qk, jnp.concatenate(v_new, axis=0).astype(BF16))

        o = o * lax.rsqrt(jnp.mean(o * o, axis=-1, keepdims=True) + EPS) * onorm
        zh = z[:, lo:lo + GDN_DK]
        o_ref[:, lo:lo + GDN_DK] = (o * (zh * _sigmoid(zh))).astype(BF16)


def _gdn(qkva, ba, z, conv_w, alog, dtb, onorm):
    s = qkva.shape[0]
    tb = GDN_ROWS
    row = lambda n: pl.BlockSpec((tb, n), lambda i: (i, 0))
    return pl.pallas_call(
        _gdn_kernel,
        grid=(s // tb,),
        in_specs=[row(GDN_CONV_CH), row(LANES), row(GDN_V), _full(conv_w.shape),
                  _full((1, LANES)), _full((1, LANES)), _full((1, LANES))],
        out_specs=row(GDN_V),
        out_shape=jax.ShapeDtypeStruct((s, GDN_V), BF16),
        scratch_shapes=[pltpu.VMEM((tb + 8, GDN_CONV_CH), F32),
                        pltpu.VMEM((GDN_HEADS, GDN_DK, 128), F32)],
        compiler_params=_params("arbitrary"),
        name="gated_delta_rule",
    )(qkva, ba, z, conv_w, alog, dtb, onorm)


def _softmax_first(scores, v_ts, m_ref, l_ref, acc_ref):
    m = scores[0].max(axis=0, keepdims=True)
    for s in scores[1:]:
        m = jnp.maximum(m, s.max(axis=0, keepdims=True))
    ps = [jnp.exp(s - m) for s in scores]
    m_ref[...] = m
    l_ref[...] = sum(p.sum(axis=0, keepdims=True) for p in ps)
    acc_ref[...] = sum(_dot(v_t, p.astype(BF16)) for v_t, p in zip(v_ts, ps))


def _softmax_next(scores, v_ts, m_ref, l_ref, acc_ref):
    m_old = m_ref[...]
    m_new = m_old
    for s in scores:
        m_new = jnp.maximum(m_new, s.max(axis=0, keepdims=True))
    alpha = jnp.exp(m_old - m_new)
    ps = [jnp.exp(s - m_new) for s in scores]
    l_ref[...] = alpha * l_ref[...] + sum(p.sum(axis=0, keepdims=True) for p in ps)
    acc_ref[...] = alpha * acc_ref[...] + sum(_dot(v_t, p.astype(BF16)) for v_t, p in zip(v_ts, ps))
    m_ref[...] = m_new


def _key_rows(j):
    return pl.ds(pl.multiple_of(j * KEY_BLOCK, KEY_BLOCK), KEY_BLOCK)


def _moba_kernel(q_ref, k_ref, v_ref, o_ref, kmean_ref, sel_ref, m_ref, l_ref, acc_ref, *, n_blocks):
    i = pl.program_id(1)
    bs = MOBA_BLOCK

    @pl.when(i == 0)
    def _():
        kmean_ref[...] = jnp.zeros_like(kmean_ref)

        def mean_body(n, carry):
            blk = k_ref[pl.ds(pl.multiple_of(n * bs, bs), bs), :].astype(F32)
            kmean_ref[pl.ds(n, 1), :] = jnp.mean(blk, axis=0, keepdims=True)
            return carry

        lax.fori_loop(0, n_blocks, mean_body, 0)

    q = q_ref[...]
    gate = _dot_nt(kmean_ref[...].astype(BF16), q)
    blk_id = lax.broadcasted_iota(jnp.int32, gate.shape, 0)
    gate = jnp.where(blk_id < i, gate, NEG_INF)
    sel = jnp.zeros(gate.shape, F32)
    for _ in range(MOBA_TOPK):
        gmax = jnp.max(gate, axis=0, keepdims=True)
        first = jnp.min(jnp.where(gate == gmax, blk_id, LANES), axis=0, keepdims=True)
        hit = blk_id == first
        sel = jnp.where(hit & (gmax > NEG_INF), 1.0, sel)
        gate = jnp.where(hit, NEG_INF, gate)
    sel_ref[...] = sel

    s = _dot_nt(k_ref[_key_rows(i), :], q)
    key = lax.broadcasted_iota(jnp.int32, s.shape, 0)
    qry = lax.broadcasted_iota(jnp.int32, s.shape, 1)
    s = jnp.where(key <= qry, s, NEG_INF)
    _softmax_first([s], [v_ref[0, i]], m_ref, l_ref, acc_ref)

    def past_body(jj, carry):
        scores = []
        for j in (2 * jj, 2 * jj + 1):
            s = _dot_nt(k_ref[_key_rows(j), :], q)
            scores.append(jnp.where(sel_ref[pl.ds(j, 1), :] > 0.0, s, NEG_INF))
        _softmax_next(scores, [v_ref[0, 2 * jj], v_ref[0, 2 * jj + 1]], m_ref, l_ref, acc_ref)
        return carry

    lax.fori_loop(0, (i + 1) // 2, past_body, 0)
    o_ref[...] = jnp.transpose(acc_ref[...] / l_ref[...]).astype(o_ref.dtype)


def _moba(q, k, v_t):
    s = q.shape[0]
    bs = MOBA_BLOCK
    n_blocks = s // bs
    assert n_blocks <= LANES and bs == KEY_BLOCK
    kv = pl.BlockSpec((s, MOBA_DH), lambda h, i: (0, h))
    qo = pl.BlockSpec((bs, MOBA_DH), lambda h, i: (i, h))
    return pl.pallas_call(
        functools.partial(_moba_kernel, n_blocks=n_blocks),
        grid=(MOBA_HEADS, n_blocks),
        in_specs=[qo, kv, pl.BlockSpec((1, n_blocks, MOBA_DH, bs), lambda h, i: (h, 0, 0, 0))],
        out_specs=qo,
        out_shape=jax.ShapeDtypeStruct((s, MOBA_W), BF16),
        scratch_shapes=[pltpu.VMEM((LANES, MOBA_DH), F32),
                        pltpu.VMEM((LANES, bs), F32),
                        pltpu.VMEM((1, bs), F32),
                        pltpu.VMEM((1, bs), F32),
                        pltpu.VMEM((MOBA_DH, bs), F32)],
        compiler_params=_params("parallel", "arbitrary"),
        name="moba_attention",
    )(q, k, v_t)


def _outproj_kernel(*refs, n_in):
    o_refs = refs[:n_in]
    w_refs = refs[n_in:2 * n_in]
    x_ref, g_ref, out_ref = refs[2 * n_in:]
    y = _dot(o_refs[0][...], w_refs[0][...])
    for o_ref, w_ref in zip(o_refs[1:], w_refs[1:]):
        y = y + _dot(o_ref[...], w_ref[...])
    out_ref[...] = x_ref[...] + g_ref[...] * y


def _outproj_residual(os_, ws, x, gate):
    s, d = x.shape
    tm = min(s, 512)
    row = lambda n: pl.BlockSpec((tm, n), lambda i: (i, 0))
    return pl.pallas_call(
        functools.partial(_outproj_kernel, n_in=len(os_)),
        grid=(s // tm,),
        in_specs=[row(o.shape[1]) for o in os_] + [_full(w.shape) for w in ws] + [row(d), _full((1, d))],
        out_specs=row(d),
        out_shape=jax.ShapeDtypeStruct((s, d), F32),
        compiler_params=_params("parallel"),
        name="out_proj_residual",
    )(*os_, *ws, x, gate)


def _mla_proj_kernel(x_ref, nw_ref, sc_ref, sh_ref, wcq_ref, wckv_ref, wkr_ref, qn_ref, kvn_ref,
                     wqn_ref, wqr_ref, wkn_ref, wv_ref, cos_ref, sin_ref, q_ref, k_ref, v_ref):
    h = _norm_mod(x_ref[...], nw_ref[...], sc_ref[...], sh_ref[...]).astype(BF16)
    cq = _dot(h, wcq_ref[...])
    ckv = _dot(h, wckv_ref[...])
    kr = _dot(h, wkr_ref[...])
    cq = (cq * lax.rsqrt(jnp.mean(cq * cq, axis=-1, keepdims=True) + EPS) * qn_ref[...]).astype(BF16)
    ckv = (ckv * lax.rsqrt(jnp.mean(ckv * ckv, axis=-1, keepdims=True) + EPS) * kvn_ref[...]).astype(BF16)
    cos = cos_ref[...]
    sin = sin_ref[...]
    scale = MLA_QK ** -0.5
    q_nope = _dot(cq, wqn_ref[...]) * scale
    q_rope = _dot(cq, wqr_ref[...])
    k_nope = _dot(ckv, wkn_ref[...])
    val = _dot(ckv, wv_ref[...])
    k_rope = _rope_heads64(kr, cos, sin)[:, :MLA_ROPE].astype(BF16)
    for pair in range(MLA_HEADS // 2):
        slab = _rope_heads64(q_rope[:, pair * LANES:(pair + 1) * LANES], cos, sin) * scale
        for sub in range(2):
            hh = 2 * pair + sub
            q_ref[hh, :, MLA_NOPE:] = slab[:, sub * MLA_ROPE:(sub + 1) * MLA_ROPE].astype(BF16)
    for hh in range(MLA_HEADS):
        q_ref[hh, :, :MLA_NOPE] = q_nope[:, hh * MLA_NOPE:(hh + 1) * MLA_NOPE].astype(BF16)
        k_ref[hh, :, :MLA_NOPE] = k_nope[:, hh * MLA_NOPE:(hh + 1) * MLA_NOPE].astype(BF16)
        k_ref[hh, :, MLA_NOPE:] = k_rope
        v_ref[hh, 0] = jnp.transpose(val[:, hh * MLA_V:(hh + 1) * MLA_V]).astype(BF16)


def _mla_proj(x, nw, sc, sh, wcq, wckv, wkr, qn, kvn, wqn, wqr, wkn, wv, cos, sin):
    s, d = x.shape
    tm = KEY_BLOCK
    row = lambda n: pl.BlockSpec((tm, n), lambda i: (i, 0))
    vec = _full((1, d))
    heads = lambda n: pl.BlockSpec((MLA_HEADS, tm, n), lambda i: (0, i, 0))
    return pl.pallas_call(
        _mla_proj_kernel,
        grid=(s // tm,),
        in_specs=[row(d), vec, vec, vec, _full(wcq.shape), _full(wckv.shape), _full(wkr.shape),
                  _full(qn.shape), _full(kvn.shape), _full(wqn.shape), _full(wqr.shape),
                  _full(wkn.shape), _full(wv.shape), row(LANES), row(LANES)],
        out_specs=[heads(MLA_QK), heads(MLA_QK),
                   pl.BlockSpec((MLA_HEADS, 1, MLA_V, tm), lambda i: (0, i, 0, 0))],
        out_shape=[jax.ShapeDtypeStruct((MLA_HEADS, s, MLA_QK), BF16),
                   jax.ShapeDtypeStruct((MLA_HEADS, s, MLA_QK), BF16),
                   jax.ShapeDtypeStruct((MLA_HEADS, s // tm, MLA_V, tm), BF16)],
        compiler_params=_params("parallel"),
        name="mla_in_proj",
    )(x, nw, sc, sh, wcq, wckv, wkr, qn, kvn, wqn, wqr, wkn, wv, cos, sin)


def _mla_attn_kernel(q_ref, k_ref, v_ref, o_ref, m_ref, l_ref, acc_ref):
    i = pl.program_id(1)
    per_q = MLA_Q_BLOCK // KEY_BLOCK
    q = q_ref[0]

    scores = []
    for sub in range(per_q):
        s = _dot_nt(k_ref[0, _key_rows(per_q * i + sub), :], q)
        key = lax.broadcasted_iota(jnp.int32, s.shape, 0) + sub * KEY_BLOCK
        qry = lax.broadcasted_iota(jnp.int32, s.shape, 1)
        scores.append(jnp.where(key <= qry, s, NEG_INF))
    _softmax_first(scores, [v_ref[0, per_q * i + sub] for sub in range(per_q)], m_ref, l_ref, acc_ref)

    def past_body(jj, carry):
        blocks = [per_q * jj + sub for sub in range(per_q)]
        scores = [_dot_nt(k_ref[0, _key_rows(j), :], q) for j in blocks]
        _softmax_next(scores, [v_ref[0, j] for j in blocks], m_ref, l_ref, acc_ref)
        return carry

    lax.fori_loop(0, i, past_body, 0)
    o_ref[...] = jnp.transpose(acc_ref[...] / l_ref[...]).astype(o_ref.dtype)


def _mla_attention(q, k, v_t):
    _, s, _ = q.shape
    bq = MLA_Q_BLOCK
    return pl.pallas_call(
        _mla_attn_kernel,
        grid=(MLA_HEADS, s // bq),
        in_specs=[pl.BlockSpec((1, bq, MLA_QK), lambda h, i: (h, i, 0)),
                  pl.BlockSpec((1, s, MLA_QK), lambda h, i: (h, 0, 0)),
                  pl.BlockSpec((1, s // KEY_BLOCK, MLA_V, KEY_BLOCK), lambda h, i: (h, 0, 0, 0))],
        out_specs=pl.BlockSpec((bq, MLA_V), lambda h, i: (i, h)),
        out_shape=jax.ShapeDtypeStruct((s, MLA_HEADS * MLA_V), BF16),
        scratch_shapes=[pltpu.VMEM((1, bq), F32), pltpu.VMEM((1, bq), F32), pltpu.VMEM((MLA_V, bq), F32)],
        compiler_params=_params("parallel", "arbitrary"),
        name="mla_attention",
    )(q, k, v_t)


def _peer_proj_kernel(x_ref, nw_ref, sc_ref, sh_ref, wq_ref, keys_ref, h_ref, s_ref):
    h = _norm_mod(x_ref[...], nw_ref[...], sc_ref[...], sh_ref[...]).astype(BF16)
    h_ref[...] = h
    q = _dot(h, wq_ref[...]).astype(BF16)
    for hp in range(2 * PEER_HEADS):
        s_ref[hp] = _dot_nt(keys_ref[hp], q[:, hp * PEER_DSUB:(hp + 1) * PEER_DSUB])


def _peer_proj(x, nw, sc, sh, wq, keys):
    s, d = x.shape
    tm = min(s, 256)
    vec = _full((1, d))
    return pl.pallas_call(
        _peer_proj_kernel,
        grid=(s // tm,),
        in_specs=[pl.BlockSpec((tm, d), lambda i: (i, 0)), vec, vec, vec, _full(wq.shape), _full(keys.shape)],
        out_specs=[pl.BlockSpec((tm, d), lambda i: (i, 0)),
                   pl.BlockSpec((2 * PEER_HEADS, PEER_NKEYS, tm), lambda i: (0, 0, i))],
        out_shape=[jax.ShapeDtypeStruct((s, d), BF16),
                   jax.ShapeDtypeStruct((2 * PEER_HEADS, PEER_NKEYS, s), F32)],
        compiler_params=_params("parallel"),
        name="peer_query_scores",
    )(x, nw, sc, sh, wq, keys)


def _top_sorted(x, k):
    idx = lax.broadcasted_iota(jnp.int32, x.shape, 0)
    rank = jnp.full(x.shape, float(k), F32)
    out = []
    for r in range(k):
        m = jnp.max(x, axis=0, keepdims=True)
        first = jnp.min(jnp.where(x == m, idx, x.shape[0]), axis=0, keepdims=True)
        hit = idx == first
        x = jnp.where(hit, NEG_INF, x)
        rank = jnp.where(hit, float(r), rank)
        out.append(m)
    return out, rank


_STAIRCASE = [(r1, r2) for r1 in range(PEER_TOPK) for r2 in range(PEER_TOPK) if (r1 + 1) * (r2 + 1) <= PEER_TOPK]


def _peer_topk_kernel(s_ref, a_ref, n_ref, b_ref, rank_ref):
    def head_body(h, carry):
        s1 = s_ref[2 * h]
        s2 = s_ref[2 * h + 1]
        v1, rank1 = _top_sorted(s1, PEER_TOPK)
        v2, rank2 = _top_sorted(s2, PEER_TOPK)
        cand = jnp.concatenate([v1[r1] + v2[r2] for r1, r2 in _STAIRCASE], axis=0)
        top, _ = _top_sorted(cand, PEER_TOPK)
        thr = top[PEER_TOPK - 1]
        z = jnp.ones_like(thr)
        for r in range(1, PEER_TOPK):
            z = z + jnp.exp(top[r] - top[0])
        n = jnp.zeros(s1.shape, F32)
        for r1 in range(PEER_TOPK):
            n_r1 = sum(((v1[r1] + v2[r2]) >= thr).astype(F32) for r2 in range(PEER_TOPK // (r1 + 1)))
            n = jnp.where(rank1 == float(r1), n_r1, n)
        a_ref[h] = jnp.exp(s1 - v1[0]) / z
        n_ref[h] = n
        b_ref[h] = jnp.exp(s2 - v2[0]).astype(BF16)
        rank_ref[h] = rank2.astype(BF16)
        return carry

    lax.fori_loop(0, PEER_HEADS, head_body, 0)


def _peer_topk(scores):
    _, nk, s = scores.shape
    tt = min(s, PEER_TOPK_TOK)
    big = pl.BlockSpec((PEER_HEADS, nk, tt), lambda i: (0, 0, i))
    shape = (PEER_HEADS, nk, s)
    return pl.pallas_call(
        _peer_topk_kernel,
        grid=(s // tt,),
        in_specs=[pl.BlockSpec((2 * PEER_HEADS, nk, tt), lambda i: (0, 0, i))],
        out_specs=[big] * 4,
        out_shape=[jax.ShapeDtypeStruct(shape, F32), jax.ShapeDtypeStruct(shape, F32),
                   jax.ShapeDtypeStruct(shape, BF16), jax.ShapeDtypeStruct(shape, BF16)],
        compiler_params=_params("parallel"),
        name="peer_topk_threshold",
    )(scores)


def _gelu(x):
    return 0.5 * x * (1.0 + lax.erf(x * (2.0 ** -0.5)))


def _peer_dense_kernel(h_ref, a_ref, n_ref, b_ref, rank_ref, u_ref, vt_ref, x_ref, g_ref, o_ref,
                       acc_ref, p_ref):
    j = pl.program_id(1)
    nk = PEER_NKEYS

    @pl.when(j == 0)
    def _():
        acc_ref[...] = jnp.zeros_like(acc_ref)

    act_t = _dot_nt(u_ref[...], h_ref[...])
    for blk in range(PEER_EXP // nk):
        i1 = j * (PEER_EXP // nk) + blk
        wgt = jnp.zeros((nk, PEER_TOK), BF16)
        for h in range(PEER_HEADS):
            n_row = n_ref[h, pl.ds(i1, 1), :].astype(BF16)
            a_row = a_ref[h, pl.ds(i1, 1), :].astype(BF16)
            wgt = wgt + jnp.where(rank_ref[h] < n_row, b_ref[h], 0.0) * a_row
        p_ref[blk * nk:(blk + 1) * nk, :] = wgt * _gelu(act_t[blk * nk:(blk + 1) * nk]).astype(BF16)
    acc_ref[...] += _dot(vt_ref[...], p_ref[...])

    @pl.when(j == pl.num_programs(1) - 1)
    def _():
        o_ref[...] = x_ref[...] + g_ref[...] * jnp.transpose(acc_ref[...])


def _peer_dense(h2, a, n, b, rank, u16, vt16, x, gate):
    s, d = x.shape
    n_exp = u16.shape[0]
    tt = min(s, PEER_TOK)
    assert tt == PEER_TOK
    tok3 = pl.BlockSpec((PEER_HEADS, PEER_NKEYS, tt), lambda i, j: (0, 0, i))
    return pl.pallas_call(
        _peer_dense_kernel,
        grid=(s // tt, n_exp // PEER_EXP),
        in_specs=[pl.BlockSpec((tt, d), lambda i, j: (i, 0)),
                  tok3, tok3, tok3, tok3,
                  pl.BlockSpec((PEER_EXP, d), lambda i, j: (j, 0)),
                  pl.BlockSpec((d, PEER_EXP), lambda i, j: (0, j)),
                  pl.BlockSpec((tt, d), lambda i, j: (i, 0)),
                  pl.BlockSpec((1, d), lambda i, j: (0, 0))],
        out_specs=pl.BlockSpec((tt, d), lambda i, j: (i, 0)),
        out_shape=jax.ShapeDtypeStruct((s, d), F32),
        scratch_shapes=[pltpu.VMEM((d, tt), F32), pltpu.VMEM((PEER_EXP, tt), BF16)],
        compiler_params=_params("parallel", "arbitrary"),
        name="peer_dense_experts",
    )(h2, a, n, b, rank, u16, vt16, x, gate)


def _final_norm_kernel(x_ref, w_ref, o_ref):
    x = x_ref[...]
    o_ref[...] = x * lax.rsqrt(jnp.mean(x * x, axis=-1, keepdims=True) + EPS) * w_ref[...]


def _final_norm(x, w):
    s, d = x.shape
    tm = min(s, 1024)
    return pl.pallas_call(
        _final_norm_kernel,
        grid=(s // tm,),
        in_specs=[pl.BlockSpec((tm, d), lambda i: (i, 0)), _full((1, d))],
        out_specs=pl.BlockSpec((tm, d), lambda i: (i, 0)),
        out_shape=jax.ShapeDtypeStruct((s, d), F32),
        compiler_params=_params("parallel"),
        name="final_rmsnorm",
    )(x, w)


def _lane_row(values, offset):
    return jnp.zeros((1, LANES), F32).at[0, offset:offset + values.shape[0]].set(values.astype(F32))


def _hybrid_layer(x, nw, sc, sh, gate, cos, sin, w_in, conv_w, a_log, dt_bias, o_norm, w_out):
    w16 = w_in.astype(BF16)
    c0 = GDN_CONV_CH
    c1 = c0 + GDN_V
    c2 = c1 + 2 * GDN_HEADS
    wba = jnp.pad(w16[:, c1:c2], ((0, 0), (0, LANES - 2 * GDN_HEADS)))
    qkva, z, ba, qb, kb, vb = _hy_proj(x, nw, sc, sh, w16[:, :c0], w16[:, c0:c1], wba, w16[:, c2:], cos, sin)
    o_a = _gdn(qkva, ba, z, conv_w, _lane_row(a_log, GDN_HEADS), _lane_row(dt_bias, GDN_HEADS),
               o_norm.reshape(1, -1))
    o_b = _moba(qb, kb, vb)
    wo16 = w_out.astype(BF16)
    return _outproj_residual([o_a, o_b], [wo16[:GDN_V], wo16[GDN_V:]], x, gate)


def _mla_layer(x, nw, sc, sh, gate, cos, sin, w_in, q_norm, kv_norm, w_uq, w_ukv, w_out):
    w16 = w_in.astype(BF16)
    r0 = MLA_Q_RANK
    r1 = r0 + MLA_KV_RANK
    wkr = jnp.pad(w16[:, r1:], ((0, 0), (0, LANES - MLA_ROPE)))
    wuq = w_uq.astype(BF16).reshape(MLA_Q_RANK, MLA_HEADS, MLA_QK)
    wqn = wuq[:, :, :MLA_NOPE].reshape(MLA_Q_RANK, MLA_HEADS * MLA_NOPE)
    wqr = wuq[:, :, MLA_NOPE:].reshape(MLA_Q_RANK, MLA_HEADS * MLA_ROPE)
    wukv = w_ukv.astype(BF16).reshape(MLA_KV_RANK, MLA_HEADS, MLA_NOPE + MLA_V)
    wkn = wukv[:, :, :MLA_NOPE].reshape(MLA_KV_RANK, MLA_HEADS * MLA_NOPE)
    wv = wukv[:, :, MLA_NOPE:].reshape(MLA_KV_RANK, MLA_HEADS * MLA_V)
    q, k, v = _mla_proj(x, nw, sc, sh, w16[:, :r0], w16[:, r0:r1], wkr, q_norm.reshape(1, -1),
                        kv_norm.reshape(1, -1), wqn, wqr, wkn, wv, cos, sin)
    o = _mla_attention(q, k, v)
    return _outproj_residual([o], [w_out.astype(BF16)], x, gate)


def _peer_layer(x, nw, sc, sh, gate, w_q, sub_keys, u, v):
    keys = sub_keys.astype(BF16).reshape(2 * PEER_HEADS, PEER_NKEYS, PEER_DSUB)
    h2, scores = _peer_proj(x, nw, sc, sh, w_q.astype(BF16), keys)
    a, n, b, rank = _peer_topk(scores)
    return _peer_dense(h2, a, n, b, rank, u.astype(BF16), v.astype(BF16).T, x, gate)


def kernel(x, c, positions, mod_w, mod_b, norm_mix, norm_ffn, hy_w_in, gdn_conv, gdn_a_log, gdn_dt_bias, gdn_o_norm, hy_w_out, mla_w_in, mla_q_norm, mla_kv_norm, mla_w_uq, mla_w_ukv, mla_w_out, peer_w_q, peer_sub_keys, peer_u, peer_v, final_norm):
    bsz, s, d = x.shape
    assert bsz == 1 and d == D_MODEL
    depth = mod_w.shape[0]
    xs = x.reshape(s, d)
    mod = _modulation(c.reshape(d, 1), mod_w, mod_b)
    cos_b, sin_b, cos_c, sin_c = _rope_tables(positions.reshape(s, 1))
    for layer in range(depth):
        sh1, sc1, g1, sh2, sc2, g2 = (mod[layer, :, n * d:(n + 1) * d] for n in range(6))
        nw = norm_mix[layer].reshape(1, d)
        i = layer // 2
        if layer % 2 == 0:
            xs = _hybrid_layer(xs, nw, sc1, sh1, g1, cos_b, sin_b, hy_w_in[i], gdn_conv[i], gdn_a_log[i],
                               gdn_dt_bias[i], gdn_o_norm[i], hy_w_out[i])
        else:
            xs = _mla_layer(xs, nw, sc1, sh1, g1, cos_c, sin_c, mla_w_in[i], mla_q_norm[i], mla_kv_norm[i],
                            mla_w_uq[i], mla_w_ukv[i], mla_w_out[i])
        xs = _peer_layer(xs, norm_ffn[layer].reshape(1, d), sc2, sh2, g2, peer_w_q[layer],
                         peer_sub_keys[layer], peer_u[layer], peer_v[layer])
    return _final_norm(xs, final_norm.reshape(1, d)).reshape(bsz, s, d)
```

```python
import functools
import math

import jax
import jax.numpy as jnp
from jax import lax
from jax.experimental import pallas as pl
from jax.experimental.pallas import tpu as pltpu

F32 = jnp.float32
BF16 = jnp.bfloat16
NEG_INF = float("-inf")
HIGHEST = lax.Precision.HIGHEST

D_MODEL = 1024
EPS = 1e-6
ROPE_THETA = 10000.0

GDN_HEADS = 4
GDN_DK = 128
GDN_CHUNK = 64
GDN_QK = GDN_HEADS * GDN_DK
GDN_V = GDN_HEADS * 128
GDN_CONV_CH = 2 * GDN_QK + GDN_V
GDN_ROWS = 256

MOBA_HEADS = 4
MOBA_DH = 128
MOBA_BLOCK = 256
MOBA_TOPK = 3
MOBA_W = MOBA_HEADS * MOBA_DH

MLA_HEADS = 8
MLA_Q_RANK = 512
MLA_KV_RANK = 256
MLA_NOPE = 128
MLA_ROPE = 64
MLA_V = 128
MLA_QK = MLA_NOPE + MLA_ROPE
KEY_BLOCK = 256
MLA_Q_BLOCK = 512

PEER_HEADS = 8
PEER_NKEYS = 128
PEER_TOPK = 16
PEER_DSUB = 128
PEER_TOK = 512
PEER_EXP = 1024
PEER_TOPK_TOK = 256

LANES = 128
BF16_ROWS = 16
VMEM_LIMIT = 56 * 1024 * 1024


def _params(*sem):
    return pltpu.CompilerParams(dimension_semantics=sem, vmem_limit_bytes=VMEM_LIMIT)


def _dot(a, b, precision=None):
    return jnp.dot(a, b, preferred_element_type=F32, precision=precision)


def _dot_nt(a, b):
    return lax.dot_general(a, b, (((1,), (1,)), ((), ())), preferred_element_type=F32)


def _sigmoid(x):
    return jax.nn.sigmoid(x)


def _norm_mod(x, nw, sc, sh):
    y = x * lax.rsqrt(jnp.mean(x * x, axis=-1, keepdims=True) + EPS) * nw
    return y * (1.0 + sc) + sh


def _full(shape):
    n = len(shape)
    return pl.BlockSpec(shape, lambda *_: (0,) * n)


def _mod_kernel(c_ref, w_ref, b_ref, o_ref):
    c = c_ref[...]
    cond = c * _sigmoid(c)
    o_ref[0] = jnp.sum(cond * w_ref[0], axis=0, keepdims=True) + b_ref[0]


def _modulation(c_col, mod_w, mod_b):
    depth, d, n = mod_w.shape
    tn = 1536
    return pl.pallas_call(
        _mod_kernel,
        grid=(depth, n // tn),
        in_specs=[pl.BlockSpec((d, 1), lambda l, j: (0, 0)),
                  pl.BlockSpec((1, d, tn), lambda l, j: (l, 0, j)),
                  pl.BlockSpec((1, 1, tn), lambda l, j: (l, 0, j))],
        out_specs=pl.BlockSpec((1, 1, tn), lambda l, j: (l, 0, j)),
        out_shape=jax.ShapeDtypeStruct((depth, 1, n), F32),
        compiler_params=_params("parallel", "parallel"),
        name="adaln_modulation",
    )(c_col, mod_w, mod_b.reshape(depth, 1, n))


def _rope_kernel(pos_ref, fb_ref, fc_ref, cb_ref, sb_ref, cc_ref, sc_ref):
    p = pos_ref[...].astype(F32)
    lane = lax.broadcasted_iota(jnp.int32, cb_ref.shape, 1)
    ab = p * fb_ref[...]
    sb = jnp.sin(ab)
    cb_ref[...] = jnp.cos(ab)
    sb_ref[...] = jnp.where(lane < MOBA_DH // 2, -sb, sb)
    ac = p * fc_ref[...]
    sc = jnp.sin(ac)
    cc_ref[...] = jnp.cos(ac)
    sc_ref[...] = jnp.where(lane % MLA_ROPE < MLA_ROPE // 2, -sc, sc)


def _rope_tables(pos_col):
    s = pos_col.shape[0]
    tm = min(s, 1024)
    fb = ROPE_THETA ** (-jnp.arange(0, MOBA_DH, 2, dtype=F32) / MOBA_DH)
    fc = ROPE_THETA ** (-jnp.arange(0, MLA_ROPE, 2, dtype=F32) / MLA_ROPE)
    fb = jnp.tile(fb, 2).reshape(1, LANES)
    fc = jnp.tile(fc, 4).reshape(1, LANES)
    tab = pl.BlockSpec((tm, LANES), lambda i: (i, 0))
    return pl.pallas_call(
        _rope_kernel,
        grid=(s // tm,),
        in_specs=[pl.BlockSpec((tm, 1), lambda i: (i, 0)), _full((1, LANES)), _full((1, LANES))],
        out_specs=[tab] * 4,
        out_shape=[jax.ShapeDtypeStruct((s, LANES), F32)] * 4,
        compiler_params=_params("parallel"),
        name="rope_tables",
    )(pos_col, fb, fc)


def _rope_head128(x, cos, sin):
    return x * cos + pltpu.roll(x, MOBA_DH // 2, 1) * sin


def _rope_heads64(x, cos, sin):
    lane = lax.broadcasted_iota(jnp.int32, x.shape, 1)
    first = lane % MLA_ROPE < MLA_ROPE // 2
    partner = jnp.where(first, pltpu.roll(x, LANES - MLA_ROPE // 2, 1), pltpu.roll(x, MLA_ROPE // 2, 1))
    return x * cos + partner * sin


def _hy_proj_kernel(x_ref, nw_ref, sc_ref, sh_ref, wa_ref, wz_ref, wba_ref, wb_ref, cos_ref, sin_ref,
                    qkva_ref, z_ref, ba_ref, qb_ref, kb_ref, vb_ref):
    h = _norm_mod(x_ref[...], nw_ref[...], sc_ref[...], sh_ref[...]).astype(BF16)
    qkva_ref[...] = _dot(h, wa_ref[...])
    z_ref[...] = _dot(h, wz_ref[...])
    ba_ref[...] = _dot(h, wba_ref[...])
    qkvb = _dot(h, wb_ref[...])
    cos = cos_ref[...]
    sin = sin_ref[...]
    scale = MOBA_DH ** -0.5
    for hh in range(MOBA_HEADS):
        lo = hh * MOBA_DH
        q = qkvb[:, lo:lo + MOBA_DH]
        k = qkvb[:, MOBA_W + lo:MOBA_W + lo + MOBA_DH]
        qb_ref[:, lo:lo + MOBA_DH] = (_rope_head128(q, cos, sin) * scale).astype(BF16)
        kb_ref[:, lo:lo + MOBA_DH] = _rope_head128(k, cos, sin).astype(BF16)
        v = qkvb[:, 2 * MOBA_W + lo:2 * MOBA_W + lo + MOBA_DH]
        vb_ref[hh, 0] = jnp.transpose(v).astype(BF16)


def _hy_proj(x, nw, sc, sh, wa, wz, wba, wb, cos, sin):
    s, d = x.shape
    tm = KEY_BLOCK
    row = lambda n: pl.BlockSpec((tm, n), lambda i: (i, 0))
    vec = _full((1, d))
    return pl.pallas_call(
        _hy_proj_kernel,
        grid=(s // tm,),
        in_specs=[row(d), vec, vec, vec, _full(wa.shape), _full(wz.shape), _full(wba.shape),
                  _full(wb.shape), row(LANES), row(LANES)],
        out_specs=[row(GDN_CONV_CH), row(GDN_V), row(LANES), row(MOBA_W), row(MOBA_W),
                   pl.BlockSpec((MOBA_HEADS, 1, MOBA_DH, tm), lambda i: (0, i, 0, 0))],
        out_shape=[jax.ShapeDtypeStruct((s, GDN_CONV_CH), F32),
                   jax.ShapeDtypeStruct((s, GDN_V), F32),
                   jax.ShapeDtypeStruct((s, LANES), F32),
                   jax.ShapeDtypeStruct((s, MOBA_W), BF16),
                   jax.ShapeDtypeStruct((s, MOBA_W), BF16),
                   jax.ShapeDtypeStruct((MOBA_HEADS, s // tm, MOBA_DH, tm), BF16)],
        compiler_params=_params("parallel"),
        name="hybrid_in_proj",
    )(x, nw, sc, sh, wa, wz, wba, wb, cos, sin)


def _unit_lower_inverse(l_strict, eye):
    neg = -l_strict
    inv = eye + neg
    power = neg
    p = 2
    while p < GDN_CHUNK:
        p16 = power.astype(BF16)
        power = _dot(p16, p16)
        inv = _dot(inv.astype(BF16), (eye + power).astype(BF16))
        p *= 2
    return inv


def _gdn_kernel(qkv_ref, ba_ref, z_ref, conv_ref, alog_ref, dtb_ref, onorm_ref, o_ref, ext_ref, state_ref):
    tb = GDN_ROWS
    cs = GDN_CHUNK

    @pl.when(pl.program_id(0) == 0)
    def _():
        ext_ref[0:8, :] = jnp.zeros((8, GDN_CONV_CH), F32)
        state_ref[...] = jnp.zeros_like(state_ref)

    ext_ref[8:8 + tb, :] = qkv_ref[...]
    cw = conv_ref[...]
    y = (ext_ref[5:5 + tb, :] * cw[0:1] + ext_ref[6:6 + tb, :] * cw[1:2]
         + ext_ref[7:7 + tb, :] * cw[2:3] + ext_ref[8:8 + tb, :] * cw[3:4])
    ext_ref[0:8, :] = ext_ref[tb:tb + 8, :]
    y = y * _sigmoid(y)

    ba = ba_ref[...]
    beta_all = _sigmoid(ba)
    t = ba + dtb_ref[...]
    softplus = jnp.maximum(t, 0.0) + jnp.log1p(jnp.exp(-jnp.abs(t)))
    g_all = -jnp.exp(alog_ref[...]) * softplus

    row = lax.broadcasted_iota(jnp.int32, (tb, tb), 0)
    col = lax.broadcasted_iota(jnp.int32, (tb, tb), 1)
    same = (row // cs) == (col // cs)
    causal = same & (col <= row)
    strict = same & (col < row)
    eye = (row == col).astype(F32)
    g_cum = _dot(causal.astype(F32), g_all, HIGHEST)
    g_tot = _dot(same.astype(F32), g_all, HIGHEST)
    g_cum_t = jnp.transpose(g_cum)
    z = z_ref[...]
    onorm = onorm_ref[...]

    for h in range(GDN_HEADS):
        lo = h * GDN_DK
        q = y[:, lo:lo + GDN_DK]
        k = y[:, GDN_QK + lo:GDN_QK + lo + GDN_DK]
        v = y[:, 2 * GDN_QK + lo:2 * GDN_QK + lo + GDN_DK]
        q = q * lax.rsqrt(jnp.sum(q * q, axis=-1, keepdims=True) + EPS) * (GDN_DK ** -0.5)
        k = k * lax.rsqrt(jnp.sum(k * k, axis=-1, keepdims=True) + EPS)
        beta = beta_all[:, h:h + 1]
        gcol = g_cum[:, GDN_HEADS + h:GDN_HEADS + h + 1]
        grow = g_cum_t[GDN_HEADS + h:GDN_HEADS + h + 1, :]
        gtot = g_tot[:, GDN_HEADS + h:GDN_HEADS + h + 1]
        decay = jnp.exp(jnp.where(causal, gcol - grow, NEG_INF))
        kb = k * beta
        k16 = k.astype(BF16)
        l_strict = jnp.where(strict, _dot_nt(kb.astype(BF16), k16) * decay, 0.0)
        t_inv = _unit_lower_inverse(l_strict, eye).astype(BF16)
        eg = jnp.exp(gcol)
        uw = _dot(t_inv, jnp.concatenate([v * beta, kb * eg], axis=1).astype(BF16))
        u = uw[:, :GDN_DK]
        w = uw[:, GDN_DK:].astype(BF16)
        qk = (_dot_nt(q.astype(BF16), k16) * decay).astype(BF16)
        qdec = (q * eg).astype(BF16)
        kdec = k * jnp.exp(gtot - gcol)

        state = state_ref[h]
        v_new = []
        o_state = []
        for c in range(tb // cs):
            r0 = c * cs
            s16 = state.astype(BF16)
            vn = u[r0:r0 + cs] - _dot(w[r0:r0 + cs], s16)
            o_state.append(_dot(qdec[r0:r0 + cs], s16))
            kd_t = jnp.transpose(kdec[r0:r0 + cs]).astype(BF16)
            state = state * jnp.exp(gtot[r0:r0 + 1]) + _dot(kd_t, vn.astype(BF16))
            v_new.append(vn)
        state_ref[h] = state
        o = jnp.concatenate(o_state, axis=0) + _dot(qk, jnp.concatenate(v_new, axis=0).astype(BF16))

        o = o * lax.rsqrt(jnp.mean(o * o, axis=-1, keepdims=True) + EPS) * onorm
        zh = z[:, lo:lo + GDN_DK]
        o_ref[:, lo:lo + GDN_DK] = (o * (zh * _sigmoid(zh))).astype(BF16)


def _gdn(qkva, ba, z, conv_w, alog, dtb, onorm):
    s = qkva.shape[0]
    tb = GDN_ROWS
    row = lambda n: pl.BlockSpec((tb, n), lambda i: (i, 0))
    return pl.pallas_call(
        _gdn_kernel,
        grid=(s // tb,),
        in_specs=[row(GDN_CONV_CH), row(LANES), row(GDN_V), _full(conv_w.shape),
                  _full((1, LANES)), _full((1, LANES)), _full((1, LANES))],
        out_specs=row(GDN_V),
        out_shape=jax.ShapeDtypeStruct((s, GDN_V), BF16),
        scratch_shapes=[pltpu.VMEM((tb + 8, GDN_CONV_CH), F32),
                        pltpu.VMEM((GDN_HEADS, GDN_DK, 128), F32)],
        compiler_params=_params("arbitrary"),
        name="gated_delta_rule",
    )(qkva, ba, z, conv_w, alog, dtb, onorm)


def _key_rows(j):
    return pl.ds(pl.multiple_of(j * KEY_BLOCK, KEY_BLOCK), KEY_BLOCK)


def _scores_into(s_ref, slot, k_at, q, blocks):
    for sub, j in enumerate(blocks):
        s_ref[slot, sub * KEY_BLOCK:(sub + 1) * KEY_BLOCK, :] = _dot_nt(k_at(j), q)


def _values(p_ref, slot, v_at, blocks):
    return sum(_dot(v_at(j), p_ref[slot, sub * KEY_BLOCK:(sub + 1) * KEY_BLOCK, :]) for sub, j in enumerate(blocks))


def _flash_steps(n_steps, first_blocks, step_blocks, k_at, v_at, q, mask_first, mask_step,
                 s_ref, p_ref, acc_ref):
    _scores_into(s_ref, 0, k_at, q, first_blocks)
    s = mask_first(s_ref[0])
    m = s.max(axis=0, keepdims=True)
    p = jnp.exp(s - m)
    l = p.sum(axis=0, keepdims=True)
    p_ref[0] = p.astype(BF16)
    acc_ref[...] = jnp.zeros_like(acc_ref)
    _scores_into(s_ref, 1, k_at, q, step_blocks(1))

    def step(t, carry, cur):
        m, l, alpha_prev = carry
        _scores_into(s_ref, 1 - cur, k_at, q, step_blocks(jnp.minimum(t + 1, jnp.maximum(n_steps, 1))))
        s = mask_step(s_ref[cur], t)
        m_new = jnp.maximum(m, s.max(axis=0, keepdims=True))
        alpha = jnp.exp(m - m_new)
        p = jnp.exp(s - m_new)
        l = alpha * l + p.sum(axis=0, keepdims=True)
        prev = [jnp.where(t == 1, a, b) for a, b in zip(first_blocks, step_blocks(jnp.maximum(t - 1, 1)))]
        acc_ref[...] = alpha_prev * acc_ref[...] + _values(p_ref, 1 - cur, v_at, prev)
        p_ref[cur] = p.astype(BF16)
        return m_new, l, alpha

    def body(t, carry):
        return lax.cond(t % 2 == 0, lambda c: step(t, c, 0), lambda c: step(t, c, 1), carry)

    m, l, alpha = lax.fori_loop(1, n_steps + 1, body, (m, l, jnp.ones_like(m)))
    last = [jnp.where(n_steps == 0, a, b) for a, b in zip(first_blocks, step_blocks(jnp.maximum(n_steps, 1)))]
    acc = alpha * acc_ref[...] + _values(p_ref, n_steps % 2, v_at, last)
    return acc / l


def _moba_kernel(q_ref, k_ref, v_ref, o_ref, kmean_ref, sel_ref, s_ref, p_ref, acc_ref, *, n_blocks):
    i = pl.program_id(1)
    bs = MOBA_BLOCK

    @pl.when(i == 0)
    def _():
        kmean_ref[...] = jnp.zeros_like(kmean_ref)

        def mean_body(n, carry):
            blk = k_ref[pl.ds(pl.multiple_of(n * bs, bs), bs), :].astype(F32)
            kmean_ref[pl.ds(n, 1), :] = jnp.mean(blk, axis=0, keepdims=True)
            return carry

        lax.fori_loop(0, n_blocks, mean_body, 0)

    q = q_ref[...]
    gate = _dot_nt(kmean_ref[...].astype(BF16), q)
    blk_id = lax.broadcasted_iota(jnp.int32, gate.shape, 0)
    gate = jnp.where(blk_id < i, gate, NEG_INF)
    sel = jnp.zeros(gate.shape, F32)
    for _ in range(MOBA_TOPK):
        gmax = jnp.max(gate, axis=0, keepdims=True)
        first = jnp.min(jnp.where(gate == gmax, blk_id, LANES), axis=0, keepdims=True)
        hit = blk_id == first
        sel = jnp.where(hit & (gmax > NEG_INF), 1.0, sel)
        gate = jnp.where(hit, NEG_INF, gate)
    sel_ref[...] = sel

    def own_mask(s):
        key = lax.broadcasted_iota(jnp.int32, s.shape, 0)
        qry = lax.broadcasted_iota(jnp.int32, s.shape, 1)
        return jnp.where(key <= qry, s, NEG_INF)

    def sel_mask(s, t):
        rows = [jnp.broadcast_to(sel_ref[pl.ds(2 * (t - 1) + sub, 1), :], (bs, bs)) for sub in range(2)]
        return jnp.where(jnp.concatenate(rows, axis=0) > 0.0, s, NEG_INF)

    out = _flash_steps(
        n_steps=(i + 1) // 2, first_blocks=[i, i], step_blocks=lambda t: [2 * (t - 1), 2 * (t - 1) + 1],
        k_at=lambda j: k_ref[_key_rows(j), :], v_at=lambda j: v_ref[0, j], q=q,
        mask_first=own_mask, mask_step=sel_mask, s_ref=s_ref, p_ref=p_ref, acc_ref=acc_ref)
    o_ref[...] = jnp.transpose(out).astype(o_ref.dtype)


def _moba(q, k, v_t):
    s = q.shape[0]
    bs = MOBA_BLOCK
    n_blocks = s // bs
    assert n_blocks <= LANES and bs == KEY_BLOCK
    kv = pl.BlockSpec((s, MOBA_DH), lambda h, i: (0, h))
    qo = pl.BlockSpec((bs, MOBA_DH), lambda h, i: (i, h))
    return pl.pallas_call(
        functools.partial(_moba_kernel, n_blocks=n_blocks),
        grid=(MOBA_HEADS, n_blocks),
        in_specs=[qo, kv, pl.BlockSpec((1, n_blocks, MOBA_DH, bs), lambda h, i: (h, 0, 0, 0))],
        out_specs=qo,
        out_shape=jax.ShapeDtypeStruct((s, MOBA_W), BF16),
        scratch_shapes=[pltpu.VMEM((LANES, MOBA_DH), F32),
                        pltpu.VMEM((LANES, bs), F32),
                        pltpu.VMEM((2, 2 * KEY_BLOCK, bs), F32),
                        pltpu.VMEM((2, 2 * KEY_BLOCK, bs), BF16),
                        pltpu.VMEM((MOBA_DH, bs), F32)],
        compiler_params=_params("parallel", "arbitrary"),
        name="moba_attention",
    )(q, k, v_t)


def _outproj_kernel(*refs, n_in):
    o_refs = refs[:n_in]
    w_refs = refs[n_in:2 * n_in]
    x_ref, g_ref, out_ref = refs[2 * n_in:]
    y = _dot(o_refs[0][...], w_refs[0][...])
    for o_ref, w_ref in zip(o_refs[1:], w_refs[1:]):
        y = y + _dot(o_ref[...], w_ref[...])
    out_ref[...] = x_ref[...] + g_ref[...] * y


def _outproj_residual(os_, ws, x, gate):
    s, d = x.shape
    tm = min(s, 512)
    row = lambda n: pl.BlockSpec((tm, n), lambda i: (i, 0))
    return pl.pallas_call(
        functools.partial(_outproj_kernel, n_in=len(os_)),
        grid=(s // tm,),
        in_specs=[row(o.shape[1]) for o in os_] + [_full(w.shape) for w in ws] + [row(d), _full((1, d))],
        out_specs=row(d),
        out_shape=jax.ShapeDtypeStruct((s, d), F32),
        compiler_params=_params("parallel"),
        name="out_proj_residual",
    )(*os_, *ws, x, gate)


def _mla_proj_kernel(x_ref, nw_ref, sc_ref, sh_ref, wcq_ref, wckv_ref, wkr_ref, qn_ref, kvn_ref,
                     wqn_ref, wqr_ref, wkn_ref, wv_ref, cos_ref, sin_ref, q_ref, k_ref, v_ref):
    h = _norm_mod(x_ref[...], nw_ref[...], sc_ref[...], sh_ref[...]).astype(BF16)
    cq = _dot(h, wcq_ref[...])
    ckv = _dot(h, wckv_ref[...])
    kr = _dot(h, wkr_ref[...])
    cq = (cq * lax.rsqrt(jnp.mean(cq * cq, axis=-1, keepdims=True) + EPS) * qn_ref[...]).astype(BF16)
    ckv = (ckv * lax.rsqrt(jnp.mean(ckv * ckv, axis=-1, keepdims=True) + EPS) * kvn_ref[...]).astype(BF16)
    cos = cos_ref[...]
    sin = sin_ref[...]
    scale = MLA_QK ** -0.5
    q_nope = _dot(cq, wqn_ref[...]) * scale
    q_rope = _dot(cq, wqr_ref[...])
    k_nope = _dot(ckv, wkn_ref[...])
    val = _dot(ckv, wv_ref[...])
    k_rope = _rope_heads64(kr, cos, sin)[:, :MLA_ROPE].astype(BF16)
    for pair in range(MLA_HEADS // 2):
        slab = _rope_heads64(q_rope[:, pair * LANES:(pair + 1) * LANES], cos, sin) * scale
        for sub in range(2):
            hh = 2 * pair + sub
            q_ref[hh, :, MLA_NOPE:] = slab[:, sub * MLA_ROPE:(sub + 1) * MLA_ROPE].astype(BF16)
    for hh in range(MLA_HEADS):
        q_ref[hh, :, :MLA_NOPE] = q_nope[:, hh * MLA_NOPE:(hh + 1) * MLA_NOPE].astype(BF16)
        k_ref[hh, :, :MLA_NOPE] = k_nope[:, hh * MLA_NOPE:(hh + 1) * MLA_NOPE].astype(BF16)
        k_ref[hh, :, MLA_NOPE:] = k_rope
        v_ref[hh, 0] = jnp.transpose(val[:, hh * MLA_V:(hh + 1) * MLA_V]).astype(BF16)


def _mla_proj(x, nw, sc, sh, wcq, wckv, wkr, qn, kvn, wqn, wqr, wkn, wv, cos, sin):
    s, d = x.shape
    tm = KEY_BLOCK
    row = lambda n: pl.BlockSpec((tm, n), lambda i: (i, 0))
    vec = _full((1, d))
    heads = lambda n: pl.BlockSpec((MLA_HEADS, tm, n), lambda i: (0, i, 0))
    return pl.pallas_call(
        _mla_proj_kernel,
        grid=(s // tm,),
        in_specs=[row(d), vec, vec, vec, _full(wcq.shape), _full(wckv.shape), _full(wkr.shape),
                  _full(qn.shape), _full(kvn.shape), _full(wqn.shape), _full(wqr.shape),
                  _full(wkn.shape), _full(wv.shape), row(LANES), row(LANES)],
        out_specs=[heads(MLA_QK), heads(MLA_QK),
                   pl.BlockSpec((MLA_HEADS, 1, MLA_V, tm), lambda i: (0, i, 0, 0))],
        out_shape=[jax.ShapeDtypeStruct((MLA_HEADS, s, MLA_QK), BF16),
                   jax.ShapeDtypeStruct((MLA_HEADS, s, MLA_QK), BF16),
                   jax.ShapeDtypeStruct((MLA_HEADS, s // tm, MLA_V, tm), BF16)],
        compiler_params=_params("parallel"),
        name="mla_in_proj",
    )(x, nw, sc, sh, wcq, wckv, wkr, qn, kvn, wqn, wqr, wkn, wv, cos, sin)


def _mla_attn_kernel(q_ref, k_ref, v_ref, o_ref, s_ref, p_ref, acc_ref):
    i = pl.program_id(1)

    def causal(s):
        key = lax.broadcasted_iota(jnp.int32, s.shape, 0)
        qry = lax.broadcasted_iota(jnp.int32, s.shape, 1)
        return jnp.where(key <= qry, s, NEG_INF)

    out = _flash_steps(
        n_steps=i, first_blocks=[2 * i, 2 * i + 1], step_blocks=lambda t: [2 * (t - 1), 2 * (t - 1) + 1],
        k_at=lambda j: k_ref[0, _key_rows(j), :], v_at=lambda j: v_ref[0, j], q=q_ref[0],
        mask_first=causal, mask_step=lambda s, t: s, s_ref=s_ref, p_ref=p_ref, acc_ref=acc_ref)
    o_ref[...] = jnp.transpose(out).astype(o_ref.dtype)


def _mla_attention(q, k, v_t):
    _, s, _ = q.shape
    bq = MLA_Q_BLOCK
    return pl.pallas_call(
        _mla_attn_kernel,
        grid=(MLA_HEADS, s // bq),
        in_specs=[pl.BlockSpec((1, bq, MLA_QK), lambda h, i: (h, i, 0)),
                  pl.BlockSpec((1, s, MLA_QK), lambda h, i: (h, 0, 0)),
                  pl.BlockSpec((1, s // KEY_BLOCK, MLA_V, KEY_BLOCK), lambda h, i: (h, 0, 0, 0))],
        out_specs=pl.BlockSpec((bq, MLA_V), lambda h, i: (i, h)),
        out_shape=jax.ShapeDtypeStruct((s, MLA_HEADS * MLA_V), BF16),
        scratch_shapes=[pltpu.VMEM((2, 2 * KEY_BLOCK, bq), F32), pltpu.VMEM((2, 2 * KEY_BLOCK, bq), BF16),
                        pltpu.VMEM((MLA_V, bq), F32)],
        compiler_params=_params("parallel", "arbitrary"),
        name="mla_attention",
    )(q, k, v_t)


def _peer_proj_kernel(x_ref, nw_ref, sc_ref, sh_ref, wq_ref, keys_ref, h_ref, s_ref):
    h = _norm_mod(x_ref[...], nw_ref[...], sc_ref[...], sh_ref[...]).astype(BF16)
    h_ref[...] = h
    q = _dot(h, wq_ref[...]).astype(BF16)
    for hp in range(2 * PEER_HEADS):
        s_ref[hp] = _dot_nt(keys_ref[hp], q[:, hp * PEER_DSUB:(hp + 1) * PEER_DSUB])


def _peer_proj(x, nw, sc, sh, wq, keys):
    s, d = x.shape
    tm = min(s, 256)
    vec = _full((1, d))
    return pl.pallas_call(
        _peer_proj_kernel,
        grid=(s // tm,),
        in_specs=[pl.BlockSpec((tm, d), lambda i: (i, 0)), vec, vec, vec, _full(wq.shape), _full(keys.shape)],
        out_specs=[pl.BlockSpec((tm, d), lambda i: (i, 0)),
                   pl.BlockSpec((2 * PEER_HEADS, PEER_NKEYS, tm), lambda i: (0, 0, i))],
        out_shape=[jax.ShapeDtypeStruct((s, d), BF16),
                   jax.ShapeDtypeStruct((2 * PEER_HEADS, PEER_NKEYS, s), F32)],
        compiler_params=_params("parallel"),
        name="peer_query_scores",
    )(x, nw, sc, sh, wq, keys)


def _compare_exchange(vs, i, l):
    vs[i], vs[l] = jnp.maximum(vs[i], vs[l]), jnp.minimum(vs[i], vs[l])


def _sort_desc(vs):
    vs = list(vs)
    n = len(vs)
    k = 2
    while k <= n:
        j = k // 2
        while j >= 1:
            for i in range(n):
                l = i ^ j
                if l > i:
                    if i & k == 0:
                        _compare_exchange(vs, i, l)
                    else:
                        _compare_exchange(vs, l, i)
            j //= 2
        k *= 2
    return vs


def _merge_top(a, b):
    n = len(a)
    vs = [jnp.maximum(a[i], b[n - 1 - i]) for i in range(n)]
    j = n // 2
    while j >= 1:
        for i in range(n):
            if i ^ j > i:
                _compare_exchange(vs, i, i ^ j)
        j //= 2
    return vs


def _top16_of_keys(groups):
    vs = _sort_desc(groups)
    for shift in (4, 2, 1):
        vs = _merge_top(vs, [pltpu.roll(v, shift, 0) for v in vs])
    return vs


def _peer_topk_kernel(s_ref, a_ref, n_ref, b_ref, rank_ref):
    sub = 8
    groups = PEER_NKEYS // sub

    def head_body(h, carry):
        g1 = [s_ref[2 * h, g * sub:(g + 1) * sub, :] for g in range(groups)]
        g2 = [s_ref[2 * h + 1, g * sub:(g + 1) * sub, :] for g in range(groups)]
        v1 = _top16_of_keys(g1)
        v2 = _top16_of_keys(g2)

        ninf = jnp.full_like(v1[0], NEG_INF)
        pad = lambda vs: vs + [ninf] * (PEER_TOPK - len(vs))
        row = lambda r1: [v1[r1] + v2[r2] for r2 in range(PEER_TOPK // (r1 + 1))]
        top = _merge_top(
            _merge_top(row(0), _merge_top(pad(row(1)), pad(row(2)))),
            _merge_top(_sort_desc(pad(row(3) + row(4) + row(5) + row(6) + row(7))),
                       pad([row(r1)[0] for r1 in range(8, PEER_TOPK)])))
        thr = top[PEER_TOPK - 1]
        z = jnp.ones_like(thr)
        for r in range(1, PEER_TOPK):
            z = z + jnp.exp(top[r] - top[0])
        half_inv_z = 0.5 / z

        b_all, rank_all = [], []
        for g in range(groups):
            rows = slice(g * sub, (g + 1) * sub)
            n = sum(jnp.where(g1[g] + v2[r2] >= thr, 1.0, 0.0) for r2 in range(PEER_TOPK))
            a_ref[h, rows, :] = jnp.exp(g1[g] - v1[0]) * half_inv_z
            n_ref[h, rows, :] = n
            b_all.append(jnp.exp(g2[g] - v2[0]))
            rank_all.append(sum(jnp.where(v2[r] > g2[g], 1.0, 0.0) for r in range(PEER_TOPK)))
        for pair in range(groups // 2):
            b_ref[h, pair] = jnp.concatenate(b_all[2 * pair:2 * pair + 2], axis=0).astype(BF16)
            rank_ref[h, pair] = jnp.concatenate(rank_all[2 * pair:2 * pair + 2], axis=0).astype(BF16)
        return carry

    lax.fori_loop(0, PEER_HEADS, head_body, 0)


def _peer_topk(scores):
    _, nk, s = scores.shape
    tt = min(s, PEER_TOPK_TOK)
    big = pl.BlockSpec((PEER_HEADS, nk, tt), lambda i: (0, 0, i))
    tiled = pl.BlockSpec((PEER_HEADS, nk // BF16_ROWS, BF16_ROWS, tt), lambda i: (0, 0, 0, i))
    shape = (PEER_HEADS, nk, s)
    shape16 = (PEER_HEADS, nk // BF16_ROWS, BF16_ROWS, s)
    return pl.pallas_call(
        _peer_topk_kernel,
        grid=(s // tt,),
        in_specs=[pl.BlockSpec((2 * PEER_HEADS, nk, tt), lambda i: (0, 0, i))],
        out_specs=[big, big, tiled, tiled],
        out_shape=[jax.ShapeDtypeStruct(shape, F32), jax.ShapeDtypeStruct(shape, F32),
                   jax.ShapeDtypeStruct(shape16, BF16), jax.ShapeDtypeStruct(shape16, BF16)],
        compiler_params=_params("parallel"),
        name="peer_topk_threshold",
    )(scores)


def _peer_dense_kernel(h_ref, a_ref, n_ref, b_ref, rank_ref, u_ref, vt_ref, x_ref, g_ref, o_ref,
                       acc_ref, p_ref):
    j = pl.program_id(1)
    nk = PEER_NKEYS

    @pl.when(j == 0)
    def _():
        acc_ref[...] = jnp.zeros_like(acc_ref)

    act_t = _dot_nt(u_ref[...], h_ref[...])
    for blk in range(PEER_EXP // nk):
        i1 = j * (PEER_EXP // nk) + blk
        wgt = jnp.zeros((nk // BF16_ROWS, BF16_ROWS, PEER_TOK), BF16)
        for h in range(PEER_HEADS):
            row = lambda ref: jnp.broadcast_to(ref[h, pl.ds(i1, 1), :], (BF16_ROWS, PEER_TOK)).astype(BF16)[None]
            chosen = rank_ref[h] < row(n_ref)
            wgt = wgt + jnp.where(chosen, b_ref[h], 0.0) * row(a_ref)
        act = act_t[blk * nk:(blk + 1) * nk]
        gelu2 = act * (1.0 + lax.erf(act * (2.0 ** -0.5)))
        p_ref[blk * nk:(blk + 1) * nk, :] = wgt.reshape(nk, PEER_TOK) * gelu2.astype(BF16)
    acc_ref[...] += _dot(vt_ref[...], p_ref[...])

    @pl.when(j == pl.num_programs(1) - 1)
    def _():
        o_ref[...] = x_ref[...] + g_ref[...] * jnp.transpose(acc_ref[...])


def _peer_dense(h2, a, n, b, rank, u16, vt16, x, gate):
    s, d = x.shape
    n_exp = u16.shape[0]
    tt = min(s, PEER_TOK)
    assert tt == PEER_TOK
    tok3 = pl.BlockSpec((PEER_HEADS, PEER_NKEYS, tt), lambda i, j: (0, 0, i))
    tok4 = pl.BlockSpec((PEER_HEADS, PEER_NKEYS // BF16_ROWS, BF16_ROWS, tt), lambda i, j: (0, 0, 0, i))
    return pl.pallas_call(
        _peer_dense_kernel,
        grid=(s // tt, n_exp // PEER_EXP),
        in_specs=[pl.BlockSpec((tt, d), lambda i, j: (i, 0)),
                  tok3, tok3, tok4, tok4,
                  pl.BlockSpec((PEER_EXP, d), lambda i, j: (j, 0)),
                  pl.BlockSpec((d, PEER_EXP), lambda i, j: (0, j)),
                  pl.BlockSpec((tt, d), lambda i, j: (i, 0)),
                  pl.BlockSpec((1, d), lambda i, j: (0, 0))],
        out_specs=pl.BlockSpec((tt, d), lambda i, j: (i, 0)),
        out_shape=jax.ShapeDtypeStruct((s, d), F32),
        scratch_shapes=[pltpu.VMEM((d, tt), F32), pltpu.VMEM((PEER_EXP, tt), BF16)],
        compiler_params=_params("parallel", "arbitrary"),
        name="peer_dense_experts",
    )(h2, a, n, b, rank, u16, vt16, x, gate)


def _final_norm_kernel(x_ref, w_ref, o_ref):
    x = x_ref[...]
    o_ref[...] = x * lax.rsqrt(jnp.mean(x * x, axis=-1, keepdims=True) + EPS) * w_ref[...]


def _final_norm(x, w):
    s, d = x.shape
    tm = min(s, 1024)
    return pl.pallas_call(
        _final_norm_kernel,
        grid=(s // tm,),
        in_specs=[pl.BlockSpec((tm, d), lambda i: (i, 0)), _full((1, d))],
        out_specs=pl.BlockSpec((tm, d), lambda i: (i, 0)),
        out_shape=jax.ShapeDtypeStruct((s, d), F32),
        compiler_params=_params("parallel"),
        name="final_rmsnorm",
    )(x, w)


def _lane_row(values, offset):
    return jnp.zeros((1, LANES), F32).at[0, offset:offset + values.shape[0]].set(values.astype(F32))


def _hybrid_layer(x, nw, sc, sh, gate, cos, sin, w_in, conv_w, a_log, dt_bias, o_norm, w_out):
    w16 = w_in.astype(BF16)
    c0 = GDN_CONV_CH
    c1 = c0 + GDN_V
    c2 = c1 + 2 * GDN_HEADS
    wba = jnp.pad(w16[:, c1:c2], ((0, 0), (0, LANES - 2 * GDN_HEADS)))
    qkva, z, ba, qb, kb, vb = _hy_proj(x, nw, sc, sh, w16[:, :c0], w16[:, c0:c1], wba, w16[:, c2:], cos, sin)
    o_a = _gdn(qkva, ba, z, conv_w, _lane_row(a_log, GDN_HEADS), _lane_row(dt_bias, GDN_HEADS),
               o_norm.reshape(1, -1))
    o_b = _moba(qb, kb, vb)
    wo16 = w_out.astype(BF16)
    return _outproj_residual([o_a, o_b], [wo16[:GDN_V], wo16[GDN_V:]], x, gate)


def _mla_layer(x, nw, sc, sh, gate, cos, sin, w_in, q_norm, kv_norm, w_uq, w_ukv, w_out):
    w16 = w_in.astype(BF16)
    r0 = MLA_Q_RANK
    r1 = r0 + MLA_KV_RANK
    wkr = jnp.pad(w16[:, r1:], ((0, 0), (0, LANES - MLA_ROPE)))
    wuq = w_uq.astype(BF16).reshape(MLA_Q_RANK, MLA_HEADS, MLA_QK)
    wqn = wuq[:, :, :MLA_NOPE].reshape(MLA_Q_RANK, MLA_HEADS * MLA_NOPE)
    wqr = wuq[:, :, MLA_NOPE:].reshape(MLA_Q_RANK, MLA_HEADS * MLA_ROPE)
    wukv = w_ukv.astype(BF16).reshape(MLA_KV_RANK, MLA_HEADS, MLA_NOPE + MLA_V)
    wkn = wukv[:, :, :MLA_NOPE].reshape(MLA_KV_RANK, MLA_HEADS * MLA_NOPE)
    wv = wukv[:, :, MLA_NOPE:].reshape(MLA_KV_RANK, MLA_HEADS * MLA_V)
    q, k, v = _mla_proj(x, nw, sc, sh, w16[:, :r0], w16[:, r0:r1], wkr, q_norm.reshape(1, -1),
                        kv_norm.reshape(1, -1), wqn, wqr, wkn, wv, cos, sin)
    o = _mla_attention(q, k, v)
    return _outproj_residual([o], [w_out.astype(BF16)], x, gate)


def _peer_layer(x, nw, sc, sh, gate, w_q, sub_keys, u, v):
    keys = sub_keys.astype(BF16).reshape(2 * PEER_HEADS, PEER_NKEYS, PEER_DSUB)
    h2, scores = _peer_proj(x, nw, sc, sh, w_q.astype(BF16), keys)
    a, n, b, rank = _peer_topk(scores)
    return _peer_dense(h2, a, n, b, rank, u.astype(BF16), v.astype(BF16).T, x, gate)


def kernel(x, c, positions, mod_w, mod_b, norm_mix, norm_ffn, hy_w_in, gdn_conv, gdn_a_log, gdn_dt_bias, gdn_o_norm, hy_w_out, mla_w_in, mla_q_norm, mla_kv_norm, mla_w_uq, mla_w_ukv, mla_w_out, peer_w_q, peer_sub_keys, peer_u, peer_v, final_norm):
    bsz, s, d = x.shape
    assert bsz == 1 and d == D_MODEL
    depth = mod_w.shape[0]
    xs = x.reshape(s, d)
    mod = _modulation(c.reshape(d, 1), mod_w, mod_b)
    cos_b, sin_b, cos_c, sin_c = _rope_tables(positions.reshape(s, 1))
    for layer in range(depth):
        sh1, sc1, g1, sh2, sc2, g2 = (mod[layer, :, n * d:(n + 1) * d] for n in range(6))
        nw = norm_mix[layer].reshape(1, d)
        i = layer // 2
        if layer % 2 == 0:
            xs = _hybrid_layer(xs, nw, sc1, sh1, g1, cos_b, sin_b, hy_w_in[i], gdn_conv[i], gdn_a_log[i],
                               gdn_dt_bias[i], gdn_o_norm[i], hy_w_out[i])
        else:
            xs = _mla_layer(xs, nw, sc1, sh1, g1, cos_c, sin_c, mla_w_in[i], mla_q_norm[i], mla_kv_norm[i],
                            mla_w_uq[i], mla_w_ukv[i], mla_w_out[i])
        xs = _peer_layer(xs, norm_ffn[layer].reshape(1, d), sc2, sh2, g2, peer_w_q[layer],
                         peer_sub_keys[layer], peer_u[layer], peer_v[layer])
    return _final_norm(xs, final_norm.reshape(1, d)).reshape(bsz, s, d)
```

```python
import functools
import math

import jax
import jax.numpy as jnp
from jax import lax
from jax.experimental import pallas as pl
from jax.experimental.pallas import tpu as pltpu

F32 = jnp.float32
BF16 = jnp.bfloat16
NEG_INF = float("-inf")
HIGHEST = lax.Precision.HIGHEST

D_MODEL = 1024
EPS = 1e-6
ROPE_THETA = 10000.0

GDN_HEADS = 4
GDN_DK = 128
GDN_CHUNK = 64
GDN_QK = GDN_HEADS * GDN_DK
GDN_V = GDN_HEADS * 128
GDN_CONV_CH = 2 * GDN_QK + GDN_V
GDN_ROWS = 256

MOBA_HEADS = 4
MOBA_DH = 128
MOBA_BLOCK = 256
MOBA_TOPK = 3
MOBA_W = MOBA_HEADS * MOBA_DH

MLA_HEADS = 8
MLA_Q_RANK = 512
MLA_KV_RANK = 256
MLA_NOPE = 128
MLA_ROPE = 64
MLA_V = 128
MLA_QK = MLA_NOPE + MLA_ROPE
KEY_BLOCK = 256
MLA_Q_BLOCK = 512

PEER_HEADS = 8
PEER_NKEYS = 128
PEER_TOPK = 16
PEER_DSUB = 128
PEER_TOK = 512
PEER_EXP = 2048
PEER_TOPK_TOK = 256

LANES = 128
BF16_ROWS = 16
SUM_ROWS = BF16_ROWS
VMEM_LIMIT = 56 * 1024 * 1024


def _params(*sem):
    return pltpu.CompilerParams(dimension_semantics=sem, vmem_limit_bytes=VMEM_LIMIT)


def _dot(a, b, precision=None):
    return jnp.dot(a, b, preferred_element_type=F32, precision=precision)


def _dot_nt(a, b):
    return lax.dot_general(a, b, (((1,), (1,)), ((), ())), preferred_element_type=F32)


def _sigmoid(x):
    return jax.nn.sigmoid(x)


def _norm_mod(x, nw, sc, sh):
    y = x * lax.rsqrt(jnp.mean(x * x, axis=-1, keepdims=True) + EPS) * nw
    return y * (1.0 + sc) + sh


def _full(shape):
    n = len(shape)
    return pl.BlockSpec(shape, lambda *_: (0,) * n)


def _mod_kernel(c_ref, w_ref, b_ref, o_ref):
    c = c_ref[...]
    cond = c * _sigmoid(c)
    o_ref[0] = jnp.sum(cond * w_ref[0], axis=0, keepdims=True) + b_ref[0]


def _modulation(c_col, mod_w, mod_b):
    depth, d, n = mod_w.shape
    tn = 1536
    return pl.pallas_call(
        _mod_kernel,
        grid=(depth, n // tn),
        in_specs=[pl.BlockSpec((d, 1), lambda l, j: (0, 0)),
                  pl.BlockSpec((1, d, tn), lambda l, j: (l, 0, j)),
                  pl.BlockSpec((1, 1, tn), lambda l, j: (l, 0, j))],
        out_specs=pl.BlockSpec((1, 1, tn), lambda l, j: (l, 0, j)),
        out_shape=jax.ShapeDtypeStruct((depth, 1, n), F32),
        compiler_params=_params("parallel", "parallel"),
        name="adaln_modulation",
    )(c_col, mod_w, mod_b.reshape(depth, 1, n))


def _rope_kernel(pos_ref, fb_ref, fc_ref, cb_ref, sb_ref, cc_ref, sc_ref):
    p = pos_ref[...].astype(F32)
    lane = lax.broadcasted_iota(jnp.int32, cb_ref.shape, 1)
    ab = p * fb_ref[...]
    sb = jnp.sin(ab)
    cb_ref[...] = jnp.cos(ab)
    sb_ref[...] = jnp.where(lane < MOBA_DH // 2, -sb, sb)
    ac = p * fc_ref[...]
    sc = jnp.sin(ac)
    cc_ref[...] = jnp.cos(ac)
    sc_ref[...] = jnp.where(lane % MLA_ROPE < MLA_ROPE // 2, -sc, sc)


def _rope_tables(pos_col):
    s = pos_col.shape[0]
    tm = min(s, 1024)
    fb = ROPE_THETA ** (-jnp.arange(0, MOBA_DH, 2, dtype=F32) / MOBA_DH)
    fc = ROPE_THETA ** (-jnp.arange(0, MLA_ROPE, 2, dtype=F32) / MLA_ROPE)
    fb = jnp.tile(fb, 2).reshape(1, LANES)
    fc = jnp.tile(fc, 4).reshape(1, LANES)
    tab = pl.BlockSpec((tm, LANES), lambda i: (i, 0))
    return pl.pallas_call(
        _rope_kernel,
        grid=(s // tm,),
        in_specs=[pl.BlockSpec((tm, 1), lambda i: (i, 0)), _full((1, LANES)), _full((1, LANES))],
        out_specs=[tab] * 4,
        out_shape=[jax.ShapeDtypeStruct((s, LANES), F32)] * 4,
        compiler_params=_params("parallel"),
        name="rope_tables",
    )(pos_col, fb, fc)


def _rope_head128(x, cos, sin):
    return x * cos + pltpu.roll(x, MOBA_DH // 2, 1) * sin


def _rope_heads64(x, cos, sin):
    lane = lax.broadcasted_iota(jnp.int32, x.shape, 1)
    first = lane % MLA_ROPE < MLA_ROPE // 2
    partner = jnp.where(first, pltpu.roll(x, LANES - MLA_ROPE // 2, 1), pltpu.roll(x, MLA_ROPE // 2, 1))
    return x * cos + partner * sin


def _hy_proj_kernel(x_ref, nw_ref, sc_ref, sh_ref, wa_ref, wz_ref, wba_ref, wb_ref, cos_ref, sin_ref,
                    qkva_ref, z_ref, ba_ref, qb_ref, kb_ref, vb_ref):
    h = _norm_mod(x_ref[...], nw_ref[...], sc_ref[...], sh_ref[...]).astype(BF16)
    qkva_ref[...] = _dot(h, wa_ref[...])
    z_ref[...] = _dot(h, wz_ref[...])
    ba_ref[...] = _dot(h, wba_ref[...])
    qkvb = _dot(h, wb_ref[...])
    cos = cos_ref[...]
    sin = sin_ref[...]
    scale = MOBA_DH ** -0.5
    for hh in range(MOBA_HEADS):
        lo = hh * MOBA_DH
        q = qkvb[:, lo:lo + MOBA_DH]
        k = qkvb[:, MOBA_W + lo:MOBA_W + lo + MOBA_DH]
        qb_ref[:, lo:lo + MOBA_DH] = (_rope_head128(q, cos, sin) * scale).astype(BF16)
        kb_ref[:, lo:lo + MOBA_DH] = _rope_head128(k, cos, sin).astype(BF16)
        v = qkvb[:, 2 * MOBA_W + lo:2 * MOBA_W + lo + MOBA_DH]
        vb_ref[hh, 0] = _with_sum_rows(jnp.transpose(v))


def _hy_proj(x, nw, sc, sh, wa, wz, wba, wb, cos, sin):
    s, d = x.shape
    tm = KEY_BLOCK
    row = lambda n: pl.BlockSpec((tm, n), lambda i: (i, 0))
    vec = _full((1, d))
    return pl.pallas_call(
        _hy_proj_kernel,
        grid=(s // tm,),
        in_specs=[row(d), vec, vec, vec, _full(wa.shape), _full(wz.shape), _full(wba.shape),
                  _full(wb.shape), row(LANES), row(LANES)],
        out_specs=[row(GDN_CONV_CH), row(GDN_V), row(LANES), row(MOBA_W), row(MOBA_W),
                   pl.BlockSpec((MOBA_HEADS, 1, MOBA_DH + SUM_ROWS, tm), lambda i: (0, i, 0, 0))],
        out_shape=[jax.ShapeDtypeStruct((s, GDN_CONV_CH), F32),
                   jax.ShapeDtypeStruct((s, GDN_V), F32),
                   jax.ShapeDtypeStruct((s, LANES), F32),
                   jax.ShapeDtypeStruct((s, MOBA_W), BF16),
                   jax.ShapeDtypeStruct((s, MOBA_W), BF16),
                   jax.ShapeDtypeStruct((MOBA_HEADS, s // tm, MOBA_DH + SUM_ROWS, tm), BF16)],
        compiler_params=_params("parallel"),
        name="hybrid_in_proj",
    )(x, nw, sc, sh, wa, wz, wba, wb, cos, sin)


def _unit_lower_inverse(l_strict, eye):
    neg = -l_strict
    inv = eye + neg
    power = neg
    p = 2
    while p < GDN_CHUNK:
        p16 = power.astype(BF16)
        power = _dot(p16, p16)
        inv = _dot(inv.astype(BF16), (eye + power).astype(BF16))
        p *= 2
    return inv


def _gdn_kernel(qkv_ref, ba_ref, z_ref, conv_ref, alog_ref, dtb_ref, onorm_ref, o_ref, ext_ref, state_ref):
    tb = GDN_ROWS
    cs = GDN_CHUNK

    @pl.when(pl.program_id(0) == 0)
    def _():
        ext_ref[0:8, :] = jnp.zeros((8, GDN_CONV_CH), F32)
        state_ref[...] = jnp.zeros_like(state_ref)

    ext_ref[8:8 + tb, :] = qkv_ref[...]
    cw = conv_ref[...]
    y = (ext_ref[5:5 + tb, :] * cw[0:1] + ext_ref[6:6 + tb, :] * cw[1:2]
         + ext_ref[7:7 + tb, :] * cw[2:3] + ext_ref[8:8 + tb, :] * cw[3:4])
    ext_ref[0:8, :] = ext_ref[tb:tb + 8, :]
    y = y * _sigmoid(y)

    ba = ba_ref[...]
    beta_all = _sigmoid(ba)
    t = ba + dtb_ref[...]
    softplus = jnp.maximum(t, 0.0) + jnp.log1p(jnp.exp(-jnp.abs(t)))
    g_all = -jnp.exp(alog_ref[...]) * softplus

    row = lax.broadcasted_iota(jnp.int32, (tb, tb), 0)
    col = lax.broadcasted_iota(jnp.int32, (tb, tb), 1)
    same = (row // cs) == (col // cs)
    causal = same & (col <= row)
    strict = same & (col < row)
    eye = (row == col).astype(F32)
    g_cum = _dot(causal.astype(F32), g_all, HIGHEST)
    g_tot = _dot(same.astype(F32), g_all, HIGHEST)
    g_cum_t = jnp.transpose(g_cum)
    z = z_ref[...]
    onorm = onorm_ref[...]

    for h in range(GDN_HEADS):
        lo = h * GDN_DK
        q = y[:, lo:lo + GDN_DK]
        k = y[:, GDN_QK + lo:GDN_QK + lo + GDN_DK]
        v = y[:, 2 * GDN_QK + lo:2 * GDN_QK + lo + GDN_DK]
        q = q * lax.rsqrt(jnp.sum(q * q, axis=-1, keepdims=True) + EPS) * (GDN_DK ** -0.5)
        k = k * lax.rsqrt(jnp.sum(k * k, axis=-1, keepdims=True) + EPS)
        beta = beta_all[:, h:h + 1]
        gcol = g_cum[:, GDN_HEADS + h:GDN_HEADS + h + 1]
        grow = g_cum_t[GDN_HEADS + h:GDN_HEADS + h + 1, :]
        gtot = g_tot[:, GDN_HEADS + h:GDN_HEADS + h + 1]
        decay = jnp.exp(jnp.where(causal, gcol - grow, NEG_INF))
        kb = k * beta
        k16 = k.astype(BF16)
        l_strict = jnp.where(strict, _dot_nt(kb.astype(BF16), k16) * decay, 0.0)
        t_inv = _unit_lower_inverse(l_strict, eye).astype(BF16)
        eg = jnp.exp(gcol)
        uw = _dot(t_inv, jnp.concatenate([v * beta, kb * eg], axis=1).astype(BF16))
        u = uw[:, :GDN_DK]
        w = uw[:, GDN_DK:].astype(BF16)
        qk = (_dot_nt(q.astype(BF16), k16) * decay).astype(BF16)
        qdec = (q * eg).astype(BF16)
        kdec = k * jnp.exp(gtot - gcol)

        state = state_ref[h]
        v_new = []
        o_state = []
        for c in range(tb // cs):
            r0 = c * cs
            s16 = state.astype(BF16)
            vn = u[r0:r0 + cs] - _dot(w[r0:r0 + cs], s16)
            o_state.append(_dot(qdec[r0:r0 + cs], s16))
            kd_t = jnp.transpose(kdec[r0:r0 + cs]).astype(BF16)
            state = state * jnp.exp(gtot[r0:r0 + 1]) + _dot(kd_t, vn.astype(BF16))
            v_new.append(vn)
        state_ref[h] = state
        o = jnp.concatenate(o_state, axis=0) + _dot(qk, jnp.concatenate(v_new, axis=0).astype(BF16))

        o = o * lax.rsqrt(jnp.mean(o * o, axis=-1, keepdims=True) + EPS) * onorm
        zh = z[:, lo:lo + GDN_DK]
        o_ref[:, lo:lo + GDN_DK] = (o * (zh * _sigmoid(zh))).astype(BF16)


def _gdn(qkva, ba, z, conv_w, alog, dtb, onorm):
    s = qkva.shape[0]
    tb = GDN_ROWS
    row = lambda n: pl.BlockSpec((tb, n), lambda i: (i, 0))
    return pl.pallas_call(
        _gdn_kernel,
        grid=(s // tb,),
        in_specs=[row(GDN_CONV_CH), row(LANES), row(GDN_V), _full(conv_w.shape),
                  _full((1, LANES)), _full((1, LANES)), _full((1, LANES))],
        out_specs=row(GDN_V),
        out_shape=jax.ShapeDtypeStruct((s, GDN_V), BF16),
        scratch_shapes=[pltpu.VMEM((tb + 8, GDN_CONV_CH), F32),
                        pltpu.VMEM((GDN_HEADS, GDN_DK, 128), F32)],
        compiler_params=_params("arbitrary"),
        name="gated_delta_rule",
    )(qkva, ba, z, conv_w, alog, dtb, onorm)


def _key_rows(j):
    return pl.ds(pl.multiple_of(j * KEY_BLOCK, KEY_BLOCK), KEY_BLOCK)


def _scores_into(s_ref, mx_ref, slot, k_at, q, blocks, mask):
    top = None
    for sub, j in enumerate(blocks):
        s = mask(_dot_nt(k_at(j), q), sub)
        s_ref[slot, sub * KEY_BLOCK:(sub + 1) * KEY_BLOCK, :] = s
        smax = s.max(axis=0, keepdims=True)
        top = smax if top is None else jnp.maximum(top, smax)
    mx_ref[slot] = top


def _values(p_ref, slot, v_at, blocks):
    return sum(_dot(v_at(j), p_ref[slot, sub * KEY_BLOCK:(sub + 1) * KEY_BLOCK, :]) for sub, j in enumerate(blocks))


def _flash_steps(n_steps, first_blocks, step_blocks, k_at, v_at, q, mask_first, mask_step,
                 s_ref, mx_ref, p_ref, acc_ref):
    _scores_into(s_ref, mx_ref, 0, k_at, q, first_blocks, mask_first)
    m = mx_ref[0]
    p_ref[0] = jnp.exp(s_ref[0] - m).astype(BF16)
    acc_ref[...] = jnp.zeros_like(acc_ref)
    _scores_into(s_ref, mx_ref, 1, k_at, q, step_blocks(1), lambda s, sub: mask_step(s, sub, 1))

    def step(t, carry, cur):
        m, alpha_prev = carry
        nxt = jnp.minimum(t + 1, jnp.maximum(n_steps, 1))
        _scores_into(s_ref, mx_ref, 1 - cur, k_at, q, step_blocks(nxt), lambda s, sub: mask_step(s, sub, nxt))
        m_new = jnp.maximum(m, mx_ref[cur])
        alpha = jnp.exp(m - m_new)
        p = jnp.exp(s_ref[cur] - m_new)
        prev = [jnp.where(t == 1, a, b) for a, b in zip(first_blocks, step_blocks(jnp.maximum(t - 1, 1)))]
        acc_ref[...] = alpha_prev * acc_ref[...] + _values(p_ref, 1 - cur, v_at, prev)
        p_ref[cur] = p.astype(BF16)
        return m_new, alpha

    def body(t, carry):
        return lax.cond(t % 2 == 0, lambda c: step(t, c, 0), lambda c: step(t, c, 1), carry)

    m, alpha = lax.fori_loop(1, n_steps + 1, body, (m, jnp.ones_like(m)))
    last = [jnp.where(n_steps == 0, a, b) for a, b in zip(first_blocks, step_blocks(jnp.maximum(n_steps, 1)))]
    acc = alpha * acc_ref[...] + _values(p_ref, n_steps % 2, v_at, last)
    dh = acc.shape[0] - SUM_ROWS
    return acc[:dh] / acc[dh:dh + 1]


def _with_sum_rows(v_t):
    extra = lax.broadcasted_iota(jnp.int32, (SUM_ROWS, v_t.shape[1]), 0) == 0
    return jnp.concatenate([v_t, extra.astype(F32)], axis=0).astype(BF16)


def _moba_kernel(q_ref, k_ref, v_ref, o_ref, kmean_ref, sel_ref, s_ref, mx_ref, p_ref, acc_ref, *, n_blocks):
    i = pl.program_id(1)
    bs = MOBA_BLOCK

    @pl.when(i == 0)
    def _():
        kmean_ref[...] = jnp.zeros_like(kmean_ref)

        def mean_body(n, carry):
            blk = k_ref[pl.ds(pl.multiple_of(n * bs, bs), bs), :].astype(F32)
            kmean_ref[pl.ds(n, 1), :] = jnp.mean(blk, axis=0, keepdims=True)
            return carry

        lax.fori_loop(0, n_blocks, mean_body, 0)

    q = q_ref[...]
    gate = _dot_nt(kmean_ref[...].astype(BF16), q)
    blk_id = lax.broadcasted_iota(jnp.int32, gate.shape, 0)
    gate = jnp.where(blk_id < i, gate, NEG_INF)
    sel = jnp.zeros(gate.shape, F32)
    for _ in range(MOBA_TOPK):
        gmax = jnp.max(gate, axis=0, keepdims=True)
        first = jnp.min(jnp.where(gate == gmax, blk_id, LANES), axis=0, keepdims=True)
        hit = blk_id == first
        sel = jnp.where(hit & (gmax > NEG_INF), 1.0, sel)
        gate = jnp.where(hit, NEG_INF, gate)
    sel_ref[...] = sel

    def own_mask(s, sub):
        key = lax.broadcasted_iota(jnp.int32, s.shape, 0) + sub * bs
        qry = lax.broadcasted_iota(jnp.int32, s.shape, 1)
        return jnp.where(key <= qry, s, NEG_INF)

    def sel_mask(s, sub, t):
        return jnp.where(sel_ref[pl.ds(2 * (t - 1) + sub, 1), :] > 0.0, s, NEG_INF)

    out = _flash_steps(
        n_steps=(i + 1) // 2, first_blocks=[i, i], step_blocks=lambda t: [2 * (t - 1), 2 * (t - 1) + 1],
        k_at=lambda j: k_ref[_key_rows(j), :], v_at=lambda j: v_ref[0, j], q=q,
        mask_first=own_mask, mask_step=sel_mask, s_ref=s_ref, mx_ref=mx_ref, p_ref=p_ref, acc_ref=acc_ref)
    o_ref[...] = jnp.transpose(out).astype(o_ref.dtype)


def _moba(q, k, v_t):
    s = q.shape[0]
    bs = MOBA_BLOCK
    n_blocks = s // bs
    assert n_blocks <= LANES and bs == KEY_BLOCK
    kv = pl.BlockSpec((s, MOBA_DH), lambda h, i: (0, h))
    qo = pl.BlockSpec((bs, MOBA_DH), lambda h, i: (i, h))
    return pl.pallas_call(
        functools.partial(_moba_kernel, n_blocks=n_blocks),
        grid=(MOBA_HEADS, n_blocks),
        in_specs=[qo, kv, pl.BlockSpec((1, n_blocks, MOBA_DH + SUM_ROWS, bs), lambda h, i: (h, 0, 0, 0))],
        out_specs=qo,
        out_shape=jax.ShapeDtypeStruct((s, MOBA_W), BF16),
        scratch_shapes=[pltpu.VMEM((LANES, MOBA_DH), F32),
                        pltpu.VMEM((LANES, bs), F32),
                        pltpu.VMEM((2, 2 * KEY_BLOCK, bs), F32),
                        pltpu.VMEM((2, 1, bs), F32),
                        pltpu.VMEM((2, 2 * KEY_BLOCK, bs), BF16),
                        pltpu.VMEM((MOBA_DH + SUM_ROWS, bs), F32)],
        compiler_params=_params("parallel", "arbitrary"),
        name="moba_attention",
    )(q, k, v_t)


def _outproj_kernel(*refs, n_in):
    o_refs = refs[:n_in]
    w_refs = refs[n_in:2 * n_in]
    x_ref, g_ref, out_ref = refs[2 * n_in:]
    y = _dot(o_refs[0][...], w_refs[0][...])
    for o_ref, w_ref in zip(o_refs[1:], w_refs[1:]):
        y = y + _dot(o_ref[...], w_ref[...])
    out_ref[...] = x_ref[...] + g_ref[...] * y


def _outproj_residual(os_, ws, x, gate):
    s, d = x.shape
    tm = min(s, 512)
    row = lambda n: pl.BlockSpec((tm, n), lambda i: (i, 0))
    return pl.pallas_call(
        functools.partial(_outproj_kernel, n_in=len(os_)),
        grid=(s // tm,),
        in_specs=[row(o.shape[1]) for o in os_] + [_full(w.shape) for w in ws] + [row(d), _full((1, d))],
        out_specs=row(d),
        out_shape=jax.ShapeDtypeStruct((s, d), F32),
        compiler_params=_params("parallel"),
        name="out_proj_residual",
    )(*os_, *ws, x, gate)


def _mla_proj_kernel(x_ref, nw_ref, sc_ref, sh_ref, wcq_ref, wckv_ref, wkr_ref, qn_ref, kvn_ref,
                     wqn_ref, wqr_ref, wkn_ref, wv_ref, cos_ref, sin_ref, q_ref, k_ref, v_ref):
    h = _norm_mod(x_ref[...], nw_ref[...], sc_ref[...], sh_ref[...]).astype(BF16)
    cq = _dot(h, wcq_ref[...])
    ckv = _dot(h, wckv_ref[...])
    kr = _dot(h, wkr_ref[...])
    cq = (cq * lax.rsqrt(jnp.mean(cq * cq, axis=-1, keepdims=True) + EPS) * qn_ref[...]).astype(BF16)
    ckv = (ckv * lax.rsqrt(jnp.mean(ckv * ckv, axis=-1, keepdims=True) + EPS) * kvn_ref[...]).astype(BF16)
    cos = cos_ref[...]
    sin = sin_ref[...]
    scale = MLA_QK ** -0.5
    q_nope = _dot(cq, wqn_ref[...]) * scale
    q_rope = _dot(cq, wqr_ref[...])
    k_nope = _dot(ckv, wkn_ref[...])
    val = _dot(ckv, wv_ref[...])
    k_rope = _rope_heads64(kr, cos, sin)[:, :MLA_ROPE].astype(BF16)
    for pair in range(MLA_HEADS // 2):
        slab = _rope_heads64(q_rope[:, pair * LANES:(pair + 1) * LANES], cos, sin) * scale
        for sub in range(2):
            hh = 2 * pair + sub
            q_ref[hh, :, MLA_NOPE:] = slab[:, sub * MLA_ROPE:(sub + 1) * MLA_ROPE].astype(BF16)
    for hh in range(MLA_HEADS):
        q_ref[hh, :, :MLA_NOPE] = q_nope[:, hh * MLA_NOPE:(hh + 1) * MLA_NOPE].astype(BF16)
        k_ref[hh, :, :MLA_NOPE] = k_nope[:, hh * MLA_NOPE:(hh + 1) * MLA_NOPE].astype(BF16)
        k_ref[hh, :, MLA_NOPE:] = k_rope
        v_ref[hh, 0] = _with_sum_rows(jnp.transpose(val[:, hh * MLA_V:(hh + 1) * MLA_V]))


def _mla_proj(x, nw, sc, sh, wcq, wckv, wkr, qn, kvn, wqn, wqr, wkn, wv, cos, sin):
    s, d = x.shape
    tm = KEY_BLOCK
    row = lambda n: pl.BlockSpec((tm, n), lambda i: (i, 0))
    vec = _full((1, d))
    heads = lambda n: pl.BlockSpec((MLA_HEADS, tm, n), lambda i: (0, i, 0))
    return pl.pallas_call(
        _mla_proj_kernel,
        grid=(s // tm,),
        in_specs=[row(d), vec, vec, vec, _full(wcq.shape), _full(wckv.shape), _full(wkr.shape),
                  _full(qn.shape), _full(kvn.shape), _full(wqn.shape), _full(wqr.shape),
                  _full(wkn.shape), _full(wv.shape), row(LANES), row(LANES)],
        out_specs=[heads(MLA_QK), heads(MLA_QK),
                   pl.BlockSpec((MLA_HEADS, 1, MLA_V + SUM_ROWS, tm), lambda i: (0, i, 0, 0))],
        out_shape=[jax.ShapeDtypeStruct((MLA_HEADS, s, MLA_QK), BF16),
                   jax.ShapeDtypeStruct((MLA_HEADS, s, MLA_QK), BF16),
                   jax.ShapeDtypeStruct((MLA_HEADS, s // tm, MLA_V + SUM_ROWS, tm), BF16)],
        compiler_params=_params("parallel"),
        name="mla_in_proj",
    )(x, nw, sc, sh, wcq, wckv, wkr, qn, kvn, wqn, wqr, wkn, wv, cos, sin)


def _mla_attn_kernel(q_ref, k_ref, v_ref, o_ref, s_ref, mx_ref, p_ref, acc_ref):
    i = pl.program_id(1)

    def causal(s, sub):
        key = lax.broadcasted_iota(jnp.int32, s.shape, 0) + sub * KEY_BLOCK
        qry = lax.broadcasted_iota(jnp.int32, s.shape, 1)
        return jnp.where(key <= qry, s, NEG_INF)

    out = _flash_steps(
        n_steps=i, first_blocks=[2 * i, 2 * i + 1], step_blocks=lambda t: [2 * (t - 1), 2 * (t - 1) + 1],
        k_at=lambda j: k_ref[0, _key_rows(j), :], v_at=lambda j: v_ref[0, j], q=q_ref[0],
        mask_first=causal, mask_step=lambda s, sub, t: s, s_ref=s_ref, mx_ref=mx_ref, p_ref=p_ref, acc_ref=acc_ref)
    o_ref[...] = jnp.transpose(out).astype(o_ref.dtype)


def _mla_attention(q, k, v_t):
    _, s, _ = q.shape
    bq = MLA_Q_BLOCK
    return pl.pallas_call(
        _mla_attn_kernel,
        grid=(MLA_HEADS, s // bq),
        in_specs=[pl.BlockSpec((1, bq, MLA_QK), lambda h, i: (h, i, 0)),
                  pl.BlockSpec((1, s, MLA_QK), lambda h, i: (h, 0, 0)),
                  pl.BlockSpec((1, s // KEY_BLOCK, MLA_V + SUM_ROWS, KEY_BLOCK), lambda h, i: (h, 0, 0, 0))],
        out_specs=pl.BlockSpec((bq, MLA_V), lambda h, i: (i, h)),
        out_shape=jax.ShapeDtypeStruct((s, MLA_HEADS * MLA_V), BF16),
        scratch_shapes=[pltpu.VMEM((2, 2 * KEY_BLOCK, bq), F32), pltpu.VMEM((2, 1, bq), F32),
                        pltpu.VMEM((2, 2 * KEY_BLOCK, bq), BF16), pltpu.VMEM((MLA_V + SUM_ROWS, bq), F32)],
        compiler_params=_params("parallel", "arbitrary"),
        name="mla_attention",
    )(q, k, v_t)


def _peer_proj_kernel(x_ref, nw_ref, sc_ref, sh_ref, wq_ref, keys_ref, h_ref, s_ref):
    h = _norm_mod(x_ref[...], nw_ref[...], sc_ref[...], sh_ref[...]).astype(BF16)
    h_ref[...] = h
    q = _dot(h, wq_ref[...]).astype(BF16)
    for hp in range(2 * PEER_HEADS):
        s_ref[hp] = _dot_nt(keys_ref[hp], q[:, hp * PEER_DSUB:(hp + 1) * PEER_DSUB])


def _peer_proj(x, nw, sc, sh, wq, keys):
    s, d = x.shape
    tm = min(s, 256)
    vec = _full((1, d))
    return pl.pallas_call(
        _peer_proj_kernel,
        grid=(s // tm,),
        in_specs=[pl.BlockSpec((tm, d), lambda i: (i, 0)), vec, vec, vec, _full(wq.shape), _full(keys.shape)],
        out_specs=[pl.BlockSpec((tm, d), lambda i: (i, 0)),
                   pl.BlockSpec((2 * PEER_HEADS, PEER_NKEYS, tm), lambda i: (0, 0, i))],
        out_shape=[jax.ShapeDtypeStruct((s, d), BF16),
                   jax.ShapeDtypeStruct((2 * PEER_HEADS, PEER_NKEYS, s), F32)],
        compiler_params=_params("parallel"),
        name="peer_query_scores",
    )(x, nw, sc, sh, wq, keys)


def _compare_exchange(vs, i, l):
    vs[i], vs[l] = jnp.maximum(vs[i], vs[l]), jnp.minimum(vs[i], vs[l])


def _sort_desc(vs):
    vs = list(vs)
    n = len(vs)
    k = 2
    while k <= n:
        j = k // 2
        while j >= 1:
            for i in range(n):
                l = i ^ j
                if l > i:
                    if i & k == 0:
                        _compare_exchange(vs, i, l)
                    else:
                        _compare_exchange(vs, l, i)
            j //= 2
        k *= 2
    return vs


def _merge_top(a, b):
    n = len(a)
    vs = [jnp.maximum(a[i], b[n - 1 - i]) for i in range(n)]
    j = n // 2
    while j >= 1:
        for i in range(n):
            if i ^ j > i:
                _compare_exchange(vs, i, i ^ j)
        j //= 2
    return vs


def _top16_of_keys(groups):
    vs = _sort_desc(groups)
    for shift in (4, 2, 1):
        vs = _merge_top(vs, [pltpu.roll(v, shift, 0) for v in vs])
    return vs


def _peer_topk_kernel(s_ref, a_ref, n_ref, b_ref, rank_ref):
    sub = 8
    groups = PEER_NKEYS // sub

    def head_body(h, carry):
        g1 = [s_ref[2 * h, g * sub:(g + 1) * sub, :] for g in range(groups)]
        g2 = [s_ref[2 * h + 1, g * sub:(g + 1) * sub, :] for g in range(groups)]
        v1 = _top16_of_keys(g1)
        v2 = _top16_of_keys(g2)

        ninf = jnp.full_like(v1[0], NEG_INF)
        pad = lambda vs: vs + [ninf] * (PEER_TOPK - len(vs))
        row = lambda r1: [v1[r1] + v2[r2] for r2 in range(PEER_TOPK // (r1 + 1))]
        top = _merge_top(
            _merge_top(row(0), _merge_top(pad(row(1)), pad(row(2)))),
            _merge_top(_sort_desc(pad(row(3) + row(4) + row(5) + row(6) + row(7))),
                       pad([row(r1)[0] for r1 in range(8, PEER_TOPK)])))
        thr = top[PEER_TOPK - 1]
        z = jnp.ones_like(thr)
        for r in range(1, PEER_TOPK):
            z = z + jnp.exp(top[r] - top[0])
        half_inv_z = 0.5 / z

        b_all, rank_all = [], []
        for g in range(groups):
            rows = slice(g * sub, (g + 1) * sub)
            n = sum(jnp.where(g1[g] + v2[r2] >= thr, 1.0, 0.0) for r2 in range(PEER_TOPK))
            a_ref[h, rows, :] = jnp.exp(g1[g] - v1[0]) * half_inv_z
            n_ref[h, rows, :] = n
            b_all.append(jnp.exp(g2[g] - v2[0]))
            rank_all.append(sum(jnp.where(v2[r] > g2[g], 1.0, 0.0) for r in range(PEER_TOPK)))
        for pair in range(groups // 2):
            b_ref[h, pair] = jnp.concatenate(b_all[2 * pair:2 * pair + 2], axis=0).astype(BF16)
            rank_ref[h, pair] = jnp.concatenate(rank_all[2 * pair:2 * pair + 2], axis=0).astype(BF16)
        return carry

    lax.fori_loop(0, PEER_HEADS, head_body, 0)


def _peer_topk(scores):
    _, nk, s = scores.shape
    tt = min(s, PEER_TOPK_TOK)
    big = pl.BlockSpec((PEER_HEADS, nk, tt), lambda i: (0, 0, i))
    tiled = pl.BlockSpec((PEER_HEADS, nk // BF16_ROWS, BF16_ROWS, tt), lambda i: (0, 0, 0, i))
    shape = (PEER_HEADS, nk, s)
    shape16 = (PEER_HEADS, nk // BF16_ROWS, BF16_ROWS, s)
    return pl.pallas_call(
        _peer_topk_kernel,
        grid=(s // tt,),
        in_specs=[pl.BlockSpec((2 * PEER_HEADS, nk, tt), lambda i: (0, 0, i))],
        out_specs=[big, big, tiled, tiled],
        out_shape=[jax.ShapeDtypeStruct(shape, F32), jax.ShapeDtypeStruct(shape, F32),
                   jax.ShapeDtypeStruct(shape16, BF16), jax.ShapeDtypeStruct(shape16, BF16)],
        compiler_params=_params("parallel"),
        name="peer_topk_threshold",
    )(scores)


def _peer_dense_kernel(h_ref, a_ref, n_ref, b_ref, rank_ref, u_ref, vt_ref, x_ref, g_ref, o_ref,
                       acc_ref, p_ref):
    j = pl.program_id(1)
    nk = PEER_NKEYS

    @pl.when(j == 0)
    def _():
        acc_ref[...] = jnp.zeros_like(acc_ref)

    act_t = _dot_nt(u_ref[...], h_ref[...])
    for blk in range(PEER_EXP // nk):
        i1 = j * (PEER_EXP // nk) + blk
        wgt = jnp.zeros((nk // BF16_ROWS, BF16_ROWS, PEER_TOK), BF16)
        for h in range(PEER_HEADS):
            row = lambda ref: jnp.broadcast_to(ref[h, pl.ds(i1, 1), :], (BF16_ROWS, PEER_TOK)).astype(BF16)[None]
            chosen = rank_ref[h] < row(n_ref)
            wgt = wgt + jnp.where(chosen, b_ref[h], 0.0) * row(a_ref)
        act = act_t[blk * nk:(blk + 1) * nk]
        gelu2 = act * (1.0 + lax.erf(act * (2.0 ** -0.5)))
        p_ref[blk * nk:(blk + 1) * nk, :] = wgt.reshape(nk, PEER_TOK) * gelu2.astype(BF16)
    acc_ref[...] += _dot(vt_ref[...], p_ref[...])

    @pl.when(j == pl.num_programs(1) - 1)
    def _():
        o_ref[...] = x_ref[...] + g_ref[...] * jnp.transpose(acc_ref[...])


def _peer_dense(h2, a, n, b, rank, u16, vt16, x, gate):
    s, d = x.shape
    n_exp = u16.shape[0]
    tt = min(s, PEER_TOK)
    assert tt == PEER_TOK
    tok3 = pl.BlockSpec((PEER_HEADS, PEER_NKEYS, tt), lambda i, j: (0, 0, i))
    tok4 = pl.BlockSpec((PEER_HEADS, PEER_NKEYS // BF16_ROWS, BF16_ROWS, tt), lambda i, j: (0, 0, 0, i))
    return pl.pallas_call(
        _peer_dense_kernel,
        grid=(s // tt, n_exp // PEER_EXP),
        in_specs=[pl.BlockSpec((tt, d), lambda i, j: (i, 0)),
                  tok3, tok3, tok4, tok4,
                  pl.BlockSpec((PEER_EXP, d), lambda i, j: (j, 0)),
                  pl.BlockSpec((d, PEER_EXP), lambda i, j: (0, j)),
                  pl.BlockSpec((tt, d), lambda i, j: (i, 0)),
                  pl.BlockSpec((1, d), lambda i, j: (0, 0))],
        out_specs=pl.BlockSpec((tt, d), lambda i, j: (i, 0)),
        out_shape=jax.ShapeDtypeStruct((s, d), F32),
        scratch_shapes=[pltpu.VMEM((d, tt), F32), pltpu.VMEM((PEER_EXP, tt), BF16)],
        compiler_params=_params("parallel", "arbitrary"),
        name="peer_dense_experts",
    )(h2, a, n, b, rank, u16, vt16, x, gate)


def _final_norm_kernel(x_ref, w_ref, o_ref):
    x = x_ref[...]
    o_ref[...] = x * lax.rsqrt(jnp.mean(x * x, axis=-1, keepdims=True) + EPS) * w_ref[...]


def _final_norm(x, w):
    s, d = x.shape
    tm = min(s, 1024)
    return pl.pallas_call(
        _final_norm_kernel,
        grid=(s // tm,),
        in_specs=[pl.BlockSpec((tm, d), lambda i: (i, 0)), _full((1, d))],
        out_specs=pl.BlockSpec((tm, d), lambda i: (i, 0)),
        out_shape=jax.ShapeDtypeStruct((s, d), F32),
        compiler_params=_params("parallel"),
        name="final_rmsnorm",
    )(x, w)


def _lane_row(values, offset):
    return jnp.zeros((1, LANES), F32).at[0, offset:offset + values.shape[0]].set(values.astype(F32))


def _hybrid_layer(x, nw, sc, sh, gate, cos, sin, w_in, conv_w, a_log, dt_bias, o_norm, w_out):
    w16 = w_in.astype(BF16)
    c0 = GDN_CONV_CH
    c1 = c0 + GDN_V
    c2 = c1 + 2 * GDN_HEADS
    wba = jnp.pad(w16[:, c1:c2], ((0, 0), (0, LANES - 2 * GDN_HEADS)))
    qkva, z, ba, qb, kb, vb = _hy_proj(x, nw, sc, sh, w16[:, :c0], w16[:, c0:c1], wba, w16[:, c2:], cos, sin)
    o_a = _gdn(qkva, ba, z, conv_w, _lane_row(a_log, GDN_HEADS), _lane_row(dt_bias, GDN_HEADS),
               o_norm.reshape(1, -1))
    o_b = _moba(qb, kb, vb)
    wo16 = w_out.astype(BF16)
    return _outproj_residual([o_a, o_b], [wo16[:GDN_V], wo16[GDN_V:]], x, gate)


def _mla_layer(x, nw, sc, sh, gate, cos, sin, w_in, q_norm, kv_norm, w_uq, w_ukv, w_out):
    w16 = w_in.astype(BF16)
    r0 = MLA_Q_RANK
    r1 = r0 + MLA_KV_RANK
    wkr = jnp.pad(w16[:, r1:], ((0, 0), (0, LANES - MLA_ROPE)))
    wuq = w_uq.astype(BF16).reshape(MLA_Q_RANK, MLA_HEADS, MLA_QK)
    wqn = wuq[:, :, :MLA_NOPE].reshape(MLA_Q_RANK, MLA_HEADS * MLA_NOPE)
    wqr = wuq[:, :, MLA_NOPE:].reshape(MLA_Q_RANK, MLA_HEADS * MLA_ROPE)
    wukv = w_ukv.astype(BF16).reshape(MLA_KV_RANK, MLA_HEADS, MLA_NOPE + MLA_V)
    wkn = wukv[:, :, :MLA_NOPE].reshape(MLA_KV_RANK, MLA_HEADS * MLA_NOPE)
    wv = wukv[:, :, MLA_NOPE:].reshape(MLA_KV_RANK, MLA_HEADS * MLA_V)
    q, k, v = _mla_proj(x, nw, sc, sh, w16[:, :r0], w16[:, r0:r1], wkr, q_norm.reshape(1, -1),
                        kv_norm.reshape(1, -1), wqn, wqr, wkn, wv, cos, sin)
    o = _mla_attention(q, k, v)
    return _outproj_residual([o], [w_out.astype(BF16)], x, gate)


def _peer_layer(x, nw, sc, sh, gate, w_q, sub_keys, u, v):
    keys = sub_keys.astype(BF16).reshape(2 * PEER_HEADS, PEER_NKEYS, PEER_DSUB)
    h2, scores = _peer_proj(x, nw, sc, sh, w_q.astype(BF16), keys)
    a, n, b, rank = _peer_topk(scores)
    return _peer_dense(h2, a, n, b, rank, u.astype(BF16), v.astype(BF16).T, x, gate)


def kernel(x, c, positions, mod_w, mod_b, norm_mix, norm_ffn, hy_w_in, gdn_conv, gdn_a_log, gdn_dt_bias, gdn_o_norm, hy_w_out, mla_w_in, mla_q_norm, mla_kv_norm, mla_w_uq, mla_w_ukv, mla_w_out, peer_w_q, peer_sub_keys, peer_u, peer_v, final_norm):
    bsz, s, d = x.shape
    assert bsz == 1 and d == D_MODEL
    depth = mod_w.shape[0]
    xs = x.reshape(s, d)
    mod = _modulation(c.reshape(d, 1), mod_w, mod_b)
    cos_b, sin_b, cos_c, sin_c = _rope_tables(positions.reshape(s, 1))
    for layer in range(depth):
        sh1, sc1, g1, sh2, sc2, g2 = (mod[layer, :, n * d:(n + 1) * d] for n in range(6))
        nw = norm_mix[layer].reshape(1, d)
        i = layer // 2
        if layer % 2 == 0:
            xs = _hybrid_layer(xs, nw, sc1, sh1, g1, cos_b, sin_b, hy_w_in[i], gdn_conv[i], gdn_a_log[i],
                               gdn_dt_bias[i], gdn_o_norm[i], hy_w_out[i])
        else:
            xs = _mla_layer(xs, nw, sc1, sh1, g1, cos_c, sin_c, mla_w_in[i], mla_q_norm[i], mla_kv_norm[i],
                            mla_w_uq[i], mla_w_ukv[i], mla_w_out[i])
        xs = _peer_layer(xs, norm_ffn[layer].reshape(1, d), sc2, sh2, g2, peer_w_q[layer],
                         peer_sub_keys[layer], peer_u[layer], peer_v[layer])
    return _final_norm(xs, final_norm.reshape(1, d)).reshape(bsz, s, d)
```

```python
import functools
import math

import jax
import jax.numpy as jnp
from jax import lax
from jax.experimental import pallas as pl
from jax.experimental.pallas import tpu as pltpu

F32 = jnp.float32
BF16 = jnp.bfloat16
NEG_INF = float("-inf")
HIGHEST = lax.Precision.HIGHEST

D_MODEL = 1024
EPS = 1e-6
ROPE_THETA = 10000.0

GDN_HEADS = 4
GDN_DK = 128
GDN_CHUNK = 64
GDN_QK = GDN_HEADS * GDN_DK
GDN_V = GDN_HEADS * 128
GDN_CONV_CH = 2 * GDN_QK + GDN_V
GDN_ROWS = 256

MOBA_HEADS = 4
MOBA_DH = 128
MOBA_BLOCK = 256
MOBA_TOPK = 3
MOBA_W = MOBA_HEADS * MOBA_DH

MLA_HEADS = 8
MLA_Q_RANK = 512
MLA_KV_RANK = 256
MLA_NOPE = 128
MLA_ROPE = 64
MLA_V = 128
MLA_QK = MLA_NOPE + MLA_ROPE
KEY_BLOCK = 256
MLA_Q_BLOCK = 512

PEER_HEADS = 8
PEER_NKEYS = 128
PEER_TOPK = 16
PEER_DSUB = 128
PEER_TOK = 512
PEER_EXP = 2048
PEER_TOPK_TOK = 256

LANES = 128
BF16_ROWS = 16
SUM_ROWS = BF16_ROWS
VMEM_LIMIT = 56 * 1024 * 1024


def _params(*sem):
    return pltpu.CompilerParams(dimension_semantics=sem, vmem_limit_bytes=VMEM_LIMIT)


def _dot(a, b, precision=None):
    return jnp.dot(a, b, preferred_element_type=F32, precision=precision)


def _dot_nt(a, b):
    return lax.dot_general(a, b, (((1,), (1,)), ((), ())), preferred_element_type=F32)


def _sigmoid(x):
    return jax.nn.sigmoid(x)


def _norm_mod(x, nw, sc, sh):
    y = x * lax.rsqrt(jnp.mean(x * x, axis=-1, keepdims=True) + EPS) * nw
    return y * (1.0 + sc) + sh


def _full(shape):
    n = len(shape)
    return pl.BlockSpec(shape, lambda *_: (0,) * n)


def _mod_kernel(c_ref, w_ref, b_ref, o_ref):
    c = c_ref[...]
    cond = c * _sigmoid(c)
    o_ref[0] = jnp.sum(cond * w_ref[0], axis=0, keepdims=True) + b_ref[0]


def _modulation(c_col, mod_w, mod_b):
    depth, d, n = mod_w.shape
    tn = 1536
    return pl.pallas_call(
        _mod_kernel,
        grid=(depth, n // tn),
        in_specs=[pl.BlockSpec((d, 1), lambda l, j: (0, 0)),
                  pl.BlockSpec((1, d, tn), lambda l, j: (l, 0, j)),
                  pl.BlockSpec((1, 1, tn), lambda l, j: (l, 0, j))],
        out_specs=pl.BlockSpec((1, 1, tn), lambda l, j: (l, 0, j)),
        out_shape=jax.ShapeDtypeStruct((depth, 1, n), F32),
        compiler_params=_params("parallel", "parallel"),
        name="adaln_modulation",
    )(c_col, mod_w, mod_b.reshape(depth, 1, n))


def _rope_kernel(pos_ref, fb_ref, fc_ref, cb_ref, sb_ref, cc_ref, sc_ref):
    p = pos_ref[...].astype(F32)
    lane = lax.broadcasted_iota(jnp.int32, cb_ref.shape, 1)
    ab = p * fb_ref[...]
    sb = jnp.sin(ab)
    cb_ref[...] = jnp.cos(ab)
    sb_ref[...] = jnp.where(lane < MOBA_DH // 2, -sb, sb)
    ac = p * fc_ref[...]
    sc = jnp.sin(ac)
    cc_ref[...] = jnp.cos(ac)
    sc_ref[...] = jnp.where(lane % MLA_ROPE < MLA_ROPE // 2, -sc, sc)


def _rope_tables(pos_col):
    s = pos_col.shape[0]
    tm = min(s, 1024)
    fb = ROPE_THETA ** (-jnp.arange(0, MOBA_DH, 2, dtype=F32) / MOBA_DH)
    fc = ROPE_THETA ** (-jnp.arange(0, MLA_ROPE, 2, dtype=F32) / MLA_ROPE)
    fb = jnp.tile(fb, 2).reshape(1, LANES)
    fc = jnp.tile(fc, 4).reshape(1, LANES)
    tab = pl.BlockSpec((tm, LANES), lambda i: (i, 0))
    return pl.pallas_call(
        _rope_kernel,
        grid=(s // tm,),
        in_specs=[pl.BlockSpec((tm, 1), lambda i: (i, 0)), _full((1, LANES)), _full((1, LANES))],
        out_specs=[tab] * 4,
        out_shape=[jax.ShapeDtypeStruct((s, LANES), F32)] * 4,
        compiler_params=_params("parallel"),
        name="rope_tables",
    )(pos_col, fb, fc)


def _rope_head128(x, cos, sin):
    return x * cos + pltpu.roll(x, MOBA_DH // 2, 1) * sin


def _rope_heads64(x, cos, sin):
    lane = lax.broadcasted_iota(jnp.int32, x.shape, 1)
    first = lane % MLA_ROPE < MLA_ROPE // 2
    partner = jnp.where(first, pltpu.roll(x, LANES - MLA_ROPE // 2, 1), pltpu.roll(x, MLA_ROPE // 2, 1))
    return x * cos + partner * sin


def _hy_proj_kernel(x_ref, nw_ref, sc_ref, sh_ref, wa_ref, wz_ref, wba_ref, wb_ref, cos_ref, sin_ref,
                    qkva_ref, z_ref, ba_ref, qb_ref, kb_ref, vb_ref):
    h = _norm_mod(x_ref[...], nw_ref[...], sc_ref[...], sh_ref[...]).astype(BF16)
    qkva_ref[...] = _dot(h, wa_ref[...])
    z_ref[...] = _dot(h, wz_ref[...])
    ba_ref[...] = _dot(h, wba_ref[...])
    qkvb = _dot(h, wb_ref[...])
    cos = cos_ref[...]
    sin = sin_ref[...]
    scale = MOBA_DH ** -0.5
    for hh in range(MOBA_HEADS):
        lo = hh * MOBA_DH
        q = qkvb[:, lo:lo + MOBA_DH]
        k = qkvb[:, MOBA_W + lo:MOBA_W + lo + MOBA_DH]
        qb_ref[hh, 0] = jnp.transpose(_rope_head128(q, cos, sin) * scale).astype(BF16)
        kb_ref[:, lo:lo + MOBA_DH] = _rope_head128(k, cos, sin).astype(BF16)
        v = qkvb[:, 2 * MOBA_W + lo:2 * MOBA_W + lo + MOBA_DH]
        vb_ref[hh, 0] = _with_sum_rows(jnp.transpose(v))


def _hy_proj(x, nw, sc, sh, wa, wz, wba, wb, cos, sin):
    s, d = x.shape
    tm = KEY_BLOCK
    row = lambda n: pl.BlockSpec((tm, n), lambda i: (i, 0))
    vec = _full((1, d))
    return pl.pallas_call(
        _hy_proj_kernel,
        grid=(s // tm,),
        in_specs=[row(d), vec, vec, vec, _full(wa.shape), _full(wz.shape), _full(wba.shape),
                  _full(wb.shape), row(LANES), row(LANES)],
        out_specs=[row(GDN_CONV_CH), row(GDN_V), row(LANES),
                   pl.BlockSpec((MOBA_HEADS, 1, MOBA_DH, tm), lambda i: (0, i, 0, 0)), row(MOBA_W),
                   pl.BlockSpec((MOBA_HEADS, 1, MOBA_DH + SUM_ROWS, tm), lambda i: (0, i, 0, 0))],
        out_shape=[jax.ShapeDtypeStruct((s, GDN_CONV_CH), F32),
                   jax.ShapeDtypeStruct((s, GDN_V), F32),
                   jax.ShapeDtypeStruct((s, LANES), F32),
                   jax.ShapeDtypeStruct((MOBA_HEADS, s // tm, MOBA_DH, tm), BF16),
                   jax.ShapeDtypeStruct((s, MOBA_W), BF16),
                   jax.ShapeDtypeStruct((MOBA_HEADS, s // tm, MOBA_DH + SUM_ROWS, tm), BF16)],
        compiler_params=_params("parallel"),
        name="hybrid_in_proj",
    )(x, nw, sc, sh, wa, wz, wba, wb, cos, sin)


def _unit_lower_inverses(l_stricts, eye):
    powers = [-l for l in l_stricts]
    invs = [eye + p for p in powers]
    p = 2
    while p < GDN_CHUNK:
        powers = [_dot(x.astype(BF16), x.astype(BF16)) for x in powers]
        invs = [_dot(inv.astype(BF16), (eye + x).astype(BF16)) for inv, x in zip(invs, powers)]
        p *= 2
    return invs


def _gdn_kernel(qkv_ref, ba_ref, z_ref, conv_ref, alog_ref, dtb_ref, onorm_ref, o_ref, ext_ref, state_ref):
    tb = GDN_ROWS
    cs = GDN_CHUNK

    @pl.when(pl.program_id(0) == 0)
    def _():
        ext_ref[0:8, :] = jnp.zeros((8, GDN_CONV_CH), F32)
        state_ref[...] = jnp.zeros_like(state_ref)

    ext_ref[8:8 + tb, :] = qkv_ref[...]
    cw = conv_ref[...]
    y = (ext_ref[5:5 + tb, :] * cw[0:1] + ext_ref[6:6 + tb, :] * cw[1:2]
         + ext_ref[7:7 + tb, :] * cw[2:3] + ext_ref[8:8 + tb, :] * cw[3:4])
    ext_ref[0:8, :] = ext_ref[tb:tb + 8, :]
    y = y * _sigmoid(y)

    ba = ba_ref[...]
    beta_all = _sigmoid(ba)
    t = ba + dtb_ref[...]
    softplus = jnp.maximum(t, 0.0) + jnp.log1p(jnp.exp(-jnp.abs(t)))
    g_all = -jnp.exp(alog_ref[...]) * softplus

    row = lax.broadcasted_iota(jnp.int32, (tb, tb), 0)
    col = lax.broadcasted_iota(jnp.int32, (tb, tb), 1)
    same = (row // cs) == (col // cs)
    causal = same & (col <= row)
    strict = same & (col < row)
    eye = (row == col).astype(F32)
    g_cum = _dot(causal.astype(F32), g_all, HIGHEST)
    g_tot = _dot(same.astype(F32), g_all, HIGHEST)
    g_cum_t = jnp.transpose(g_cum)
    z = z_ref[...]
    onorm = onorm_ref[...]

    heads = range(GDN_HEADS)
    rhs, l_strict, gtot, qk, qdec, kdec = [], [], [], [], [], []
    for h in heads:
        lo = h * GDN_DK
        q = y[:, lo:lo + GDN_DK]
        k = y[:, GDN_QK + lo:GDN_QK + lo + GDN_DK]
        v = y[:, 2 * GDN_QK + lo:2 * GDN_QK + lo + GDN_DK]
        q = q * lax.rsqrt(jnp.sum(q * q, axis=-1, keepdims=True) + EPS) * (GDN_DK ** -0.5)
        k = k * lax.rsqrt(jnp.sum(k * k, axis=-1, keepdims=True) + EPS)
        beta = beta_all[:, h:h + 1]
        gcol = g_cum[:, GDN_HEADS + h:GDN_HEADS + h + 1]
        grow = g_cum_t[GDN_HEADS + h:GDN_HEADS + h + 1, :]
        gtot.append(g_tot[:, GDN_HEADS + h:GDN_HEADS + h + 1])
        decay = jnp.exp(jnp.where(causal, gcol - grow, NEG_INF))
        kb = k * beta
        k16 = k.astype(BF16)
        l_strict.append(jnp.where(strict, _dot_nt(kb.astype(BF16), k16) * decay, 0.0))
        eg = jnp.exp(gcol)
        rhs.append(jnp.concatenate([v * beta, kb * eg], axis=1).astype(BF16))
        qk.append((_dot_nt(q.astype(BF16), k16) * decay).astype(BF16))
        qdec.append((q * eg).astype(BF16))
        kdec.append(k * jnp.exp(gtot[h] - gcol))

    uw = [_dot(t_inv.astype(BF16), rhs[h]) for h, t_inv in enumerate(_unit_lower_inverses(l_strict, eye))]
    u = [x[:, :GDN_DK] for x in uw]
    w = [x[:, GDN_DK:].astype(BF16) for x in uw]

    state = [state_ref[h] for h in heads]
    v_new = [[] for _ in heads]
    o_state = [[] for _ in heads]
    for c in range(tb // cs):
        r0 = c * cs
        for h in heads:
            s16 = state[h].astype(BF16)
            vn = u[h][r0:r0 + cs] - _dot(w[h][r0:r0 + cs], s16)
            o_state[h].append(_dot(qdec[h][r0:r0 + cs], s16))
            kd_t = jnp.transpose(kdec[h][r0:r0 + cs]).astype(BF16)
            state[h] = state[h] * jnp.exp(gtot[h][r0:r0 + 1]) + _dot(kd_t, vn.astype(BF16))
            v_new[h].append(vn)

    for h in heads:
        lo = h * GDN_DK
        state_ref[h] = state[h]
        o = jnp.concatenate(o_state[h], axis=0) + _dot(qk[h], jnp.concatenate(v_new[h], axis=0).astype(BF16))
        o = o * lax.rsqrt(jnp.mean(o * o, axis=-1, keepdims=True) + EPS) * onorm
        zh = z[:, lo:lo + GDN_DK]
        o_ref[:, lo:lo + GDN_DK] = (o * (zh * _sigmoid(zh))).astype(BF16)


def _gdn(qkva, ba, z, conv_w, alog, dtb, onorm):
    s = qkva.shape[0]
    tb = GDN_ROWS
    row = lambda n: pl.BlockSpec((tb, n), lambda i: (i, 0))
    return pl.pallas_call(
        _gdn_kernel,
        grid=(s // tb,),
        in_specs=[row(GDN_CONV_CH), row(LANES), row(GDN_V), _full(conv_w.shape),
                  _full((1, LANES)), _full((1, LANES)), _full((1, LANES))],
        out_specs=row(GDN_V),
        out_shape=jax.ShapeDtypeStruct((s, GDN_V), BF16),
        scratch_shapes=[pltpu.VMEM((tb + 8, GDN_CONV_CH), F32),
                        pltpu.VMEM((GDN_HEADS, GDN_DK, 128), F32)],
        compiler_params=_params("arbitrary"),
        name="gated_delta_rule",
    )(qkva, ba, z, conv_w, alog, dtb, onorm)


def _scores_into(s_ref, mx_ref, slot, k_tile, q_t, mask):
    s = mask(_dot(k_tile, q_t))
    rows = s.shape[0]
    s_ref[slot, :rows, :] = s
    if rows < s_ref.shape[1]:
        s_ref[slot, rows:, :] = jnp.full((s_ref.shape[1] - rows, s.shape[1]), NEG_INF, F32)
    mx_ref[slot] = s.max(axis=0, keepdims=True)


def _values(p_ref, slot, v_at, blocks):
    v_t = jnp.concatenate([v_at(j) for j in blocks], axis=1)
    return _dot(v_t, p_ref[slot])


def _flash_steps(n_steps, first_blocks, first_keys, step_blocks, k_at, v_at, q_t, mask_first, mask_step,
                 s_ref, mx_ref, p_ref, acc_ref):
    per_step = len(first_blocks)
    _scores_into(s_ref, mx_ref, 0, k_at(first_blocks[0], first_keys), q_t, mask_first)
    m = mx_ref[0]
    p_ref[0] = jnp.exp(s_ref[0] - m).astype(BF16)
    acc_ref[...] = jnp.zeros_like(acc_ref)
    _scores_into(s_ref, mx_ref, 1, k_at(step_blocks(1)[0], per_step), q_t, lambda s: mask_step(s, 1))

    def step(t, carry, cur):
        m, alpha_prev = carry
        nxt = jnp.minimum(t + 1, jnp.maximum(n_steps, 1))
        _scores_into(s_ref, mx_ref, 1 - cur, k_at(step_blocks(nxt)[0], per_step), q_t, lambda s: mask_step(s, nxt))
        m_new = jnp.maximum(m, mx_ref[cur])
        alpha = jnp.exp(m - m_new)
        p = jnp.exp(s_ref[cur] - m_new)
        prev = [jnp.where(t == 1, a, b) for a, b in zip(first_blocks, step_blocks(jnp.maximum(t - 1, 1)))]
        acc_ref[...] = alpha_prev * acc_ref[...] + _values(p_ref, 1 - cur, v_at, prev)
        p_ref[cur] = p.astype(BF16)
        return m_new, alpha

    def body(t, carry):
        return lax.cond(t % 2 == 0, lambda c: step(t, c, 0), lambda c: step(t, c, 1), carry)

    m, alpha = lax.fori_loop(1, n_steps + 1, body, (m, jnp.ones_like(m)))
    last = [jnp.where(n_steps == 0, a, b) for a, b in zip(first_blocks, step_blocks(jnp.maximum(n_steps, 1)))]
    acc = alpha * acc_ref[...] + _values(p_ref, n_steps % 2, v_at, last)
    dh = acc.shape[0] - SUM_ROWS
    return acc[:dh] / acc[dh:dh + 1]


def _with_sum_rows(v_t):
    extra = lax.broadcasted_iota(jnp.int32, (SUM_ROWS, v_t.shape[1]), 0) == 0
    return jnp.concatenate([v_t, extra.astype(F32)], axis=0).astype(BF16)


def _moba_kernel(q_ref, k_ref, v_ref, o_ref, kmean_ref, sel_ref, s_ref, mx_ref, p_ref, acc_ref, *, n_blocks):
    i = pl.program_id(1)
    bs = MOBA_BLOCK

    @pl.when(i == 0)
    def _():
        kmean_ref[...] = jnp.zeros_like(kmean_ref)

        def mean_body(n, carry):
            blk = k_ref[pl.ds(pl.multiple_of(n * bs, bs), bs), :].astype(F32)
            kmean_ref[pl.ds(n, 1), :] = jnp.mean(blk, axis=0, keepdims=True)
            return carry

        lax.fori_loop(0, n_blocks, mean_body, 0)

    q_t = q_ref[0, 0]
    gate = _dot(kmean_ref[...].astype(BF16), q_t)
    blk_id = lax.broadcasted_iota(jnp.int32, gate.shape, 0)
    gate = jnp.where(blk_id < i, gate, NEG_INF)
    sel = jnp.zeros(gate.shape, F32)
    for _ in range(MOBA_TOPK):
        gmax = jnp.max(gate, axis=0, keepdims=True)
        first = jnp.min(jnp.where(gate == gmax, blk_id, LANES), axis=0, keepdims=True)
        hit = blk_id == first
        sel = jnp.where(hit & (gmax > NEG_INF), 1.0, sel)
        gate = jnp.where(hit, NEG_INF, gate)
    sel_ref[...] = sel

    def own_mask(s):
        key = lax.broadcasted_iota(jnp.int32, s.shape, 0)
        qry = lax.broadcasted_iota(jnp.int32, s.shape, 1)
        return jnp.where(key <= qry, s, NEG_INF)

    def sel_mask(s, t):
        halves = [jnp.where(sel_ref[pl.ds(2 * (t - 1) + sub, 1), :] > 0.0, s[sub * bs:(sub + 1) * bs], NEG_INF)
                  for sub in range(2)]
        return jnp.concatenate(halves, axis=0)

    out = _flash_steps(
        n_steps=(i + 1) // 2, first_blocks=[i, i], first_keys=1,
        step_blocks=lambda t: [2 * (t - 1), 2 * (t - 1) + 1],
        k_at=lambda j, n: k_ref[pl.ds(pl.multiple_of(j * bs, bs), n * bs), :], v_at=lambda j: v_ref[0, j],
        q_t=q_t, mask_first=own_mask, mask_step=sel_mask,
        s_ref=s_ref, mx_ref=mx_ref, p_ref=p_ref, acc_ref=acc_ref)
    o_ref[...] = jnp.transpose(out).astype(o_ref.dtype)


def _moba(q_t, k, v_t):
    s = k.shape[0]
    bs = MOBA_BLOCK
    n_blocks = s // bs
    assert n_blocks <= LANES and bs == KEY_BLOCK
    kv = pl.BlockSpec((s, MOBA_DH), lambda h, i: (0, h))
    return pl.pallas_call(
        functools.partial(_moba_kernel, n_blocks=n_blocks),
        grid=(MOBA_HEADS, n_blocks),
        in_specs=[pl.BlockSpec((1, 1, MOBA_DH, bs), lambda h, i: (h, i, 0, 0)), kv,
                  pl.BlockSpec((1, n_blocks, MOBA_DH + SUM_ROWS, bs), lambda h, i: (h, 0, 0, 0))],
        out_specs=pl.BlockSpec((bs, MOBA_DH), lambda h, i: (i, h)),
        out_shape=jax.ShapeDtypeStruct((s, MOBA_W), BF16),
        scratch_shapes=[pltpu.VMEM((LANES, MOBA_DH), F32),
                        pltpu.VMEM((LANES, bs), F32),
                        pltpu.VMEM((2, 2 * KEY_BLOCK, bs), F32),
                        pltpu.VMEM((2, 1, bs), F32),
                        pltpu.VMEM((2, 2 * KEY_BLOCK, bs), BF16),
                        pltpu.VMEM((MOBA_DH + SUM_ROWS, bs), F32)],
        compiler_params=_params("parallel", "arbitrary"),
        name="moba_attention",
    )(q_t, k, v_t)


def _outproj_kernel(*refs, n_in):
    o_refs = refs[:n_in]
    w_refs = refs[n_in:2 * n_in]
    x_ref, g_ref, out_ref = refs[2 * n_in:]
    y = _dot(o_refs[0][...], w_refs[0][...])
    for o_ref, w_ref in zip(o_refs[1:], w_refs[1:]):
        y = y + _dot(o_ref[...], w_ref[...])
    out_ref[...] = x_ref[...] + g_ref[...] * y


def _outproj_residual(os_, ws, x, gate):
    s, d = x.shape
    tm = min(s, 512)
    row = lambda n: pl.BlockSpec((tm, n), lambda i: (i, 0))
    return pl.pallas_call(
        functools.partial(_outproj_kernel, n_in=len(os_)),
        grid=(s // tm,),
        in_specs=[row(o.shape[1]) for o in os_] + [_full(w.shape) for w in ws] + [row(d), _full((1, d))],
        out_specs=row(d),
        out_shape=jax.ShapeDtypeStruct((s, d), F32),
        compiler_params=_params("parallel"),
        name="out_proj_residual",
    )(*os_, *ws, x, gate)


def _mla_proj_kernel(x_ref, nw_ref, sc_ref, sh_ref, wcq_ref, wckv_ref, wkr_ref, qn_ref, kvn_ref,
                     wqn_ref, wqr_ref, wkn_ref, wv_ref, cos_ref, sin_ref, q_ref, k_ref, v_ref):
    h = _norm_mod(x_ref[...], nw_ref[...], sc_ref[...], sh_ref[...]).astype(BF16)
    cq = _dot(h, wcq_ref[...])
    ckv = _dot(h, wckv_ref[...])
    kr = _dot(h, wkr_ref[...])
    cq = (cq * lax.rsqrt(jnp.mean(cq * cq, axis=-1, keepdims=True) + EPS) * qn_ref[...]).astype(BF16)
    ckv = (ckv * lax.rsqrt(jnp.mean(ckv * ckv, axis=-1, keepdims=True) + EPS) * kvn_ref[...]).astype(BF16)
    cos = cos_ref[...]
    sin = sin_ref[...]
    scale = MLA_QK ** -0.5
    q_nope = _dot(cq, wqn_ref[...]) * scale
    q_rope = _dot(cq, wqr_ref[...])
    k_nope = _dot(ckv, wkn_ref[...])
    val = _dot(ckv, wv_ref[...])
    k_rope = _rope_heads64(kr, cos, sin)[:, :MLA_ROPE].astype(BF16)
    for pair in range(MLA_HEADS // 2):
        slab_t = jnp.transpose(_rope_heads64(q_rope[:, pair * LANES:(pair + 1) * LANES], cos, sin) * scale)
        for sub in range(2):
            hh = 2 * pair + sub
            q_ref[hh, 0, MLA_NOPE:, :] = slab_t[sub * MLA_ROPE:(sub + 1) * MLA_ROPE].astype(BF16)
    for hh in range(MLA_HEADS):
        q_ref[hh, 0, :MLA_NOPE, :] = jnp.transpose(q_nope[:, hh * MLA_NOPE:(hh + 1) * MLA_NOPE]).astype(BF16)
        k_ref[hh, :, :MLA_NOPE] = k_nope[:, hh * MLA_NOPE:(hh + 1) * MLA_NOPE].astype(BF16)
        k_ref[hh, :, MLA_NOPE:] = k_rope
        v_ref[hh, 0] = _with_sum_rows(jnp.transpose(val[:, hh * MLA_V:(hh + 1) * MLA_V]))


def _mla_proj(x, nw, sc, sh, wcq, wckv, wkr, qn, kvn, wqn, wqr, wkn, wv, cos, sin):
    s, d = x.shape
    tm = KEY_BLOCK
    row = lambda n: pl.BlockSpec((tm, n), lambda i: (i, 0))
    vec = _full((1, d))
    heads = lambda n: pl.BlockSpec((MLA_HEADS, tm, n), lambda i: (0, i, 0))
    return pl.pallas_call(
        _mla_proj_kernel,
        grid=(s // tm,),
        in_specs=[row(d), vec, vec, vec, _full(wcq.shape), _full(wckv.shape), _full(wkr.shape),
                  _full(qn.shape), _full(kvn.shape), _full(wqn.shape), _full(wqr.shape),
                  _full(wkn.shape), _full(wv.shape), row(LANES), row(LANES)],
        out_specs=[pl.BlockSpec((MLA_HEADS, 1, MLA_QK, tm), lambda i: (0, i, 0, 0)), heads(MLA_QK),
                   pl.BlockSpec((MLA_HEADS, 1, MLA_V + SUM_ROWS, tm), lambda i: (0, i, 0, 0))],
        out_shape=[jax.ShapeDtypeStruct((MLA_HEADS, s // tm, MLA_QK, tm), BF16),
                   jax.ShapeDtypeStruct((MLA_HEADS, s, MLA_QK), BF16),
                   jax.ShapeDtypeStruct((MLA_HEADS, s // tm, MLA_V + SUM_ROWS, tm), BF16)],
        compiler_params=_params("parallel"),
        name="mla_in_proj",
    )(x, nw, sc, sh, wcq, wckv, wkr, qn, kvn, wqn, wqr, wkn, wv, cos, sin)


def _mla_attn_kernel(q_ref, k_ref, v_ref, o_ref, s_ref, mx_ref, p_ref, acc_ref):
    i = pl.program_id(1)

    def causal(s):
        key = lax.broadcasted_iota(jnp.int32, s.shape, 0)
        qry = lax.broadcasted_iota(jnp.int32, s.shape, 1)
        return jnp.where(key <= qry, s, NEG_INF)

    per_q = MLA_Q_BLOCK // KEY_BLOCK
    q_t = jnp.concatenate([q_ref[0, b] for b in range(per_q)], axis=1)
    out = _flash_steps(
        n_steps=i, first_blocks=[2 * i, 2 * i + 1], first_keys=2,
        step_blocks=lambda t: [2 * (t - 1), 2 * (t - 1) + 1],
        k_at=lambda j, n: k_ref[0, pl.ds(pl.multiple_of(j * KEY_BLOCK, KEY_BLOCK), n * KEY_BLOCK), :],
        v_at=lambda j: v_ref[0, j], q_t=q_t, mask_first=causal, mask_step=lambda s, t: s,
        s_ref=s_ref, mx_ref=mx_ref, p_ref=p_ref, acc_ref=acc_ref)
    o_ref[...] = jnp.transpose(out).astype(o_ref.dtype)


def _mla_attention(q_t, k, v_t):
    _, s, _ = k.shape
    bq = MLA_Q_BLOCK
    per_q = bq // KEY_BLOCK
    assert per_q == 2
    return pl.pallas_call(
        _mla_attn_kernel,
        grid=(MLA_HEADS, s // bq),
        in_specs=[pl.BlockSpec((1, per_q, MLA_QK, KEY_BLOCK), lambda h, i: (h, i, 0, 0)),
                  pl.BlockSpec((1, s, MLA_QK), lambda h, i: (h, 0, 0)),
                  pl.BlockSpec((1, s // KEY_BLOCK, MLA_V + SUM_ROWS, KEY_BLOCK), lambda h, i: (h, 0, 0, 0))],
        out_specs=pl.BlockSpec((bq, MLA_V), lambda h, i: (i, h)),
        out_shape=jax.ShapeDtypeStruct((s, MLA_HEADS * MLA_V), BF16),
        scratch_shapes=[pltpu.VMEM((2, 2 * KEY_BLOCK, bq), F32), pltpu.VMEM((2, 1, bq), F32),
                        pltpu.VMEM((2, 2 * KEY_BLOCK, bq), BF16), pltpu.VMEM((MLA_V + SUM_ROWS, bq), F32)],
        compiler_params=_params("parallel", "arbitrary"),
        name="mla_attention",
    )(q_t, k, v_t)


def _peer_proj_kernel(x_ref, nw_ref, sc_ref, sh_ref, wq_ref, keys_ref, h_ref, s_ref):
    h = _norm_mod(x_ref[...], nw_ref[...], sc_ref[...], sh_ref[...]).astype(BF16)
    h_ref[...] = h
    q = _dot(h, wq_ref[...]).astype(BF16)
    for hp in range(2 * PEER_HEADS):
        s_ref[hp] = _dot_nt(keys_ref[hp], q[:, hp * PEER_DSUB:(hp + 1) * PEER_DSUB])


def _peer_proj(x, nw, sc, sh, wq, keys):
    s, d = x.shape
    tm = min(s, 256)
    vec = _full((1, d))
    return pl.pallas_call(
        _peer_proj_kernel,
        grid=(s // tm,),
        in_specs=[pl.BlockSpec((tm, d), lambda i: (i, 0)), vec, vec, vec, _full(wq.shape), _full(keys.shape)],
        out_specs=[pl.BlockSpec((tm, d), lambda i: (i, 0)),
                   pl.BlockSpec((2 * PEER_HEADS, PEER_NKEYS, tm), lambda i: (0, 0, i))],
        out_shape=[jax.ShapeDtypeStruct((s, d), BF16),
                   jax.ShapeDtypeStruct((2 * PEER_HEADS, PEER_NKEYS, s), F32)],
        compiler_params=_params("parallel"),
        name="peer_query_scores",
    )(x, nw, sc, sh, wq, keys)


def _compare_exchange(vs, i, l):
    vs[i], vs[l] = jnp.maximum(vs[i], vs[l]), jnp.minimum(vs[i], vs[l])


def _sort_desc(vs):
    vs = list(vs)
    n = len(vs)
    k = 2
    while k <= n:
        j = k // 2
        while j >= 1:
            for i in range(n):
                l = i ^ j
                if l > i:
                    if i & k == 0:
                        _compare_exchange(vs, i, l)
                    else:
                        _compare_exchange(vs, l, i)
            j //= 2
        k *= 2
    return vs


def _merge_top(a, b):
    n = len(a)
    vs = [jnp.maximum(a[i], b[n - 1 - i]) for i in range(n)]
    j = n // 2
    while j >= 1:
        for i in range(n):
            if i ^ j > i:
                _compare_exchange(vs, i, i ^ j)
        j //= 2
    return vs


def _top16_of_keys(groups):
    vs = _sort_desc(groups)
    for shift in (4, 2, 1):
        vs = _merge_top(vs, [pltpu.roll(v, shift, 0) for v in vs])
    return vs


def _peer_topk_kernel(s_ref, a_ref, n_ref, b_ref, rank_ref):
    sub = 8
    groups = PEER_NKEYS // sub

    def head_body(h, carry):
        g1 = [s_ref[2 * h, g * sub:(g + 1) * sub, :] for g in range(groups)]
        g2 = [s_ref[2 * h + 1, g * sub:(g + 1) * sub, :] for g in range(groups)]
        v1 = _top16_of_keys(g1)
        v2 = _top16_of_keys(g2)

        ninf = jnp.full_like(v1[0], NEG_INF)
        pad = lambda vs: vs + [ninf] * (PEER_TOPK - len(vs))
        row = lambda r1: [v1[r1] + v2[r2] for r2 in range(PEER_TOPK // (r1 + 1))]
        top = _merge_top(
            _merge_top(row(0), _merge_top(pad(row(1)), pad(row(2)))),
            _merge_top(_sort_desc(pad(row(3) + row(4) + row(5) + row(6) + row(7))),
                       pad([row(r1)[0] for r1 in range(8, PEER_TOPK)])))
        thr = top[PEER_TOPK - 1]
        z = jnp.ones_like(thr)
        for r in range(1, PEER_TOPK):
            z = z + jnp.exp(top[r] - top[0])
        half_inv_z = 0.5 / z

        b_all, rank_all = [], []
        for g in range(groups):
            rows = slice(g * sub, (g + 1) * sub)
            n = sum(jnp.where(g1[g] + v2[r2] >= thr, 1.0, 0.0) for r2 in range(PEER_TOPK))
            a_ref[h, rows, :] = jnp.exp(g1[g] - v1[0]) * half_inv_z
            n_ref[h, rows, :] = n
            b_all.append(jnp.exp(g2[g] - v2[0]))
            rank_all.append(sum(jnp.where(v2[r] > g2[g], 1.0, 0.0) for r in range(PEER_TOPK)))
        for pair in range(groups // 2):
            b_ref[h, pair] = jnp.concatenate(b_all[2 * pair:2 * pair + 2], axis=0).astype(BF16)
            rank_ref[h, pair] = jnp.concatenate(rank_all[2 * pair:2 * pair + 2], axis=0).astype(BF16)
        return carry

    lax.fori_loop(0, PEER_HEADS, head_body, 0)


def _peer_topk(scores):
    _, nk, s = scores.shape
    tt = min(s, PEER_TOPK_TOK)
    big = pl.BlockSpec((PEER_HEADS, nk, tt), lambda i: (0, 0, i))
    tiled = pl.BlockSpec((PEER_HEADS, nk // BF16_ROWS, BF16_ROWS, tt), lambda i: (0, 0, 0, i))
    shape = (PEER_HEADS, nk, s)
    shape16 = (PEER_HEADS, nk // BF16_ROWS, BF16_ROWS, s)
    return pl.pallas_call(
        _peer_topk_kernel,
        grid=(s // tt,),
        in_specs=[pl.BlockSpec((2 * PEER_HEADS, nk, tt), lambda i: (0, 0, i))],
        out_specs=[big, big, tiled, tiled],
        out_shape=[jax.ShapeDtypeStruct(shape, F32), jax.ShapeDtypeStruct(shape, F32),
                   jax.ShapeDtypeStruct(shape16, BF16), jax.ShapeDtypeStruct(shape16, BF16)],
        compiler_params=_params("parallel"),
        name="peer_topk_threshold",
    )(scores)


def _peer_dense_kernel(h_ref, a_ref, n_ref, b_ref, rank_ref, u_ref, vt_ref, x_ref, g_ref, o_ref,
                       acc_ref, p_ref):
    j = pl.program_id(1)
    nk = PEER_NKEYS

    @pl.when(j == 0)
    def _():
        acc_ref[...] = jnp.zeros_like(acc_ref)

    act_t = _dot_nt(u_ref[...], h_ref[...])
    for blk in range(PEER_EXP // nk):
        i1 = j * (PEER_EXP // nk) + blk
        wgt = jnp.zeros((nk // BF16_ROWS, BF16_ROWS, PEER_TOK), BF16)
        for h in range(PEER_HEADS):
            row = lambda ref: jnp.broadcast_to(ref[h, pl.ds(i1, 1), :], (BF16_ROWS, PEER_TOK)).astype(BF16)[None]
            chosen = rank_ref[h] < row(n_ref)
            wgt = wgt + jnp.where(chosen, b_ref[h], 0.0) * row(a_ref)
        act = act_t[blk * nk:(blk + 1) * nk]
        gelu2 = act * (1.0 + lax.erf(act * (2.0 ** -0.5)))
        p_ref[blk * nk:(blk + 1) * nk, :] = wgt.reshape(nk, PEER_TOK) * gelu2.astype(BF16)
    acc_ref[...] += _dot(vt_ref[...], p_ref[...])

    @pl.when(j == pl.num_programs(1) - 1)
    def _():
        o_ref[...] = x_ref[...] + g_ref[...] * jnp.transpose(acc_ref[...])


def _peer_dense(h2, a, n, b, rank, u16, vt16, x, gate):
    s, d = x.shape
    n_exp = u16.shape[0]
    tt = min(s, PEER_TOK)
    assert tt == PEER_TOK
    tok3 = pl.BlockSpec((PEER_HEADS, PEER_NKEYS, tt), lambda i, j: (0, 0, i))
    tok4 = pl.BlockSpec((PEER_HEADS, PEER_NKEYS // BF16_ROWS, BF16_ROWS, tt), lambda i, j: (0, 0, 0, i))
    return pl.pallas_call(
        _peer_dense_kernel,
        grid=(s // tt, n_exp // PEER_EXP),
        in_specs=[pl.BlockSpec((tt, d), lambda i, j: (i, 0)),
                  tok3, tok3, tok4, tok4,
                  pl.BlockSpec((PEER_EXP, d), lambda i, j: (j, 0)),
                  pl.BlockSpec((d, PEER_EXP), lambda i, j: (0, j)),
                  pl.BlockSpec((tt, d), lambda i, j: (i, 0)),
                  pl.BlockSpec((1, d), lambda i, j: (0, 0))],
        out_specs=pl.BlockSpec((tt, d), lambda i, j: (i, 0)),
        out_shape=jax.ShapeDtypeStruct((s, d), F32),
        scratch_shapes=[pltpu.VMEM((d, tt), F32), pltpu.VMEM((PEER_EXP, tt), BF16)],
        compiler_params=_params("parallel", "arbitrary"),
        name="peer_dense_experts",
    )(h2, a, n, b, rank, u16, vt16, x, gate)


def _final_norm_kernel(x_ref, w_ref, o_ref):
    x = x_ref[...]
    o_ref[...] = x * lax.rsqrt(jnp.mean(x * x, axis=-1, keepdims=True) + EPS) * w_ref[...]


def _final_norm(x, w):
    s, d = x.shape
    tm = min(s, 1024)
    return pl.pallas_call(
        _final_norm_kernel,
        grid=(s // tm,),
        in_specs=[pl.BlockSpec((tm, d), lambda i: (i, 0)), _full((1, d))],
        out_specs=pl.BlockSpec((tm, d), lambda i: (i, 0)),
        out_shape=jax.ShapeDtypeStruct((s, d), F32),
        compiler_params=_params("parallel"),
        name="final_rmsnorm",
    )(x, w)


def _lane_row(values, offset):
    return jnp.zeros((1, LANES), F32).at[0, offset:offset + values.shape[0]].set(values.astype(F32))


def _hybrid_layer(x, nw, sc, sh, gate, cos, sin, w_in, conv_w, a_log, dt_bias, o_norm, w_out):
    w16 = w_in.astype(BF16)
    c0 = GDN_CONV_CH
    c1 = c0 + GDN_V
    c2 = c1 + 2 * GDN_HEADS
    wba = jnp.pad(w16[:, c1:c2], ((0, 0), (0, LANES - 2 * GDN_HEADS)))
    qkva, z, ba, qb, kb, vb = _hy_proj(x, nw, sc, sh, w16[:, :c0], w16[:, c0:c1], wba, w16[:, c2:], cos, sin)
    o_a = _gdn(qkva, ba, z, conv_w, _lane_row(a_log, GDN_HEADS), _lane_row(dt_bias, GDN_HEADS),
               o_norm.reshape(1, -1))
    o_b = _moba(qb, kb, vb)
    wo16 = w_out.astype(BF16)
    return _outproj_residual([o_a, o_b], [wo16[:GDN_V], wo16[GDN_V:]], x, gate)


def _mla_layer(x, nw, sc, sh, gate, cos, sin, w_in, q_norm, kv_norm, w_uq, w_ukv, w_out):
    w16 = w_in.astype(BF16)
    r0 = MLA_Q_RANK
    r1 = r0 + MLA_KV_RANK
    wkr = jnp.pad(w16[:, r1:], ((0, 0), (0, LANES - MLA_ROPE)))
    wuq = w_uq.astype(BF16).reshape(MLA_Q_RANK, MLA_HEADS, MLA_QK)
    wqn = wuq[:, :, :MLA_NOPE].reshape(MLA_Q_RANK, MLA_HEADS * MLA_NOPE)
    wqr = wuq[:, :, MLA_NOPE:].reshape(MLA_Q_RANK, MLA_HEADS * MLA_ROPE)
    wukv = w_ukv.astype(BF16).reshape(MLA_KV_RANK, MLA_HEADS, MLA_NOPE + MLA_V)
    wkn = wukv[:, :, :MLA_NOPE].reshape(MLA_KV_RANK, MLA_HEADS * MLA_NOPE)
    wv = wukv[:, :, MLA_NOPE:].reshape(MLA_KV_RANK, MLA_HEADS * MLA_V)
    q, k, v = _mla_proj(x, nw, sc, sh, w16[:, :r0], w16[:, r0:r1], wkr, q_norm.reshape(1, -1),
                        kv_norm.reshape(1, -1), wqn, wqr, wkn, wv, cos, sin)
    o = _mla_attention(q, k, v)
    return _outproj_residual([o], [w_out.astype(BF16)], x, gate)


def _peer_layer(x, nw, sc, sh, gate, w_q, sub_keys, u, v):
    keys = sub_keys.astype(BF16).reshape(2 * PEER_HEADS, PEER_NKEYS, PEER_DSUB)
    h2, scores = _peer_proj(x, nw, sc, sh, w_q.astype(BF16), keys)
    a, n, b, rank = _peer_topk(scores)
    return _peer_dense(h2, a, n, b, rank, u.astype(BF16), v.astype(BF16).T, x, gate)


def kernel(x, c, positions, mod_w, mod_b, norm_mix, norm_ffn, hy_w_in, gdn_conv, gdn_a_log, gdn_dt_bias, gdn_o_norm, hy_w_out, mla_w_in, mla_q_norm, mla_kv_norm, mla_w_uq, mla_w_ukv, mla_w_out, peer_w_q, peer_sub_keys, peer_u, peer_v, final_norm):
    bsz, s, d = x.shape
    assert bsz == 1 and d == D_MODEL
    depth = mod_w.shape[0]
    xs = x.reshape(s, d)
    mod = _modulation(c.reshape(d, 1), mod_w, mod_b)
    cos_b, sin_b, cos_c, sin_c = _rope_tables(positions.reshape(s, 1))
    for layer in range(depth):
        sh1, sc1, g1, sh2, sc2, g2 = (mod[layer, :, n * d:(n + 1) * d] for n in range(6))
        nw = norm_mix[layer].reshape(1, d)
        i = layer // 2
        if layer % 2 == 0:
            xs = _hybrid_layer(xs, nw, sc1, sh1, g1, cos_b, sin_b, hy_w_in[i], gdn_conv[i], gdn_a_log[i],
                               gdn_dt_bias[i], gdn_o_norm[i], hy_w_out[i])
        else:
            xs = _mla_layer(xs, nw, sc1, sh1, g1, cos_c, sin_c, mla_w_in[i], mla_q_norm[i], mla_kv_norm[i],
                            mla_w_uq[i], mla_w_ukv[i], mla_w_out[i])
        xs = _peer_layer(xs, norm_ffn[layer].reshape(1, d), sc2, sh2, g2, peer_w_q[layer],
                         peer_sub_keys[layer], peer_u[layer], peer_v[layer])
    return _final_norm(xs, final_norm.reshape(1, d)).reshape(bsz, s, d)
```

```python
import functools
import math

import jax
import jax.numpy as jnp
from jax import lax
from jax.experimental import pallas as pl
from jax.experimental.pallas import tpu as pltpu

F32 = jnp.float32
BF16 = jnp.bfloat16
NEG_INF = float("-inf")
HIGHEST = lax.Precision.HIGHEST

D_MODEL = 1024
EPS = 1e-6
ROPE_THETA = 10000.0

GDN_HEADS = 4
GDN_DK = 128
GDN_CHUNK = 64
GDN_QK = GDN_HEADS * GDN_DK
GDN_V = GDN_HEADS * 128
GDN_CONV_CH = 2 * GDN_QK + GDN_V
GDN_ROWS = 256

MOBA_HEADS = 4
MOBA_DH = 128
MOBA_BLOCK = 256
MOBA_TOPK = 3
MOBA_W = MOBA_HEADS * MOBA_DH

MLA_HEADS = 8
MLA_Q_RANK = 512
MLA_KV_RANK = 256
MLA_NOPE = 128
MLA_ROPE = 64
MLA_V = 128
MLA_QK = MLA_NOPE + MLA_ROPE
KEY_BLOCK = 256
MLA_Q_BLOCK = 512

PEER_HEADS = 8
PEER_NKEYS = 128
PEER_TOPK = 16
PEER_DSUB = 128
PEER_TOK = 512
PEER_EXP = 2048
PEER_TOPK_TOK = 256

LANES = 128
BF16_ROWS = 16
SUM_ROWS = BF16_ROWS
VMEM_LIMIT = 56 * 1024 * 1024


def _params(*sem):
    return pltpu.CompilerParams(dimension_semantics=sem, vmem_limit_bytes=VMEM_LIMIT)


def _dot(a, b, precision=None):
    return jnp.dot(a, b, preferred_element_type=F32, precision=precision)


def _dot_nt(a, b):
    return lax.dot_general(a, b, (((1,), (1,)), ((), ())), preferred_element_type=F32)


def _sigmoid(x):
    return jax.nn.sigmoid(x)


def _norm_mod(x, nw, sc, sh):
    y = x * lax.rsqrt(jnp.mean(x * x, axis=-1, keepdims=True) + EPS) * nw
    return y * (1.0 + sc) + sh


def _full(shape):
    n = len(shape)
    return pl.BlockSpec(shape, lambda *_: (0,) * n)


def _mod_kernel(c_ref, w_ref, b_ref, o_ref):
    c = c_ref[...]
    cond = c * _sigmoid(c)
    o_ref[0] = jnp.sum(cond * w_ref[0], axis=0, keepdims=True) + b_ref[0]


def _modulation(c_col, mod_w, mod_b):
    depth, d, n = mod_w.shape
    tn = 1536
    return pl.pallas_call(
        _mod_kernel,
        grid=(depth, n // tn),
        in_specs=[pl.BlockSpec((d, 1), lambda l, j: (0, 0)),
                  pl.BlockSpec((1, d, tn), lambda l, j: (l, 0, j)),
                  pl.BlockSpec((1, 1, tn), lambda l, j: (l, 0, j))],
        out_specs=pl.BlockSpec((1, 1, tn), lambda l, j: (l, 0, j)),
        out_shape=jax.ShapeDtypeStruct((depth, 1, n), F32),
        compiler_params=_params("parallel", "parallel"),
        name="adaln_modulation",
    )(c_col, mod_w, mod_b.reshape(depth, 1, n))


def _rope_kernel(pos_ref, fb_ref, fc_ref, cb_ref, sb_ref, cc_ref, sc_ref):
    p = pos_ref[...].astype(F32)
    lane = lax.broadcasted_iota(jnp.int32, cb_ref.shape, 1)
    ab = p * fb_ref[...]
    sb = jnp.sin(ab)
    cb_ref[...] = jnp.cos(ab)
    sb_ref[...] = jnp.where(lane < MOBA_DH // 2, -sb, sb)
    ac = p * fc_ref[...]
    sc = jnp.sin(ac)
    cc_ref[...] = jnp.cos(ac)
    sc_ref[...] = jnp.where(lane % MLA_ROPE < MLA_ROPE // 2, -sc, sc)


def _rope_tables(pos_col):
    s = pos_col.shape[0]
    tm = min(s, 1024)
    fb = ROPE_THETA ** (-jnp.arange(0, MOBA_DH, 2, dtype=F32) / MOBA_DH)
    fc = ROPE_THETA ** (-jnp.arange(0, MLA_ROPE, 2, dtype=F32) / MLA_ROPE)
    fb = jnp.tile(fb, 2).reshape(1, LANES)
    fc = jnp.tile(fc, 4).reshape(1, LANES)
    tab = pl.BlockSpec((tm, LANES), lambda i: (i, 0))
    return pl.pallas_call(
        _rope_kernel,
        grid=(s // tm,),
        in_specs=[pl.BlockSpec((tm, 1), lambda i: (i, 0)), _full((1, LANES)), _full((1, LANES))],
        out_specs=[tab] * 4,
        out_shape=[jax.ShapeDtypeStruct((s, LANES), F32)] * 4,
        compiler_params=_params("parallel"),
        name="rope_tables",
    )(pos_col, fb, fc)


def _rope_head128(x, cos, sin):
    return x * cos + pltpu.roll(x, MOBA_DH // 2, 1) * sin


def _rope_heads64(x, cos, sin):
    lane = lax.broadcasted_iota(jnp.int32, x.shape, 1)
    first = lane % MLA_ROPE < MLA_ROPE // 2
    partner = jnp.where(first, pltpu.roll(x, LANES - MLA_ROPE // 2, 1), pltpu.roll(x, MLA_ROPE // 2, 1))
    return x * cos + partner * sin


def _hy_proj_kernel(x_ref, nw_ref, sc_ref, sh_ref, wa_ref, wz_ref, wba_ref, wb_ref, cos_ref, sin_ref,
                    qkva_ref, z_ref, ba_ref, qb_ref, kb_ref, vb_ref):
    h = _norm_mod(x_ref[...], nw_ref[...], sc_ref[...], sh_ref[...]).astype(BF16)
    qkva_ref[...] = _dot(h, wa_ref[...])
    z_ref[...] = _dot(h, wz_ref[...])
    ba_ref[...] = _dot(h, wba_ref[...])
    qkvb = _dot(h, wb_ref[...])
    cos = cos_ref[...]
    sin = sin_ref[...]
    scale = MOBA_DH ** -0.5
    for hh in range(MOBA_HEADS):
        lo = hh * MOBA_DH
        q = qkvb[:, lo:lo + MOBA_DH]
        k = qkvb[:, MOBA_W + lo:MOBA_W + lo + MOBA_DH]
        qb_ref[hh, 0] = jnp.transpose(_rope_head128(q, cos, sin) * scale).astype(BF16)
        kb_ref[:, lo:lo + MOBA_DH] = _rope_head128(k, cos, sin).astype(BF16)
        v = qkvb[:, 2 * MOBA_W + lo:2 * MOBA_W + lo + MOBA_DH]
        vb_ref[hh, 0] = _with_sum_rows(jnp.transpose(v))


def _hy_proj(x, nw, sc, sh, wa, wz, wba, wb, cos, sin):
    s, d = x.shape
    tm = KEY_BLOCK
    row = lambda n: pl.BlockSpec((tm, n), lambda i: (i, 0))
    vec = _full((1, d))
    return pl.pallas_call(
        _hy_proj_kernel,
        grid=(s // tm,),
        in_specs=[row(d), vec, vec, vec, _full(wa.shape), _full(wz.shape), _full(wba.shape),
                  _full(wb.shape), row(LANES), row(LANES)],
        out_specs=[row(GDN_CONV_CH), row(GDN_V), row(LANES),
                   pl.BlockSpec((MOBA_HEADS, 1, MOBA_DH, tm), lambda i: (0, i, 0, 0)), row(MOBA_W),
                   pl.BlockSpec((MOBA_HEADS, 1, MOBA_DH + SUM_ROWS, tm), lambda i: (0, i, 0, 0))],
        out_shape=[jax.ShapeDtypeStruct((s, GDN_CONV_CH), F32),
                   jax.ShapeDtypeStruct((s, GDN_V), F32),
                   jax.ShapeDtypeStruct((s, LANES), F32),
                   jax.ShapeDtypeStruct((MOBA_HEADS, s // tm, MOBA_DH, tm), BF16),
                   jax.ShapeDtypeStruct((s, MOBA_W), BF16),
                   jax.ShapeDtypeStruct((MOBA_HEADS, s // tm, MOBA_DH + SUM_ROWS, tm), BF16)],
        compiler_params=_params("parallel"),
        name="hybrid_in_proj",
    )(x, nw, sc, sh, wa, wz, wba, wb, cos, sin)


def _unit_lower_inverses(l_stricts, eye):
    powers = [-l for l in l_stricts]
    invs = [eye + p for p in powers]
    p = 2
    while p < GDN_CHUNK:
        powers = [_dot(x.astype(BF16), x.astype(BF16)) for x in powers]
        invs = [_dot(inv.astype(BF16), (eye + x).astype(BF16)) for inv, x in zip(invs, powers)]
        p *= 2
    return invs


def _gdn_kernel(qkv_ref, ba_ref, z_ref, conv_ref, alog_ref, dtb_ref, onorm_ref, o_ref, ext_ref, state_ref):
    tb = GDN_ROWS
    cs = GDN_CHUNK

    @pl.when(pl.program_id(0) == 0)
    def _():
        ext_ref[0:8, :] = jnp.zeros((8, GDN_CONV_CH), F32)
        state_ref[...] = jnp.zeros_like(state_ref)

    ext_ref[8:8 + tb, :] = qkv_ref[...]
    cw = conv_ref[...]
    y = (ext_ref[5:5 + tb, :] * cw[0:1] + ext_ref[6:6 + tb, :] * cw[1:2]
         + ext_ref[7:7 + tb, :] * cw[2:3] + ext_ref[8:8 + tb, :] * cw[3:4])
    ext_ref[0:8, :] = ext_ref[tb:tb + 8, :]
    y = y * _sigmoid(y)

    ba = ba_ref[...]
    beta_all = _sigmoid(ba)
    t = ba + dtb_ref[...]
    softplus = jnp.maximum(t, 0.0) + jnp.log1p(jnp.exp(-jnp.abs(t)))
    g_all = -jnp.exp(alog_ref[...]) * softplus

    row = lax.broadcasted_iota(jnp.int32, (tb, tb), 0)
    col = lax.broadcasted_iota(jnp.int32, (tb, tb), 1)
    same = (row // cs) == (col // cs)
    causal = same & (col <= row)
    strict = same & (col < row)
    eye = (row == col).astype(F32)
    g_cum = _dot(causal.astype(F32), g_all, HIGHEST)
    g_tot = _dot(same.astype(F32), g_all, HIGHEST)
    g_cum_t = jnp.transpose(g_cum)
    z = z_ref[...]
    onorm = onorm_ref[...]

    heads = range(GDN_HEADS)
    rhs, l_strict, gtot, qk, qdec, kdec = [], [], [], [], [], []
    for h in heads:
        lo = h * GDN_DK
        q = y[:, lo:lo + GDN_DK]
        k = y[:, GDN_QK + lo:GDN_QK + lo + GDN_DK]
        v = y[:, 2 * GDN_QK + lo:2 * GDN_QK + lo + GDN_DK]
        q = q * lax.rsqrt(jnp.sum(q * q, axis=-1, keepdims=True) + EPS) * (GDN_DK ** -0.5)
        k = k * lax.rsqrt(jnp.sum(k * k, axis=-1, keepdims=True) + EPS)
        beta = beta_all[:, h:h + 1]
        gcol = g_cum[:, GDN_HEADS + h:GDN_HEADS + h + 1]
        grow = g_cum_t[GDN_HEADS + h:GDN_HEADS + h + 1, :]
        gtot.append(g_tot[:, GDN_HEADS + h:GDN_HEADS + h + 1])
        decay = jnp.exp(jnp.where(causal, gcol - grow, NEG_INF))
        kb = k * beta
        k16 = k.astype(BF16)
        l_strict.append(jnp.where(strict, _dot_nt(kb.astype(BF16), k16) * decay, 0.0))
        eg = jnp.exp(gcol)
        rhs.append(jnp.concatenate([v * beta, kb * eg], axis=1).astype(BF16))
        qk.append((_dot_nt(q.astype(BF16), k16) * decay).astype(BF16))
        qdec.append((q * eg).astype(BF16))
        kdec.append(k * jnp.exp(gtot[h] - gcol))

    uw = [_dot(t_inv.astype(BF16), rhs[h]) for h, t_inv in enumerate(_unit_lower_inverses(l_strict, eye))]
    u = [x[:, :GDN_DK] for x in uw]
    w = [x[:, GDN_DK:].astype(BF16) for x in uw]

    state = [state_ref[h] for h in heads]
    v_new = [[] for _ in heads]
    o_state = [[] for _ in heads]
    for c in range(tb // cs):
        r0 = c * cs
        for h in heads:
            s16 = state[h].astype(BF16)
            vn = u[h][r0:r0 + cs] - _dot(w[h][r0:r0 + cs], s16)
            o_state[h].append(_dot(qdec[h][r0:r0 + cs], s16))
            kd_t = jnp.transpose(kdec[h][r0:r0 + cs]).astype(BF16)
            state[h] = state[h] * jnp.exp(gtot[h][r0:r0 + 1]) + _dot(kd_t, vn.astype(BF16))
            v_new[h].append(vn)

    for h in heads:
        lo = h * GDN_DK
        state_ref[h] = state[h]
        o = jnp.concatenate(o_state[h], axis=0) + _dot(qk[h], jnp.concatenate(v_new[h], axis=0).astype(BF16))
        o = o * lax.rsqrt(jnp.mean(o * o, axis=-1, keepdims=True) + EPS) * onorm
        zh = z[:, lo:lo + GDN_DK]
        o_ref[:, lo:lo + GDN_DK] = (o * (zh * _sigmoid(zh))).astype(BF16)


def _gdn(qkva, ba, z, conv_w, alog, dtb, onorm):
    s = qkva.shape[0]
    tb = GDN_ROWS
    row = lambda n: pl.BlockSpec((tb, n), lambda i: (i, 0))
    return pl.pallas_call(
        _gdn_kernel,
        grid=(s // tb,),
        in_specs=[row(GDN_CONV_CH), row(LANES), row(GDN_V), _full(conv_w.shape),
                  _full((1, LANES)), _full((1, LANES)), _full((1, LANES))],
        out_specs=row(GDN_V),
        out_shape=jax.ShapeDtypeStruct((s, GDN_V), BF16),
        scratch_shapes=[pltpu.VMEM((tb + 8, GDN_CONV_CH), F32),
                        pltpu.VMEM((GDN_HEADS, GDN_DK, 128), F32)],
        compiler_params=_params("arbitrary"),
        name="gated_delta_rule",
    )(qkva, ba, z, conv_w, alog, dtb, onorm)


def _scores_into(s_ref, mx_ref, slot, k_tile, q_t, mask):
    s = _dot(k_tile, q_t)
    rows = s.shape[0]
    top = None
    for lo, piece in mask(s):
        s_ref[slot, lo:lo + piece.shape[0], :] = piece
        pmax = piece.max(axis=0, keepdims=True)
        top = pmax if top is None else jnp.maximum(top, pmax)
    if rows < s_ref.shape[1]:
        s_ref[slot, rows:, :] = jnp.full((s_ref.shape[1] - rows, s.shape[1]), NEG_INF, F32)
    mx_ref[slot] = top


def _values(p_ref, slot, v_at, blocks):
    v_t = jnp.concatenate([v_at(j) for j in blocks], axis=1)
    return _dot(v_t, p_ref[slot])


def _flash_steps(n_steps, first_blocks, first_keys, step_blocks, k_at, v_at, q_t, mask_first, mask_step,
                 s_ref, mx_ref, p_ref, acc_ref):
    per_step = len(first_blocks)
    _scores_into(s_ref, mx_ref, 0, k_at(first_blocks[0], first_keys), q_t, mask_first)
    m = mx_ref[0]
    p_ref[0] = jnp.exp(s_ref[0] - m).astype(BF16)
    acc_ref[...] = jnp.zeros_like(acc_ref)
    _scores_into(s_ref, mx_ref, 1, k_at(step_blocks(1)[0], per_step), q_t, lambda s: mask_step(s, 1))

    def step(t, carry, cur):
        m, alpha_prev = carry
        nxt = jnp.minimum(t + 1, jnp.maximum(n_steps, 1))
        _scores_into(s_ref, mx_ref, 1 - cur, k_at(step_blocks(nxt)[0], per_step), q_t, lambda s: mask_step(s, nxt))
        m_new = jnp.maximum(m, mx_ref[cur])
        alpha = jnp.exp(m - m_new)
        p = jnp.exp(s_ref[cur] - m_new)
        prev = [jnp.where(t == 1, a, b) for a, b in zip(first_blocks, step_blocks(jnp.maximum(t - 1, 1)))]
        acc_ref[...] = alpha_prev * acc_ref[...] + _values(p_ref, 1 - cur, v_at, prev)
        p_ref[cur] = p.astype(BF16)
        return m_new, alpha

    def body(t, carry):
        return lax.cond(t % 2 == 0, lambda c: step(t, c, 0), lambda c: step(t, c, 1), carry)

    m, alpha = lax.fori_loop(1, n_steps + 1, body, (m, jnp.ones_like(m)))
    last = [jnp.where(n_steps == 0, a, b) for a, b in zip(first_blocks, step_blocks(jnp.maximum(n_steps, 1)))]
    acc = alpha * acc_ref[...] + _values(p_ref, n_steps % 2, v_at, last)
    dh = acc.shape[0] - SUM_ROWS
    return acc[:dh] / acc[dh:dh + 1]


def _with_sum_rows(v_t):
    extra = lax.broadcasted_iota(jnp.int32, (SUM_ROWS, v_t.shape[1]), 0) == 0
    return jnp.concatenate([v_t, extra.astype(F32)], axis=0).astype(BF16)


def _moba_kernel(q_ref, k_ref, v_ref, o_ref, kmean_ref, sel_ref, s_ref, mx_ref, p_ref, acc_ref, *, n_blocks):
    i = pl.program_id(1)
    bs = MOBA_BLOCK

    @pl.when(i == 0)
    def _():
        kmean_ref[...] = jnp.zeros_like(kmean_ref)

        def mean_body(n, carry):
            blk = k_ref[pl.ds(pl.multiple_of(n * bs, bs), bs), :].astype(F32)
            kmean_ref[pl.ds(n, 1), :] = jnp.mean(blk, axis=0, keepdims=True)
            return carry

        lax.fori_loop(0, n_blocks, mean_body, 0)

    q_t = jnp.concatenate([q_ref[0, 0], q_ref[0, 1]], axis=1)
    gate = _dot(kmean_ref[...].astype(BF16), q_t)
    blk_id = lax.broadcasted_iota(jnp.int32, gate.shape, 0)
    second = lax.broadcasted_iota(jnp.int32, gate.shape, 1) >= bs
    gate = jnp.where(blk_id < 2 * i + second.astype(jnp.int32), gate, NEG_INF)
    sel = jnp.zeros(gate.shape, F32)
    for _ in range(MOBA_TOPK):
        gmax = jnp.max(gate, axis=0, keepdims=True)
        first = jnp.min(jnp.where(gate == gmax, blk_id, LANES), axis=0, keepdims=True)
        hit = blk_id == first
        sel = jnp.where(hit & (gmax > NEG_INF), 1.0, sel)
        gate = jnp.where(hit, NEG_INF, gate)
    sel_ref[...] = sel

    def own_mask(s):
        key = lax.broadcasted_iota(jnp.int32, (bs, 2 * bs), 0)
        qry = lax.broadcasted_iota(jnp.int32, (bs, 2 * bs), 1)
        picked = sel_ref[pl.ds(2 * i, 1), :] > 0.0
        first_ok = (key <= qry) & ((qry < bs) | picked)
        return [(0, jnp.where(first_ok, s[:bs], NEG_INF)), (bs, jnp.where(key + bs <= qry, s[bs:], NEG_INF))]

    def sel_mask(s, t):
        return [(sub * bs, jnp.where(sel_ref[pl.ds(2 * (t - 1) + sub, 1), :] > 0.0, s[sub * bs:(sub + 1) * bs], NEG_INF))
                for sub in range(2)]

    out = _flash_steps(
        n_steps=i, first_blocks=[2 * i, 2 * i + 1], first_keys=2,
        step_blocks=lambda t: [2 * (t - 1), 2 * (t - 1) + 1],
        k_at=lambda j, n: k_ref[pl.ds(pl.multiple_of(j * bs, bs), n * bs), :], v_at=lambda j: v_ref[0, j],
        q_t=q_t, mask_first=own_mask, mask_step=sel_mask,
        s_ref=s_ref, mx_ref=mx_ref, p_ref=p_ref, acc_ref=acc_ref)
    o_ref[...] = jnp.transpose(out).astype(o_ref.dtype)


def _moba(q_t, k, v_t):
    s = k.shape[0]
    bs = MOBA_BLOCK
    n_blocks = s // bs
    assert n_blocks <= LANES and bs == KEY_BLOCK and n_blocks % 2 == 0
    bq = 2 * bs
    kv = pl.BlockSpec((s, MOBA_DH), lambda h, i: (0, h))
    return pl.pallas_call(
        functools.partial(_moba_kernel, n_blocks=n_blocks),
        grid=(MOBA_HEADS, n_blocks // 2),
        in_specs=[pl.BlockSpec((1, 2, MOBA_DH, bs), lambda h, i: (h, i, 0, 0)), kv,
                  pl.BlockSpec((1, n_blocks, MOBA_DH + SUM_ROWS, bs), lambda h, i: (h, 0, 0, 0))],
        out_specs=pl.BlockSpec((bq, MOBA_DH), lambda h, i: (i, h)),
        out_shape=jax.ShapeDtypeStruct((s, MOBA_W), BF16),
        scratch_shapes=[pltpu.VMEM((LANES, MOBA_DH), F32),
                        pltpu.VMEM((LANES, bq), F32),
                        pltpu.VMEM((2, 2 * KEY_BLOCK, bq), F32),
                        pltpu.VMEM((2, 1, bq), F32),
                        pltpu.VMEM((2, 2 * KEY_BLOCK, bq), BF16),
                        pltpu.VMEM((MOBA_DH + SUM_ROWS, bq), F32)],
        compiler_params=_params("parallel", "arbitrary"),
        name="moba_attention",
    )(q_t, k, v_t)


def _outproj_kernel(*refs, n_in):
    o_refs = refs[:n_in]
    w_refs = refs[n_in:2 * n_in]
    x_ref, g_ref, out_ref = refs[2 * n_in:]
    y = _dot(o_refs[0][...], w_refs[0][...])
    for o_ref, w_ref in zip(o_refs[1:], w_refs[1:]):
        y = y + _dot(o_ref[...], w_ref[...])
    out_ref[...] = x_ref[...] + g_ref[...] * y


def _outproj_residual(os_, ws, x, gate):
    s, d = x.shape
    tm = min(s, 512)
    row = lambda n: pl.BlockSpec((tm, n), lambda i: (i, 0))
    return pl.pallas_call(
        functools.partial(_outproj_kernel, n_in=len(os_)),
        grid=(s // tm,),
        in_specs=[row(o.shape[1]) for o in os_] + [_full(w.shape) for w in ws] + [row(d), _full((1, d))],
        out_specs=row(d),
        out_shape=jax.ShapeDtypeStruct((s, d), F32),
        compiler_params=_params("parallel"),
        name="out_proj_residual",
    )(*os_, *ws, x, gate)


def _mla_proj_kernel(x_ref, nw_ref, sc_ref, sh_ref, wcq_ref, wckv_ref, wkr_ref, qn_ref, kvn_ref,
                     wqn_ref, wqr_ref, wkn_ref, wv_ref, cos_ref, sin_ref, q_ref, k_ref, v_ref):
    h = _norm_mod(x_ref[...], nw_ref[...], sc_ref[...], sh_ref[...]).astype(BF16)
    cq = _dot(h, wcq_ref[...])
    ckv = _dot(h, wckv_ref[...])
    kr = _dot(h, wkr_ref[...])
    cq = (cq * lax.rsqrt(jnp.mean(cq * cq, axis=-1, keepdims=True) + EPS) * qn_ref[...]).astype(BF16)
    ckv = (ckv * lax.rsqrt(jnp.mean(ckv * ckv, axis=-1, keepdims=True) + EPS) * kvn_ref[...]).astype(BF16)
    cos = cos_ref[...]
    sin = sin_ref[...]
    scale = MLA_QK ** -0.5
    q_nope = _dot(cq, wqn_ref[...]) * scale
    q_rope = _dot(cq, wqr_ref[...])
    k_nope = _dot(ckv, wkn_ref[...])
    val = _dot(ckv, wv_ref[...])
    k_rope = _rope_heads64(kr, cos, sin)[:, :MLA_ROPE].astype(BF16)
    for pair in range(MLA_HEADS // 2):
        slab_t = jnp.transpose(_rope_heads64(q_rope[:, pair * LANES:(pair + 1) * LANES], cos, sin) * scale)
        for sub in range(2):
            hh = 2 * pair + sub
            q_ref[hh, 0, MLA_NOPE:, :] = slab_t[sub * MLA_ROPE:(sub + 1) * MLA_ROPE].astype(BF16)
    for hh in range(MLA_HEADS):
        q_ref[hh, 0, :MLA_NOPE, :] = jnp.transpose(q_nope[:, hh * MLA_NOPE:(hh + 1) * MLA_NOPE]).astype(BF16)
        k_ref[hh, :, :MLA_NOPE] = k_nope[:, hh * MLA_NOPE:(hh + 1) * MLA_NOPE].astype(BF16)
        k_ref[hh, :, MLA_NOPE:] = k_rope
        v_ref[hh, 0] = _with_sum_rows(jnp.transpose(val[:, hh * MLA_V:(hh + 1) * MLA_V]))


def _mla_proj(x, nw, sc, sh, wcq, wckv, wkr, qn, kvn, wqn, wqr, wkn, wv, cos, sin):
    s, d = x.shape
    tm = KEY_BLOCK
    row = lambda n: pl.BlockSpec((tm, n), lambda i: (i, 0))
    vec = _full((1, d))
    heads = lambda n: pl.BlockSpec((MLA_HEADS, tm, n), lambda i: (0, i, 0))
    return pl.pallas_call(
        _mla_proj_kernel,
        grid=(s // tm,),
        in_specs=[row(d), vec, vec, vec, _full(wcq.shape), _full(wckv.shape), _full(wkr.shape),
                  _full(qn.shape), _full(kvn.shape), _full(wqn.shape), _full(wqr.shape),
                  _full(wkn.shape), _full(wv.shape), row(LANES), row(LANES)],
        out_specs=[pl.BlockSpec((MLA_HEADS, 1, MLA_QK, tm), lambda i: (0, i, 0, 0)), heads(MLA_QK),
                   pl.BlockSpec((MLA_HEADS, 1, MLA_V + SUM_ROWS, tm), lambda i: (0, i, 0, 0))],
        out_shape=[jax.ShapeDtypeStruct((MLA_HEADS, s // tm, MLA_QK, tm), BF16),
                   jax.ShapeDtypeStruct((MLA_HEADS, s, MLA_QK), BF16),
                   jax.ShapeDtypeStruct((MLA_HEADS, s // tm, MLA_V + SUM_ROWS, tm), BF16)],
        compiler_params=_params("parallel"),
        name="mla_in_proj",
    )(x, nw, sc, sh, wcq, wckv, wkr, qn, kvn, wqn, wqr, wkn, wv, cos, sin)


def _mla_attn_kernel(q_ref, k_ref, v_ref, o_ref, s_ref, mx_ref, p_ref, acc_ref):
    i = pl.program_id(1)

    def causal(s):
        key = lax.broadcasted_iota(jnp.int32, s.shape, 0)
        qry = lax.broadcasted_iota(jnp.int32, s.shape, 1)
        return [(0, jnp.where(key <= qry, s, NEG_INF))]

    per_q = MLA_Q_BLOCK // KEY_BLOCK
    q_t = jnp.concatenate([q_ref[0, b] for b in range(per_q)], axis=1)
    out = _flash_steps(
        n_steps=i, first_blocks=[2 * i, 2 * i + 1], first_keys=2,
        step_blocks=lambda t: [2 * (t - 1), 2 * (t - 1) + 1],
        k_at=lambda j, n: k_ref[0, pl.ds(pl.multiple_of(j * KEY_BLOCK, KEY_BLOCK), n * KEY_BLOCK), :],
        v_at=lambda j: v_ref[0, j], q_t=q_t, mask_first=causal, mask_step=lambda s, t: [(0, s)],
        s_ref=s_ref, mx_ref=mx_ref, p_ref=p_ref, acc_ref=acc_ref)
    o_ref[...] = jnp.transpose(out).astype(o_ref.dtype)


def _mla_attention(q_t, k, v_t):
    _, s, _ = k.shape
    bq = MLA_Q_BLOCK
    per_q = bq // KEY_BLOCK
    assert per_q == 2
    return pl.pallas_call(
        _mla_attn_kernel,
        grid=(MLA_HEADS, s // bq),
        in_specs=[pl.BlockSpec((1, per_q, MLA_QK, KEY_BLOCK), lambda h, i: (h, i, 0, 0)),
                  pl.BlockSpec((1, s, MLA_QK), lambda h, i: (h, 0, 0)),
                  pl.BlockSpec((1, s // KEY_BLOCK, MLA_V + SUM_ROWS, KEY_BLOCK), lambda h, i: (h, 0, 0, 0))],
        out_specs=pl.BlockSpec((bq, MLA_V), lambda h, i: (i, h)),
        out_shape=jax.ShapeDtypeStruct((s, MLA_HEADS * MLA_V), BF16),
        scratch_shapes=[pltpu.VMEM((2, 2 * KEY_BLOCK, bq), F32), pltpu.VMEM((2, 1, bq), F32),
                        pltpu.VMEM((2, 2 * KEY_BLOCK, bq), BF16), pltpu.VMEM((MLA_V + SUM_ROWS, bq), F32)],
        compiler_params=_params("parallel", "arbitrary"),
        name="mla_attention",
    )(q_t, k, v_t)


def _peer_proj_kernel(x_ref, nw_ref, sc_ref, sh_ref, wq_ref, keys_ref, h_ref, s_ref):
    h = _norm_mod(x_ref[...], nw_ref[...], sc_ref[...], sh_ref[...]).astype(BF16)
    h_ref[...] = h
    q = _dot(h, wq_ref[...]).astype(BF16)
    for hp in range(2 * PEER_HEADS):
        s_ref[hp] = _dot_nt(keys_ref[hp], q[:, hp * PEER_DSUB:(hp + 1) * PEER_DSUB])


def _peer_proj(x, nw, sc, sh, wq, keys):
    s, d = x.shape
    tm = min(s, 256)
    vec = _full((1, d))
    return pl.pallas_call(
        _peer_proj_kernel,
        grid=(s // tm,),
        in_specs=[pl.BlockSpec((tm, d), lambda i: (i, 0)), vec, vec, vec, _full(wq.shape), _full(keys.shape)],
        out_specs=[pl.BlockSpec((tm, d), lambda i: (i, 0)),
                   pl.BlockSpec((2 * PEER_HEADS, PEER_NKEYS, tm), lambda i: (0, 0, i))],
        out_shape=[jax.ShapeDtypeStruct((s, d), BF16),
                   jax.ShapeDtypeStruct((2 * PEER_HEADS, PEER_NKEYS, s), F32)],
        compiler_params=_params("parallel"),
        name="peer_query_scores",
    )(x, nw, sc, sh, wq, keys)


def _compare_exchange(vs, i, l):
    vs[i], vs[l] = jnp.maximum(vs[i], vs[l]), jnp.minimum(vs[i], vs[l])


def _sort_desc(vs):
    vs = list(vs)
    n = len(vs)
    k = 2
    while k <= n:
        j = k // 2
        while j >= 1:
            for i in range(n):
                l = i ^ j
                if l > i:
                    if i & k == 0:
                        _compare_exchange(vs, i, l)
                    else:
                        _compare_exchange(vs, l, i)
            j //= 2
        k *= 2
    return vs


def _merge_top(a, b):
    n = len(a)
    vs = [jnp.maximum(a[i], b[n - 1 - i]) for i in range(n)]
    j = n // 2
    while j >= 1:
        for i in range(n):
            if i ^ j > i:
                _compare_exchange(vs, i, i ^ j)
        j //= 2
    return vs


def _top16_of_keys(groups):
    vs = _sort_desc(groups)
    for shift in (4, 2, 1):
        vs = _merge_top(vs, [pltpu.roll(v, shift, 0) for v in vs])
    return vs


def _peer_topk_kernel(s_ref, a_ref, n_ref, b_ref, rank_ref):
    sub = 8
    groups = PEER_NKEYS // sub

    def head_body(h, carry):
        g1 = [s_ref[2 * h, g * sub:(g + 1) * sub, :] for g in range(groups)]
        g2 = [s_ref[2 * h + 1, g * sub:(g + 1) * sub, :] for g in range(groups)]
        v1 = _top16_of_keys(g1)
        v2 = _top16_of_keys(g2)

        ninf = jnp.full_like(v1[0], NEG_INF)
        pad = lambda vs: vs + [ninf] * (PEER_TOPK - len(vs))
        row = lambda r1: [v1[r1] + v2[r2] for r2 in range(PEER_TOPK // (r1 + 1))]
        top = _merge_top(
            _merge_top(row(0), _merge_top(pad(row(1)), pad(row(2)))),
            _merge_top(_sort_desc(pad(row(3) + row(4) + row(5) + row(6) + row(7))),
                       pad([row(r1)[0] for r1 in range(8, PEER_TOPK)])))
        thr = top[PEER_TOPK - 1]
        z = jnp.ones_like(thr)
        for r in range(1, PEER_TOPK):
            z = z + jnp.exp(top[r] - top[0])
        half_inv_z = 0.5 / z

        b_all, rank_all = [], []
        for g in range(groups):
            rows = slice(g * sub, (g + 1) * sub)
            n = sum(jnp.where(g1[g] + v2[r2] >= thr, 1.0, 0.0) for r2 in range(PEER_TOPK))
            a_ref[h, rows, :] = jnp.exp(g1[g] - v1[0]) * half_inv_z
            n_ref[h, rows, :] = n
            b_all.append(jnp.exp(g2[g] - v2[0]))
            rank_all.append(sum(jnp.where(v2[r] > g2[g], 1.0, 0.0) for r in range(PEER_TOPK)))
        for pair in range(groups // 2):
            b_ref[h, pair] = jnp.concatenate(b_all[2 * pair:2 * pair + 2], axis=0).astype(BF16)
            rank_ref[h, pair] = jnp.concatenate(rank_all[2 * pair:2 * pair + 2], axis=0).astype(BF16)
        return carry

    lax.fori_loop(0, PEER_HEADS, head_body, 0)


def _peer_topk(scores):
    _, nk, s = scores.shape
    tt = min(s, PEER_TOPK_TOK)
    big = pl.BlockSpec((PEER_HEADS, nk, tt), lambda i: (0, 0, i))
    tiled = pl.BlockSpec((PEER_HEADS, nk // BF16_ROWS, BF16_ROWS, tt), lambda i: (0, 0, 0, i))
    shape = (PEER_HEADS, nk, s)
    shape16 = (PEER_HEADS, nk // BF16_ROWS, BF16_ROWS, s)
    return pl.pallas_call(
        _peer_topk_kernel,
        grid=(s // tt,),
        in_specs=[pl.BlockSpec((2 * PEER_HEADS, nk, tt), lambda i: (0, 0, i))],
        out_specs=[big, big, tiled, tiled],
        out_shape=[jax.ShapeDtypeStruct(shape, F32), jax.ShapeDtypeStruct(shape, F32),
                   jax.ShapeDtypeStruct(shape16, BF16), jax.ShapeDtypeStruct(shape16, BF16)],
        compiler_params=_params("parallel"),
        name="peer_topk_threshold",
    )(scores)


def _peer_dense_kernel(h_ref, a_ref, n_ref, b_ref, rank_ref, u_ref, vt_ref, x_ref, g_ref, o_ref,
                       acc_ref, p_ref):
    j = pl.program_id(1)
    nk = PEER_NKEYS

    @pl.when(j == 0)
    def _():
        acc_ref[...] = jnp.zeros_like(acc_ref)

    act_t = _dot_nt(u_ref[...], h_ref[...])
    for blk in range(PEER_EXP // nk):
        i1 = j * (PEER_EXP // nk) + blk
        wgt = jnp.zeros((nk // BF16_ROWS, BF16_ROWS, PEER_TOK), BF16)
        for h in range(PEER_HEADS):
            row = lambda ref: jnp.broadcast_to(ref[h, pl.ds(i1, 1), :], (BF16_ROWS, PEER_TOK)).astype(BF16)[None]
            chosen = rank_ref[h] < row(n_ref)
            wgt = wgt + jnp.where(chosen, b_ref[h], 0.0) * row(a_ref)
        act = act_t[blk * nk:(blk + 1) * nk]
        gelu2 = act * (1.0 + lax.erf(act * (2.0 ** -0.5)))
        p_ref[blk * nk:(blk + 1) * nk, :] = wgt.reshape(nk, PEER_TOK) * gelu2.astype(BF16)
    acc_ref[...] += _dot(vt_ref[...], p_ref[...])

    @pl.when(j == pl.num_programs(1) - 1)
    def _():
        o_ref[...] = x_ref[...] + g_ref[...] * jnp.transpose(acc_ref[...])


def _peer_dense(h2, a, n, b, rank, u16, vt16, x, gate):
    s, d = x.shape
    n_exp = u16.shape[0]
    tt = min(s, PEER_TOK)
    assert tt == PEER_TOK
    tok3 = pl.BlockSpec((PEER_HEADS, PEER_NKEYS, tt), lambda i, j: (0, 0, i))
    tok4 = pl.BlockSpec((PEER_HEADS, PEER_NKEYS // BF16_ROWS, BF16_ROWS, tt), lambda i, j: (0, 0, 0, i))
    return pl.pallas_call(
        _peer_dense_kernel,
        grid=(s // tt, n_exp // PEER_EXP),
        in_specs=[pl.BlockSpec((tt, d), lambda i, j: (i, 0)),
                  tok3, tok3, tok4, tok4,
                  pl.BlockSpec((PEER_EXP, d), lambda i, j: (j, 0)),
                  pl.BlockSpec((d, PEER_EXP), lambda i, j: (0, j)),
                  pl.BlockSpec((tt, d), lambda i, j: (i, 0)),
                  pl.BlockSpec((1, d), lambda i, j: (0, 0))],
        out_specs=pl.BlockSpec((tt, d), lambda i, j: (i, 0)),
        out_shape=jax.ShapeDtypeStruct((s, d), F32),
        scratch_shapes=[pltpu.VMEM((d, tt), F32), pltpu.VMEM((PEER_EXP, tt), BF16)],
        compiler_params=_params("parallel", "arbitrary"),
        name="peer_dense_experts",
    )(h2, a, n, b, rank, u16, vt16, x, gate)


def _final_norm_kernel(x_ref, w_ref, o_ref):
    x = x_ref[...]
    o_ref[...] = x * lax.rsqrt(jnp.mean(x * x, axis=-1, keepdims=True) + EPS) * w_ref[...]


def _final_norm(x, w):
    s, d = x.shape
    tm = min(s, 1024)
    return pl.pallas_call(
        _final_norm_kernel,
        grid=(s // tm,),
        in_specs=[pl.BlockSpec((tm, d), lambda i: (i, 0)), _full((1, d))],
        out_specs=pl.BlockSpec((tm, d), lambda i: (i, 0)),
        out_shape=jax.ShapeDtypeStruct((s, d), F32),
        compiler_params=_params("parallel"),
        name="final_rmsnorm",
    )(x, w)


def _lane_row(values, offset):
    return jnp.zeros((1, LANES), F32).at[0, offset:offset + values.shape[0]].set(values.astype(F32))


def _hybrid_layer(x, nw, sc, sh, gate, cos, sin, w_in, conv_w, a_log, dt_bias, o_norm, w_out):
    w16 = w_in.astype(BF16)
    c0 = GDN_CONV_CH
    c1 = c0 + GDN_V
    c2 = c1 + 2 * GDN_HEADS
    wba = jnp.pad(w16[:, c1:c2], ((0, 0), (0, LANES - 2 * GDN_HEADS)))
    qkva, z, ba, qb, kb, vb = _hy_proj(x, nw, sc, sh, w16[:, :c0], w16[:, c0:c1], wba, w16[:, c2:], cos, sin)
    o_a = _gdn(qkva, ba, z, conv_w, _lane_row(a_log, GDN_HEADS), _lane_row(dt_bias, GDN_HEADS),
               o_norm.reshape(1, -1))
    o_b = _moba(qb, kb, vb)
    wo16 = w_out.astype(BF16)
    return _outproj_residual([o_a, o_b], [wo16[:GDN_V], wo16[GDN_V:]], x, gate)


def _mla_layer(x, nw, sc, sh, gate, cos, sin, w_in, q_norm, kv_norm, w_uq, w_ukv, w_out):
    w16 = w_in.astype(BF16)
    r0 = MLA_Q_RANK
    r1 = r0 + MLA_KV_RANK
    wkr = jnp.pad(w16[:, r1:], ((0, 0), (0, LANES - MLA_ROPE)))
    wuq = w_uq.astype(BF16).reshape(MLA_Q_RANK, MLA_HEADS, MLA_QK)
    wqn = wuq[:, :, :MLA_NOPE].reshape(MLA_Q_RANK, MLA_HEADS * MLA_NOPE)
    wqr = wuq[:, :, MLA_NOPE:].reshape(MLA_Q_RANK, MLA_HEADS * MLA_ROPE)
    wukv = w_ukv.astype(BF16).reshape(MLA_KV_RANK, MLA_HEADS, MLA_NOPE + MLA_V)
    wkn = wukv[:, :, :MLA_NOPE].reshape(MLA_KV_RANK, MLA_HEADS * MLA_NOPE)
    wv = wukv[:, :, MLA_NOPE:].reshape(MLA_KV_RANK, MLA_HEADS * MLA_V)
    q, k, v = _mla_proj(x, nw, sc, sh, w16[:, :r0], w16[:, r0:r1], wkr, q_norm.reshape(1, -1),
                        kv_norm.reshape(1, -1), wqn, wqr, wkn, wv, cos, sin)
    o = _mla_attention(q, k, v)
    return _outproj_residual([o], [w_out.astype(BF16)], x, gate)


def _peer_layer(x, nw, sc, sh, gate, w_q, sub_keys, u, v):
    keys = sub_keys.astype(BF16).reshape(2 * PEER_HEADS, PEER_NKEYS, PEER_DSUB)
    h2, scores = _peer_proj(x, nw, sc, sh, w_q.astype(BF16), keys)
    a, n, b, rank = _peer_topk(scores)
    return _peer_dense(h2, a, n, b, rank, u.astype(BF16), v.astype(BF16).T, x, gate)


def kernel(x, c, positions, mod_w, mod_b, norm_mix, norm_ffn, hy_w_in, gdn_conv, gdn_a_log, gdn_dt_bias, gdn_o_norm, hy_w_out, mla_w_in, mla_q_norm, mla_kv_norm, mla_w_uq, mla_w_ukv, mla_w_out, peer_w_q, peer_sub_keys, peer_u, peer_v, final_norm):
    bsz, s, d = x.shape
    assert bsz == 1 and d == D_MODEL
    depth = mod_w.shape[0]
    xs = x.reshape(s, d)
    mod = _modulation(c.reshape(d, 1), mod_w, mod_b)
    cos_b, sin_b, cos_c, sin_c = _rope_tables(positions.reshape(s, 1))
    for layer in range(depth):
        sh1, sc1, g1, sh2, sc2, g2 = (mod[layer, :, n * d:(n + 1) * d] for n in range(6))
        nw = norm_mix[layer].reshape(1, d)
        i = layer // 2
        if layer % 2 == 0:
            xs = _hybrid_layer(xs, nw, sc1, sh1, g1, cos_b, sin_b, hy_w_in[i], gdn_conv[i], gdn_a_log[i],
                               gdn_dt_bias[i], gdn_o_norm[i], hy_w_out[i])
        else:
            xs = _mla_layer(xs, nw, sc1, sh1, g1, cos_c, sin_c, mla_w_in[i], mla_q_norm[i], mla_kv_norm[i],
                            mla_w_uq[i], mla_w_ukv[i], mla_w_out[i])
        xs = _peer_layer(xs, norm_ffn[layer].reshape(1, d), sc2, sh2, g2, peer_w_q[layer],
                         peer_sub_keys[layer], peer_u[layer], peer_v[layer])
    return _final_norm(xs, final_norm.reshape(1, d)).reshape(bsz, s, d)
```

```python
import functools
import math

import jax
import jax.numpy as jnp
from jax import lax
from jax.experimental import pallas as pl
from jax.experimental.pallas import tpu as pltpu

F32 = jnp.float32
BF16 = jnp.bfloat16
NEG_INF = float("-inf")
HIGHEST = lax.Precision.HIGHEST

D_MODEL = 1024
EPS = 1e-6
ROPE_THETA = 10000.0
LOG2_E = math.log2(math.e)

GDN_HEADS = 4
GDN_DK = 128
GDN_CHUNK = 64
GDN_QK = GDN_HEADS * GDN_DK
GDN_V = GDN_HEADS * 128
GDN_CONV_CH = 2 * GDN_QK + GDN_V
GDN_ROWS = 256

MOBA_HEADS = 4
MOBA_DH = 128
MOBA_BLOCK = 256
MOBA_TOPK = 3
MOBA_W = MOBA_HEADS * MOBA_DH

MLA_HEADS = 8
MLA_Q_RANK = 512
MLA_KV_RANK = 256
MLA_NOPE = 128
MLA_ROPE = 64
MLA_V = 128
MLA_QK = MLA_NOPE + MLA_ROPE
KEY_BLOCK = 256
MLA_Q_BLOCK = 512

PEER_HEADS = 8
PEER_NKEYS = 128
PEER_TOPK = 16
PEER_DSUB = 128
PEER_TOK = 512
PEER_EXP = 2048
PEER_TOPK_TOK = 256

LANES = 128
BF16_ROWS = 16
SUM_ROWS = BF16_ROWS
VMEM_LIMIT = 56 * 1024 * 1024


def _params(*sem):
    return pltpu.CompilerParams(dimension_semantics=sem, vmem_limit_bytes=VMEM_LIMIT)


def _dot(a, b, precision=None):
    return jnp.dot(a, b, preferred_element_type=F32, precision=precision)


def _dot_nt(a, b):
    return lax.dot_general(a, b, (((1,), (1,)), ((), ())), preferred_element_type=F32)


def _sigmoid(x):
    return jax.nn.sigmoid(x)


def _norm_mod(x, nw, sc, sh):
    y = x * lax.rsqrt(jnp.mean(x * x, axis=-1, keepdims=True) + EPS) * nw
    return y * (1.0 + sc) + sh


def _full(shape):
    n = len(shape)
    return pl.BlockSpec(shape, lambda *_: (0,) * n)


def _mod_kernel(c_ref, w_ref, b_ref, o_ref):
    c = c_ref[...]
    cond = c * _sigmoid(c)
    o_ref[0] = jnp.sum(cond * w_ref[0], axis=0, keepdims=True) + b_ref[0]


def _modulation(c_col, mod_w, mod_b):
    depth, d, n = mod_w.shape
    tn = 1536
    return pl.pallas_call(
        _mod_kernel,
        grid=(depth, n // tn),
        in_specs=[pl.BlockSpec((d, 1), lambda l, j: (0, 0)),
                  pl.BlockSpec((1, d, tn), lambda l, j: (l, 0, j)),
                  pl.BlockSpec((1, 1, tn), lambda l, j: (l, 0, j))],
        out_specs=pl.BlockSpec((1, 1, tn), lambda l, j: (l, 0, j)),
        out_shape=jax.ShapeDtypeStruct((depth, 1, n), F32),
        compiler_params=_params("parallel", "parallel"),
        name="adaln_modulation",
    )(c_col, mod_w, mod_b.reshape(depth, 1, n))


def _rope_kernel(pos_ref, fb_ref, fc_ref, cb_ref, sb_ref, cc_ref, sc_ref):
    p = pos_ref[...].astype(F32)
    lane = lax.broadcasted_iota(jnp.int32, cb_ref.shape, 1)
    ab = p * fb_ref[...]
    sb = jnp.sin(ab)
    cb_ref[...] = jnp.cos(ab)
    sb_ref[...] = jnp.where(lane < MOBA_DH // 2, -sb, sb)
    ac = p * fc_ref[...]
    sc = jnp.sin(ac)
    cc_ref[...] = jnp.cos(ac)
    sc_ref[...] = jnp.where(lane % MLA_ROPE < MLA_ROPE // 2, -sc, sc)


def _rope_tables(pos_col):
    s = pos_col.shape[0]
    tm = min(s, 1024)
    fb = ROPE_THETA ** (-jnp.arange(0, MOBA_DH, 2, dtype=F32) / MOBA_DH)
    fc = ROPE_THETA ** (-jnp.arange(0, MLA_ROPE, 2, dtype=F32) / MLA_ROPE)
    fb = jnp.tile(fb, 2).reshape(1, LANES)
    fc = jnp.tile(fc, 4).reshape(1, LANES)
    tab = pl.BlockSpec((tm, LANES), lambda i: (i, 0))
    return pl.pallas_call(
        _rope_kernel,
        grid=(s // tm,),
        in_specs=[pl.BlockSpec((tm, 1), lambda i: (i, 0)), _full((1, LANES)), _full((1, LANES))],
        out_specs=[tab] * 4,
        out_shape=[jax.ShapeDtypeStruct((s, LANES), F32)] * 4,
        compiler_params=_params("parallel"),
        name="rope_tables",
    )(pos_col, fb, fc)


def _rope_head128(x, cos, sin):
    return x * cos + pltpu.roll(x, MOBA_DH // 2, 1) * sin


def _rope_heads64(x, cos, sin):
    lane = lax.broadcasted_iota(jnp.int32, x.shape, 1)
    first = lane % MLA_ROPE < MLA_ROPE // 2
    partner = jnp.where(first, pltpu.roll(x, LANES - MLA_ROPE // 2, 1), pltpu.roll(x, MLA_ROPE // 2, 1))
    return x * cos + partner * sin


def _hy_proj_kernel(x_ref, nw_ref, sc_ref, sh_ref, wa_ref, wz_ref, wba_ref, wb_ref, cos_ref, sin_ref,
                    qkva_ref, z_ref, ba_ref, qb_ref, kb_ref, vb_ref):
    h = _norm_mod(x_ref[...], nw_ref[...], sc_ref[...], sh_ref[...]).astype(BF16)
    qkva_ref[...] = _dot(h, wa_ref[...])
    z_ref[...] = _dot(h, wz_ref[...])
    ba_ref[...] = _dot(h, wba_ref[...])
    qkvb = _dot(h, wb_ref[...])
    cos = cos_ref[...]
    sin = sin_ref[...]
    scale = MOBA_DH ** -0.5 * LOG2_E
    for hh in range(MOBA_HEADS):
        lo = hh * MOBA_DH
        q = qkvb[:, lo:lo + MOBA_DH]
        k = qkvb[:, MOBA_W + lo:MOBA_W + lo + MOBA_DH]
        qb_ref[hh, 0] = jnp.transpose(_rope_head128(q, cos, sin) * scale).astype(BF16)
        kb_ref[:, lo:lo + MOBA_DH] = _rope_head128(k, cos, sin).astype(BF16)
        v = qkvb[:, 2 * MOBA_W + lo:2 * MOBA_W + lo + MOBA_DH]
        vb_ref[hh, 0] = _with_sum_rows(jnp.transpose(v))


def _hy_proj(x, nw, sc, sh, wa, wz, wba, wb, cos, sin):
    s, d = x.shape
    tm = KEY_BLOCK
    row = lambda n: pl.BlockSpec((tm, n), lambda i: (i, 0))
    vec = _full((1, d))
    return pl.pallas_call(
        _hy_proj_kernel,
        grid=(s // tm,),
        in_specs=[row(d), vec, vec, vec, _full(wa.shape), _full(wz.shape), _full(wba.shape),
                  _full(wb.shape), row(LANES), row(LANES)],
        out_specs=[row(GDN_CONV_CH), row(GDN_V), row(LANES),
                   pl.BlockSpec((MOBA_HEADS, 1, MOBA_DH, tm), lambda i: (0, i, 0, 0)), row(MOBA_W),
                   pl.BlockSpec((MOBA_HEADS, 1, MOBA_DH + SUM_ROWS, tm), lambda i: (0, i, 0, 0))],
        out_shape=[jax.ShapeDtypeStruct((s, GDN_CONV_CH), F32),
                   jax.ShapeDtypeStruct((s, GDN_V), F32),
                   jax.ShapeDtypeStruct((s, LANES), F32),
                   jax.ShapeDtypeStruct((MOBA_HEADS, s // tm, MOBA_DH, tm), BF16),
                   jax.ShapeDtypeStruct((s, MOBA_W), BF16),
                   jax.ShapeDtypeStruct((MOBA_HEADS, s // tm, MOBA_DH + SUM_ROWS, tm), BF16)],
        compiler_params=_params("parallel"),
        name="hybrid_in_proj",
    )(x, nw, sc, sh, wa, wz, wba, wb, cos, sin)


def _unit_lower_inverses(l_stricts, eye):
    powers = [-l for l in l_stricts]
    invs = [eye + p for p in powers]
    p = 2
    while p < GDN_CHUNK:
        powers = [_dot(x.astype(BF16), x.astype(BF16)) for x in powers]
        invs = [_dot(inv.astype(BF16), (eye + x).astype(BF16)) for inv, x in zip(invs, powers)]
        p *= 2
    return invs


def _gdn_kernel(qkv_ref, ba_ref, z_ref, conv_ref, alog_ref, dtb_ref, onorm_ref, o_ref, ext_ref, state_ref):
    tb = GDN_ROWS
    cs = GDN_CHUNK

    @pl.when(pl.program_id(0) == 0)
    def _():
        ext_ref[0:8, :] = jnp.zeros((8, GDN_CONV_CH), F32)
        state_ref[...] = jnp.zeros_like(state_ref)

    ext_ref[8:8 + tb, :] = qkv_ref[...]
    cw = conv_ref[...]
    y = (ext_ref[5:5 + tb, :] * cw[0:1] + ext_ref[6:6 + tb, :] * cw[1:2]
         + ext_ref[7:7 + tb, :] * cw[2:3] + ext_ref[8:8 + tb, :] * cw[3:4])
    ext_ref[0:8, :] = ext_ref[tb:tb + 8, :]
    y = y * _sigmoid(y)

    ba = ba_ref[...]
    beta_all = _sigmoid(ba)
    t = ba + dtb_ref[...]
    softplus = jnp.maximum(t, 0.0) + jnp.log1p(jnp.exp(-jnp.abs(t)))
    g_all = -jnp.exp(alog_ref[...]) * softplus

    row = lax.broadcasted_iota(jnp.int32, (tb, tb), 0)
    col = lax.broadcasted_iota(jnp.int32, (tb, tb), 1)
    same = (row // cs) == (col // cs)
    causal = same & (col <= row)
    strict = same & (col < row)
    eye = (row == col).astype(F32)
    g_cum = _dot(causal.astype(F32), g_all, HIGHEST)
    g_tot = _dot(same.astype(F32), g_all, HIGHEST)
    g_cum_t = jnp.transpose(g_cum)
    z = z_ref[...]
    onorm = onorm_ref[...]

    heads = range(GDN_HEADS)
    rhs, l_strict, gtot, qk, qdec, kdec = [], [], [], [], [], []
    for h in heads:
        lo = h * GDN_DK
        q = y[:, lo:lo + GDN_DK]
        k = y[:, GDN_QK + lo:GDN_QK + lo + GDN_DK]
        v = y[:, 2 * GDN_QK + lo:2 * GDN_QK + lo + GDN_DK]
        q = q * lax.rsqrt(jnp.sum(q * q, axis=-1, keepdims=True) + EPS) * (GDN_DK ** -0.5)
        k = k * lax.rsqrt(jnp.sum(k * k, axis=-1, keepdims=True) + EPS)
        beta = beta_all[:, h:h + 1]
        gcol = g_cum[:, GDN_HEADS + h:GDN_HEADS + h + 1]
        grow = g_cum_t[GDN_HEADS + h:GDN_HEADS + h + 1, :]
        gtot.append(g_tot[:, GDN_HEADS + h:GDN_HEADS + h + 1])
        decay = jnp.exp(jnp.where(causal, gcol - grow, NEG_INF))
        kb = k * beta
        k16 = k.astype(BF16)
        l_strict.append(jnp.where(strict, _dot_nt(kb.astype(BF16), k16) * decay, 0.0))
        eg = jnp.exp(gcol)
        rhs.append(jnp.concatenate([v * beta, kb * eg], axis=1).astype(BF16))
        qk.append((_dot_nt(q.astype(BF16), k16) * decay).astype(BF16))
        qdec.append((q * eg).astype(BF16))
        kdec.append(k * jnp.exp(gtot[h] - gcol))

    uw = [_dot(t_inv.astype(BF16), rhs[h]) for h, t_inv in enumerate(_unit_lower_inverses(l_strict, eye))]
    u = [x[:, :GDN_DK] for x in uw]
    w = [x[:, GDN_DK:].astype(BF16) for x in uw]

    state = [state_ref[h] for h in heads]
    v_new = [[] for _ in heads]
    o_state = [[] for _ in heads]
    for c in range(tb // cs):
        r0 = c * cs
        for h in heads:
            s16 = state[h].astype(BF16)
            vn = u[h][r0:r0 + cs] - _dot(w[h][r0:r0 + cs], s16)
            o_state[h].append(_dot(qdec[h][r0:r0 + cs], s16))
            kd_t = jnp.transpose(kdec[h][r0:r0 + cs]).astype(BF16)
            state[h] = state[h] * jnp.exp(gtot[h][r0:r0 + 1]) + _dot(kd_t, vn.astype(BF16))
            v_new[h].append(vn)

    for h in heads:
        lo = h * GDN_DK
        state_ref[h] = state[h]
        o = jnp.concatenate(o_state[h], axis=0) + _dot(qk[h], jnp.concatenate(v_new[h], axis=0).astype(BF16))
        o = o * lax.rsqrt(jnp.mean(o * o, axis=-1, keepdims=True) + EPS) * onorm
        zh = z[:, lo:lo + GDN_DK]
        o_ref[:, lo:lo + GDN_DK] = (o * (zh * _sigmoid(zh))).astype(BF16)


def _gdn(qkva, ba, z, conv_w, alog, dtb, onorm):
    s = qkva.shape[0]
    tb = GDN_ROWS
    row = lambda n: pl.BlockSpec((tb, n), lambda i: (i, 0))
    return pl.pallas_call(
        _gdn_kernel,
        grid=(s // tb,),
        in_specs=[row(GDN_CONV_CH), row(LANES), row(GDN_V), _full(conv_w.shape),
                  _full((1, LANES)), _full((1, LANES)), _full((1, LANES))],
        out_specs=row(GDN_V),
        out_shape=jax.ShapeDtypeStruct((s, GDN_V), BF16),
        scratch_shapes=[pltpu.VMEM((tb + 8, GDN_CONV_CH), F32),
                        pltpu.VMEM((GDN_HEADS, GDN_DK, 128), F32)],
        compiler_params=_params("arbitrary"),
        name="gated_delta_rule",
    )(qkva, ba, z, conv_w, alog, dtb, onorm)


def _scores_into(s_ref, mx_ref, slot, k_tile, q_t, mask):
    s = _dot(k_tile, q_t)
    rows = s.shape[0]
    top = None
    for lo, piece in mask(s):
        s_ref[slot, lo:lo + piece.shape[0], :] = piece
        pmax = piece.max(axis=0, keepdims=True)
        top = pmax if top is None else jnp.maximum(top, pmax)
    if rows < s_ref.shape[1]:
        s_ref[slot, rows:, :] = jnp.full((s_ref.shape[1] - rows, s.shape[1]), NEG_INF, F32)
    mx_ref[slot] = top


def _values(p_ref, slot, v_at, blocks):
    v_t = jnp.concatenate([v_at(j) for j in blocks], axis=1)
    return _dot(v_t, p_ref[slot])


def _flash_steps(n_steps, first_blocks, first_keys, step_blocks, k_at, v_at, q_t, mask_first, mask_step,
                 s_ref, mx_ref, p_ref, acc_ref):
    per_step = len(first_blocks)
    _scores_into(s_ref, mx_ref, 0, k_at(first_blocks[0], first_keys), q_t, mask_first)
    m = mx_ref[0]
    p_ref[0] = jnp.exp2(s_ref[0] - m).astype(BF16)
    acc_ref[...] = jnp.zeros_like(acc_ref)
    _scores_into(s_ref, mx_ref, 1, k_at(step_blocks(1)[0], per_step), q_t, lambda s: mask_step(s, 1))

    def step(t, carry, cur):
        m, alpha_prev = carry
        nxt = jnp.minimum(t + 1, jnp.maximum(n_steps, 1))
        _scores_into(s_ref, mx_ref, 1 - cur, k_at(step_blocks(nxt)[0], per_step), q_t, lambda s: mask_step(s, nxt))
        m_new = jnp.maximum(m, mx_ref[cur])
        alpha = jnp.exp2(m - m_new)
        p = jnp.exp2(s_ref[cur] - m_new)
        prev = [jnp.where(t == 1, a, b) for a, b in zip(first_blocks, step_blocks(jnp.maximum(t - 1, 1)))]
        acc_ref[...] = alpha_prev * acc_ref[...] + _values(p_ref, 1 - cur, v_at, prev)
        p_ref[cur] = p.astype(BF16)
        return m_new, alpha

    def body(t, carry):
        return lax.cond(t % 2 == 0, lambda c: step(t, c, 0), lambda c: step(t, c, 1), carry)

    m, alpha = lax.fori_loop(1, n_steps + 1, body, (m, jnp.ones_like(m)))
    last = [jnp.where(n_steps == 0, a, b) for a, b in zip(first_blocks, step_blocks(jnp.maximum(n_steps, 1)))]
    acc = alpha * acc_ref[...] + _values(p_ref, n_steps % 2, v_at, last)
    dh = acc.shape[0] - SUM_ROWS
    return acc[:dh] / acc[dh:dh + 1]


def _with_sum_rows(v_t):
    extra = lax.broadcasted_iota(jnp.int32, (SUM_ROWS, v_t.shape[1]), 0) == 0
    return jnp.concatenate([v_t, extra.astype(F32)], axis=0).astype(BF16)


def _moba_kernel(q_ref, k_ref, v_ref, o_ref, kmean_ref, sel_ref, s_ref, mx_ref, p_ref, acc_ref, *, n_blocks):
    i = pl.program_id(1)
    bs = MOBA_BLOCK

    @pl.when(i == 0)
    def _():
        kmean_ref[...] = jnp.zeros_like(kmean_ref)

        def mean_body(n, carry):
            blk = k_ref[pl.ds(pl.multiple_of(n * bs, bs), bs), :].astype(F32)
            kmean_ref[pl.ds(n, 1), :] = jnp.mean(blk, axis=0, keepdims=True)
            return carry

        lax.fori_loop(0, n_blocks, mean_body, 0)

    q_t = jnp.concatenate([q_ref[0, 0], q_ref[0, 1]], axis=1)
    gate = _dot(kmean_ref[...].astype(BF16), q_t)
    blk_id = lax.broadcasted_iota(jnp.int32, gate.shape, 0)
    second = lax.broadcasted_iota(jnp.int32, gate.shape, 1) >= bs
    gate = jnp.where(blk_id < 2 * i + second.astype(jnp.int32), gate, NEG_INF)
    sel = jnp.zeros(gate.shape, F32)
    for _ in range(MOBA_TOPK):
        gmax = jnp.max(gate, axis=0, keepdims=True)
        first = jnp.min(jnp.where(gate == gmax, blk_id, LANES), axis=0, keepdims=True)
        hit = blk_id == first
        sel = jnp.where(hit & (gmax > NEG_INF), 1.0, sel)
        gate = jnp.where(hit, NEG_INF, gate)
    sel_ref[...] = sel

    def own_mask(s):
        key = lax.broadcasted_iota(jnp.int32, (bs, 2 * bs), 0)
        qry = lax.broadcasted_iota(jnp.int32, (bs, 2 * bs), 1)
        picked = sel_ref[pl.ds(2 * i, 1), :] > 0.0
        first_ok = (key <= qry) & ((qry < bs) | picked)
        return [(0, jnp.where(first_ok, s[:bs], NEG_INF)), (bs, jnp.where(key + bs <= qry, s[bs:], NEG_INF))]

    def sel_mask(s, t):
        return [(sub * bs, jnp.where(sel_ref[pl.ds(2 * (t - 1) + sub, 1), :] > 0.0, s[sub * bs:(sub + 1) * bs], NEG_INF))
                for sub in range(2)]

    out = _flash_steps(
        n_steps=i, first_blocks=[2 * i, 2 * i + 1], first_keys=2,
        step_blocks=lambda t: [2 * (t - 1), 2 * (t - 1) + 1],
        k_at=lambda j, n: k_ref[pl.ds(pl.multiple_of(j * bs, bs), n * bs), :], v_at=lambda j: v_ref[0, j],
        q_t=q_t, mask_first=own_mask, mask_step=sel_mask,
        s_ref=s_ref, mx_ref=mx_ref, p_ref=p_ref, acc_ref=acc_ref)
    o_ref[...] = jnp.transpose(out).astype(o_ref.dtype)


def _moba(q_t, k, v_t):
    s = k.shape[0]
    bs = MOBA_BLOCK
    n_blocks = s // bs
    assert n_blocks <= LANES and bs == KEY_BLOCK and n_blocks % 2 == 0
    bq = 2 * bs
    kv = pl.BlockSpec((s, MOBA_DH), lambda h, i: (0, h))
    return pl.pallas_call(
        functools.partial(_moba_kernel, n_blocks=n_blocks),
        grid=(MOBA_HEADS, n_blocks // 2),
        in_specs=[pl.BlockSpec((1, 2, MOBA_DH, bs), lambda h, i: (h, i, 0, 0)), kv,
                  pl.BlockSpec((1, n_blocks, MOBA_DH + SUM_ROWS, bs), lambda h, i: (h, 0, 0, 0))],
        out_specs=pl.BlockSpec((bq, MOBA_DH), lambda h, i: (i, h)),
        out_shape=jax.ShapeDtypeStruct((s, MOBA_W), BF16),
        scratch_shapes=[pltpu.VMEM((LANES, MOBA_DH), F32),
                        pltpu.VMEM((LANES, bq), F32),
                        pltpu.VMEM((2, 2 * KEY_BLOCK, bq), F32),
                        pltpu.VMEM((2, 1, bq), F32),
                        pltpu.VMEM((2, 2 * KEY_BLOCK, bq), BF16),
                        pltpu.VMEM((MOBA_DH + SUM_ROWS, bq), F32)],
        compiler_params=_params("parallel", "arbitrary"),
        name="moba_attention",
    )(q_t, k, v_t)


def _outproj_kernel(*refs, n_in):
    o_refs = refs[:n_in]
    w_refs = refs[n_in:2 * n_in]
    x_ref, g_ref, out_ref = refs[2 * n_in:]
    y = _dot(o_refs[0][...], w_refs[0][...])
    for o_ref, w_ref in zip(o_refs[1:], w_refs[1:]):
        y = y + _dot(o_ref[...], w_ref[...])
    out_ref[...] = x_ref[...] + g_ref[...] * y


def _outproj_residual(os_, ws, x, gate):
    s, d = x.shape
    tm = min(s, 512)
    row = lambda n: pl.BlockSpec((tm, n), lambda i: (i, 0))
    return pl.pallas_call(
        functools.partial(_outproj_kernel, n_in=len(os_)),
        grid=(s // tm,),
        in_specs=[row(o.shape[1]) for o in os_] + [_full(w.shape) for w in ws] + [row(d), _full((1, d))],
        out_specs=row(d),
        out_shape=jax.ShapeDtypeStruct((s, d), F32),
        compiler_params=_params("parallel"),
        name="out_proj_residual",
    )(*os_, *ws, x, gate)


def _mla_proj_kernel(x_ref, nw_ref, sc_ref, sh_ref, wcq_ref, wckv_ref, wkr_ref, qn_ref, kvn_ref,
                     wqn_ref, wqr_ref, wkn_ref, wv_ref, cos_ref, sin_ref, q_ref, k_ref, v_ref):
    h = _norm_mod(x_ref[...], nw_ref[...], sc_ref[...], sh_ref[...]).astype(BF16)
    cq = _dot(h, wcq_ref[...])
    ckv = _dot(h, wckv_ref[...])
    kr = _dot(h, wkr_ref[...])
    cq = (cq * lax.rsqrt(jnp.mean(cq * cq, axis=-1, keepdims=True) + EPS) * qn_ref[...]).astype(BF16)
    ckv = (ckv * lax.rsqrt(jnp.mean(ckv * ckv, axis=-1, keepdims=True) + EPS) * kvn_ref[...]).astype(BF16)
    cos = cos_ref[...]
    sin = sin_ref[...]
    scale = MLA_QK ** -0.5 * LOG2_E
    q_nope = _dot(cq, wqn_ref[...]) * scale
    q_rope = _dot(cq, wqr_ref[...])
    k_nope = _dot(ckv, wkn_ref[...])
    val = _dot(ckv, wv_ref[...])
    k_rope = _rope_heads64(kr, cos, sin)[:, :MLA_ROPE].astype(BF16)
    for pair in range(MLA_HEADS // 2):
        slab_t = jnp.transpose(_rope_heads64(q_rope[:, pair * LANES:(pair + 1) * LANES], cos, sin) * scale)
        for sub in range(2):
            hh = 2 * pair + sub
            q_ref[hh, 0, MLA_NOPE:, :] = slab_t[sub * MLA_ROPE:(sub + 1) * MLA_ROPE].astype(BF16)
    for hh in range(MLA_HEADS):
        q_ref[hh, 0, :MLA_NOPE, :] = jnp.transpose(q_nope[:, hh * MLA_NOPE:(hh + 1) * MLA_NOPE]).astype(BF16)
        k_ref[hh, :, :MLA_NOPE] = k_nope[:, hh * MLA_NOPE:(hh + 1) * MLA_NOPE].astype(BF16)
        k_ref[hh, :, MLA_NOPE:] = k_rope
        v_ref[hh, 0] = _with_sum_rows(jnp.transpose(val[:, hh * MLA_V:(hh + 1) * MLA_V]))


def _mla_proj(x, nw, sc, sh, wcq, wckv, wkr, qn, kvn, wqn, wqr, wkn, wv, cos, sin):
    s, d = x.shape
    tm = KEY_BLOCK
    row = lambda n: pl.BlockSpec((tm, n), lambda i: (i, 0))
    vec = _full((1, d))
    heads = lambda n: pl.BlockSpec((MLA_HEADS, tm, n), lambda i: (0, i, 0))
    return pl.pallas_call(
        _mla_proj_kernel,
        grid=(s // tm,),
        in_specs=[row(d), vec, vec, vec, _full(wcq.shape), _full(wckv.shape), _full(wkr.shape),
                  _full(qn.shape), _full(kvn.shape), _full(wqn.shape), _full(wqr.shape),
                  _full(wkn.shape), _full(wv.shape), row(LANES), row(LANES)],
        out_specs=[pl.BlockSpec((MLA_HEADS, 1, MLA_QK, tm), lambda i: (0, i, 0, 0)), heads(MLA_QK),
                   pl.BlockSpec((MLA_HEADS, 1, MLA_V + SUM_ROWS, tm), lambda i: (0, i, 0, 0))],
        out_shape=[jax.ShapeDtypeStruct((MLA_HEADS, s // tm, MLA_QK, tm), BF16),
                   jax.ShapeDtypeStruct((MLA_HEADS, s, MLA_QK), BF16),
                   jax.ShapeDtypeStruct((MLA_HEADS, s // tm, MLA_V + SUM_ROWS, tm), BF16)],
        compiler_params=_params("parallel"),
        name="mla_in_proj",
    )(x, nw, sc, sh, wcq, wckv, wkr, qn, kvn, wqn, wqr, wkn, wv, cos, sin)


def _mla_attn_kernel(q_ref, k_ref, v_ref, o_ref, s_ref, mx_ref, p_ref, acc_ref):
    i = pl.program_id(1)

    def causal(s):
        key = lax.broadcasted_iota(jnp.int32, s.shape, 0)
        qry = lax.broadcasted_iota(jnp.int32, s.shape, 1)
        return [(0, jnp.where(key <= qry, s, NEG_INF))]

    per_q = MLA_Q_BLOCK // KEY_BLOCK
    q_t = jnp.concatenate([q_ref[0, b] for b in range(per_q)], axis=1)
    out = _flash_steps(
        n_steps=i, first_blocks=[2 * i, 2 * i + 1], first_keys=2,
        step_blocks=lambda t: [2 * (t - 1), 2 * (t - 1) + 1],
        k_at=lambda j, n: k_ref[0, pl.ds(pl.multiple_of(j * KEY_BLOCK, KEY_BLOCK), n * KEY_BLOCK), :],
        v_at=lambda j: v_ref[0, j], q_t=q_t, mask_first=causal, mask_step=lambda s, t: [(0, s)],
        s_ref=s_ref, mx_ref=mx_ref, p_ref=p_ref, acc_ref=acc_ref)
    o_ref[...] = jnp.transpose(out).astype(o_ref.dtype)


def _mla_attention(q_t, k, v_t):
    _, s, _ = k.shape
    bq = MLA_Q_BLOCK
    per_q = bq // KEY_BLOCK
    assert per_q == 2
    return pl.pallas_call(
        _mla_attn_kernel,
        grid=(MLA_HEADS, s // bq),
        in_specs=[pl.BlockSpec((1, per_q, MLA_QK, KEY_BLOCK), lambda h, i: (h, i, 0, 0)),
                  pl.BlockSpec((1, s, MLA_QK), lambda h, i: (h, 0, 0)),
                  pl.BlockSpec((1, s // KEY_BLOCK, MLA_V + SUM_ROWS, KEY_BLOCK), lambda h, i: (h, 0, 0, 0))],
        out_specs=pl.BlockSpec((bq, MLA_V), lambda h, i: (i, h)),
        out_shape=jax.ShapeDtypeStruct((s, MLA_HEADS * MLA_V), BF16),
        scratch_shapes=[pltpu.VMEM((2, 2 * KEY_BLOCK, bq), F32), pltpu.VMEM((2, 1, bq), F32),
                        pltpu.VMEM((2, 2 * KEY_BLOCK, bq), BF16), pltpu.VMEM((MLA_V + SUM_ROWS, bq), F32)],
        compiler_params=_params("parallel", "arbitrary"),
        name="mla_attention",
    )(q_t, k, v_t)


def _peer_proj_kernel(x_ref, nw_ref, sc_ref, sh_ref, wq_ref, keys_ref, h_ref, s_ref):
    h = _norm_mod(x_ref[...], nw_ref[...], sc_ref[...], sh_ref[...]).astype(BF16)
    h_ref[...] = h
    q = _dot(h, wq_ref[...]).astype(BF16)
    for hp in range(2 * PEER_HEADS):
        s_ref[hp] = _dot_nt(keys_ref[hp], q[:, hp * PEER_DSUB:(hp + 1) * PEER_DSUB])


def _peer_proj(x, nw, sc, sh, wq, keys):
    s, d = x.shape
    tm = min(s, 256)
    vec = _full((1, d))
    return pl.pallas_call(
        _peer_proj_kernel,
        grid=(s // tm,),
        in_specs=[pl.BlockSpec((tm, d), lambda i: (i, 0)), vec, vec, vec, _full(wq.shape), _full(keys.shape)],
        out_specs=[pl.BlockSpec((tm, d), lambda i: (i, 0)),
                   pl.BlockSpec((2 * PEER_HEADS, PEER_NKEYS, tm), lambda i: (0, 0, i))],
        out_shape=[jax.ShapeDtypeStruct((s, d), BF16),
                   jax.ShapeDtypeStruct((2 * PEER_HEADS, PEER_NKEYS, s), F32)],
        compiler_params=_params("parallel"),
        name="peer_query_scores",
    )(x, nw, sc, sh, wq, keys)


def _compare_exchange(vs, i, l):
    vs[i], vs[l] = jnp.maximum(vs[i], vs[l]), jnp.minimum(vs[i], vs[l])


def _sort_desc(vs):
    vs = list(vs)
    n = len(vs)
    k = 2
    while k <= n:
        j = k // 2
        while j >= 1:
            for i in range(n):
                l = i ^ j
                if l > i:
                    if i & k == 0:
                        _compare_exchange(vs, i, l)
                    else:
                        _compare_exchange(vs, l, i)
            j //= 2
        k *= 2
    return vs


def _merge_top(a, b):
    n = len(a)
    vs = [jnp.maximum(a[i], b[n - 1 - i]) for i in range(n)]
    j = n // 2
    while j >= 1:
        for i in range(n):
            if i ^ j > i:
                _compare_exchange(vs, i, i ^ j)
        j //= 2
    return vs


def _top16_of_keys(groups):
    vs = _sort_desc(groups)
    for shift in (4, 2, 1):
        vs = _merge_top(vs, [pltpu.roll(v, shift, 0) for v in vs])
    return vs


def _peer_topk_kernel(s_ref, a_ref, n_ref, b_ref, rank_ref):
    sub = 8
    groups = PEER_NKEYS // sub

    def head_body(h, carry):
        g1 = [s_ref[2 * h, g * sub:(g + 1) * sub, :] for g in range(groups)]
        g2 = [s_ref[2 * h + 1, g * sub:(g + 1) * sub, :] for g in range(groups)]
        v1 = _top16_of_keys(g1)
        v2 = _top16_of_keys(g2)

        ninf = jnp.full_like(v1[0], NEG_INF)
        pad = lambda vs: vs + [ninf] * (PEER_TOPK - len(vs))
        row = lambda r1: [v1[r1] + v2[r2] for r2 in range(PEER_TOPK // (r1 + 1))]
        top = _merge_top(
            _merge_top(row(0), _merge_top(pad(row(1)), pad(row(2)))),
            _merge_top(_sort_desc(pad(row(3) + row(4) + row(5) + row(6) + row(7))),
                       pad([row(r1)[0] for r1 in range(8, PEER_TOPK)])))
        thr = top[PEER_TOPK - 1]
        z = jnp.ones_like(thr)
        for r in range(1, PEER_TOPK):
            z = z + jnp.exp(top[r] - top[0])
        half_inv_z = 0.5 / z

        b_all, rank_all = [], []
        for g in range(groups):
            rows = slice(g * sub, (g + 1) * sub)
            n = sum(jnp.where(g1[g] + v2[r2] >= thr, 1.0, 0.0) for r2 in range(PEER_TOPK))
            a_ref[h, rows, :] = jnp.exp(g1[g] - v1[0]) * half_inv_z
            n_ref[h, rows, :] = n
            b_all.append(jnp.exp(g2[g] - v2[0]))
            rank_all.append(sum(jnp.where(v2[r] > g2[g], 1.0, 0.0) for r in range(PEER_TOPK)))
        for pair in range(groups // 2):
            b_ref[h, pair] = jnp.concatenate(b_all[2 * pair:2 * pair + 2], axis=0).astype(BF16)
            rank_ref[h, pair] = jnp.concatenate(rank_all[2 * pair:2 * pair + 2], axis=0).astype(BF16)
        return carry

    lax.fori_loop(0, PEER_HEADS, head_body, 0)


def _peer_topk(scores):
    _, nk, s = scores.shape
    tt = min(s, PEER_TOPK_TOK)
    big = pl.BlockSpec((PEER_HEADS, nk, tt), lambda i: (0, 0, i))
    tiled = pl.BlockSpec((PEER_HEADS, nk // BF16_ROWS, BF16_ROWS, tt), lambda i: (0, 0, 0, i))
    shape = (PEER_HEADS, nk, s)
    shape16 = (PEER_HEADS, nk // BF16_ROWS, BF16_ROWS, s)
    return pl.pallas_call(
        _peer_topk_kernel,
        grid=(s // tt,),
        in_specs=[pl.BlockSpec((2 * PEER_HEADS, nk, tt), lambda i: (0, 0, i))],
        out_specs=[big, big, tiled, tiled],
        out_shape=[jax.ShapeDtypeStruct(shape, F32), jax.ShapeDtypeStruct(shape, F32),
                   jax.ShapeDtypeStruct(shape16, BF16), jax.ShapeDtypeStruct(shape16, BF16)],
        compiler_params=_params("parallel"),
        name="peer_topk_threshold",
    )(scores)


def _peer_dense_kernel(h_ref, a_ref, n_ref, b_ref, rank_ref, u_ref, vt_ref, x_ref, g_ref, o_ref,
                       acc_ref, p_ref):
    j = pl.program_id(1)
    nk = PEER_NKEYS

    @pl.when(j == 0)
    def _():
        acc_ref[...] = jnp.zeros_like(acc_ref)

    act_t = _dot_nt(u_ref[...], h_ref[...])
    for blk in range(PEER_EXP // nk):
        i1 = j * (PEER_EXP // nk) + blk
        wgt = jnp.zeros((nk // BF16_ROWS, BF16_ROWS, PEER_TOK), BF16)
        for h in range(PEER_HEADS):
            row = lambda ref: jnp.broadcast_to(ref[h, pl.ds(i1, 1), :], (BF16_ROWS, PEER_TOK)).astype(BF16)[None]
            chosen = rank_ref[h] < row(n_ref)
            wgt = wgt + jnp.where(chosen, b_ref[h], 0.0) * row(a_ref)
        act = act_t[blk * nk:(blk + 1) * nk]
        gelu2 = act * (1.0 + lax.erf(act * (2.0 ** -0.5)))
        p_ref[blk * nk:(blk + 1) * nk, :] = wgt.reshape(nk, PEER_TOK) * gelu2.astype(BF16)
    acc_ref[...] += _dot(vt_ref[...], p_ref[...])

    @pl.when(j == pl.num_programs(1) - 1)
    def _():
        o_ref[...] = x_ref[...] + g_ref[...] * jnp.transpose(acc_ref[...])


def _peer_dense(h2, a, n, b, rank, u16, vt16, x, gate):
    s, d = x.shape
    n_exp = u16.shape[0]
    tt = min(s, PEER_TOK)
    assert tt == PEER_TOK
    tok3 = pl.BlockSpec((PEER_HEADS, PEER_NKEYS, tt), lambda i, j: (0, 0, i))
    tok4 = pl.BlockSpec((PEER_HEADS, PEER_NKEYS // BF16_ROWS, BF16_ROWS, tt), lambda i, j: (0, 0, 0, i))
    return pl.pallas_call(
        _peer_dense_kernel,
        grid=(s // tt, n_exp // PEER_EXP),
        in_specs=[pl.BlockSpec((tt, d), lambda i, j: (i, 0)),
                  tok3, tok3, tok4, tok4,
                  pl.BlockSpec((PEER_EXP, d), lambda i, j: (j, 0)),
                  pl.BlockSpec((d, PEER_EXP), lambda i, j: (0, j)),
                  pl.BlockSpec((tt, d), lambda i, j: (i, 0)),
                  pl.BlockSpec((1, d), lambda i, j: (0, 0))],
        out_specs=pl.BlockSpec((tt, d), lambda i, j: (i, 0)),
        out_shape=jax.ShapeDtypeStruct((s, d), F32),
        scratch_shapes=[pltpu.VMEM((d, tt), F32), pltpu.VMEM((PEER_EXP, tt), BF16)],
        compiler_params=_params("parallel", "arbitrary"),
        name="peer_dense_experts",
    )(h2, a, n, b, rank, u16, vt16, x, gate)


def _final_norm_kernel(x_ref, w_ref, o_ref):
    x = x_ref[...]
    o_ref[...] = x * lax.rsqrt(jnp.mean(x * x, axis=-1, keepdims=True) + EPS) * w_ref[...]


def _final_norm(x, w):
    s, d = x.shape
    tm = min(s, 1024)
    return pl.pallas_call(
        _final_norm_kernel,
        grid=(s // tm,),
        in_specs=[pl.BlockSpec((tm, d), lambda i: (i, 0)), _full((1, d))],
        out_specs=pl.BlockSpec((tm, d), lambda i: (i, 0)),
        out_shape=jax.ShapeDtypeStruct((s, d), F32),
        compiler_params=_params("parallel"),
        name="final_rmsnorm",
    )(x, w)


def _lane_row(values, offset):
    return jnp.zeros((1, LANES), F32).at[0, offset:offset + values.shape[0]].set(values.astype(F32))


def _hybrid_layer(x, nw, sc, sh, gate, cos, sin, w_in, conv_w, a_log, dt_bias, o_norm, w_out):
    w16 = w_in.astype(BF16)
    c0 = GDN_CONV_CH
    c1 = c0 + GDN_V
    c2 = c1 + 2 * GDN_HEADS
    wba = jnp.pad(w16[:, c1:c2], ((0, 0), (0, LANES - 2 * GDN_HEADS)))
    qkva, z, ba, qb, kb, vb = _hy_proj(x, nw, sc, sh, w16[:, :c0], w16[:, c0:c1], wba, w16[:, c2:], cos, sin)
    o_a = _gdn(qkva, ba, z, conv_w, _lane_row(a_log, GDN_HEADS), _lane_row(dt_bias, GDN_HEADS),
               o_norm.reshape(1, -1))
    o_b = _moba(qb, kb, vb)
    wo16 = w_out.astype(BF16)
    return _outproj_residual([o_a, o_b], [wo16[:GDN_V], wo16[GDN_V:]], x, gate)


def _mla_layer(x, nw, sc, sh, gate, cos, sin, w_in, q_norm, kv_norm, w_uq, w_ukv, w_out):
    w16 = w_in.astype(BF16)
    r0 = MLA_Q_RANK
    r1 = r0 + MLA_KV_RANK
    wkr = jnp.pad(w16[:, r1:], ((0, 0), (0, LANES - MLA_ROPE)))
    wuq = w_uq.astype(BF16).reshape(MLA_Q_RANK, MLA_HEADS, MLA_QK)
    wqn = wuq[:, :, :MLA_NOPE].reshape(MLA_Q_RANK, MLA_HEADS * MLA_NOPE)
    wqr = wuq[:, :, MLA_NOPE:].reshape(MLA_Q_RANK, MLA_HEADS * MLA_ROPE)
    wukv = w_ukv.astype(BF16).reshape(MLA_KV_RANK, MLA_HEADS, MLA_NOPE + MLA_V)
    wkn = wukv[:, :, :MLA_NOPE].reshape(MLA_KV_RANK, MLA_HEADS * MLA_NOPE)
    wv = wukv[:, :, MLA_NOPE:].reshape(MLA_KV_RANK, MLA_HEADS * MLA_V)
    q, k, v = _mla_proj(x, nw, sc, sh, w16[:, :r0], w16[:, r0:r1], wkr, q_norm.reshape(1, -1),
                        kv_norm.reshape(1, -1), wqn, wqr, wkn, wv, cos, sin)
    o = _mla_attention(q, k, v)
    return _outproj_residual([o], [w_out.astype(BF16)], x, gate)


def _peer_layer(x, nw, sc, sh, gate, w_q, sub_keys, u, v):
    keys = sub_keys.astype(BF16).reshape(2 * PEER_HEADS, PEER_NKEYS, PEER_DSUB)
    h2, scores = _peer_proj(x, nw, sc, sh, w_q.astype(BF16), keys)
    a, n, b, rank = _peer_topk(scores)
    return _peer_dense(h2, a, n, b, rank, u.astype(BF16), v.astype(BF16).T, x, gate)


def kernel(x, c, positions, mod_w, mod_b, norm_mix, norm_ffn, hy_w_in, gdn_conv, gdn_a_log, gdn_dt_bias, gdn_o_norm, hy_w_out, mla_w_in, mla_q_norm, mla_kv_norm, mla_w_uq, mla_w_ukv, mla_w_out, peer_w_q, peer_sub_keys, peer_u, peer_v, final_norm):
    bsz, s, d = x.shape
    assert bsz == 1 and d == D_MODEL
    depth = mod_w.shape[0]
    xs = x.reshape(s, d)
    mod = _modulation(c.reshape(d, 1), mod_w, mod_b)
    cos_b, sin_b, cos_c, sin_c = _rope_tables(positions.reshape(s, 1))
    for layer in range(depth):
        sh1, sc1, g1, sh2, sc2, g2 = (mod[layer, :, n * d:(n + 1) * d] for n in range(6))
        nw = norm_mix[layer].reshape(1, d)
        i = layer // 2
        if layer % 2 == 0:
            xs = _hybrid_layer(xs, nw, sc1, sh1, g1, cos_b, sin_b, hy_w_in[i], gdn_conv[i], gdn_a_log[i],
                               gdn_dt_bias[i], gdn_o_norm[i], hy_w_out[i])
        else:
            xs = _mla_layer(xs, nw, sc1, sh1, g1, cos_c, sin_c, mla_w_in[i], mla_q_norm[i], mla_kv_norm[i],
                            mla_w_uq[i], mla_w_ukv[i], mla_w_out[i])
        xs = _peer_layer(xs, norm_ffn[layer].reshape(1, d), sc2, sh2, g2, peer_w_q[layer],
                         peer_sub_keys[layer], peer_u[layer], peer_v[layer])
    return _final_norm(xs, final_norm.reshape(1, d)).reshape(bsz, s, d)
```

```python
import functools
import math

import jax
import jax.numpy as jnp
from jax import lax
from jax.experimental import pallas as pl
from jax.experimental.pallas import tpu as pltpu

F32 = jnp.float32
BF16 = jnp.bfloat16
NEG_INF = float("-inf")
HIGHEST = lax.Precision.HIGHEST

D_MODEL = 1024
EPS = 1e-6
ROPE_THETA = 10000.0
LOG2_E = math.log2(math.e)

GDN_HEADS = 4
GDN_DK = 128
GDN_CHUNK = 64
GDN_QK = GDN_HEADS * GDN_DK
GDN_V = GDN_HEADS * 128
GDN_CONV_CH = 2 * GDN_QK + GDN_V
GDN_ROWS = 256

MOBA_HEADS = 4
MOBA_DH = 128
MOBA_BLOCK = 256
MOBA_TOPK = 3
MOBA_W = MOBA_HEADS * MOBA_DH

MLA_HEADS = 8
MLA_Q_RANK = 512
MLA_KV_RANK = 256
MLA_NOPE = 128
MLA_ROPE = 64
MLA_V = 128
MLA_QK = MLA_NOPE + MLA_ROPE
KEY_BLOCK = 256
MLA_Q_BLOCK = 512

PEER_HEADS = 8
PEER_NKEYS = 128
PEER_TOPK = 16
PEER_DSUB = 128
PEER_TOK = 512
PEER_EXP = 2048
PEER_TOPK_TOK = 256

LANES = 128
BF16_ROWS = 16
SUM_ROWS = BF16_ROWS
VMEM_LIMIT = 56 * 1024 * 1024


def _params(*sem):
    return pltpu.CompilerParams(dimension_semantics=sem, vmem_limit_bytes=VMEM_LIMIT)


def _dot(a, b, precision=None):
    return jnp.dot(a, b, preferred_element_type=F32, precision=precision)


def _dot_nt(a, b):
    return lax.dot_general(a, b, (((1,), (1,)), ((), ())), preferred_element_type=F32)


def _sigmoid(x):
    return jax.nn.sigmoid(x)


def _norm_mod(x, nw, sc, sh):
    y = x * lax.rsqrt(jnp.mean(x * x, axis=-1, keepdims=True) + EPS) * nw
    return y * (1.0 + sc) + sh


def _full(shape):
    n = len(shape)
    return pl.BlockSpec(shape, lambda *_: (0,) * n)


def _mod_kernel(c_ref, w_ref, b_ref, o_ref):
    c = c_ref[...]
    cond = c * _sigmoid(c)
    o_ref[0] = jnp.sum(cond * w_ref[0], axis=0, keepdims=True) + b_ref[0]


def _modulation(c_col, mod_w, mod_b):
    depth, d, n = mod_w.shape
    tn = 1536
    return pl.pallas_call(
        _mod_kernel,
        grid=(depth, n // tn),
        in_specs=[pl.BlockSpec((d, 1), lambda l, j: (0, 0)),
                  pl.BlockSpec((1, d, tn), lambda l, j: (l, 0, j)),
                  pl.BlockSpec((1, 1, tn), lambda l, j: (l, 0, j))],
        out_specs=pl.BlockSpec((1, 1, tn), lambda l, j: (l, 0, j)),
        out_shape=jax.ShapeDtypeStruct((depth, 1, n), F32),
        compiler_params=_params("parallel", "parallel"),
        name="adaln_modulation",
    )(c_col, mod_w, mod_b.reshape(depth, 1, n))


def _rope_kernel(pos_ref, fb_ref, fc_ref, cb_ref, sb_ref, cc_ref, sc_ref):
    p = pos_ref[...].astype(F32)
    lane = lax.broadcasted_iota(jnp.int32, cb_ref.shape, 1)
    ab = p * fb_ref[...]
    sb = jnp.sin(ab)
    cb_ref[...] = jnp.cos(ab)
    sb_ref[...] = jnp.where(lane < MOBA_DH // 2, -sb, sb)
    ac = p * fc_ref[...]
    sc = jnp.sin(ac)
    cc_ref[...] = jnp.cos(ac)
    sc_ref[...] = jnp.where(lane % MLA_ROPE < MLA_ROPE // 2, -sc, sc)


def _rope_tables(pos_col):
    s = pos_col.shape[0]
    tm = min(s, 1024)
    fb = ROPE_THETA ** (-jnp.arange(0, MOBA_DH, 2, dtype=F32) / MOBA_DH)
    fc = ROPE_THETA ** (-jnp.arange(0, MLA_ROPE, 2, dtype=F32) / MLA_ROPE)
    fb = jnp.tile(fb, 2).reshape(1, LANES)
    fc = jnp.tile(fc, 4).reshape(1, LANES)
    tab = pl.BlockSpec((tm, LANES), lambda i: (i, 0))
    return pl.pallas_call(
        _rope_kernel,
        grid=(s // tm,),
        in_specs=[pl.BlockSpec((tm, 1), lambda i: (i, 0)), _full((1, LANES)), _full((1, LANES))],
        out_specs=[tab] * 4,
        out_shape=[jax.ShapeDtypeStruct((s, LANES), F32)] * 4,
        compiler_params=_params("parallel"),
        name="rope_tables",
    )(pos_col, fb, fc)


def _rope_head128(x, cos, sin):
    return x * cos + pltpu.roll(x, MOBA_DH // 2, 1) * sin


def _rope_heads64(x, cos, sin):
    lane = lax.broadcasted_iota(jnp.int32, x.shape, 1)
    first = lane % MLA_ROPE < MLA_ROPE // 2
    partner = jnp.where(first, pltpu.roll(x, LANES - MLA_ROPE // 2, 1), pltpu.roll(x, MLA_ROPE // 2, 1))
    return x * cos + partner * sin


def _hy_proj_kernel(x_ref, nw_ref, sc_ref, sh_ref, wa_ref, wz_ref, wba_ref, wb_ref, cos_ref, sin_ref,
                    qkva_ref, z_ref, ba_ref, qb_ref, kb_ref, vb_ref):
    h = _norm_mod(x_ref[...], nw_ref[...], sc_ref[...], sh_ref[...]).astype(BF16)
    qkva_ref[...] = _dot(h, wa_ref[...])
    z_ref[...] = _dot(h, wz_ref[...])
    ba_ref[...] = _dot(h, wba_ref[...])
    qkvb = _dot(h, wb_ref[...])
    cos = cos_ref[...]
    sin = sin_ref[...]
    scale = MOBA_DH ** -0.5 * LOG2_E
    for hh in range(MOBA_HEADS):
        lo = hh * MOBA_DH
        q = qkvb[:, lo:lo + MOBA_DH]
        k = qkvb[:, MOBA_W + lo:MOBA_W + lo + MOBA_DH]
        qb_ref[hh, 0] = jnp.transpose(_rope_head128(q, cos, sin) * scale).astype(BF16)
        kb_ref[:, lo:lo + MOBA_DH] = _rope_head128(k, cos, sin).astype(BF16)
        v = qkvb[:, 2 * MOBA_W + lo:2 * MOBA_W + lo + MOBA_DH]
        vb_ref[hh, 0] = _with_sum_rows(jnp.transpose(v))


def _hy_proj(x, nw, sc, sh, wa, wz, wba, wb, cos, sin):
    s, d = x.shape
    tm = KEY_BLOCK
    row = lambda n: pl.BlockSpec((tm, n), lambda i: (i, 0))
    vec = _full((1, d))
    return pl.pallas_call(
        _hy_proj_kernel,
        grid=(s // tm,),
        in_specs=[row(d), vec, vec, vec, _full(wa.shape), _full(wz.shape), _full(wba.shape),
                  _full(wb.shape), row(LANES), row(LANES)],
        out_specs=[row(GDN_CONV_CH), row(GDN_V), row(LANES),
                   pl.BlockSpec((MOBA_HEADS, 1, MOBA_DH, tm), lambda i: (0, i, 0, 0)), row(MOBA_W),
                   pl.BlockSpec((MOBA_HEADS, 1, MOBA_DH + SUM_ROWS, tm), lambda i: (0, i, 0, 0))],
        out_shape=[jax.ShapeDtypeStruct((s, GDN_CONV_CH), F32),
                   jax.ShapeDtypeStruct((s, GDN_V), F32),
                   jax.ShapeDtypeStruct((s, LANES), F32),
                   jax.ShapeDtypeStruct((MOBA_HEADS, s // tm, MOBA_DH, tm), BF16),
                   jax.ShapeDtypeStruct((s, MOBA_W), BF16),
                   jax.ShapeDtypeStruct((MOBA_HEADS, s // tm, MOBA_DH + SUM_ROWS, tm), BF16)],
        compiler_params=_params("parallel"),
        name="hybrid_in_proj",
    )(x, nw, sc, sh, wa, wz, wba, wb, cos, sin)


def _unit_lower_inverses(l_stricts, eye):
    powers = [-l for l in l_stricts]
    invs = [eye + p for p in powers]
    p = 2
    while p < GDN_CHUNK:
        powers = [_dot(x.astype(BF16), x.astype(BF16)) for x in powers]
        invs = [_dot(inv.astype(BF16), (eye + x).astype(BF16)) for inv, x in zip(invs, powers)]
        p *= 2
    return invs


def _gdn_kernel(qkv_ref, ba_ref, z_ref, conv_ref, alog_ref, dtb_ref, onorm_ref, o_ref, ext_ref, state_ref):
    tb = GDN_ROWS
    cs = GDN_CHUNK

    @pl.when(pl.program_id(0) == 0)
    def _():
        ext_ref[0:8, :] = jnp.zeros((8, GDN_CONV_CH), F32)
        state_ref[...] = jnp.zeros_like(state_ref)

    ext_ref[8:8 + tb, :] = qkv_ref[...]
    cw = conv_ref[...]
    y = (ext_ref[5:5 + tb, :] * cw[0:1] + ext_ref[6:6 + tb, :] * cw[1:2]
         + ext_ref[7:7 + tb, :] * cw[2:3] + ext_ref[8:8 + tb, :] * cw[3:4])
    ext_ref[0:8, :] = ext_ref[tb:tb + 8, :]
    y = y * _sigmoid(y)

    ba = ba_ref[...]
    beta_all = _sigmoid(ba)
    t = ba + dtb_ref[...]
    softplus = jnp.maximum(t, 0.0) + jnp.log1p(jnp.exp(-jnp.abs(t)))
    g_all = -jnp.exp(alog_ref[...]) * softplus

    row = lax.broadcasted_iota(jnp.int32, (tb, tb), 0)
    col = lax.broadcasted_iota(jnp.int32, (tb, tb), 1)
    same = (row // cs) == (col // cs)
    causal = same & (col <= row)
    strict = same & (col < row)
    eye = (row == col).astype(F32)
    g_cum = _dot(causal.astype(F32), g_all, HIGHEST)
    g_tot = _dot(same.astype(F32), g_all, HIGHEST)
    g_cum_t = jnp.transpose(g_cum)
    z = z_ref[...]
    onorm = onorm_ref[...]

    heads = range(GDN_HEADS)
    rhs, l_strict, gtot, qk, qdec, kdec = [], [], [], [], [], []
    for h in heads:
        lo = h * GDN_DK
        q = y[:, lo:lo + GDN_DK]
        k = y[:, GDN_QK + lo:GDN_QK + lo + GDN_DK]
        v = y[:, 2 * GDN_QK + lo:2 * GDN_QK + lo + GDN_DK]
        q = q * lax.rsqrt(jnp.sum(q * q, axis=-1, keepdims=True) + EPS) * (GDN_DK ** -0.5)
        k = k * lax.rsqrt(jnp.sum(k * k, axis=-1, keepdims=True) + EPS)
        beta = beta_all[:, h:h + 1]
        gcol = g_cum[:, GDN_HEADS + h:GDN_HEADS + h + 1]
        grow = g_cum_t[GDN_HEADS + h:GDN_HEADS + h + 1, :]
        gtot.append(g_tot[:, GDN_HEADS + h:GDN_HEADS + h + 1])
        decay = jnp.exp(jnp.where(causal, gcol - grow, NEG_INF))
        kb = k * beta
        k16 = k.astype(BF16)
        l_strict.append(jnp.where(strict, _dot_nt(kb.astype(BF16), k16) * decay, 0.0))
        eg = jnp.exp(gcol)
        rhs.append(jnp.concatenate([v * beta, kb * eg], axis=1).astype(BF16))
        qk.append((_dot_nt(q.astype(BF16), k16) * decay).astype(BF16))
        qdec.append((q * eg).astype(BF16))
        kdec.append(k * jnp.exp(gtot[h] - gcol))

    uw = [_dot(t_inv.astype(BF16), rhs[h]) for h, t_inv in enumerate(_unit_lower_inverses(l_strict, eye))]
    u = [x[:, :GDN_DK] for x in uw]
    w = [x[:, GDN_DK:].astype(BF16) for x in uw]

    state = [state_ref[h] for h in heads]
    v_new = [[] for _ in heads]
    o_state = [[] for _ in heads]
    for c in range(tb // cs):
        r0 = c * cs
        for h in heads:
            s16 = state[h].astype(BF16)
            vn = u[h][r0:r0 + cs] - _dot(w[h][r0:r0 + cs], s16)
            o_state[h].append(_dot(qdec[h][r0:r0 + cs], s16))
            kd_t = jnp.transpose(kdec[h][r0:r0 + cs]).astype(BF16)
            state[h] = state[h] * jnp.exp(gtot[h][r0:r0 + 1]) + _dot(kd_t, vn.astype(BF16))
            v_new[h].append(vn)

    for h in heads:
        lo = h * GDN_DK
        state_ref[h] = state[h]
        o = jnp.concatenate(o_state[h], axis=0) + _dot(qk[h], jnp.concatenate(v_new[h], axis=0).astype(BF16))
        o = o * lax.rsqrt(jnp.mean(o * o, axis=-1, keepdims=True) + EPS) * onorm
        zh = z[:, lo:lo + GDN_DK]
        o_ref[:, lo:lo + GDN_DK] = (o * (zh * _sigmoid(zh))).astype(BF16)


def _gdn(qkva, ba, z, conv_w, alog, dtb, onorm):
    s = qkva.shape[0]
    tb = GDN_ROWS
    row = lambda n: pl.BlockSpec((tb, n), lambda i: (i, 0))
    return pl.pallas_call(
        _gdn_kernel,
        grid=(s // tb,),
        in_specs=[row(GDN_CONV_CH), row(LANES), row(GDN_V), _full(conv_w.shape),
                  _full((1, LANES)), _full((1, LANES)), _full((1, LANES))],
        out_specs=row(GDN_V),
        out_shape=jax.ShapeDtypeStruct((s, GDN_V), BF16),
        scratch_shapes=[pltpu.VMEM((tb + 8, GDN_CONV_CH), F32),
                        pltpu.VMEM((GDN_HEADS, GDN_DK, 128), F32)],
        compiler_params=_params("arbitrary"),
        name="gated_delta_rule",
    )(qkva, ba, z, conv_w, alog, dtb, onorm)


def _scores_into(s_ref, mx_ref, slot, k_tile, q_t, mask):
    s = _dot(k_tile, q_t)
    rows = s.shape[0]
    top = None
    for lo, piece in mask(s):
        s_ref[slot, lo:lo + piece.shape[0], :] = piece
        pmax = piece.max(axis=0, keepdims=True)
        top = pmax if top is None else jnp.maximum(top, pmax)
    if rows < s_ref.shape[1]:
        s_ref[slot, rows:, :] = jnp.full((s_ref.shape[1] - rows, s.shape[1]), NEG_INF, F32)
    mx_ref[slot] = top


def _values(p_ref, slot, v_at, blocks):
    v_t = jnp.concatenate([v_at(j) for j in blocks], axis=1)
    return _dot(v_t, p_ref[slot])


def _flash_steps(n_steps, first_blocks, first_keys, step_blocks, k_at, v_at, q_t, mask_first, mask_step,
                 s_ref, mx_ref, p_ref, acc_ref):
    per_step = len(first_blocks)
    _scores_into(s_ref, mx_ref, 0, k_at(first_blocks[0], first_keys), q_t, mask_first)
    m = mx_ref[0]
    p_ref[0] = jnp.exp2(s_ref[0] - m).astype(BF16)
    acc_ref[...] = jnp.zeros_like(acc_ref)
    _scores_into(s_ref, mx_ref, 1, k_at(step_blocks(1)[0], per_step), q_t, lambda s: mask_step(s, 1))

    def step(t, carry, cur):
        m, alpha_prev = carry
        nxt = jnp.minimum(t + 1, jnp.maximum(n_steps, 1))
        _scores_into(s_ref, mx_ref, 1 - cur, k_at(step_blocks(nxt)[0], per_step), q_t, lambda s: mask_step(s, nxt))
        m_new = jnp.maximum(m, mx_ref[cur])
        alpha = jnp.exp2(m - m_new)
        p = jnp.exp2(s_ref[cur] - m_new)
        prev = [jnp.where(t == 1, a, b) for a, b in zip(first_blocks, step_blocks(jnp.maximum(t - 1, 1)))]
        acc_ref[...] = alpha_prev * acc_ref[...] + _values(p_ref, 1 - cur, v_at, prev)
        p_ref[cur] = p.astype(BF16)
        return m_new, alpha

    def body(t, carry):
        return lax.cond(t % 2 == 0, lambda c: step(t, c, 0), lambda c: step(t, c, 1), carry)

    m, alpha = lax.fori_loop(1, n_steps + 1, body, (m, jnp.ones_like(m)))
    last = [jnp.where(n_steps == 0, a, b) for a, b in zip(first_blocks, step_blocks(jnp.maximum(n_steps, 1)))]
    acc = alpha * acc_ref[...] + _values(p_ref, n_steps % 2, v_at, last)
    dh = acc.shape[0] - SUM_ROWS
    return acc[:dh] / acc[dh:dh + 1]


def _with_sum_rows(v_t):
    extra = lax.broadcasted_iota(jnp.int32, (SUM_ROWS, v_t.shape[1]), 0) == 0
    return jnp.concatenate([v_t, extra.astype(F32)], axis=0).astype(BF16)


def _moba_kernel(q_ref, k_ref, v_ref, o_ref, kmean_ref, sel_ref, s_ref, mx_ref, p_ref, acc_ref, *, n_blocks):
    i = pl.program_id(1)
    bs = MOBA_BLOCK

    @pl.when(i == 0)
    def _():
        kmean_ref[...] = jnp.zeros_like(kmean_ref)

        def mean_body(n, carry):
            blk = k_ref[pl.ds(pl.multiple_of(n * bs, bs), bs), :].astype(F32)
            kmean_ref[pl.ds(n, 1), :] = jnp.mean(blk, axis=0, keepdims=True)
            return carry

        lax.fori_loop(0, n_blocks, mean_body, 0)

    q_t = jnp.concatenate([q_ref[0, 0], q_ref[0, 1]], axis=1)
    gate = _dot(kmean_ref[...].astype(BF16), q_t)
    blk_id = lax.broadcasted_iota(jnp.int32, gate.shape, 0)
    second = lax.broadcasted_iota(jnp.int32, gate.shape, 1) >= bs
    gate = jnp.where(blk_id < 2 * i + second.astype(jnp.int32), gate, NEG_INF)
    sel = jnp.zeros(gate.shape, F32)
    for _ in range(MOBA_TOPK):
        gmax = jnp.max(gate, axis=0, keepdims=True)
        first = jnp.min(jnp.where(gate == gmax, blk_id, LANES), axis=0, keepdims=True)
        hit = blk_id == first
        sel = jnp.where(hit & (gmax > NEG_INF), 1.0, sel)
        gate = jnp.where(hit, NEG_INF, gate)
    sel_ref[...] = sel

    def own_mask(s):
        key = lax.broadcasted_iota(jnp.int32, (bs, 2 * bs), 0)
        qry = lax.broadcasted_iota(jnp.int32, (bs, 2 * bs), 1)
        picked = sel_ref[pl.ds(2 * i, 1), :] > 0.0
        first_ok = (key <= qry) & ((qry < bs) | picked)
        return [(0, jnp.where(first_ok, s[:bs], NEG_INF)), (bs, jnp.where(key + bs <= qry, s[bs:], NEG_INF))]

    def sel_mask(s, t):
        return [(sub * bs, jnp.where(sel_ref[pl.ds(2 * (t - 1) + sub, 1), :] > 0.0, s[sub * bs:(sub + 1) * bs], NEG_INF))
                for sub in range(2)]

    out = _flash_steps(
        n_steps=i, first_blocks=[2 * i, 2 * i + 1], first_keys=2,
        step_blocks=lambda t: [2 * (t - 1), 2 * (t - 1) + 1],
        k_at=lambda j, n: k_ref[pl.ds(pl.multiple_of(j * bs, bs), n * bs), :], v_at=lambda j: v_ref[0, j],
        q_t=q_t, mask_first=own_mask, mask_step=sel_mask,
        s_ref=s_ref, mx_ref=mx_ref, p_ref=p_ref, acc_ref=acc_ref)
    o_ref[...] = jnp.transpose(out).astype(o_ref.dtype)


def _moba(q_t, k, v_t):
    s = k.shape[0]
    bs = MOBA_BLOCK
    n_blocks = s // bs
    assert n_blocks <= LANES and bs == KEY_BLOCK and n_blocks % 2 == 0
    bq = 2 * bs
    kv = pl.BlockSpec((s, MOBA_DH), lambda h, i: (0, h))
    return pl.pallas_call(
        functools.partial(_moba_kernel, n_blocks=n_blocks),
        grid=(MOBA_HEADS, n_blocks // 2),
        in_specs=[pl.BlockSpec((1, 2, MOBA_DH, bs), lambda h, i: (h, i, 0, 0)), kv,
                  pl.BlockSpec((1, n_blocks, MOBA_DH + SUM_ROWS, bs), lambda h, i: (h, 0, 0, 0))],
        out_specs=pl.BlockSpec((bq, MOBA_DH), lambda h, i: (i, h)),
        out_shape=jax.ShapeDtypeStruct((s, MOBA_W), BF16),
        scratch_shapes=[pltpu.VMEM((LANES, MOBA_DH), F32),
                        pltpu.VMEM((LANES, bq), F32),
                        pltpu.VMEM((2, 2 * KEY_BLOCK, bq), F32),
                        pltpu.VMEM((2, 1, bq), F32),
                        pltpu.VMEM((2, 2 * KEY_BLOCK, bq), BF16),
                        pltpu.VMEM((MOBA_DH + SUM_ROWS, bq), F32)],
        compiler_params=_params("parallel", "arbitrary"),
        name="moba_attention",
    )(q_t, k, v_t)


def _outproj_kernel(*refs, n_in):
    o_refs = refs[:n_in]
    w_refs = refs[n_in:2 * n_in]
    x_ref, g_ref, out_ref = refs[2 * n_in:]
    y = _dot(o_refs[0][...], w_refs[0][...])
    for o_ref, w_ref in zip(o_refs[1:], w_refs[1:]):
        y = y + _dot(o_ref[...], w_ref[...])
    out_ref[...] = x_ref[...] + g_ref[...] * y


def _outproj_residual(os_, ws, x, gate):
    s, d = x.shape
    tm = min(s, 512)
    row = lambda n: pl.BlockSpec((tm, n), lambda i: (i, 0))
    return pl.pallas_call(
        functools.partial(_outproj_kernel, n_in=len(os_)),
        grid=(s // tm,),
        in_specs=[row(o.shape[1]) for o in os_] + [_full(w.shape) for w in ws] + [row(d), _full((1, d))],
        out_specs=row(d),
        out_shape=jax.ShapeDtypeStruct((s, d), F32),
        compiler_params=_params("parallel"),
        name="out_proj_residual",
    )(*os_, *ws, x, gate)


def _mla_proj_kernel(x_ref, nw_ref, sc_ref, sh_ref, wcq_ref, wckv_ref, wkr_ref, qn_ref, kvn_ref,
                     wqn_ref, wqr_ref, wkn_ref, wv_ref, cos_ref, sin_ref, q_ref, k_ref, v_ref):
    h = _norm_mod(x_ref[...], nw_ref[...], sc_ref[...], sh_ref[...]).astype(BF16)
    cq = _dot(h, wcq_ref[...])
    ckv = _dot(h, wckv_ref[...])
    kr = _dot(h, wkr_ref[...])
    cq = (cq * lax.rsqrt(jnp.mean(cq * cq, axis=-1, keepdims=True) + EPS) * qn_ref[...]).astype(BF16)
    ckv = (ckv * lax.rsqrt(jnp.mean(ckv * ckv, axis=-1, keepdims=True) + EPS) * kvn_ref[...]).astype(BF16)
    cos = cos_ref[...]
    sin = sin_ref[...]
    scale = MLA_QK ** -0.5 * LOG2_E
    q_nope = _dot(cq, wqn_ref[...]) * scale
    q_rope = _dot(cq, wqr_ref[...])
    k_nope = _dot(ckv, wkn_ref[...])
    val = _dot(ckv, wv_ref[...])
    k_rope = _rope_heads64(kr, cos, sin)[:, :MLA_ROPE].astype(BF16)
    for pair in range(MLA_HEADS // 2):
        slab_t = jnp.transpose(_rope_heads64(q_rope[:, pair * LANES:(pair + 1) * LANES], cos, sin) * scale)
        for sub in range(2):
            hh = 2 * pair + sub
            q_ref[hh, 0, MLA_NOPE:, :] = slab_t[sub * MLA_ROPE:(sub + 1) * MLA_ROPE].astype(BF16)
    for hh in range(MLA_HEADS):
        q_ref[hh, 0, :MLA_NOPE, :] = jnp.transpose(q_nope[:, hh * MLA_NOPE:(hh + 1) * MLA_NOPE]).astype(BF16)
        k_ref[hh, :, :MLA_NOPE] = k_nope[:, hh * MLA_NOPE:(hh + 1) * MLA_NOPE].astype(BF16)
        k_ref[hh, :, MLA_NOPE:] = k_rope
        v_ref[hh, 0] = _with_sum_rows(jnp.transpose(val[:, hh * MLA_V:(hh + 1) * MLA_V]))


def _mla_proj(x, nw, sc, sh, wcq, wckv, wkr, qn, kvn, wqn, wqr, wkn, wv, cos, sin):
    s, d = x.shape
    tm = KEY_BLOCK
    row = lambda n: pl.BlockSpec((tm, n), lambda i: (i, 0))
    vec = _full((1, d))
    heads = lambda n: pl.BlockSpec((MLA_HEADS, tm, n), lambda i: (0, i, 0))
    return pl.pallas_call(
        _mla_proj_kernel,
        grid=(s // tm,),
        in_specs=[row(d), vec, vec, vec, _full(wcq.shape), _full(wckv.shape), _full(wkr.shape),
                  _full(qn.shape), _full(kvn.shape), _full(wqn.shape), _full(wqr.shape),
                  _full(wkn.shape), _full(wv.shape), row(LANES), row(LANES)],
        out_specs=[pl.BlockSpec((MLA_HEADS, 1, MLA_QK, tm), lambda i: (0, i, 0, 0)), heads(MLA_QK),
                   pl.BlockSpec((MLA_HEADS, 1, MLA_V + SUM_ROWS, tm), lambda i: (0, i, 0, 0))],
        out_shape=[jax.ShapeDtypeStruct((MLA_HEADS, s // tm, MLA_QK, tm), BF16),
                   jax.ShapeDtypeStruct((MLA_HEADS, s, MLA_QK), BF16),
                   jax.ShapeDtypeStruct((MLA_HEADS, s // tm, MLA_V + SUM_ROWS, tm), BF16)],
        compiler_params=_params("parallel"),
        name="mla_in_proj",
    )(x, nw, sc, sh, wcq, wckv, wkr, qn, kvn, wqn, wqr, wkn, wv, cos, sin)


def _mla_attn_kernel(q_ref, k_ref, v_ref, o_ref, s_ref, mx_ref, p_ref, acc_ref):
    i = pl.program_id(1)

    def causal(s):
        key = lax.broadcasted_iota(jnp.int32, s.shape, 0)
        qry = lax.broadcasted_iota(jnp.int32, s.shape, 1)
        return [(0, jnp.where(key <= qry, s, NEG_INF))]

    per_q = MLA_Q_BLOCK // KEY_BLOCK
    q_t = jnp.concatenate([q_ref[0, b] for b in range(per_q)], axis=1)
    out = _flash_steps(
        n_steps=i, first_blocks=[2 * i, 2 * i + 1], first_keys=2,
        step_blocks=lambda t: [2 * (t - 1), 2 * (t - 1) + 1],
        k_at=lambda j, n: k_ref[0, pl.ds(pl.multiple_of(j * KEY_BLOCK, KEY_BLOCK), n * KEY_BLOCK), :],
        v_at=lambda j: v_ref[0, j], q_t=q_t, mask_first=causal, mask_step=lambda s, t: [(0, s)],
        s_ref=s_ref, mx_ref=mx_ref, p_ref=p_ref, acc_ref=acc_ref)
    o_ref[...] = jnp.transpose(out).astype(o_ref.dtype)


def _mla_attention(q_t, k, v_t):
    _, s, _ = k.shape
    bq = MLA_Q_BLOCK
    per_q = bq // KEY_BLOCK
    assert per_q == 2
    return pl.pallas_call(
        _mla_attn_kernel,
        grid=(MLA_HEADS, s // bq),
        in_specs=[pl.BlockSpec((1, per_q, MLA_QK, KEY_BLOCK), lambda h, i: (h, i, 0, 0)),
                  pl.BlockSpec((1, s, MLA_QK), lambda h, i: (h, 0, 0)),
                  pl.BlockSpec((1, s // KEY_BLOCK, MLA_V + SUM_ROWS, KEY_BLOCK), lambda h, i: (h, 0, 0, 0))],
        out_specs=pl.BlockSpec((bq, MLA_V), lambda h, i: (i, h)),
        out_shape=jax.ShapeDtypeStruct((s, MLA_HEADS * MLA_V), BF16),
        scratch_shapes=[pltpu.VMEM((2, 2 * KEY_BLOCK, bq), F32), pltpu.VMEM((2, 1, bq), F32),
                        pltpu.VMEM((2, 2 * KEY_BLOCK, bq), BF16), pltpu.VMEM((MLA_V + SUM_ROWS, bq), F32)],
        compiler_params=_params("parallel", "arbitrary"),
        name="mla_attention",
    )(q_t, k, v_t)


def _peer_proj_kernel(x_ref, nw_ref, sc_ref, sh_ref, wq_ref, keys_ref, h_ref, s_ref):
    h = _norm_mod(x_ref[...], nw_ref[...], sc_ref[...], sh_ref[...]).astype(BF16)
    h_ref[...] = h
    q = _dot(h, wq_ref[...]).astype(BF16)
    for hp in range(2 * PEER_HEADS):
        s_ref[hp] = _dot_nt(keys_ref[hp], q[:, hp * PEER_DSUB:(hp + 1) * PEER_DSUB])


def _peer_proj(x, nw, sc, sh, wq, keys):
    s, d = x.shape
    tm = min(s, 256)
    vec = _full((1, d))
    return pl.pallas_call(
        _peer_proj_kernel,
        grid=(s // tm,),
        in_specs=[pl.BlockSpec((tm, d), lambda i: (i, 0)), vec, vec, vec, _full(wq.shape), _full(keys.shape)],
        out_specs=[pl.BlockSpec((tm, d), lambda i: (i, 0)),
                   pl.BlockSpec((2 * PEER_HEADS, PEER_NKEYS, tm), lambda i: (0, 0, i))],
        out_shape=[jax.ShapeDtypeStruct((s, d), BF16),
                   jax.ShapeDtypeStruct((2 * PEER_HEADS, PEER_NKEYS, s), F32)],
        compiler_params=_params("parallel"),
        name="peer_query_scores",
    )(x, nw, sc, sh, wq, keys)


def _compare_exchange(vs, i, l):
    vs[i], vs[l] = jnp.maximum(vs[i], vs[l]), jnp.minimum(vs[i], vs[l])


def _sort_desc(vs):
    vs = list(vs)
    n = len(vs)
    k = 2
    while k <= n:
        j = k // 2
        while j >= 1:
            for i in range(n):
                l = i ^ j
                if l > i:
                    if i & k == 0:
                        _compare_exchange(vs, i, l)
                    else:
                        _compare_exchange(vs, l, i)
            j //= 2
        k *= 2
    return vs


def _merge_top(a, b):
    n = len(a)
    vs = [jnp.maximum(a[i], b[n - 1 - i]) for i in range(n)]
    j = n // 2
    while j >= 1:
        for i in range(n):
            if i ^ j > i:
                _compare_exchange(vs, i, i ^ j)
        j //= 2
    return vs


def _top16_of_keys(groups):
    vs = _sort_desc(groups)
    for shift in (4, 2, 1):
        vs = _merge_top(vs, [pltpu.roll(v, shift, 0) for v in vs])
    return vs


def _peer_topk_kernel(s_ref, a_ref, n_ref, b_ref, rank_ref):
    sub = 8
    groups = PEER_NKEYS // sub

    def head_body(h, carry):
        g1 = [s_ref[2 * h, g * sub:(g + 1) * sub, :] for g in range(groups)]
        g2 = [s_ref[2 * h + 1, g * sub:(g + 1) * sub, :] for g in range(groups)]
        v1 = _top16_of_keys(g1)
        v2 = _top16_of_keys(g2)

        ninf = jnp.full_like(v1[0], NEG_INF)
        pad = lambda vs: vs + [ninf] * (PEER_TOPK - len(vs))
        row = lambda r1: [v1[r1] + v2[r2] for r2 in range(PEER_TOPK // (r1 + 1))]
        top = _merge_top(
            _merge_top(row(0), _merge_top(pad(row(1)), pad(row(2)))),
            _merge_top(_sort_desc(pad(row(3) + row(4) + row(5) + row(6) + row(7))),
                       pad([row(r1)[0] for r1 in range(8, PEER_TOPK)])))
        thr = top[PEER_TOPK - 1]
        z = jnp.ones_like(thr)
        for r in range(1, PEER_TOPK):
            z = z + jnp.exp(top[r] - top[0])
        half_inv_z = 0.5 / z

        b_all, rank_all = [], []
        for g in range(groups):
            rows = slice(g * sub, (g + 1) * sub)
            n = sum(jnp.where(g1[g] + v2[r2] >= thr, 1.0, 0.0) for r2 in range(PEER_TOPK))
            a_ref[h, rows, :] = jnp.exp(g1[g] - v1[0]) * half_inv_z
            n_ref[h, rows, :] = n
            b_all.append(jnp.exp(g2[g] - v2[0]))
            rank_all.append(sum(jnp.where(v2[r] > g2[g], 1.0, 0.0) for r in range(PEER_TOPK)))
        for pair in range(groups // 2):
            b_ref[h, pair] = jnp.concatenate(b_all[2 * pair:2 * pair + 2], axis=0).astype(BF16)
            rank_ref[h, pair] = jnp.concatenate(rank_all[2 * pair:2 * pair + 2], axis=0).astype(BF16)
        return carry

    lax.fori_loop(0, PEER_HEADS, head_body, 0)


def _peer_topk(scores):
    _, nk, s = scores.shape
    tt = min(s, PEER_TOPK_TOK)
    big = pl.BlockSpec((PEER_HEADS, nk, tt), lambda i: (0, 0, i))
    tiled = pl.BlockSpec((PEER_HEADS, nk // BF16_ROWS, BF16_ROWS, tt), lambda i: (0, 0, 0, i))
    shape = (PEER_HEADS, nk, s)
    shape16 = (PEER_HEADS, nk // BF16_ROWS, BF16_ROWS, s)
    return pl.pallas_call(
        _peer_topk_kernel,
        grid=(s // tt,),
        in_specs=[pl.BlockSpec((2 * PEER_HEADS, nk, tt), lambda i: (0, 0, i))],
        out_specs=[big, big, tiled, tiled],
        out_shape=[jax.ShapeDtypeStruct(shape, F32), jax.ShapeDtypeStruct(shape, F32),
                   jax.ShapeDtypeStruct(shape16, BF16), jax.ShapeDtypeStruct(shape16, BF16)],
        compiler_params=_params("parallel"),
        name="peer_topk_threshold",
    )(scores)


def _peer_dense_kernel(h_ref, a_ref, n_ref, b_ref, rank_ref, u_ref, vt_ref, x_ref, g_ref, o_ref,
                       acc_ref, p_ref):
    j = pl.program_id(1)
    nk = PEER_NKEYS

    @pl.when(j == 0)
    def _():
        acc_ref[...] = jnp.zeros_like(acc_ref)

    act_t = _dot_nt(u_ref[...], h_ref[...])
    for blk in range(PEER_EXP // nk):
        i1 = j * (PEER_EXP // nk) + blk
        wgt = jnp.zeros((nk // BF16_ROWS, BF16_ROWS, PEER_TOK), BF16)
        for h in range(PEER_HEADS):
            row = lambda ref: jnp.broadcast_to(ref[h, pl.ds(i1, 1), :], (BF16_ROWS, PEER_TOK)).astype(BF16)[None]
            chosen = rank_ref[h] < row(n_ref)
            wgt = wgt + jnp.where(chosen, b_ref[h], 0.0) * row(a_ref)
        act = act_t[blk * nk:(blk + 1) * nk]
        gelu2 = act * (1.0 + lax.erf(act * (2.0 ** -0.5)))
        p_ref[blk * nk:(blk + 1) * nk, :] = wgt.reshape(nk, PEER_TOK) * gelu2.astype(BF16)
    acc_ref[...] += _dot(vt_ref[...], p_ref[...])

    @pl.when(j == pl.num_programs(1) - 1)
    def _():
        o_ref[...] = x_ref[...] + g_ref[...] * jnp.transpose(acc_ref[...])


def _peer_dense(h2, a, n, b, rank, u16, vt16, layer, x, gate):
    s, d = x.shape
    n_exp = u16.shape[1]
    tt = min(s, PEER_TOK)
    assert tt == PEER_TOK
    tok3 = pl.BlockSpec((PEER_HEADS, PEER_NKEYS, tt), lambda i, j: (0, 0, i))
    tok4 = pl.BlockSpec((PEER_HEADS, PEER_NKEYS // BF16_ROWS, BF16_ROWS, tt), lambda i, j: (0, 0, 0, i))
    return pl.pallas_call(
        _peer_dense_kernel,
        grid=(s // tt, n_exp // PEER_EXP),
        in_specs=[pl.BlockSpec((tt, d), lambda i, j: (i, 0)),
                  tok3, tok3, tok4, tok4,
                  pl.BlockSpec((None, PEER_EXP, d), lambda i, j: (layer, j, 0)),
                  pl.BlockSpec((None, d, PEER_EXP), lambda i, j: (layer, 0, j)),
                  pl.BlockSpec((tt, d), lambda i, j: (i, 0)),
                  pl.BlockSpec((1, d), lambda i, j: (0, 0))],
        out_specs=pl.BlockSpec((tt, d), lambda i, j: (i, 0)),
        out_shape=jax.ShapeDtypeStruct((s, d), F32),
        scratch_shapes=[pltpu.VMEM((d, tt), F32), pltpu.VMEM((PEER_EXP, tt), BF16)],
        compiler_params=_params("parallel", "arbitrary"),
        name="peer_dense_experts",
    )(h2, a, n, b, rank, u16, vt16, x, gate)


def _final_norm_kernel(x_ref, w_ref, o_ref):
    x = x_ref[...]
    o_ref[...] = x * lax.rsqrt(jnp.mean(x * x, axis=-1, keepdims=True) + EPS) * w_ref[...]


def _final_norm(x, w):
    s, d = x.shape
    tm = min(s, 1024)
    return pl.pallas_call(
        _final_norm_kernel,
        grid=(s // tm,),
        in_specs=[pl.BlockSpec((tm, d), lambda i: (i, 0)), _full((1, d))],
        out_specs=pl.BlockSpec((tm, d), lambda i: (i, 0)),
        out_shape=jax.ShapeDtypeStruct((s, d), F32),
        compiler_params=_params("parallel"),
        name="final_rmsnorm",
    )(x, w)


def _lane_row(values, offset):
    return jnp.zeros((1, LANES), F32).at[0, offset:offset + values.shape[0]].set(values.astype(F32))


def _hybrid_layer(x, nw, sc, sh, gate, cos, sin, w_in, conv_w, a_log, dt_bias, o_norm, w_out):
    w16 = w_in.astype(BF16)
    c0 = GDN_CONV_CH
    c1 = c0 + GDN_V
    c2 = c1 + 2 * GDN_HEADS
    wba = jnp.pad(w16[:, c1:c2], ((0, 0), (0, LANES - 2 * GDN_HEADS)))
    qkva, z, ba, qb, kb, vb = _hy_proj(x, nw, sc, sh, w16[:, :c0], w16[:, c0:c1], wba, w16[:, c2:], cos, sin)
    o_a = _gdn(qkva, ba, z, conv_w, _lane_row(a_log, GDN_HEADS), _lane_row(dt_bias, GDN_HEADS),
               o_norm.reshape(1, -1))
    o_b = _moba(qb, kb, vb)
    wo16 = w_out.astype(BF16)
    return _outproj_residual([o_a, o_b], [wo16[:GDN_V], wo16[GDN_V:]], x, gate)


def _mla_layer(x, nw, sc, sh, gate, cos, sin, w_in, q_norm, kv_norm, w_uq, w_ukv, w_out):
    w16 = w_in.astype(BF16)
    r0 = MLA_Q_RANK
    r1 = r0 + MLA_KV_RANK
    wkr = jnp.pad(w16[:, r1:], ((0, 0), (0, LANES - MLA_ROPE)))
    wuq = w_uq.astype(BF16).reshape(MLA_Q_RANK, MLA_HEADS, MLA_QK)
    wqn = wuq[:, :, :MLA_NOPE].reshape(MLA_Q_RANK, MLA_HEADS * MLA_NOPE)
    wqr = wuq[:, :, MLA_NOPE:].reshape(MLA_Q_RANK, MLA_HEADS * MLA_ROPE)
    wukv = w_ukv.astype(BF16).reshape(MLA_KV_RANK, MLA_HEADS, MLA_NOPE + MLA_V)
    wkn = wukv[:, :, :MLA_NOPE].reshape(MLA_KV_RANK, MLA_HEADS * MLA_NOPE)
    wv = wukv[:, :, MLA_NOPE:].reshape(MLA_KV_RANK, MLA_HEADS * MLA_V)
    q, k, v = _mla_proj(x, nw, sc, sh, w16[:, :r0], w16[:, r0:r1], wkr, q_norm.reshape(1, -1),
                        kv_norm.reshape(1, -1), wqn, wqr, wkn, wv, cos, sin)
    o = _mla_attention(q, k, v)
    return _outproj_residual([o], [w_out.astype(BF16)], x, gate)


def _peer_layer(x, nw, sc, sh, gate, w_q, sub_keys, u16, vt16, layer):
    keys = sub_keys.astype(BF16).reshape(2 * PEER_HEADS, PEER_NKEYS, PEER_DSUB)
    h2, scores = _peer_proj(x, nw, sc, sh, w_q.astype(BF16), keys)
    a, n, b, rank = _peer_topk(scores)
    return _peer_dense(h2, a, n, b, rank, u16, vt16, layer, x, gate)


def kernel(x, c, positions, mod_w, mod_b, norm_mix, norm_ffn, hy_w_in, gdn_conv, gdn_a_log, gdn_dt_bias, gdn_o_norm, hy_w_out, mla_w_in, mla_q_norm, mla_kv_norm, mla_w_uq, mla_w_ukv, mla_w_out, peer_w_q, peer_sub_keys, peer_u, peer_v, final_norm):
    bsz, s, d = x.shape
    assert bsz == 1 and d == D_MODEL
    depth = mod_w.shape[0]
    xs = x.reshape(s, d)
    mod = _modulation(c.reshape(d, 1), mod_w, mod_b)
    cos_b, sin_b, cos_c, sin_c = _rope_tables(positions.reshape(s, 1))
    u16 = peer_u.astype(BF16)
    vt16 = jnp.swapaxes(peer_v.astype(BF16), 1, 2)
    for layer in range(depth):
        sh1, sc1, g1, sh2, sc2, g2 = (mod[layer, :, n * d:(n + 1) * d] for n in range(6))
        nw = norm_mix[layer].reshape(1, d)
        i = layer // 2
        if layer % 2 == 0:
            xs = _hybrid_layer(xs, nw, sc1, sh1, g1, cos_b, sin_b, hy_w_in[i], gdn_conv[i], gdn_a_log[i],
                               gdn_dt_bias[i], gdn_o_norm[i], hy_w_out[i])
        else:
            xs = _mla_layer(xs, nw, sc1, sh1, g1, cos_c, sin_c, mla_w_in[i], mla_q_norm[i], mla_kv_norm[i],
                            mla_w_uq[i], mla_w_ukv[i], mla_w_out[i])
        xs = _peer_layer(xs, norm_ffn[layer].reshape(1, d), sc2, sh2, g2, peer_w_q[layer],
                         peer_sub_keys[layer], u16, vt16, layer)
    return _final_norm(xs, final_norm.reshape(1, d)).reshape(bsz, s, d)
```

```python
import functools
import math

import jax
import jax.numpy as jnp
from jax import lax
from jax.experimental import pallas as pl
from jax.experimental.pallas import tpu as pltpu

F32 = jnp.float32
BF16 = jnp.bfloat16
NEG_INF = float("-inf")
HIGHEST = lax.Precision.HIGHEST

D_MODEL = 1024
EPS = 1e-6
ROPE_THETA = 10000.0
LOG2_E = math.log2(math.e)

GDN_HEADS = 4
GDN_DK = 128
GDN_CHUNK = 64
GDN_QK = GDN_HEADS * GDN_DK
GDN_V = GDN_HEADS * 128
GDN_CONV_CH = 2 * GDN_QK + GDN_V
GDN_ROWS = 256

MOBA_HEADS = 4
MOBA_DH = 128
MOBA_BLOCK = 256
MOBA_TOPK = 3
MOBA_W = MOBA_HEADS * MOBA_DH

MLA_HEADS = 8
MLA_Q_RANK = 512
MLA_KV_RANK = 256
MLA_NOPE = 128
MLA_ROPE = 64
MLA_V = 128
MLA_QK = MLA_NOPE + MLA_ROPE
KEY_BLOCK = 256
MLA_Q_BLOCK = 512

PEER_HEADS = 8
PEER_NKEYS = 128
PEER_TOPK = 16
PEER_DSUB = 128
PEER_TOK = 512
PEER_EXP = 2048
PEER_TOPK_TOK = 256

LANES = 128
BF16_ROWS = 16
SUM_ROWS = BF16_ROWS
VMEM_LIMIT = 56 * 1024 * 1024


def _params(*sem):
    return pltpu.CompilerParams(dimension_semantics=sem, vmem_limit_bytes=VMEM_LIMIT)


def _dot(a, b, precision=None):
    return jnp.dot(a, b, preferred_element_type=F32, precision=precision)


def _dot_nt(a, b):
    return lax.dot_general(a, b, (((1,), (1,)), ((), ())), preferred_element_type=F32)


def _sigmoid(x):
    return jax.nn.sigmoid(x)


def _norm_mod(x, nw, sc, sh):
    y = x * lax.rsqrt(jnp.mean(x * x, axis=-1, keepdims=True) + EPS) * nw
    return y * (1.0 + sc) + sh


def _full(shape):
    n = len(shape)
    return pl.BlockSpec(shape, lambda *_: (0,) * n)


def _mod_kernel(c_ref, w_ref, b_ref, o_ref):
    c = c_ref[...]
    cond = c * _sigmoid(c)
    o_ref[0] = jnp.sum(cond * w_ref[0], axis=0, keepdims=True) + b_ref[0]


def _modulation(c_col, mod_w, mod_b):
    depth, d, n = mod_w.shape
    tn = 1536
    return pl.pallas_call(
        _mod_kernel,
        grid=(depth, n // tn),
        in_specs=[pl.BlockSpec((d, 1), lambda l, j: (0, 0)),
                  pl.BlockSpec((1, d, tn), lambda l, j: (l, 0, j)),
                  pl.BlockSpec((1, 1, tn), lambda l, j: (l, 0, j))],
        out_specs=pl.BlockSpec((1, 1, tn), lambda l, j: (l, 0, j)),
        out_shape=jax.ShapeDtypeStruct((depth, 1, n), F32),
        compiler_params=_params("parallel", "parallel"),
        name="adaln_modulation",
    )(c_col, mod_w, mod_b.reshape(depth, 1, n))


def _rope_kernel(pos_ref, fb_ref, fc_ref, cb_ref, sb_ref, cc_ref, sc_ref):
    p = pos_ref[...].astype(F32)
    lane = lax.broadcasted_iota(jnp.int32, cb_ref.shape, 1)
    ab = p * fb_ref[...]
    sb = jnp.sin(ab)
    cb_ref[...] = jnp.cos(ab)
    sb_ref[...] = jnp.where(lane < MOBA_DH // 2, -sb, sb)
    ac = p * fc_ref[...]
    sc = jnp.sin(ac)
    cc_ref[...] = jnp.cos(ac)
    sc_ref[...] = jnp.where(lane % MLA_ROPE < MLA_ROPE // 2, -sc, sc)


def _rope_tables(pos_col):
    s = pos_col.shape[0]
    tm = min(s, 1024)
    fb = ROPE_THETA ** (-jnp.arange(0, MOBA_DH, 2, dtype=F32) / MOBA_DH)
    fc = ROPE_THETA ** (-jnp.arange(0, MLA_ROPE, 2, dtype=F32) / MLA_ROPE)
    fb = jnp.tile(fb, 2).reshape(1, LANES)
    fc = jnp.tile(fc, 4).reshape(1, LANES)
    tab = pl.BlockSpec((tm, LANES), lambda i: (i, 0))
    return pl.pallas_call(
        _rope_kernel,
        grid=(s // tm,),
        in_specs=[pl.BlockSpec((tm, 1), lambda i: (i, 0)), _full((1, LANES)), _full((1, LANES))],
        out_specs=[tab] * 4,
        out_shape=[jax.ShapeDtypeStruct((s, LANES), F32)] * 4,
        compiler_params=_params("parallel"),
        name="rope_tables",
    )(pos_col, fb, fc)


def _rope_head128(x, cos, sin):
    return x * cos + pltpu.roll(x, MOBA_DH // 2, 1) * sin


def _rope_heads64(x, cos, sin):
    lane = lax.broadcasted_iota(jnp.int32, x.shape, 1)
    first = lane % MLA_ROPE < MLA_ROPE // 2
    partner = jnp.where(first, pltpu.roll(x, LANES - MLA_ROPE // 2, 1), pltpu.roll(x, MLA_ROPE // 2, 1))
    return x * cos + partner * sin


def _hy_proj_kernel(x_ref, nw_ref, sc_ref, sh_ref, wa_ref, wz_ref, wba_ref, wb_ref, cos_ref, sin_ref,
                    qkva_ref, z_ref, ba_ref, qb_ref, kb_ref, vb_ref):
    h = _norm_mod(x_ref[...], nw_ref[...], sc_ref[...], sh_ref[...]).astype(BF16)
    qkva_ref[...] = _dot(h, wa_ref[...])
    z_ref[...] = _dot(h, wz_ref[...])
    ba_ref[...] = _dot(h, wba_ref[...])
    qkvb = _dot(h, wb_ref[...])
    cos = cos_ref[...]
    sin = sin_ref[...]
    scale = MOBA_DH ** -0.5 * LOG2_E
    for hh in range(MOBA_HEADS):
        lo = hh * MOBA_DH
        q = qkvb[:, lo:lo + MOBA_DH]
        k = qkvb[:, MOBA_W + lo:MOBA_W + lo + MOBA_DH]
        qb_ref[hh, 0] = jnp.transpose(_rope_head128(q, cos, sin) * scale).astype(BF16)
        kb_ref[:, lo:lo + MOBA_DH] = _rope_head128(k, cos, sin).astype(BF16)
        v = qkvb[:, 2 * MOBA_W + lo:2 * MOBA_W + lo + MOBA_DH]
        vb_ref[hh, 0] = _with_sum_rows(jnp.transpose(v))


def _hy_proj(x, nw, sc, sh, wa, wz, wba, wb, cos, sin):
    s, d = x.shape
    tm = KEY_BLOCK
    row = lambda n: pl.BlockSpec((tm, n), lambda i: (i, 0))
    vec = _full((1, d))
    return pl.pallas_call(
        _hy_proj_kernel,
        grid=(s // tm,),
        in_specs=[row(d), vec, vec, vec, _full(wa.shape), _full(wz.shape), _full(wba.shape),
                  _full(wb.shape), row(LANES), row(LANES)],
        out_specs=[row(GDN_CONV_CH), row(GDN_V), row(LANES),
                   pl.BlockSpec((MOBA_HEADS, 1, MOBA_DH, tm), lambda i: (0, i, 0, 0)), row(MOBA_W),
                   pl.BlockSpec((MOBA_HEADS, 1, MOBA_DH + SUM_ROWS, tm), lambda i: (0, i, 0, 0))],
        out_shape=[jax.ShapeDtypeStruct((s, GDN_CONV_CH), F32),
                   jax.ShapeDtypeStruct((s, GDN_V), F32),
                   jax.ShapeDtypeStruct((s, LANES), F32),
                   jax.ShapeDtypeStruct((MOBA_HEADS, s // tm, MOBA_DH, tm), BF16),
                   jax.ShapeDtypeStruct((s, MOBA_W), BF16),
                   jax.ShapeDtypeStruct((MOBA_HEADS, s // tm, MOBA_DH + SUM_ROWS, tm), BF16)],
        compiler_params=_params("parallel"),
        name="hybrid_in_proj",
    )(x, nw, sc, sh, wa, wz, wba, wb, cos, sin)


def _unit_lower_inverses(l_stricts, eye):
    powers = [-l for l in l_stricts]
    invs = [eye + p for p in powers]
    p = 2
    while p < GDN_CHUNK:
        powers = [_dot(x.astype(BF16), x.astype(BF16)) for x in powers]
        invs = [_dot(inv.astype(BF16), (eye + x).astype(BF16)) for inv, x in zip(invs, powers)]
        p *= 2
    return invs


def _gdn_kernel(qkv_ref, ba_ref, z_ref, conv_ref, alog_ref, dtb_ref, onorm_ref, o_ref, ext_ref, state_ref):
    tb = GDN_ROWS
    cs = GDN_CHUNK

    @pl.when(pl.program_id(0) == 0)
    def _():
        ext_ref[0:8, :] = jnp.zeros((8, GDN_CONV_CH), F32)
        state_ref[...] = jnp.zeros_like(state_ref)

    ext_ref[8:8 + tb, :] = qkv_ref[...]
    cw = conv_ref[...]
    y = (ext_ref[5:5 + tb, :] * cw[0:1] + ext_ref[6:6 + tb, :] * cw[1:2]
         + ext_ref[7:7 + tb, :] * cw[2:3] + ext_ref[8:8 + tb, :] * cw[3:4])
    ext_ref[0:8, :] = ext_ref[tb:tb + 8, :]
    y = y * _sigmoid(y)

    ba = ba_ref[...]
    beta_all = _sigmoid(ba)
    t = ba + dtb_ref[...]
    softplus = jnp.maximum(t, 0.0) + jnp.log1p(jnp.exp(-jnp.abs(t)))
    g_all = -jnp.exp(alog_ref[...]) * softplus

    row = lax.broadcasted_iota(jnp.int32, (tb, tb), 0)
    col = lax.broadcasted_iota(jnp.int32, (tb, tb), 1)
    same = (row // cs) == (col // cs)
    causal = same & (col <= row)
    strict = same & (col < row)
    eye = (row == col).astype(F32)
    g_cum = _dot(causal.astype(F32), g_all, HIGHEST)
    g_tot = _dot(same.astype(F32), g_all, HIGHEST)
    g_cum_t = jnp.transpose(g_cum)
    z = z_ref[...]
    onorm = onorm_ref[...]

    heads = range(GDN_HEADS)
    rhs, l_strict, gtot, qk, qdec, kdec = [], [], [], [], [], []
    for h in heads:
        lo = h * GDN_DK
        q = y[:, lo:lo + GDN_DK]
        k = y[:, GDN_QK + lo:GDN_QK + lo + GDN_DK]
        v = y[:, 2 * GDN_QK + lo:2 * GDN_QK + lo + GDN_DK]
        q = q * lax.rsqrt(jnp.sum(q * q, axis=-1, keepdims=True) + EPS) * (GDN_DK ** -0.5)
        k = k * lax.rsqrt(jnp.sum(k * k, axis=-1, keepdims=True) + EPS)
        beta = beta_all[:, h:h + 1]
        gcol = g_cum[:, GDN_HEADS + h:GDN_HEADS + h + 1]
        grow = g_cum_t[GDN_HEADS + h:GDN_HEADS + h + 1, :]
        gtot.append(g_tot[:, GDN_HEADS + h:GDN_HEADS + h + 1])
        decay = jnp.exp(jnp.where(causal, gcol - grow, NEG_INF))
        kb = k * beta
        k16 = k.astype(BF16)
        l_strict.append(jnp.where(strict, _dot_nt(kb.astype(BF16), k16) * decay, 0.0))
        eg = jnp.exp(gcol)
        rhs.append(jnp.concatenate([v * beta, kb * eg], axis=1).astype(BF16))
        qk.append((_dot_nt(q.astype(BF16), k16) * decay).astype(BF16))
        qdec.append((q * eg).astype(BF16))
        kdec.append(k * jnp.exp(gtot[h] - gcol))

    uw = [_dot(t_inv.astype(BF16), rhs[h]) for h, t_inv in enumerate(_unit_lower_inverses(l_strict, eye))]
    u = [x[:, :GDN_DK] for x in uw]
    w = [x[:, GDN_DK:].astype(BF16) for x in uw]

    state = [state_ref[h] for h in heads]
    v_new = [[] for _ in heads]
    o_state = [[] for _ in heads]
    for c in range(tb // cs):
        r0 = c * cs
        for h in heads:
            s16 = state[h].astype(BF16)
            vn = u[h][r0:r0 + cs] - _dot(w[h][r0:r0 + cs], s16)
            o_state[h].append(_dot(qdec[h][r0:r0 + cs], s16))
            kd_t = jnp.transpose(kdec[h][r0:r0 + cs]).astype(BF16)
            state[h] = state[h] * jnp.exp(gtot[h][r0:r0 + 1]) + _dot(kd_t, vn.astype(BF16))
            v_new[h].append(vn)

    for h in heads:
        lo = h * GDN_DK
        state_ref[h] = state[h]
        o = jnp.concatenate(o_state[h], axis=0) + _dot(qk[h], jnp.concatenate(v_new[h], axis=0).astype(BF16))
        o = o * lax.rsqrt(jnp.mean(o * o, axis=-1, keepdims=True) + EPS) * onorm
        zh = z[:, lo:lo + GDN_DK]
        o_ref[:, lo:lo + GDN_DK] = (o * (zh * _sigmoid(zh))).astype(BF16)


def _gdn(qkva, ba, z, conv_w, alog, dtb, onorm):
    s = qkva.shape[0]
    tb = GDN_ROWS
    row = lambda n: pl.BlockSpec((tb, n), lambda i: (i, 0))
    return pl.pallas_call(
        _gdn_kernel,
        grid=(s // tb,),
        in_specs=[row(GDN_CONV_CH), row(LANES), row(GDN_V), _full(conv_w.shape),
                  _full((1, LANES)), _full((1, LANES)), _full((1, LANES))],
        out_specs=row(GDN_V),
        out_shape=jax.ShapeDtypeStruct((s, GDN_V), BF16),
        scratch_shapes=[pltpu.VMEM((tb + 8, GDN_CONV_CH), F32),
                        pltpu.VMEM((GDN_HEADS, GDN_DK, 128), F32)],
        compiler_params=_params("arbitrary"),
        name="gated_delta_rule",
    )(qkva, ba, z, conv_w, alog, dtb, onorm)


def _scores_into(s_ref, mx_ref, slot, k_tile, q_t, mask):
    s = _dot(k_tile, q_t)
    rows = s.shape[0]
    top = None
    for lo, piece in mask(s):
        s_ref[slot, lo:lo + piece.shape[0], :] = piece
        pmax = piece.max(axis=0, keepdims=True)
        top = pmax if top is None else jnp.maximum(top, pmax)
    if rows < s_ref.shape[1]:
        s_ref[slot, rows:, :] = jnp.full((s_ref.shape[1] - rows, s.shape[1]), NEG_INF, F32)
    mx_ref[slot] = top


def _values(p_ref, slot, v_at, blocks):
    v_t = jnp.concatenate([v_at(j) for j in blocks], axis=1)
    return _dot(v_t, p_ref[slot])


def _flash_steps(n_steps, first_blocks, first_keys, step_blocks, k_at, v_at, q_t, mask_first, mask_step,
                 s_ref, mx_ref, p_ref, acc_ref):
    per_step = len(first_blocks)
    _scores_into(s_ref, mx_ref, 0, k_at(first_blocks[0], first_keys), q_t, mask_first)
    m = mx_ref[0]
    p_ref[0] = jnp.exp2(s_ref[0] - m).astype(BF16)
    acc_ref[...] = jnp.zeros_like(acc_ref)
    _scores_into(s_ref, mx_ref, 1, k_at(step_blocks(1)[0], per_step), q_t, lambda s: mask_step(s, 1))

    def step(t, carry, cur):
        m, alpha_prev = carry
        nxt = jnp.minimum(t + 1, jnp.maximum(n_steps, 1))
        _scores_into(s_ref, mx_ref, 1 - cur, k_at(step_blocks(nxt)[0], per_step), q_t, lambda s: mask_step(s, nxt))
        m_new = jnp.maximum(m, mx_ref[cur])
        alpha = jnp.exp2(m - m_new)
        p = jnp.exp2(s_ref[cur] - m_new)
        prev = [jnp.where(t == 1, a, b) for a, b in zip(first_blocks, step_blocks(jnp.maximum(t - 1, 1)))]
        acc_ref[...] = alpha_prev * acc_ref[...] + _values(p_ref, 1 - cur, v_at, prev)
        p_ref[cur] = p.astype(BF16)
        return m_new, alpha

    def body(t, carry):
        return lax.cond(t % 2 == 0, lambda c: step(t, c, 0), lambda c: step(t, c, 1), carry)

    m, alpha = lax.fori_loop(1, n_steps + 1, body, (m, jnp.ones_like(m)))
    last = [jnp.where(n_steps == 0, a, b) for a, b in zip(first_blocks, step_blocks(jnp.maximum(n_steps, 1)))]
    acc = alpha * acc_ref[...] + _values(p_ref, n_steps % 2, v_at, last)
    dh = acc.shape[0] - SUM_ROWS
    return acc[:dh] / acc[dh:dh + 1]


def _with_sum_rows(v_t):
    extra = lax.broadcasted_iota(jnp.int32, (SUM_ROWS, v_t.shape[1]), 0) == 0
    return jnp.concatenate([v_t, extra.astype(F32)], axis=0).astype(BF16)


def _moba_kernel(q_ref, k_ref, v_ref, o_ref, kmean_ref, sel_ref, s_ref, mx_ref, p_ref, acc_ref, *, n_blocks):
    i = pl.program_id(1)
    bs = MOBA_BLOCK

    @pl.when(i == 0)
    def _():
        kmean_ref[...] = jnp.zeros_like(kmean_ref)

        def mean_body(n, carry):
            blk = k_ref[pl.ds(pl.multiple_of(n * bs, bs), bs), :].astype(F32)
            kmean_ref[pl.ds(n, 1), :] = jnp.mean(blk, axis=0, keepdims=True)
            return carry

        lax.fori_loop(0, n_blocks, mean_body, 0)

    q_t = jnp.concatenate([q_ref[0, 0], q_ref[0, 1]], axis=1)
    gate = _dot(kmean_ref[...].astype(BF16), q_t)
    blk_id = lax.broadcasted_iota(jnp.int32, gate.shape, 0)
    second = lax.broadcasted_iota(jnp.int32, gate.shape, 1) >= bs
    gate = jnp.where(blk_id < 2 * i + second.astype(jnp.int32), gate, NEG_INF)
    sel = jnp.zeros(gate.shape, F32)
    for _ in range(MOBA_TOPK):
        gmax = jnp.max(gate, axis=0, keepdims=True)
        first = jnp.min(jnp.where(gate == gmax, blk_id, LANES), axis=0, keepdims=True)
        hit = blk_id == first
        sel = jnp.where(hit & (gmax > NEG_INF), 1.0, sel)
        gate = jnp.where(hit, NEG_INF, gate)
    sel_ref[...] = sel

    def own_mask(s):
        key = lax.broadcasted_iota(jnp.int32, (bs, 2 * bs), 0)
        qry = lax.broadcasted_iota(jnp.int32, (bs, 2 * bs), 1)
        picked = sel_ref[pl.ds(2 * i, 1), :] > 0.0
        first_ok = (key <= qry) & ((qry < bs) | picked)
        return [(0, jnp.where(first_ok, s[:bs], NEG_INF)), (bs, jnp.where(key + bs <= qry, s[bs:], NEG_INF))]

    def sel_mask(s, t):
        return [(sub * bs, jnp.where(sel_ref[pl.ds(2 * (t - 1) + sub, 1), :] > 0.0, s[sub * bs:(sub + 1) * bs], NEG_INF))
                for sub in range(2)]

    out = _flash_steps(
        n_steps=i, first_blocks=[2 * i, 2 * i + 1], first_keys=2,
        step_blocks=lambda t: [2 * (t - 1), 2 * (t - 1) + 1],
        k_at=lambda j, n: k_ref[pl.ds(pl.multiple_of(j * bs, bs), n * bs), :], v_at=lambda j: v_ref[0, j],
        q_t=q_t, mask_first=own_mask, mask_step=sel_mask,
        s_ref=s_ref, mx_ref=mx_ref, p_ref=p_ref, acc_ref=acc_ref)
    o_ref[...] = jnp.transpose(out).astype(o_ref.dtype)


def _moba(q_t, k, v_t):
    s = k.shape[0]
    bs = MOBA_BLOCK
    n_blocks = s // bs
    assert n_blocks <= LANES and bs == KEY_BLOCK and n_blocks % 2 == 0
    bq = 2 * bs
    kv = pl.BlockSpec((s, MOBA_DH), lambda h, i: (0, h))
    return pl.pallas_call(
        functools.partial(_moba_kernel, n_blocks=n_blocks),
        grid=(MOBA_HEADS, n_blocks // 2),
        in_specs=[pl.BlockSpec((1, 2, MOBA_DH, bs), lambda h, i: (h, i, 0, 0)), kv,
                  pl.BlockSpec((1, n_blocks, MOBA_DH + SUM_ROWS, bs), lambda h, i: (h, 0, 0, 0))],
        out_specs=pl.BlockSpec((bq, MOBA_DH), lambda h, i: (i, h)),
        out_shape=jax.ShapeDtypeStruct((s, MOBA_W), BF16),
        scratch_shapes=[pltpu.VMEM((LANES, MOBA_DH), F32),
                        pltpu.VMEM((LANES, bq), F32),
                        pltpu.VMEM((2, 2 * KEY_BLOCK, bq), F32),
                        pltpu.VMEM((2, 1, bq), F32),
                        pltpu.VMEM((2, 2 * KEY_BLOCK, bq), BF16),
                        pltpu.VMEM((MOBA_DH + SUM_ROWS, bq), F32)],
        compiler_params=_params("parallel", "arbitrary"),
        name="moba_attention",
    )(q_t, k, v_t)


def _outproj_peer_kernel(*refs, n_in):
    o_refs = refs[:n_in]
    w_refs = refs[n_in:2 * n_in]
    x_ref, g_ref, nw_ref, sc_ref, sh_ref, wq_ref, keys_ref, xo_ref, h_ref, s_ref = refs[2 * n_in:]
    y = _dot(o_refs[0][...], w_refs[0][...])
    for o_ref, w_ref in zip(o_refs[1:], w_refs[1:]):
        y = y + _dot(o_ref[...], w_ref[...])
    x = x_ref[...] + g_ref[...] * y
    xo_ref[...] = x
    h = _norm_mod(x, nw_ref[...], sc_ref[...], sh_ref[...]).astype(BF16)
    h_ref[...] = h
    q = _dot(h, wq_ref[...]).astype(BF16)
    for hp in range(2 * PEER_HEADS):
        s_ref[hp] = _dot_nt(keys_ref[hp], q[:, hp * PEER_DSUB:(hp + 1) * PEER_DSUB])


def _outproj_peer_proj(os_, ws, x, gate, nw, sc, sh, wq, keys):
    s, d = x.shape
    tm = min(s, 256)
    row = lambda n: pl.BlockSpec((tm, n), lambda i: (i, 0))
    vec = _full((1, d))
    return pl.pallas_call(
        functools.partial(_outproj_peer_kernel, n_in=len(os_)),
        grid=(s // tm,),
        in_specs=[row(o.shape[1]) for o in os_] + [_full(w.shape) for w in ws]
                 + [row(d), vec, vec, vec, vec, _full(wq.shape), _full(keys.shape)],
        out_specs=[row(d), row(d), pl.BlockSpec((2 * PEER_HEADS, PEER_NKEYS, tm), lambda i: (0, 0, i))],
        out_shape=[jax.ShapeDtypeStruct((s, d), F32), jax.ShapeDtypeStruct((s, d), BF16),
                   jax.ShapeDtypeStruct((2 * PEER_HEADS, PEER_NKEYS, s), F32)],
        compiler_params=_params("parallel"),
        name="out_proj_peer_scores",
    )(*os_, *ws, x, gate, nw, sc, sh, wq, keys)


def _mla_proj_kernel(x_ref, nw_ref, sc_ref, sh_ref, wcq_ref, wckv_ref, wkr_ref, qn_ref, kvn_ref,
                     wqn_ref, wqr_ref, wkn_ref, wv_ref, cos_ref, sin_ref, q_ref, k_ref, v_ref):
    h = _norm_mod(x_ref[...], nw_ref[...], sc_ref[...], sh_ref[...]).astype(BF16)
    cq = _dot(h, wcq_ref[...])
    ckv = _dot(h, wckv_ref[...])
    kr = _dot(h, wkr_ref[...])
    cq = (cq * lax.rsqrt(jnp.mean(cq * cq, axis=-1, keepdims=True) + EPS) * qn_ref[...]).astype(BF16)
    ckv = (ckv * lax.rsqrt(jnp.mean(ckv * ckv, axis=-1, keepdims=True) + EPS) * kvn_ref[...]).astype(BF16)
    cos = cos_ref[...]
    sin = sin_ref[...]
    scale = MLA_QK ** -0.5 * LOG2_E
    q_nope = _dot(cq, wqn_ref[...]) * scale
    q_rope = _dot(cq, wqr_ref[...])
    k_nope = _dot(ckv, wkn_ref[...])
    val = _dot(ckv, wv_ref[...])
    k_rope = _rope_heads64(kr, cos, sin)[:, :MLA_ROPE].astype(BF16)
    for pair in range(MLA_HEADS // 2):
        slab_t = jnp.transpose(_rope_heads64(q_rope[:, pair * LANES:(pair + 1) * LANES], cos, sin) * scale)
        for sub in range(2):
            hh = 2 * pair + sub
            q_ref[hh, 0, MLA_NOPE:, :] = slab_t[sub * MLA_ROPE:(sub + 1) * MLA_ROPE].astype(BF16)
    for hh in range(MLA_HEADS):
        q_ref[hh, 0, :MLA_NOPE, :] = jnp.transpose(q_nope[:, hh * MLA_NOPE:(hh + 1) * MLA_NOPE]).astype(BF16)
        k_ref[hh, :, :MLA_NOPE] = k_nope[:, hh * MLA_NOPE:(hh + 1) * MLA_NOPE].astype(BF16)
        k_ref[hh, :, MLA_NOPE:] = k_rope
        v_ref[hh, 0] = _with_sum_rows(jnp.transpose(val[:, hh * MLA_V:(hh + 1) * MLA_V]))


def _mla_proj(x, nw, sc, sh, wcq, wckv, wkr, qn, kvn, wqn, wqr, wkn, wv, cos, sin):
    s, d = x.shape
    tm = KEY_BLOCK
    row = lambda n: pl.BlockSpec((tm, n), lambda i: (i, 0))
    vec = _full((1, d))
    heads = lambda n: pl.BlockSpec((MLA_HEADS, tm, n), lambda i: (0, i, 0))
    return pl.pallas_call(
        _mla_proj_kernel,
        grid=(s // tm,),
        in_specs=[row(d), vec, vec, vec, _full(wcq.shape), _full(wckv.shape), _full(wkr.shape),
                  _full(qn.shape), _full(kvn.shape), _full(wqn.shape), _full(wqr.shape),
                  _full(wkn.shape), _full(wv.shape), row(LANES), row(LANES)],
        out_specs=[pl.BlockSpec((MLA_HEADS, 1, MLA_QK, tm), lambda i: (0, i, 0, 0)), heads(MLA_QK),
                   pl.BlockSpec((MLA_HEADS, 1, MLA_V + SUM_ROWS, tm), lambda i: (0, i, 0, 0))],
        out_shape=[jax.ShapeDtypeStruct((MLA_HEADS, s // tm, MLA_QK, tm), BF16),
                   jax.ShapeDtypeStruct((MLA_HEADS, s, MLA_QK), BF16),
                   jax.ShapeDtypeStruct((MLA_HEADS, s // tm, MLA_V + SUM_ROWS, tm), BF16)],
        compiler_params=_params("parallel"),
        name="mla_in_proj",
    )(x, nw, sc, sh, wcq, wckv, wkr, qn, kvn, wqn, wqr, wkn, wv, cos, sin)


def _mla_attn_kernel(q_ref, k_ref, v_ref, o_ref, s_ref, mx_ref, p_ref, acc_ref):
    i = pl.program_id(1)

    def causal(s):
        key = lax.broadcasted_iota(jnp.int32, s.shape, 0)
        qry = lax.broadcasted_iota(jnp.int32, s.shape, 1)
        return [(0, jnp.where(key <= qry, s, NEG_INF))]

    per_q = MLA_Q_BLOCK // KEY_BLOCK
    q_t = jnp.concatenate([q_ref[0, b] for b in range(per_q)], axis=1)
    out = _flash_steps(
        n_steps=i, first_blocks=[2 * i, 2 * i + 1], first_keys=2,
        step_blocks=lambda t: [2 * (t - 1), 2 * (t - 1) + 1],
        k_at=lambda j, n: k_ref[0, pl.ds(pl.multiple_of(j * KEY_BLOCK, KEY_BLOCK), n * KEY_BLOCK), :],
        v_at=lambda j: v_ref[0, j], q_t=q_t, mask_first=causal, mask_step=lambda s, t: [(0, s)],
        s_ref=s_ref, mx_ref=mx_ref, p_ref=p_ref, acc_ref=acc_ref)
    o_ref[...] = jnp.transpose(out).astype(o_ref.dtype)


def _mla_attention(q_t, k, v_t):
    _, s, _ = k.shape
    bq = MLA_Q_BLOCK
    per_q = bq // KEY_BLOCK
    assert per_q == 2
    return pl.pallas_call(
        _mla_attn_kernel,
        grid=(MLA_HEADS, s // bq),
        in_specs=[pl.BlockSpec((1, per_q, MLA_QK, KEY_BLOCK), lambda h, i: (h, i, 0, 0)),
                  pl.BlockSpec((1, s, MLA_QK), lambda h, i: (h, 0, 0)),
                  pl.BlockSpec((1, s // KEY_BLOCK, MLA_V + SUM_ROWS, KEY_BLOCK), lambda h, i: (h, 0, 0, 0))],
        out_specs=pl.BlockSpec((bq, MLA_V), lambda h, i: (i, h)),
        out_shape=jax.ShapeDtypeStruct((s, MLA_HEADS * MLA_V), BF16),
        scratch_shapes=[pltpu.VMEM((2, 2 * KEY_BLOCK, bq), F32), pltpu.VMEM((2, 1, bq), F32),
                        pltpu.VMEM((2, 2 * KEY_BLOCK, bq), BF16), pltpu.VMEM((MLA_V + SUM_ROWS, bq), F32)],
        compiler_params=_params("parallel", "arbitrary"),
        name="mla_attention",
    )(q_t, k, v_t)


def _compare_exchange(vs, i, l):
    vs[i], vs[l] = jnp.maximum(vs[i], vs[l]), jnp.minimum(vs[i], vs[l])


def _sort_desc(vs):
    vs = list(vs)
    n = len(vs)
    k = 2
    while k <= n:
        j = k // 2
        while j >= 1:
            for i in range(n):
                l = i ^ j
                if l > i:
                    if i & k == 0:
                        _compare_exchange(vs, i, l)
                    else:
                        _compare_exchange(vs, l, i)
            j //= 2
        k *= 2
    return vs


def _merge_top(a, b):
    n = len(a)
    vs = [jnp.maximum(a[i], b[n - 1 - i]) for i in range(n)]
    j = n // 2
    while j >= 1:
        for i in range(n):
            if i ^ j > i:
                _compare_exchange(vs, i, i ^ j)
        j //= 2
    return vs


def _top16_of_keys(groups):
    vs = _sort_desc(groups)
    for shift in (4, 2, 1):
        vs = _merge_top(vs, [pltpu.roll(v, shift, 0) for v in vs])
    return vs


def _peer_topk_kernel(s_ref, a_ref, n_ref, b_ref, rank_ref):
    sub = 8
    groups = PEER_NKEYS // sub

    def head_body(h, carry):
        g1 = [s_ref[2 * h, g * sub:(g + 1) * sub, :] for g in range(groups)]
        g2 = [s_ref[2 * h + 1, g * sub:(g + 1) * sub, :] for g in range(groups)]
        v1 = _top16_of_keys(g1)
        v2 = _top16_of_keys(g2)

        ninf = jnp.full_like(v1[0], NEG_INF)
        pad = lambda vs: vs + [ninf] * (PEER_TOPK - len(vs))
        row = lambda r1: [v1[r1] + v2[r2] for r2 in range(PEER_TOPK // (r1 + 1))]
        top = _merge_top(
            _merge_top(row(0), _merge_top(pad(row(1)), pad(row(2)))),
            _merge_top(_sort_desc(pad(row(3) + row(4) + row(5) + row(6) + row(7))),
                       pad([row(r1)[0] for r1 in range(8, PEER_TOPK)])))
        thr = top[PEER_TOPK - 1]
        z = jnp.ones_like(thr)
        for r in range(1, PEER_TOPK):
            z = z + jnp.exp(top[r] - top[0])
        half_inv_z = 0.5 / z

        b_all, rank_all = [], []
        for g in range(groups):
            rows = slice(g * sub, (g + 1) * sub)
            n = sum(jnp.where(g1[g] + v2[r2] >= thr, 1.0, 0.0) for r2 in range(PEER_TOPK))
            a_ref[h, rows, :] = jnp.exp(g1[g] - v1[0]) * half_inv_z
            n_ref[h, rows, :] = n
            b_all.append(jnp.exp(g2[g] - v2[0]))
            rank_all.append(sum(jnp.where(v2[r] > g2[g], 1.0, 0.0) for r in range(PEER_TOPK)))
        for pair in range(groups // 2):
            b_ref[h, pair] = jnp.concatenate(b_all[2 * pair:2 * pair + 2], axis=0).astype(BF16)
            rank_ref[h, pair] = jnp.concatenate(rank_all[2 * pair:2 * pair + 2], axis=0).astype(BF16)
        return carry

    lax.fori_loop(0, PEER_HEADS, head_body, 0)


def _peer_topk(scores):
    _, nk, s = scores.shape
    tt = min(s, PEER_TOPK_TOK)
    big = pl.BlockSpec((PEER_HEADS, nk, tt), lambda i: (0, 0, i))
    tiled = pl.BlockSpec((PEER_HEADS, nk // BF16_ROWS, BF16_ROWS, tt), lambda i: (0, 0, 0, i))
    shape = (PEER_HEADS, nk, s)
    shape16 = (PEER_HEADS, nk // BF16_ROWS, BF16_ROWS, s)
    return pl.pallas_call(
        _peer_topk_kernel,
        grid=(s // tt,),
        in_specs=[pl.BlockSpec((2 * PEER_HEADS, nk, tt), lambda i: (0, 0, i))],
        out_specs=[big, big, tiled, tiled],
        out_shape=[jax.ShapeDtypeStruct(shape, F32), jax.ShapeDtypeStruct(shape, F32),
                   jax.ShapeDtypeStruct(shape16, BF16), jax.ShapeDtypeStruct(shape16, BF16)],
        compiler_params=_params("parallel"),
        name="peer_topk_threshold",
    )(scores)


def _peer_dense_kernel(h_ref, a_ref, n_ref, b_ref, rank_ref, u_ref, vt_ref, x_ref, g_ref, o_ref,
                       acc_ref, p_ref):
    j = pl.program_id(1)
    nk = PEER_NKEYS

    @pl.when(j == 0)
    def _():
        acc_ref[...] = jnp.zeros_like(acc_ref)

    act_t = _dot_nt(u_ref[...], h_ref[...])
    for blk in range(PEER_EXP // nk):
        i1 = j * (PEER_EXP // nk) + blk
        wgt = jnp.zeros((nk // BF16_ROWS, BF16_ROWS, PEER_TOK), BF16)
        for h in range(PEER_HEADS):
            row = lambda ref: jnp.broadcast_to(ref[h, pl.ds(i1, 1), :], (BF16_ROWS, PEER_TOK)).astype(BF16)[None]
            chosen = rank_ref[h] < row(n_ref)
            wgt = wgt + jnp.where(chosen, b_ref[h], 0.0) * row(a_ref)
        act = act_t[blk * nk:(blk + 1) * nk]
        gelu2 = act * (1.0 + lax.erf(act * (2.0 ** -0.5)))
        p_ref[blk * nk:(blk + 1) * nk, :] = wgt.reshape(nk, PEER_TOK) * gelu2.astype(BF16)
    acc_ref[...] += _dot(vt_ref[...], p_ref[...])

    @pl.when(j == pl.num_programs(1) - 1)
    def _():
        o_ref[...] = x_ref[...] + g_ref[...] * jnp.transpose(acc_ref[...])


def _peer_dense(h2, a, n, b, rank, u16, vt16, layer, x, gate):
    s, d = x.shape
    n_exp = u16.shape[1]
    tt = min(s, PEER_TOK)
    assert tt == PEER_TOK
    tok3 = pl.BlockSpec((PEER_HEADS, PEER_NKEYS, tt), lambda i, j: (0, 0, i))
    tok4 = pl.BlockSpec((PEER_HEADS, PEER_NKEYS // BF16_ROWS, BF16_ROWS, tt), lambda i, j: (0, 0, 0, i))
    return pl.pallas_call(
        _peer_dense_kernel,
        grid=(s // tt, n_exp // PEER_EXP),
        in_specs=[pl.BlockSpec((tt, d), lambda i, j: (i, 0)),
                  tok3, tok3, tok4, tok4,
                  pl.BlockSpec((None, PEER_EXP, d), lambda i, j: (layer, j, 0)),
                  pl.BlockSpec((None, d, PEER_EXP), lambda i, j: (layer, 0, j)),
                  pl.BlockSpec((tt, d), lambda i, j: (i, 0)),
                  pl.BlockSpec((1, d), lambda i, j: (0, 0))],
        out_specs=pl.BlockSpec((tt, d), lambda i, j: (i, 0)),
        out_shape=jax.ShapeDtypeStruct((s, d), F32),
        scratch_shapes=[pltpu.VMEM((d, tt), F32), pltpu.VMEM((PEER_EXP, tt), BF16)],
        compiler_params=_params("parallel", "arbitrary"),
        name="peer_dense_experts",
    )(h2, a, n, b, rank, u16, vt16, x, gate)


def _final_norm_kernel(x_ref, w_ref, o_ref):
    x = x_ref[...]
    o_ref[...] = x * lax.rsqrt(jnp.mean(x * x, axis=-1, keepdims=True) + EPS) * w_ref[...]


def _final_norm(x, w):
    s, d = x.shape
    tm = min(s, 1024)
    return pl.pallas_call(
        _final_norm_kernel,
        grid=(s // tm,),
        in_specs=[pl.BlockSpec((tm, d), lambda i: (i, 0)), _full((1, d))],
        out_specs=pl.BlockSpec((tm, d), lambda i: (i, 0)),
        out_shape=jax.ShapeDtypeStruct((s, d), F32),
        compiler_params=_params("parallel"),
        name="final_rmsnorm",
    )(x, w)


def _lane_row(values, offset):
    return jnp.zeros((1, LANES), F32).at[0, offset:offset + values.shape[0]].set(values.astype(F32))


def _hybrid_mixer(x, nw, sc, sh, cos, sin, w_in, conv_w, a_log, dt_bias, o_norm, w_out):
    w16 = w_in.astype(BF16)
    c0 = GDN_CONV_CH
    c1 = c0 + GDN_V
    c2 = c1 + 2 * GDN_HEADS
    wba = jnp.pad(w16[:, c1:c2], ((0, 0), (0, LANES - 2 * GDN_HEADS)))
    qkva, z, ba, qb, kb, vb = _hy_proj(x, nw, sc, sh, w16[:, :c0], w16[:, c0:c1], wba, w16[:, c2:], cos, sin)
    o_a = _gdn(qkva, ba, z, conv_w, _lane_row(a_log, GDN_HEADS), _lane_row(dt_bias, GDN_HEADS),
               o_norm.reshape(1, -1))
    o_b = _moba(qb, kb, vb)
    wo16 = w_out.astype(BF16)
    return [o_a, o_b], [wo16[:GDN_V], wo16[GDN_V:]]


def _mla_mixer(x, nw, sc, sh, cos, sin, w_in, q_norm, kv_norm, w_uq, w_ukv, w_out):
    w16 = w_in.astype(BF16)
    r0 = MLA_Q_RANK
    r1 = r0 + MLA_KV_RANK
    wkr = jnp.pad(w16[:, r1:], ((0, 0), (0, LANES - MLA_ROPE)))
    wuq = w_uq.astype(BF16).reshape(MLA_Q_RANK, MLA_HEADS, MLA_QK)
    wqn = wuq[:, :, :MLA_NOPE].reshape(MLA_Q_RANK, MLA_HEADS * MLA_NOPE)
    wqr = wuq[:, :, MLA_NOPE:].reshape(MLA_Q_RANK, MLA_HEADS * MLA_ROPE)
    wukv = w_ukv.astype(BF16).reshape(MLA_KV_RANK, MLA_HEADS, MLA_NOPE + MLA_V)
    wkn = wukv[:, :, :MLA_NOPE].reshape(MLA_KV_RANK, MLA_HEADS * MLA_NOPE)
    wv = wukv[:, :, MLA_NOPE:].reshape(MLA_KV_RANK, MLA_HEADS * MLA_V)
    q, k, v = _mla_proj(x, nw, sc, sh, w16[:, :r0], w16[:, r0:r1], wkr, q_norm.reshape(1, -1),
                        kv_norm.reshape(1, -1), wqn, wqr, wkn, wv, cos, sin)
    o = _mla_attention(q, k, v)
    return [o], [w_out.astype(BF16)]


def kernel(x, c, positions, mod_w, mod_b, norm_mix, norm_ffn, hy_w_in, gdn_conv, gdn_a_log, gdn_dt_bias, gdn_o_norm, hy_w_out, mla_w_in, mla_q_norm, mla_kv_norm, mla_w_uq, mla_w_ukv, mla_w_out, peer_w_q, peer_sub_keys, peer_u, peer_v, final_norm):
    bsz, s, d = x.shape
    assert bsz == 1 and d == D_MODEL
    depth = mod_w.shape[0]
    xs = x.reshape(s, d)
    mod = _modulation(c.reshape(d, 1), mod_w, mod_b)
    cos_b, sin_b, cos_c, sin_c = _rope_tables(positions.reshape(s, 1))
    u16 = peer_u.astype(BF16)
    vt16 = jnp.swapaxes(peer_v, 1, 2).astype(BF16)
    wq16 = peer_w_q.astype(BF16)
    keys16 = peer_sub_keys.astype(BF16).reshape(depth, 2 * PEER_HEADS, PEER_NKEYS, PEER_DSUB)
    for layer in range(depth):
        sh1, sc1, g1, sh2, sc2, g2 = (mod[layer, :, n * d:(n + 1) * d] for n in range(6))
        nw = norm_mix[layer].reshape(1, d)
        i = layer // 2
        if layer % 2 == 0:
            outs, w_outs = _hybrid_mixer(xs, nw, sc1, sh1, cos_b, sin_b, hy_w_in[i], gdn_conv[i], gdn_a_log[i],
                                         gdn_dt_bias[i], gdn_o_norm[i], hy_w_out[i])
        else:
            outs, w_outs = _mla_mixer(xs, nw, sc1, sh1, cos_c, sin_c, mla_w_in[i], mla_q_norm[i], mla_kv_norm[i],
                                      mla_w_uq[i], mla_w_ukv[i], mla_w_out[i])
        xs, h2, scores = _outproj_peer_proj(outs, w_outs, xs, g1, norm_ffn[layer].reshape(1, d), sc2, sh2,
                                            wq16[layer], keys16[layer])
        a, n, b, rank = _peer_topk(scores)
        xs = _peer_dense(h2, a, n, b, rank, u16, vt16, layer, xs, g2)
    return _final_norm(xs, final_norm.reshape(1, d)).reshape(bsz, s, d)
```

```python
import functools
import math

import jax
import jax.numpy as jnp
from jax import lax
from jax.experimental import pallas as pl
from jax.experimental.pallas import tpu as pltpu

F32 = jnp.float32
BF16 = jnp.bfloat16
NEG_INF = float("-inf")
HIGHEST = lax.Precision.HIGHEST

D_MODEL = 1024
EPS = 1e-6
ROPE_THETA = 10000.0
LOG2_E = math.log2(math.e)

GDN_HEADS = 4
GDN_DK = 128
GDN_CHUNK = 64
GDN_QK = GDN_HEADS * GDN_DK
GDN_V = GDN_HEADS * 128
GDN_CONV_CH = 2 * GDN_QK + GDN_V
GDN_ROWS = 256

MOBA_HEADS = 4
MOBA_DH = 128
MOBA_BLOCK = 256
MOBA_TOPK = 3
MOBA_W = MOBA_HEADS * MOBA_DH

MLA_HEADS = 8
MLA_Q_RANK = 512
MLA_KV_RANK = 256
MLA_NOPE = 128
MLA_ROPE = 64
MLA_V = 128
MLA_QK = MLA_NOPE + MLA_ROPE
KEY_BLOCK = 256
MLA_Q_BLOCK = 512

PEER_HEADS = 8
PEER_NKEYS = 128
PEER_TOPK = 16
PEER_DSUB = 128
PEER_TOK = 512
PEER_EXP = 2048
PEER_TOPK_TOK = 256

LANES = 128
BF16_ROWS = 16
SUM_ROWS = BF16_ROWS
VMEM_LIMIT = 56 * 1024 * 1024


def _params(*sem):
    return pltpu.CompilerParams(dimension_semantics=sem, vmem_limit_bytes=VMEM_LIMIT)


def _dot(a, b, precision=None):
    return jnp.dot(a, b, preferred_element_type=F32, precision=precision)


def _dot_nt(a, b):
    return lax.dot_general(a, b, (((1,), (1,)), ((), ())), preferred_element_type=F32)


def _sigmoid(x):
    return jax.nn.sigmoid(x)


def _norm_mod(x, nw, sc, sh):
    y = x * lax.rsqrt(jnp.mean(x * x, axis=-1, keepdims=True) + EPS) * nw
    return y * (1.0 + sc) + sh


def _full(shape):
    n = len(shape)
    return pl.BlockSpec(shape, lambda *_: (0,) * n)


def _mod_kernel(c_ref, w_ref, b_ref, o_ref):
    c = c_ref[...]
    cond = c * _sigmoid(c)
    o_ref[0] = jnp.sum(cond * w_ref[0], axis=0, keepdims=True) + b_ref[0]


def _modulation(c_col, mod_w, mod_b):
    depth, d, n = mod_w.shape
    tn = 1536
    return pl.pallas_call(
        _mod_kernel,
        grid=(depth, n // tn),
        in_specs=[pl.BlockSpec((d, 1), lambda l, j: (0, 0)),
                  pl.BlockSpec((1, d, tn), lambda l, j: (l, 0, j)),
                  pl.BlockSpec((1, 1, tn), lambda l, j: (l, 0, j))],
        out_specs=pl.BlockSpec((1, 1, tn), lambda l, j: (l, 0, j)),
        out_shape=jax.ShapeDtypeStruct((depth, 1, n), F32),
        compiler_params=_params("parallel", "parallel"),
        name="adaln_modulation",
    )(c_col, mod_w, mod_b.reshape(depth, 1, n))


def _rope_kernel(pos_ref, fb_ref, fc_ref, cb_ref, sb_ref, cc_ref, sc_ref):
    p = pos_ref[...].astype(F32)
    lane = lax.broadcasted_iota(jnp.int32, cb_ref.shape, 1)
    ab = p * fb_ref[...]
    sb = jnp.sin(ab)
    cb_ref[...] = jnp.cos(ab)
    sb_ref[...] = jnp.where(lane < MOBA_DH // 2, -sb, sb)
    ac = p * fc_ref[...]
    sc = jnp.sin(ac)
    cc_ref[...] = jnp.cos(ac)
    sc_ref[...] = jnp.where(lane % MLA_ROPE < MLA_ROPE // 2, -sc, sc)


def _rope_tables(pos_col):
    s = pos_col.shape[0]
    tm = min(s, 1024)
    fb = ROPE_THETA ** (-jnp.arange(0, MOBA_DH, 2, dtype=F32) / MOBA_DH)
    fc = ROPE_THETA ** (-jnp.arange(0, MLA_ROPE, 2, dtype=F32) / MLA_ROPE)
    fb = jnp.tile(fb, 2).reshape(1, LANES)
    fc = jnp.tile(fc, 4).reshape(1, LANES)
    tab = pl.BlockSpec((tm, LANES), lambda i: (i, 0))
    return pl.pallas_call(
        _rope_kernel,
        grid=(s // tm,),
        in_specs=[pl.BlockSpec((tm, 1), lambda i: (i, 0)), _full((1, LANES)), _full((1, LANES))],
        out_specs=[tab] * 4,
        out_shape=[jax.ShapeDtypeStruct((s, LANES), F32)] * 4,
        compiler_params=_params("parallel"),
        name="rope_tables",
    )(pos_col, fb, fc)


def _rope_head128(x, cos, sin):
    return x * cos + pltpu.roll(x, MOBA_DH // 2, 1) * sin


def _rope_heads64(x, cos, sin):
    lane = lax.broadcasted_iota(jnp.int32, x.shape, 1)
    first = lane % MLA_ROPE < MLA_ROPE // 2
    partner = jnp.where(first, pltpu.roll(x, LANES - MLA_ROPE // 2, 1), pltpu.roll(x, MLA_ROPE // 2, 1))
    return x * cos + partner * sin


def _hy_proj_kernel(x_ref, nw_ref, sc_ref, sh_ref, wa_ref, wz_ref, wba_ref, wb_ref, cos_ref, sin_ref,
                    qkva_ref, z_ref, ba_ref, qb_ref, kb_ref, vb_ref):
    h = _norm_mod(x_ref[...], nw_ref[...], sc_ref[...], sh_ref[...]).astype(BF16)
    qkva_ref[...] = _dot(h, wa_ref[...])
    z_ref[...] = _dot(h, wz_ref[...])
    ba_ref[...] = _dot(h, wba_ref[...])
    qkvb = _dot(h, wb_ref[...])
    cos = cos_ref[...]
    sin = sin_ref[...]
    scale = MOBA_DH ** -0.5 * LOG2_E
    for hh in range(MOBA_HEADS):
        lo = hh * MOBA_DH
        q = qkvb[:, lo:lo + MOBA_DH]
        k = qkvb[:, MOBA_W + lo:MOBA_W + lo + MOBA_DH]
        qb_ref[hh, 0] = jnp.transpose(_rope_head128(q, cos, sin) * scale).astype(BF16)
        kb_ref[:, lo:lo + MOBA_DH] = _rope_head128(k, cos, sin).astype(BF16)
        v = qkvb[:, 2 * MOBA_W + lo:2 * MOBA_W + lo + MOBA_DH]
        vb_ref[hh, 0] = _with_sum_rows(jnp.transpose(v))


def _hy_proj(x, nw, sc, sh, wa, wz, wba, wb, cos, sin):
    s, d = x.shape
    tm = KEY_BLOCK
    row = lambda n: pl.BlockSpec((tm, n), lambda i: (i, 0))
    vec = _full((1, d))
    return pl.pallas_call(
        _hy_proj_kernel,
        grid=(s // tm,),
        in_specs=[row(d), vec, vec, vec, _full(wa.shape), _full(wz.shape), _full(wba.shape),
                  _full(wb.shape), row(LANES), row(LANES)],
        out_specs=[row(GDN_CONV_CH), row(GDN_V), row(LANES),
                   pl.BlockSpec((MOBA_HEADS, 1, MOBA_DH, tm), lambda i: (0, i, 0, 0)), row(MOBA_W),
                   pl.BlockSpec((MOBA_HEADS, 1, MOBA_DH + SUM_ROWS, tm), lambda i: (0, i, 0, 0))],
        out_shape=[jax.ShapeDtypeStruct((s, GDN_CONV_CH), F32),
                   jax.ShapeDtypeStruct((s, GDN_V), F32),
                   jax.ShapeDtypeStruct((s, LANES), F32),
                   jax.ShapeDtypeStruct((MOBA_HEADS, s // tm, MOBA_DH, tm), BF16),
                   jax.ShapeDtypeStruct((s, MOBA_W), BF16),
                   jax.ShapeDtypeStruct((MOBA_HEADS, s // tm, MOBA_DH + SUM_ROWS, tm), BF16)],
        compiler_params=_params("parallel"),
        name="hybrid_in_proj",
    )(x, nw, sc, sh, wa, wz, wba, wb, cos, sin)


def _unit_lower_inverses(l_stricts, eye):
    powers = [-l for l in l_stricts]
    invs = [eye + p for p in powers]
    p = 2
    while p < GDN_CHUNK:
        powers = [_dot(x.astype(BF16), x.astype(BF16)) for x in powers]
        invs = [_dot(inv.astype(BF16), (eye + x).astype(BF16)) for inv, x in zip(invs, powers)]
        p *= 2
    return invs


def _gdn_kernel(qkv_ref, ba_ref, z_ref, conv_ref, alog_ref, dtb_ref, onorm_ref, o_ref, ext_ref, state_ref):
    tb = GDN_ROWS
    cs = GDN_CHUNK

    @pl.when(pl.program_id(0) == 0)
    def _():
        ext_ref[0:8, :] = jnp.zeros((8, GDN_CONV_CH), F32)
        state_ref[...] = jnp.zeros_like(state_ref)

    ext_ref[8:8 + tb, :] = qkv_ref[...]
    cw = conv_ref[...]
    y = (ext_ref[5:5 + tb, :] * cw[0:1] + ext_ref[6:6 + tb, :] * cw[1:2]
         + ext_ref[7:7 + tb, :] * cw[2:3] + ext_ref[8:8 + tb, :] * cw[3:4])
    ext_ref[0:8, :] = ext_ref[tb:tb + 8, :]
    y = y * _sigmoid(y)

    ba = ba_ref[...]
    beta_all = _sigmoid(ba)
    t = ba + dtb_ref[...]
    softplus = jnp.maximum(t, 0.0) + jnp.log1p(jnp.exp(-jnp.abs(t)))
    g_all = -jnp.exp(alog_ref[...]) * softplus

    row = lax.broadcasted_iota(jnp.int32, (tb, tb), 0)
    col = lax.broadcasted_iota(jnp.int32, (tb, tb), 1)
    same = (row // cs) == (col // cs)
    causal = same & (col <= row)
    strict = same & (col < row)
    eye = (row == col).astype(F32)
    g_cum = _dot(causal.astype(F32), g_all, HIGHEST)
    g_tot = _dot(same.astype(F32), g_all, HIGHEST)
    g_cum_t = jnp.transpose(g_cum)
    z = z_ref[...]
    onorm = onorm_ref[...]

    heads = range(GDN_HEADS)
    rhs, l_strict, gtot, qk, qdec, kdec = [], [], [], [], [], []
    for h in heads:
        lo = h * GDN_DK
        q = y[:, lo:lo + GDN_DK]
        k = y[:, GDN_QK + lo:GDN_QK + lo + GDN_DK]
        v = y[:, 2 * GDN_QK + lo:2 * GDN_QK + lo + GDN_DK]
        q = q * lax.rsqrt(jnp.sum(q * q, axis=-1, keepdims=True) + EPS) * (GDN_DK ** -0.5)
        k = k * lax.rsqrt(jnp.sum(k * k, axis=-1, keepdims=True) + EPS)
        beta = beta_all[:, h:h + 1]
        gcol = g_cum[:, GDN_HEADS + h:GDN_HEADS + h + 1]
        grow = g_cum_t[GDN_HEADS + h:GDN_HEADS + h + 1, :]
        gtot.append(g_tot[:, GDN_HEADS + h:GDN_HEADS + h + 1])
        decay = jnp.exp(jnp.where(causal, gcol - grow, NEG_INF))
        kb = k * beta
        k16 = k.astype(BF16)
        l_strict.append(jnp.where(strict, _dot_nt(kb.astype(BF16), k16) * decay, 0.0))
        eg = jnp.exp(gcol)
        rhs.append(jnp.concatenate([v * beta, kb * eg], axis=1).astype(BF16))
        qk.append((_dot_nt(q.astype(BF16), k16) * decay).astype(BF16))
        qdec.append((q * eg).astype(BF16))
        kdec.append(k * jnp.exp(gtot[h] - gcol))

    uw = [_dot(t_inv.astype(BF16), rhs[h]) for h, t_inv in enumerate(_unit_lower_inverses(l_strict, eye))]
    u = [x[:, :GDN_DK] for x in uw]
    w = [x[:, GDN_DK:].astype(BF16) for x in uw]

    state = [state_ref[h] for h in heads]
    v_new = [[] for _ in heads]
    o_state = [[] for _ in heads]
    for c in range(tb // cs):
        r0 = c * cs
        for h in heads:
            s16 = state[h].astype(BF16)
            vn = u[h][r0:r0 + cs] - _dot(w[h][r0:r0 + cs], s16)
            o_state[h].append(_dot(qdec[h][r0:r0 + cs], s16))
            kd_t = jnp.transpose(kdec[h][r0:r0 + cs]).astype(BF16)
            state[h] = state[h] * jnp.exp(gtot[h][r0:r0 + 1]) + _dot(kd_t, vn.astype(BF16))
            v_new[h].append(vn)

    for h in heads:
        lo = h * GDN_DK
        state_ref[h] = state[h]
        o = jnp.concatenate(o_state[h], axis=0) + _dot(qk[h], jnp.concatenate(v_new[h], axis=0).astype(BF16))
        o = o * lax.rsqrt(jnp.mean(o * o, axis=-1, keepdims=True) + EPS) * onorm
        zh = z[:, lo:lo + GDN_DK]
        o_ref[:, lo:lo + GDN_DK] = (o * (zh * _sigmoid(zh))).astype(BF16)


def _gdn(qkva, ba, z, conv_w, alog, dtb, onorm):
    s = qkva.shape[0]
    tb = GDN_ROWS
    row = lambda n: pl.BlockSpec((tb, n), lambda i: (i, 0))
    return pl.pallas_call(
        _gdn_kernel,
        grid=(s // tb,),
        in_specs=[row(GDN_CONV_CH), row(LANES), row(GDN_V), _full(conv_w.shape),
                  _full((1, LANES)), _full((1, LANES)), _full((1, LANES))],
        out_specs=row(GDN_V),
        out_shape=jax.ShapeDtypeStruct((s, GDN_V), BF16),
        scratch_shapes=[pltpu.VMEM((tb + 8, GDN_CONV_CH), F32),
                        pltpu.VMEM((GDN_HEADS, GDN_DK, 128), F32)],
        compiler_params=_params("arbitrary"),
        name="gated_delta_rule",
    )(qkva, ba, z, conv_w, alog, dtb, onorm)


def _scores_into(s_ref, mx_ref, slot, k_tile, q_t, mask):
    s = _dot(k_tile, q_t)
    rows = s.shape[0]
    top = None
    for lo, piece in mask(s):
        s_ref[slot, lo:lo + piece.shape[0], :] = piece
        pmax = piece.max(axis=0, keepdims=True)
        top = pmax if top is None else jnp.maximum(top, pmax)
    if rows < s_ref.shape[1]:
        s_ref[slot, rows:, :] = jnp.full((s_ref.shape[1] - rows, s.shape[1]), NEG_INF, F32)
    mx_ref[slot] = top


def _values(p_ref, slot, v_at, blocks):
    v_t = jnp.concatenate([v_at(j) for j in blocks], axis=1)
    return _dot(v_t, p_ref[slot])


def _flash_steps(n_steps, first_blocks, first_keys, step_blocks, k_at, v_at, q_t, mask_first, mask_step,
                 s_ref, mx_ref, p_ref, acc_ref):
    per_step = len(first_blocks)
    _scores_into(s_ref, mx_ref, 0, k_at(first_blocks[0], first_keys), q_t, mask_first)
    m = mx_ref[0]
    p_ref[0] = jnp.exp2(s_ref[0] - m).astype(BF16)
    acc_ref[...] = jnp.zeros_like(acc_ref)
    _scores_into(s_ref, mx_ref, 1, k_at(step_blocks(1)[0], per_step), q_t, lambda s: mask_step(s, 1))

    def step(t, carry, cur):
        m, alpha_prev = carry
        nxt = jnp.minimum(t + 1, jnp.maximum(n_steps, 1))
        _scores_into(s_ref, mx_ref, 1 - cur, k_at(step_blocks(nxt)[0], per_step), q_t, lambda s: mask_step(s, nxt))
        m_new = jnp.maximum(m, mx_ref[cur])
        alpha = jnp.exp2(m - m_new)
        p = jnp.exp2(s_ref[cur] - m_new)
        prev = [jnp.where(t == 1, a, b) for a, b in zip(first_blocks, step_blocks(jnp.maximum(t - 1, 1)))]
        acc_ref[...] = alpha_prev * acc_ref[...] + _values(p_ref, 1 - cur, v_at, prev)
        p_ref[cur] = p.astype(BF16)
        return m_new, alpha

    def body(t, carry):
        return lax.cond(t % 2 == 0, lambda c: step(t, c, 0), lambda c: step(t, c, 1), carry)

    m, alpha = lax.fori_loop(1, n_steps + 1, body, (m, jnp.ones_like(m)))
    last = [jnp.where(n_steps == 0, a, b) for a, b in zip(first_blocks, step_blocks(jnp.maximum(n_steps, 1)))]
    acc = alpha * acc_ref[...] + _values(p_ref, n_steps % 2, v_at, last)
    dh = acc.shape[0] - SUM_ROWS
    return acc[:dh] / acc[dh:dh + 1]


def _with_sum_rows(v_t):
    extra = lax.broadcasted_iota(jnp.int32, (SUM_ROWS, v_t.shape[1]), 0) == 0
    return jnp.concatenate([v_t, extra.astype(F32)], axis=0).astype(BF16)


def _moba_kernel(q_ref, k_ref, v_ref, o_ref, kmean_ref, sel_ref, s_ref, mx_ref, p_ref, acc_ref, *, n_blocks):
    i = pl.program_id(1)
    bs = MOBA_BLOCK

    @pl.when(i == 0)
    def _():
        kmean_ref[...] = jnp.zeros_like(kmean_ref)

        def mean_body(n, carry):
            blk = k_ref[pl.ds(pl.multiple_of(n * bs, bs), bs), :].astype(F32)
            kmean_ref[pl.ds(n, 1), :] = jnp.mean(blk, axis=0, keepdims=True)
            return carry

        lax.fori_loop(0, n_blocks, mean_body, 0)

    q_t = jnp.concatenate([q_ref[0, 0], q_ref[0, 1]], axis=1)
    gate = _dot(kmean_ref[...].astype(BF16), q_t)
    blk_id = lax.broadcasted_iota(jnp.int32, gate.shape, 0)
    second = lax.broadcasted_iota(jnp.int32, gate.shape, 1) >= bs
    gate = jnp.where(blk_id < 2 * i + second.astype(jnp.int32), gate, NEG_INF)
    sel = jnp.zeros(gate.shape, F32)
    for _ in range(MOBA_TOPK):
        gmax = jnp.max(gate, axis=0, keepdims=True)
        first = jnp.min(jnp.where(gate == gmax, blk_id, LANES), axis=0, keepdims=True)
        hit = blk_id == first
        sel = jnp.where(hit & (gmax > NEG_INF), 1.0, sel)
        gate = jnp.where(hit, NEG_INF, gate)
    sel_ref[...] = sel

    def own_mask(s):
        key = lax.broadcasted_iota(jnp.int32, (bs, 2 * bs), 0)
        qry = lax.broadcasted_iota(jnp.int32, (bs, 2 * bs), 1)
        picked = sel_ref[pl.ds(2 * i, 1), :] > 0.0
        first_ok = (key <= qry) & ((qry < bs) | picked)
        return [(0, jnp.where(first_ok, s[:bs], NEG_INF)), (bs, jnp.where(key + bs <= qry, s[bs:], NEG_INF))]

    def sel_mask(s, t):
        return [(sub * bs, jnp.where(sel_ref[pl.ds(2 * (t - 1) + sub, 1), :] > 0.0, s[sub * bs:(sub + 1) * bs], NEG_INF))
                for sub in range(2)]

    out = _flash_steps(
        n_steps=i, first_blocks=[2 * i, 2 * i + 1], first_keys=2,
        step_blocks=lambda t: [2 * (t - 1), 2 * (t - 1) + 1],
        k_at=lambda j, n: k_ref[pl.ds(pl.multiple_of(j * bs, bs), n * bs), :], v_at=lambda j: v_ref[0, j],
        q_t=q_t, mask_first=own_mask, mask_step=sel_mask,
        s_ref=s_ref, mx_ref=mx_ref, p_ref=p_ref, acc_ref=acc_ref)
    o_ref[...] = jnp.transpose(out).astype(o_ref.dtype)


def _moba(q_t, k, v_t):
    s = k.shape[0]
    bs = MOBA_BLOCK
    n_blocks = s // bs
    assert n_blocks <= LANES and bs == KEY_BLOCK and n_blocks % 2 == 0
    bq = 2 * bs
    kv = pl.BlockSpec((s, MOBA_DH), lambda h, i: (0, h))
    return pl.pallas_call(
        functools.partial(_moba_kernel, n_blocks=n_blocks),
        grid=(MOBA_HEADS, n_blocks // 2),
        in_specs=[pl.BlockSpec((1, 2, MOBA_DH, bs), lambda h, i: (h, i, 0, 0)), kv,
                  pl.BlockSpec((1, n_blocks, MOBA_DH + SUM_ROWS, bs), lambda h, i: (h, 0, 0, 0))],
        out_specs=pl.BlockSpec((bq, MOBA_DH), lambda h, i: (i, h)),
        out_shape=jax.ShapeDtypeStruct((s, MOBA_W), BF16),
        scratch_shapes=[pltpu.VMEM((LANES, MOBA_DH), F32),
                        pltpu.VMEM((LANES, bq), F32),
                        pltpu.VMEM((2, 2 * KEY_BLOCK, bq), F32),
                        pltpu.VMEM((2, 1, bq), F32),
                        pltpu.VMEM((2, 2 * KEY_BLOCK, bq), BF16),
                        pltpu.VMEM((MOBA_DH + SUM_ROWS, bq), F32)],
        compiler_params=_params("parallel", "arbitrary"),
        name="moba_attention",
    )(q_t, k, v_t)


def _outproj_kernel(*refs, n_in):
    o_refs = refs[:n_in]
    w_refs = refs[n_in:2 * n_in]
    x_ref, g_ref, out_ref = refs[2 * n_in:]
    y = _dot(o_refs[0][...], w_refs[0][...])
    for o_ref, w_ref in zip(o_refs[1:], w_refs[1:]):
        y = y + _dot(o_ref[...], w_ref[...])
    out_ref[...] = x_ref[...] + g_ref[...] * y


def _outproj_residual(os_, ws, x, gate):
    s, d = x.shape
    tm = min(s, 512)
    row = lambda n: pl.BlockSpec((tm, n), lambda i: (i, 0))
    return pl.pallas_call(
        functools.partial(_outproj_kernel, n_in=len(os_)),
        grid=(s // tm,),
        in_specs=[row(o.shape[1]) for o in os_] + [_full(w.shape) for w in ws] + [row(d), _full((1, d))],
        out_specs=row(d),
        out_shape=jax.ShapeDtypeStruct((s, d), F32),
        compiler_params=_params("parallel"),
        name="out_proj_residual",
    )(*os_, *ws, x, gate)


def _mla_proj_kernel(x_ref, nw_ref, sc_ref, sh_ref, wcq_ref, wckv_ref, wkr_ref, qn_ref, kvn_ref,
                     wqn_ref, wqr_ref, wkn_ref, wv_ref, cos_ref, sin_ref, q_ref, k_ref, v_ref):
    h = _norm_mod(x_ref[...], nw_ref[...], sc_ref[...], sh_ref[...]).astype(BF16)
    cq = _dot(h, wcq_ref[...])
    ckv = _dot(h, wckv_ref[...])
    kr = _dot(h, wkr_ref[...])
    cq = (cq * lax.rsqrt(jnp.mean(cq * cq, axis=-1, keepdims=True) + EPS) * qn_ref[...]).astype(BF16)
    ckv = (ckv * lax.rsqrt(jnp.mean(ckv * ckv, axis=-1, keepdims=True) + EPS) * kvn_ref[...]).astype(BF16)
    cos = cos_ref[...]
    sin = sin_ref[...]
    scale = MLA_QK ** -0.5 * LOG2_E
    q_nope = _dot(cq, wqn_ref[...]) * scale
    q_rope = _dot(cq, wqr_ref[...])
    k_nope = _dot(ckv, wkn_ref[...])
    val = _dot(ckv, wv_ref[...])
    k_rope = _rope_heads64(kr, cos, sin)[:, :MLA_ROPE].astype(BF16)
    for pair in range(MLA_HEADS // 2):
        slab_t = jnp.transpose(_rope_heads64(q_rope[:, pair * LANES:(pair + 1) * LANES], cos, sin) * scale)
        for sub in range(2):
            hh = 2 * pair + sub
            q_ref[hh, 0, MLA_NOPE:, :] = slab_t[sub * MLA_ROPE:(sub + 1) * MLA_ROPE].astype(BF16)
    for hh in range(MLA_HEADS):
        q_ref[hh, 0, :MLA_NOPE, :] = jnp.transpose(q_nope[:, hh * MLA_NOPE:(hh + 1) * MLA_NOPE]).astype(BF16)
        k_ref[hh, :, :MLA_NOPE] = k_nope[:, hh * MLA_NOPE:(hh + 1) * MLA_NOPE].astype(BF16)
        k_ref[hh, :, MLA_NOPE:] = k_rope
        v_ref[hh, 0] = _with_sum_rows(jnp.transpose(val[:, hh * MLA_V:(hh + 1) * MLA_V]))


def _mla_proj(x, nw, sc, sh, wcq, wckv, wkr, qn, kvn, wqn, wqr, wkn, wv, cos, sin):
    s, d = x.shape
    tm = KEY_BLOCK
    row = lambda n: pl.BlockSpec((tm, n), lambda i: (i, 0))
    vec = _full((1, d))
    heads = lambda n: pl.BlockSpec((MLA_HEADS, tm, n), lambda i: (0, i, 0))
    return pl.pallas_call(
        _mla_proj_kernel,
        grid=(s // tm,),
        in_specs=[row(d), vec, vec, vec, _full(wcq.shape), _full(wckv.shape), _full(wkr.shape),
                  _full(qn.shape), _full(kvn.shape), _full(wqn.shape), _full(wqr.shape),
                  _full(wkn.shape), _full(wv.shape), row(LANES), row(LANES)],
        out_specs=[pl.BlockSpec((MLA_HEADS, 1, MLA_QK, tm), lambda i: (0, i, 0, 0)), heads(MLA_QK),
                   pl.BlockSpec((MLA_HEADS, 1, MLA_V + SUM_ROWS, tm), lambda i: (0, i, 0, 0))],
        out_shape=[jax.ShapeDtypeStruct((MLA_HEADS, s // tm, MLA_QK, tm), BF16),
                   jax.ShapeDtypeStruct((MLA_HEADS, s, MLA_QK), BF16),
                   jax.ShapeDtypeStruct((MLA_HEADS, s // tm, MLA_V + SUM_ROWS, tm), BF16)],
        compiler_params=_params("parallel"),
        name="mla_in_proj",
    )(x, nw, sc, sh, wcq, wckv, wkr, qn, kvn, wqn, wqr, wkn, wv, cos, sin)


def _mla_attn_kernel(q_ref, k_ref, v_ref, o_ref, s_ref, mx_ref, p_ref, acc_ref):
    i = pl.program_id(1)

    def causal(s):
        key = lax.broadcasted_iota(jnp.int32, s.shape, 0)
        qry = lax.broadcasted_iota(jnp.int32, s.shape, 1)
        return [(0, jnp.where(key <= qry, s, NEG_INF))]

    per_q = MLA_Q_BLOCK // KEY_BLOCK
    q_t = jnp.concatenate([q_ref[0, b] for b in range(per_q)], axis=1)
    out = _flash_steps(
        n_steps=i, first_blocks=[2 * i, 2 * i + 1], first_keys=2,
        step_blocks=lambda t: [2 * (t - 1), 2 * (t - 1) + 1],
        k_at=lambda j, n: k_ref[0, pl.ds(pl.multiple_of(j * KEY_BLOCK, KEY_BLOCK), n * KEY_BLOCK), :],
        v_at=lambda j: v_ref[0, j], q_t=q_t, mask_first=causal, mask_step=lambda s, t: [(0, s)],
        s_ref=s_ref, mx_ref=mx_ref, p_ref=p_ref, acc_ref=acc_ref)
    o_ref[...] = jnp.transpose(out).astype(o_ref.dtype)


def _mla_attention(q_t, k, v_t):
    _, s, _ = k.shape
    bq = MLA_Q_BLOCK
    per_q = bq // KEY_BLOCK
    assert per_q == 2
    return pl.pallas_call(
        _mla_attn_kernel,
        grid=(MLA_HEADS, s // bq),
        in_specs=[pl.BlockSpec((1, per_q, MLA_QK, KEY_BLOCK), lambda h, i: (h, i, 0, 0)),
                  pl.BlockSpec((1, s, MLA_QK), lambda h, i: (h, 0, 0)),
                  pl.BlockSpec((1, s // KEY_BLOCK, MLA_V + SUM_ROWS, KEY_BLOCK), lambda h, i: (h, 0, 0, 0))],
        out_specs=pl.BlockSpec((bq, MLA_V), lambda h, i: (i, h)),
        out_shape=jax.ShapeDtypeStruct((s, MLA_HEADS * MLA_V), BF16),
        scratch_shapes=[pltpu.VMEM((2, 2 * KEY_BLOCK, bq), F32), pltpu.VMEM((2, 1, bq), F32),
                        pltpu.VMEM((2, 2 * KEY_BLOCK, bq), BF16), pltpu.VMEM((MLA_V + SUM_ROWS, bq), F32)],
        compiler_params=_params("parallel", "arbitrary"),
        name="mla_attention",
    )(q_t, k, v_t)


def _peer_proj_kernel(x_ref, nw_ref, sc_ref, sh_ref, wq_ref, keys_ref, h_ref, s_ref):
    h = _norm_mod(x_ref[...], nw_ref[...], sc_ref[...], sh_ref[...]).astype(BF16)
    h_ref[...] = h
    q = _dot(h, wq_ref[...]).astype(BF16)
    for hp in range(2 * PEER_HEADS):
        s_ref[hp] = _dot_nt(keys_ref[hp], q[:, hp * PEER_DSUB:(hp + 1) * PEER_DSUB])


def _peer_proj(x, nw, sc, sh, wq, keys):
    s, d = x.shape
    tm = min(s, 256)
    vec = _full((1, d))
    return pl.pallas_call(
        _peer_proj_kernel,
        grid=(s // tm,),
        in_specs=[pl.BlockSpec((tm, d), lambda i: (i, 0)), vec, vec, vec, _full(wq.shape), _full(keys.shape)],
        out_specs=[pl.BlockSpec((tm, d), lambda i: (i, 0)),
                   pl.BlockSpec((2 * PEER_HEADS, PEER_NKEYS, tm), lambda i: (0, 0, i))],
        out_shape=[jax.ShapeDtypeStruct((s, d), BF16),
                   jax.ShapeDtypeStruct((2 * PEER_HEADS, PEER_NKEYS, s), F32)],
        compiler_params=_params("parallel"),
        name="peer_query_scores",
    )(x, nw, sc, sh, wq, keys)


def _compare_exchange(vs, i, l):
    vs[i], vs[l] = jnp.maximum(vs[i], vs[l]), jnp.minimum(vs[i], vs[l])


def _sort_desc(vs):
    vs = list(vs)
    n = len(vs)
    k = 2
    while k <= n:
        j = k // 2
        while j >= 1:
            for i in range(n):
                l = i ^ j
                if l > i:
                    if i & k == 0:
                        _compare_exchange(vs, i, l)
                    else:
                        _compare_exchange(vs, l, i)
            j //= 2
        k *= 2
    return vs


def _merge_top(a, b):
    n = len(a)
    vs = [jnp.maximum(a[i], b[n - 1 - i]) for i in range(n)]
    j = n // 2
    while j >= 1:
        for i in range(n):
            if i ^ j > i:
                _compare_exchange(vs, i, i ^ j)
        j //= 2
    return vs


def _top16_of_keys(groups):
    vs = _sort_desc(groups)
    for shift in (4, 2, 1):
        vs = _merge_top(vs, [pltpu.roll(v, shift, 0) for v in vs])
    return vs


def _peer_topk_kernel(s_ref, a_ref, n_ref, b_ref, rank_ref):
    sub = 8
    groups = PEER_NKEYS // sub

    def head_body(h, carry):
        g1 = [s_ref[2 * h, g * sub:(g + 1) * sub, :] for g in range(groups)]
        g2 = [s_ref[2 * h + 1, g * sub:(g + 1) * sub, :] for g in range(groups)]
        v1 = _top16_of_keys(g1)
        v2 = _top16_of_keys(g2)

        ninf = jnp.full_like(v1[0], NEG_INF)
        pad = lambda vs: vs + [ninf] * (PEER_TOPK - len(vs))
        row = lambda r1: [v1[r1] + v2[r2] for r2 in range(PEER_TOPK // (r1 + 1))]
        top = _merge_top(
            _merge_top(row(0), _merge_top(pad(row(1)), pad(row(2)))),
            _merge_top(_sort_desc(pad(row(3) + row(4) + row(5) + row(6) + row(7))),
                       pad([row(r1)[0] for r1 in range(8, PEER_TOPK)])))
        thr = top[PEER_TOPK - 1]
        z = jnp.ones_like(thr)
        for r in range(1, PEER_TOPK):
            z = z + jnp.exp(top[r] - top[0])
        half_inv_z = 0.5 / z

        split = 5
        hit = lambda s1, r2s: sum(jnp.where(s1 + v2[r2] >= thr, 1.0, 0.0) for r2 in r2s)
        n_top = [hit(v1[r1], range(split, PEER_TOPK // (r1 + 1))) for r1 in range(2)]

        b_all, rank_all = [], []
        for g in range(groups):
            rows = slice(g * sub, (g + 1) * sub)
            n = hit(g1[g], range(split)) + jnp.where(g1[g] == v1[0], n_top[0],
                                                     jnp.where(g1[g] == v1[1], n_top[1], 0.0))
            a_ref[h, rows, :] = jnp.exp(g1[g] - v1[0]) * half_inv_z
            n_ref[h, rows, :] = n
            b_all.append(jnp.exp(g2[g] - v2[0]))
            rank_all.append(sum(jnp.where(v2[r] > g2[g], 1.0, 0.0) for r in range(PEER_TOPK)))
        for pair in range(groups // 2):
            b_ref[h, pair] = jnp.concatenate(b_all[2 * pair:2 * pair + 2], axis=0).astype(BF16)
            rank_ref[h, pair] = jnp.concatenate(rank_all[2 * pair:2 * pair + 2], axis=0).astype(BF16)
        return carry

    lax.fori_loop(0, PEER_HEADS, head_body, 0)


def _peer_topk(scores):
    _, nk, s = scores.shape
    tt = min(s, PEER_TOPK_TOK)
    big = pl.BlockSpec((PEER_HEADS, nk, tt), lambda i: (0, 0, i))
    tiled = pl.BlockSpec((PEER_HEADS, nk // BF16_ROWS, BF16_ROWS, tt), lambda i: (0, 0, 0, i))
    shape = (PEER_HEADS, nk, s)
    shape16 = (PEER_HEADS, nk // BF16_ROWS, BF16_ROWS, s)
    return pl.pallas_call(
        _peer_topk_kernel,
        grid=(s // tt,),
        in_specs=[pl.BlockSpec((2 * PEER_HEADS, nk, tt), lambda i: (0, 0, i))],
        out_specs=[big, big, tiled, tiled],
        out_shape=[jax.ShapeDtypeStruct(shape, F32), jax.ShapeDtypeStruct(shape, F32),
                   jax.ShapeDtypeStruct(shape16, BF16), jax.ShapeDtypeStruct(shape16, BF16)],
        compiler_params=_params("parallel"),
        name="peer_topk_threshold",
    )(scores)


def _peer_dense_kernel(h_ref, a_ref, n_ref, b_ref, rank_ref, u_ref, vt_ref, x_ref, g_ref, o_ref,
                       acc_ref, p_ref):
    j = pl.program_id(1)
    nk = PEER_NKEYS

    @pl.when(j == 0)
    def _():
        acc_ref[...] = jnp.zeros_like(acc_ref)

    act_t = _dot_nt(u_ref[...], h_ref[...])
    for blk in range(PEER_EXP // nk):
        i1 = j * (PEER_EXP // nk) + blk
        wgt = jnp.zeros((nk // BF16_ROWS, BF16_ROWS, PEER_TOK), BF16)
        for h in range(PEER_HEADS):
            row = lambda ref: jnp.broadcast_to(ref[h, pl.ds(i1, 1), :], (BF16_ROWS, PEER_TOK)).astype(BF16)[None]
            chosen = rank_ref[h] < row(n_ref)
            wgt = wgt + jnp.where(chosen, b_ref[h], 0.0) * row(a_ref)
        act = act_t[blk * nk:(blk + 1) * nk]
        gelu2 = act * (1.0 + lax.erf(act * (2.0 ** -0.5)))
        p_ref[blk * nk:(blk + 1) * nk, :] = wgt.reshape(nk, PEER_TOK) * gelu2.astype(BF16)
    acc_ref[...] += _dot(vt_ref[...], p_ref[...])

    @pl.when(j == pl.num_programs(1) - 1)
    def _():
        o_ref[...] = x_ref[...] + g_ref[...] * jnp.transpose(acc_ref[...])


def _peer_dense(h2, a, n, b, rank, u16, vt16, layer, x, gate):
    s, d = x.shape
    n_exp = u16.shape[1]
    tt = min(s, PEER_TOK)
    assert tt == PEER_TOK
    tok3 = pl.BlockSpec((PEER_HEADS, PEER_NKEYS, tt), lambda i, j: (0, 0, i))
    tok4 = pl.BlockSpec((PEER_HEADS, PEER_NKEYS // BF16_ROWS, BF16_ROWS, tt), lambda i, j: (0, 0, 0, i))
    return pl.pallas_call(
        _peer_dense_kernel,
        grid=(s // tt, n_exp // PEER_EXP),
        in_specs=[pl.BlockSpec((tt, d), lambda i, j: (i, 0)),
                  tok3, tok3, tok4, tok4,
                  pl.BlockSpec((None, PEER_EXP, d), lambda i, j: (layer, j, 0)),
                  pl.BlockSpec((None, d, PEER_EXP), lambda i, j: (layer, 0, j)),
                  pl.BlockSpec((tt, d), lambda i, j: (i, 0)),
                  pl.BlockSpec((1, d), lambda i, j: (0, 0))],
        out_specs=pl.BlockSpec((tt, d), lambda i, j: (i, 0)),
        out_shape=jax.ShapeDtypeStruct((s, d), F32),
        scratch_shapes=[pltpu.VMEM((d, tt), F32), pltpu.VMEM((PEER_EXP, tt), BF16)],
        compiler_params=_params("parallel", "arbitrary"),
        name="peer_dense_experts",
    )(h2, a, n, b, rank, u16, vt16, x, gate)


def _final_norm_kernel(x_ref, w_ref, o_ref):
    x = x_ref[...]
    o_ref[...] = x * lax.rsqrt(jnp.mean(x * x, axis=-1, keepdims=True) + EPS) * w_ref[...]


def _final_norm(x, w):
    s, d = x.shape
    tm = min(s, 1024)
    return pl.pallas_call(
        _final_norm_kernel,
        grid=(s // tm,),
        in_specs=[pl.BlockSpec((tm, d), lambda i: (i, 0)), _full((1, d))],
        out_specs=pl.BlockSpec((tm, d), lambda i: (i, 0)),
        out_shape=jax.ShapeDtypeStruct((s, d), F32),
        compiler_params=_params("parallel"),
        name="final_rmsnorm",
    )(x, w)


def _lane_row(values, offset):
    return jnp.zeros((1, LANES), F32).at[0, offset:offset + values.shape[0]].set(values.astype(F32))


def _hybrid_layer(x, nw, sc, sh, gate, cos, sin, w_in, conv_w, a_log, dt_bias, o_norm, w_out):
    w16 = w_in.astype(BF16)
    c0 = GDN_CONV_CH
    c1 = c0 + GDN_V
    c2 = c1 + 2 * GDN_HEADS
    wba = jnp.pad(w16[:, c1:c2], ((0, 0), (0, LANES - 2 * GDN_HEADS)))
    qkva, z, ba, qb, kb, vb = _hy_proj(x, nw, sc, sh, w16[:, :c0], w16[:, c0:c1], wba, w16[:, c2:], cos, sin)
    o_a = _gdn(qkva, ba, z, conv_w, _lane_row(a_log, GDN_HEADS), _lane_row(dt_bias, GDN_HEADS),
               o_norm.reshape(1, -1))
    o_b = _moba(qb, kb, vb)
    wo16 = w_out.astype(BF16)
    return _outproj_residual([o_a, o_b], [wo16[:GDN_V], wo16[GDN_V:]], x, gate)


def _mla_layer(x, nw, sc, sh, gate, cos, sin, w_in, q_norm, kv_norm, w_uq, w_ukv, w_out):
    w16 = w_in.astype(BF16)
    r0 = MLA_Q_RANK
    r1 = r0 + MLA_KV_RANK
    wkr = jnp.pad(w16[:, r1:], ((0, 0), (0, LANES - MLA_ROPE)))
    wuq = w_uq.astype(BF16).reshape(MLA_Q_RANK, MLA_HEADS, MLA_QK)
    wqn = wuq[:, :, :MLA_NOPE].reshape(MLA_Q_RANK, MLA_HEADS * MLA_NOPE)
    wqr = wuq[:, :, MLA_NOPE:].reshape(MLA_Q_RANK, MLA_HEADS * MLA_ROPE)
    wukv = w_ukv.astype(BF16).reshape(MLA_KV_RANK, MLA_HEADS, MLA_NOPE + MLA_V)
    wkn = wukv[:, :, :MLA_NOPE].reshape(MLA_KV_RANK, MLA_HEADS * MLA_NOPE)
    wv = wukv[:, :, MLA_NOPE:].reshape(MLA_KV_RANK, MLA_HEADS * MLA_V)
    q, k, v = _mla_proj(x, nw, sc, sh, w16[:, :r0], w16[:, r0:r1], wkr, q_norm.reshape(1, -1),
                        kv_norm.reshape(1, -1), wqn, wqr, wkn, wv, cos, sin)
    o = _mla_attention(q, k, v)
    return _outproj_residual([o], [w_out.astype(BF16)], x, gate)


def _peer_layer(x, nw, sc, sh, gate, w_q, sub_keys, u16, vt16, layer):
    keys = sub_keys.astype(BF16).reshape(2 * PEER_HEADS, PEER_NKEYS, PEER_DSUB)
    h2, scores = _peer_proj(x, nw, sc, sh, w_q.astype(BF16), keys)
    a, n, b, rank = _peer_topk(scores)
    return _peer_dense(h2, a, n, b, rank, u16, vt16, layer, x, gate)


def kernel(x, c, positions, mod_w, mod_b, norm_mix, norm_ffn, hy_w_in, gdn_conv, gdn_a_log, gdn_dt_bias, gdn_o_norm, hy_w_out, mla_w_in, mla_q_norm, mla_kv_norm, mla_w_uq, mla_w_ukv, mla_w_out, peer_w_q, peer_sub_keys, peer_u, peer_v, final_norm):
    bsz, s, d = x.shape
    assert bsz == 1 and d == D_MODEL
    depth = mod_w.shape[0]
    xs = x.reshape(s, d)
    mod = _modulation(c.reshape(d, 1), mod_w, mod_b)
    cos_b, sin_b, cos_c, sin_c = _rope_tables(positions.reshape(s, 1))
    u16 = peer_u.astype(BF16)
    vt16 = jnp.swapaxes(peer_v.astype(BF16), 1, 2)
    for layer in range(depth):
        sh1, sc1, g1, sh2, sc2, g2 = (mod[layer, :, n * d:(n + 1) * d] for n in range(6))
        nw = norm_mix[layer].reshape(1, d)
        i = layer // 2
        if layer % 2 == 0:
            xs = _hybrid_layer(xs, nw, sc1, sh1, g1, cos_b, sin_b, hy_w_in[i], gdn_conv[i], gdn_a_log[i],
                               gdn_dt_bias[i], gdn_o_norm[i], hy_w_out[i])
        else:
            xs = _mla_layer(xs, nw, sc1, sh1, g1, cos_c, sin_c, mla_w_in[i], mla_q_norm[i], mla_kv_norm[i],
                            mla_w_uq[i], mla_w_ukv[i], mla_w_out[i])
        xs = _peer_layer(xs, norm_ffn[layer].reshape(1, d), sc2, sh2, g2, peer_w_q[layer],
                         peer_sub_keys[layer], u16, vt16, layer)
    return _final_norm(xs, final_norm.reshape(1, d)).reshape(bsz, s, d)
```

```python
import functools
import math

import jax
import jax.numpy as jnp
from jax import lax
from jax.experimental import pallas as pl
from jax.experimental.pallas import tpu as pltpu

F32 = jnp.float32
BF16 = jnp.bfloat16
NEG_INF = float("-inf")
HIGHEST = lax.Precision.HIGHEST

D_MODEL = 1024
MOD_COLS = 1536
EPS = 1e-6
ROPE_THETA = 10000.0
LOG2_E = math.log2(math.e)

GDN_HEADS = 4
GDN_DK = 128
GDN_CHUNK = 64
GDN_QK = GDN_HEADS * GDN_DK
GDN_DV = 128
GDN_V = GDN_HEADS * GDN_DV
GDN_CONV = 4
GDN_CONV_CH = 2 * GDN_QK + GDN_V
GDN_ROWS = 256

MOBA_HEADS = 4
MOBA_DH = 128
MOBA_BLOCK = 256
MOBA_TOPK = 3
MOBA_W = MOBA_HEADS * MOBA_DH

MLA_HEADS = 8
MLA_Q_RANK = 512
MLA_KV_RANK = 256
MLA_NOPE = 128
MLA_ROPE = 64
MLA_V = 128
MLA_QK = MLA_NOPE + MLA_ROPE
KEY_BLOCK = 256
MLA_Q_BLOCK = 512

PEER_HEADS = 8
PEER_NKEYS = 128
PEER_TOPK = 16
PEER_DSUB = 128
PEER_TOK = 512
PEER_EXP = 2048
PEER_TOPK_TOK = 256

LANES = 128
SUBLANES = 8
BF16_ROWS = 16
SUM_ROWS = BF16_ROWS
VMEM_LIMIT = 56 * 1024 * 1024


def _params(*sem):
    return pltpu.CompilerParams(dimension_semantics=sem, vmem_limit_bytes=VMEM_LIMIT)


def _dot(a, b, precision=None):
    return jnp.dot(a, b, preferred_element_type=F32, precision=precision)


def _dot_nt(a, b):
    return lax.dot_general(a, b, (((1,), (1,)), ((), ())), preferred_element_type=F32)


def _sigmoid(x):
    return jax.nn.sigmoid(x)


def _norm_mod(x, nw, sc, sh):
    y = x * lax.rsqrt(jnp.mean(x * x, axis=-1, keepdims=True) + EPS) * nw
    return y * (1.0 + sc) + sh


def _full(shape):
    n = len(shape)
    return pl.BlockSpec(shape, lambda *_: (0,) * n)


def _mod_kernel(c_ref, w_ref, b_ref, o_ref):
    c = c_ref[...]
    cond = c * _sigmoid(c)
    o_ref[0] = jnp.sum(cond * w_ref[0], axis=0, keepdims=True) + b_ref[0]


def _modulation(c_col, mod_w, mod_b):
    depth, d, n = mod_w.shape
    tn = MOD_COLS
    return pl.pallas_call(
        _mod_kernel,
        grid=(depth, n // tn),
        in_specs=[pl.BlockSpec((d, 1), lambda l, j: (0, 0)),
                  pl.BlockSpec((1, d, tn), lambda l, j: (l, 0, j)),
                  pl.BlockSpec((1, 1, tn), lambda l, j: (l, 0, j))],
        out_specs=pl.BlockSpec((1, 1, tn), lambda l, j: (l, 0, j)),
        out_shape=jax.ShapeDtypeStruct((depth, 1, n), F32),
        compiler_params=_params("parallel", "parallel"),
        name="adaln_modulation",
    )(c_col, mod_w, mod_b.reshape(depth, 1, n))


def _rope_kernel(pos_ref, fb_ref, fc_ref, cb_ref, sb_ref, cc_ref, sc_ref):
    p = pos_ref[...].astype(F32)
    lane = lax.broadcasted_iota(jnp.int32, cb_ref.shape, 1)
    ab = p * fb_ref[...]
    sb = jnp.sin(ab)
    cb_ref[...] = jnp.cos(ab)
    sb_ref[...] = jnp.where(lane < MOBA_DH // 2, -sb, sb)
    ac = p * fc_ref[...]
    sc = jnp.sin(ac)
    cc_ref[...] = jnp.cos(ac)
    sc_ref[...] = jnp.where(lane % MLA_ROPE < MLA_ROPE // 2, -sc, sc)


def _rope_tables(pos_col):
    s = pos_col.shape[0]
    tm = min(s, 1024)
    fb = ROPE_THETA ** (-jnp.arange(0, MOBA_DH, 2, dtype=F32) / MOBA_DH)
    fc = ROPE_THETA ** (-jnp.arange(0, MLA_ROPE, 2, dtype=F32) / MLA_ROPE)
    fb = jnp.tile(fb, 2).reshape(1, LANES)
    fc = jnp.tile(fc, 4).reshape(1, LANES)
    tab = pl.BlockSpec((tm, LANES), lambda i: (i, 0))
    return pl.pallas_call(
        _rope_kernel,
        grid=(s // tm,),
        in_specs=[pl.BlockSpec((tm, 1), lambda i: (i, 0)), _full((1, LANES)), _full((1, LANES))],
        out_specs=[tab] * 4,
        out_shape=[jax.ShapeDtypeStruct((s, LANES), F32)] * 4,
        compiler_params=_params("parallel"),
        name="rope_tables",
    )(pos_col, fb, fc)


def _rope_head128(x, cos, sin):
    return x * cos + pltpu.roll(x, MOBA_DH // 2, 1) * sin


def _rope_heads64(x, cos, sin):
    lane = lax.broadcasted_iota(jnp.int32, x.shape, 1)
    first = lane % MLA_ROPE < MLA_ROPE // 2
    partner = jnp.where(first, pltpu.roll(x, LANES - MLA_ROPE // 2, 1), pltpu.roll(x, MLA_ROPE // 2, 1))
    return x * cos + partner * sin


def _hy_proj_kernel(x_ref, nw_ref, sc_ref, sh_ref, wa_ref, wz_ref, wba_ref, wb_ref, cos_ref, sin_ref,
                    qkva_ref, z_ref, ba_ref, qb_ref, kb_ref, vb_ref):
    h = _norm_mod(x_ref[...], nw_ref[...], sc_ref[...], sh_ref[...]).astype(BF16)
    qkva_ref[...] = _dot(h, wa_ref[...])
    z_ref[...] = _dot(h, wz_ref[...])
    ba_ref[...] = _dot(h, wba_ref[...])
    qkvb = _dot(h, wb_ref[...])
    cos = cos_ref[...]
    sin = sin_ref[...]
    scale = MOBA_DH ** -0.5 * LOG2_E
    for hh in range(MOBA_HEADS):
        lo = hh * MOBA_DH
        q = qkvb[:, lo:lo + MOBA_DH]
        k = qkvb[:, MOBA_W + lo:MOBA_W + lo + MOBA_DH]
        qb_ref[hh, 0] = jnp.transpose(_rope_head128(q, cos, sin) * scale).astype(BF16)
        kb_ref[:, lo:lo + MOBA_DH] = _rope_head128(k, cos, sin).astype(BF16)
        v = qkvb[:, 2 * MOBA_W + lo:2 * MOBA_W + lo + MOBA_DH]
        vb_ref[hh, 0] = _with_sum_rows(jnp.transpose(v))


def _hy_proj(x, nw, sc, sh, wa, wz, wba, wb, cos, sin):
    s, d = x.shape
    tm = KEY_BLOCK
    row = lambda n: pl.BlockSpec((tm, n), lambda i: (i, 0))
    vec = _full((1, d))
    return pl.pallas_call(
        _hy_proj_kernel,
        grid=(s // tm,),
        in_specs=[row(d), vec, vec, vec, _full(wa.shape), _full(wz.shape), _full(wba.shape),
                  _full(wb.shape), row(LANES), row(LANES)],
        out_specs=[row(GDN_CONV_CH), row(GDN_V), row(LANES),
                   pl.BlockSpec((MOBA_HEADS, 1, MOBA_DH, tm), lambda i: (0, i, 0, 0)), row(MOBA_W),
                   pl.BlockSpec((MOBA_HEADS, 1, MOBA_DH + SUM_ROWS, tm), lambda i: (0, i, 0, 0))],
        out_shape=[jax.ShapeDtypeStruct((s, GDN_CONV_CH), F32),
                   jax.ShapeDtypeStruct((s, GDN_V), F32),
                   jax.ShapeDtypeStruct((s, LANES), F32),
                   jax.ShapeDtypeStruct((MOBA_HEADS, s // tm, MOBA_DH, tm), BF16),
                   jax.ShapeDtypeStruct((s, MOBA_W), BF16),
                   jax.ShapeDtypeStruct((MOBA_HEADS, s // tm, MOBA_DH + SUM_ROWS, tm), BF16)],
        compiler_params=_params("parallel"),
        name="hybrid_in_proj",
    )(x, nw, sc, sh, wa, wz, wba, wb, cos, sin)


def _unit_lower_inverses(l_stricts, eye):
    powers = [-l for l in l_stricts]
    invs = [eye + p for p in powers]
    p = 2
    while p < GDN_CHUNK:
        powers = [_dot(x.astype(BF16), x.astype(BF16)) for x in powers]
        invs = [_dot(inv.astype(BF16), (eye + x).astype(BF16)) for inv, x in zip(invs, powers)]
        p *= 2
    return invs


def _gdn_kernel(qkv_ref, ba_ref, z_ref, conv_ref, alog_ref, dtb_ref, onorm_ref, o_ref, ext_ref, state_ref):
    tb = GDN_ROWS
    cs = GDN_CHUNK
    pad = SUBLANES

    @pl.when(pl.program_id(0) == 0)
    def _():
        ext_ref[0:pad, :] = jnp.zeros((pad, GDN_CONV_CH), F32)
        state_ref[...] = jnp.zeros_like(state_ref)

    ext_ref[pad:pad + tb, :] = qkv_ref[...]
    cw = conv_ref[...]
    first = pad - (GDN_CONV - 1)
    y = ext_ref[first:first + tb, :] * cw[0:1]
    for j in range(1, GDN_CONV):
        y = y + ext_ref[first + j:first + j + tb, :] * cw[j:j + 1]
    ext_ref[0:pad, :] = ext_ref[tb:tb + pad, :]
    y = y * _sigmoid(y)

    ba = ba_ref[...]
    beta_all = _sigmoid(ba)
    t = ba + dtb_ref[...]
    softplus = jnp.maximum(t, 0.0) + jnp.log1p(jnp.exp(-jnp.abs(t)))
    g_all = -jnp.exp(alog_ref[...]) * softplus

    row = lax.broadcasted_iota(jnp.int32, (tb, tb), 0)
    col = lax.broadcasted_iota(jnp.int32, (tb, tb), 1)
    same = (row // cs) == (col // cs)
    causal = same & (col <= row)
    strict = same & (col < row)
    eye = (row == col).astype(F32)
    g_cum = _dot(causal.astype(F32), g_all, HIGHEST)
    g_tot = _dot(same.astype(F32), g_all, HIGHEST)
    g_cum_t = jnp.transpose(g_cum)
    z = z_ref[...]
    onorm = onorm_ref[...]

    heads = range(GDN_HEADS)
    rhs, l_strict, gtot, qk, qdec, kdec = [], [], [], [], [], []
    for h in heads:
        lo = h * GDN_DK
        q = y[:, lo:lo + GDN_DK]
        k = y[:, GDN_QK + lo:GDN_QK + lo + GDN_DK]
        v = y[:, 2 * GDN_QK + lo:2 * GDN_QK + lo + GDN_DK]
        q = q * lax.rsqrt(jnp.sum(q * q, axis=-1, keepdims=True) + EPS) * (GDN_DK ** -0.5)
        k = k * lax.rsqrt(jnp.sum(k * k, axis=-1, keepdims=True) + EPS)
        beta = beta_all[:, h:h + 1]
        gcol = g_cum[:, GDN_HEADS + h:GDN_HEADS + h + 1]
        grow = g_cum_t[GDN_HEADS + h:GDN_HEADS + h + 1, :]
        gtot.append(g_tot[:, GDN_HEADS + h:GDN_HEADS + h + 1])
        decay = jnp.exp(jnp.where(causal, gcol - grow, NEG_INF))
        kb = k * beta
        k16 = k.astype(BF16)
        l_strict.append(jnp.where(strict, _dot_nt(kb.astype(BF16), k16) * decay, 0.0))
        eg = jnp.exp(gcol)
        rhs.append(jnp.concatenate([v * beta, kb * eg], axis=1).astype(BF16))
        qk.append((_dot_nt(q.astype(BF16), k16) * decay).astype(BF16))
        qdec.append((q * eg).astype(BF16))
        kdec.append(k * jnp.exp(gtot[h] - gcol))

    uw = [_dot(t_inv.astype(BF16), rhs[h]) for h, t_inv in enumerate(_unit_lower_inverses(l_strict, eye))]
    u = [x[:, :GDN_DK] for x in uw]
    w = [x[:, GDN_DK:].astype(BF16) for x in uw]

    state = [state_ref[h] for h in heads]
    v_new = [[] for _ in heads]
    o_state = [[] for _ in heads]
    for c in range(tb // cs):
        r0 = c * cs
        for h in heads:
            s16 = state[h].astype(BF16)
            vn = u[h][r0:r0 + cs] - _dot(w[h][r0:r0 + cs], s16)
            o_state[h].append(_dot(qdec[h][r0:r0 + cs], s16))
            kd_t = jnp.transpose(kdec[h][r0:r0 + cs]).astype(BF16)
            state[h] = state[h] * jnp.exp(gtot[h][r0:r0 + 1]) + _dot(kd_t, vn.astype(BF16))
            v_new[h].append(vn)

    for h in heads:
        lo = h * GDN_DK
        state_ref[h] = state[h]
        o = jnp.concatenate(o_state[h], axis=0) + _dot(qk[h], jnp.concatenate(v_new[h], axis=0).astype(BF16))
        o = o * lax.rsqrt(jnp.mean(o * o, axis=-1, keepdims=True) + EPS) * onorm
        zh = z[:, lo:lo + GDN_DK]
        o_ref[:, lo:lo + GDN_DK] = (o * (zh * _sigmoid(zh))).astype(BF16)


def _gdn(qkva, ba, z, conv_w, alog, dtb, onorm):
    s = qkva.shape[0]
    tb = GDN_ROWS
    row = lambda n: pl.BlockSpec((tb, n), lambda i: (i, 0))
    return pl.pallas_call(
        _gdn_kernel,
        grid=(s // tb,),
        in_specs=[row(GDN_CONV_CH), row(LANES), row(GDN_V), _full(conv_w.shape),
                  _full((1, LANES)), _full((1, LANES)), _full((1, LANES))],
        out_specs=row(GDN_V),
        out_shape=jax.ShapeDtypeStruct((s, GDN_V), BF16),
        scratch_shapes=[pltpu.VMEM((tb + SUBLANES, GDN_CONV_CH), F32),
                        pltpu.VMEM((GDN_HEADS, GDN_DK, GDN_DV), F32)],
        compiler_params=_params("arbitrary"),
        name="gated_delta_rule",
    )(qkva, ba, z, conv_w, alog, dtb, onorm)


def _scores_into(s_ref, mx_ref, slot, k_tile, q_t, mask):
    s = _dot(k_tile, q_t)
    rows = s.shape[0]
    top = None
    for lo, piece in mask(s):
        s_ref[slot, lo:lo + piece.shape[0], :] = piece
        pmax = piece.max(axis=0, keepdims=True)
        top = pmax if top is None else jnp.maximum(top, pmax)
    if rows < s_ref.shape[1]:
        s_ref[slot, rows:, :] = jnp.full((s_ref.shape[1] - rows, s.shape[1]), NEG_INF, F32)
    mx_ref[slot] = top


def _values(p_ref, slot, v_at, blocks):
    v_t = jnp.concatenate([v_at(j) for j in blocks], axis=1)
    return _dot(v_t, p_ref[slot])


def _flash_steps(n_steps, first_blocks, first_keys, step_blocks, k_at, v_at, q_t, mask_first, mask_step,
                 s_ref, mx_ref, p_ref, acc_ref):
    per_step = len(first_blocks)
    _scores_into(s_ref, mx_ref, 0, k_at(first_blocks[0], first_keys), q_t, mask_first)
    m = mx_ref[0]
    p_ref[0] = jnp.exp2(s_ref[0] - m).astype(BF16)
    acc_ref[...] = jnp.zeros_like(acc_ref)
    _scores_into(s_ref, mx_ref, 1, k_at(step_blocks(1)[0], per_step), q_t, lambda s: mask_step(s, 1))

    def step(t, carry, cur):
        m, alpha_prev = carry
        nxt = jnp.minimum(t + 1, jnp.maximum(n_steps, 1))
        _scores_into(s_ref, mx_ref, 1 - cur, k_at(step_blocks(nxt)[0], per_step), q_t, lambda s: mask_step(s, nxt))
        m_new = jnp.maximum(m, mx_ref[cur])
        alpha = jnp.exp2(m - m_new)
        p = jnp.exp2(s_ref[cur] - m_new)
        prev = [jnp.where(t == 1, a, b) for a, b in zip(first_blocks, step_blocks(jnp.maximum(t - 1, 1)))]
        acc_ref[...] = alpha_prev * acc_ref[...] + _values(p_ref, 1 - cur, v_at, prev)
        p_ref[cur] = p.astype(BF16)
        return m_new, alpha

    def body(t, carry):
        return lax.cond(t % 2 == 0, lambda c: step(t, c, 0), lambda c: step(t, c, 1), carry)

    m, alpha = lax.fori_loop(1, n_steps + 1, body, (m, jnp.ones_like(m)))
    last = [jnp.where(n_steps == 0, a, b) for a, b in zip(first_blocks, step_blocks(jnp.maximum(n_steps, 1)))]
    acc = alpha * acc_ref[...] + _values(p_ref, n_steps % 2, v_at, last)
    dh = acc.shape[0] - SUM_ROWS
    return acc[:dh] / acc[dh:dh + 1]


def _with_sum_rows(v_t):
    extra = lax.broadcasted_iota(jnp.int32, (SUM_ROWS, v_t.shape[1]), 0) == 0
    return jnp.concatenate([v_t, extra.astype(F32)], axis=0).astype(BF16)


def _moba_kernel(q_ref, k_ref, v_ref, o_ref, kmean_ref, sel_ref, s_ref, mx_ref, p_ref, acc_ref, *, n_blocks):
    i = pl.program_id(1)
    bs = MOBA_BLOCK

    @pl.when(i == 0)
    def _():
        kmean_ref[...] = jnp.zeros_like(kmean_ref)

        def mean_body(n, carry):
            blk = k_ref[pl.ds(pl.multiple_of(n * bs, bs), bs), :].astype(F32)
            kmean_ref[pl.ds(n, 1), :] = jnp.mean(blk, axis=0, keepdims=True)
            return carry

        lax.fori_loop(0, n_blocks, mean_body, 0)

    q_t = jnp.concatenate([q_ref[0, 0], q_ref[0, 1]], axis=1)
    gate = _dot(kmean_ref[...].astype(BF16), q_t)
    blk_id = lax.broadcasted_iota(jnp.int32, gate.shape, 0)
    second = lax.broadcasted_iota(jnp.int32, gate.shape, 1) >= bs
    gate = jnp.where(blk_id < 2 * i + second.astype(jnp.int32), gate, NEG_INF)
    sel = jnp.zeros(gate.shape, F32)
    for _ in range(MOBA_TOPK):
        gmax = jnp.max(gate, axis=0, keepdims=True)
        first = jnp.min(jnp.where(gate == gmax, blk_id, LANES), axis=0, keepdims=True)
        hit = blk_id == first
        sel = jnp.where(hit & (gmax > NEG_INF), 1.0, sel)
        gate = jnp.where(hit, NEG_INF, gate)
    sel_ref[...] = sel

    def own_mask(s):
        key = lax.broadcasted_iota(jnp.int32, (bs, 2 * bs), 0)
        qry = lax.broadcasted_iota(jnp.int32, (bs, 2 * bs), 1)
        picked = sel_ref[pl.ds(2 * i, 1), :] > 0.0
        first_ok = (key <= qry) & ((qry < bs) | picked)
        return [(0, jnp.where(first_ok, s[:bs], NEG_INF)), (bs, jnp.where(key + bs <= qry, s[bs:], NEG_INF))]

    def sel_mask(s, t):
        return [(sub * bs, jnp.where(sel_ref[pl.ds(2 * (t - 1) + sub, 1), :] > 0.0, s[sub * bs:(sub + 1) * bs], NEG_INF))
                for sub in range(2)]

    out = _flash_steps(
        n_steps=i, first_blocks=[2 * i, 2 * i + 1], first_keys=2,
        step_blocks=lambda t: [2 * (t - 1), 2 * (t - 1) + 1],
        k_at=lambda j, n: k_ref[pl.ds(pl.multiple_of(j * bs, bs), n * bs), :], v_at=lambda j: v_ref[0, j],
        q_t=q_t, mask_first=own_mask, mask_step=sel_mask,
        s_ref=s_ref, mx_ref=mx_ref, p_ref=p_ref, acc_ref=acc_ref)
    o_ref[...] = jnp.transpose(out).astype(o_ref.dtype)


def _moba(q_t, k, v_t):
    s = k.shape[0]
    bs = MOBA_BLOCK
    n_blocks = s // bs
    assert n_blocks <= LANES and bs == KEY_BLOCK and n_blocks % 2 == 0
    bq = 2 * bs
    kv = pl.BlockSpec((s, MOBA_DH), lambda h, i: (0, h))
    return pl.pallas_call(
        functools.partial(_moba_kernel, n_blocks=n_blocks),
        grid=(MOBA_HEADS, n_blocks // 2),
        in_specs=[pl.BlockSpec((1, 2, MOBA_DH, bs), lambda h, i: (h, i, 0, 0)), kv,
                  pl.BlockSpec((1, n_blocks, MOBA_DH + SUM_ROWS, bs), lambda h, i: (h, 0, 0, 0))],
        out_specs=pl.BlockSpec((bq, MOBA_DH), lambda h, i: (i, h)),
        out_shape=jax.ShapeDtypeStruct((s, MOBA_W), BF16),
        scratch_shapes=[pltpu.VMEM((LANES, MOBA_DH), F32),
                        pltpu.VMEM((LANES, bq), F32),
                        pltpu.VMEM((2, 2 * KEY_BLOCK, bq), F32),
                        pltpu.VMEM((2, 1, bq), F32),
                        pltpu.VMEM((2, 2 * KEY_BLOCK, bq), BF16),
                        pltpu.VMEM((MOBA_DH + SUM_ROWS, bq), F32)],
        compiler_params=_params("parallel", "arbitrary"),
        name="moba_attention",
    )(q_t, k, v_t)


def _outproj_peer_kernel(*refs, n_in):
    o_refs = refs[:n_in]
    w_refs = refs[n_in:2 * n_in]
    x_ref, g_ref, nw_ref, sc_ref, sh_ref, wq_ref, keys_ref, xo_ref, h_ref, s_ref = refs[2 * n_in:]
    y = _dot(o_refs[0][...], w_refs[0][...])
    for o_ref, w_ref in zip(o_refs[1:], w_refs[1:]):
        y = y + _dot(o_ref[...], w_ref[...])
    x = x_ref[...] + g_ref[...] * y
    xo_ref[...] = x
    h = _norm_mod(x, nw_ref[...], sc_ref[...], sh_ref[...]).astype(BF16)
    h_ref[...] = h
    q = _dot(h, wq_ref[...]).astype(BF16)
    for hp in range(2 * PEER_HEADS):
        s_ref[hp] = _dot_nt(keys_ref[hp], q[:, hp * PEER_DSUB:(hp + 1) * PEER_DSUB])


def _outproj_peer_proj(os_, ws, x, gate, nw, sc, sh, wq, keys):
    s, d = x.shape
    tm = min(s, 256)
    row = lambda n: pl.BlockSpec((tm, n), lambda i: (i, 0))
    vec = _full((1, d))
    return pl.pallas_call(
        functools.partial(_outproj_peer_kernel, n_in=len(os_)),
        grid=(s // tm,),
        in_specs=[row(o.shape[1]) for o in os_] + [_full(w.shape) for w in ws]
                 + [row(d), vec, vec, vec, vec, _full(wq.shape), _full(keys.shape)],
        out_specs=[row(d), row(d), pl.BlockSpec((2 * PEER_HEADS, PEER_NKEYS, tm), lambda i: (0, 0, i))],
        out_shape=[jax.ShapeDtypeStruct((s, d), F32), jax.ShapeDtypeStruct((s, d), BF16),
                   jax.ShapeDtypeStruct((2 * PEER_HEADS, PEER_NKEYS, s), F32)],
        compiler_params=_params("parallel"),
        name="out_proj_peer_scores",
    )(*os_, *ws, x, gate, nw, sc, sh, wq, keys)


def _mla_proj_kernel(x_ref, nw_ref, sc_ref, sh_ref, wcq_ref, wckv_ref, wkr_ref, qn_ref, kvn_ref,
                     wqn_ref, wqr_ref, wkn_ref, wv_ref, cos_ref, sin_ref, q_ref, k_ref, v_ref):
    h = _norm_mod(x_ref[...], nw_ref[...], sc_ref[...], sh_ref[...]).astype(BF16)
    cq = _dot(h, wcq_ref[...])
    ckv = _dot(h, wckv_ref[...])
    kr = _dot(h, wkr_ref[...])
    cq = (cq * lax.rsqrt(jnp.mean(cq * cq, axis=-1, keepdims=True) + EPS) * qn_ref[...]).astype(BF16)
    ckv = (ckv * lax.rsqrt(jnp.mean(ckv * ckv, axis=-1, keepdims=True) + EPS) * kvn_ref[...]).astype(BF16)
    cos = cos_ref[...]
    sin = sin_ref[...]
    scale = MLA_QK ** -0.5 * LOG2_E
    q_nope = _dot(cq, wqn_ref[...]) * scale
    q_rope = _dot(cq, wqr_ref[...])
    k_nope = _dot(ckv, wkn_ref[...])
    val = _dot(ckv, wv_ref[...])
    k_rope = _rope_heads64(kr, cos, sin)[:, :MLA_ROPE].astype(BF16)
    for pair in range(MLA_HEADS // 2):
        slab_t = jnp.transpose(_rope_heads64(q_rope[:, pair * LANES:(pair + 1) * LANES], cos, sin) * scale)
        for sub in range(2):
            hh = 2 * pair + sub
            q_ref[hh, 0, MLA_NOPE:, :] = slab_t[sub * MLA_ROPE:(sub + 1) * MLA_ROPE].astype(BF16)
    for hh in range(MLA_HEADS):
        q_ref[hh, 0, :MLA_NOPE, :] = jnp.transpose(q_nope[:, hh * MLA_NOPE:(hh + 1) * MLA_NOPE]).astype(BF16)
        k_ref[hh, :, :MLA_NOPE] = k_nope[:, hh * MLA_NOPE:(hh + 1) * MLA_NOPE].astype(BF16)
        k_ref[hh, :, MLA_NOPE:] = k_rope
        v_ref[hh, 0] = _with_sum_rows(jnp.transpose(val[:, hh * MLA_V:(hh + 1) * MLA_V]))


def _mla_proj(x, nw, sc, sh, wcq, wckv, wkr, qn, kvn, wqn, wqr, wkn, wv, cos, sin):
    s, d = x.shape
    tm = KEY_BLOCK
    row = lambda n: pl.BlockSpec((tm, n), lambda i: (i, 0))
    vec = _full((1, d))
    heads = lambda n: pl.BlockSpec((MLA_HEADS, tm, n), lambda i: (0, i, 0))
    return pl.pallas_call(
        _mla_proj_kernel,
        grid=(s // tm,),
        in_specs=[row(d), vec, vec, vec, _full(wcq.shape), _full(wckv.shape), _full(wkr.shape),
                  _full(qn.shape), _full(kvn.shape), _full(wqn.shape), _full(wqr.shape),
                  _full(wkn.shape), _full(wv.shape), row(LANES), row(LANES)],
        out_specs=[pl.BlockSpec((MLA_HEADS, 1, MLA_QK, tm), lambda i: (0, i, 0, 0)), heads(MLA_QK),
                   pl.BlockSpec((MLA_HEADS, 1, MLA_V + SUM_ROWS, tm), lambda i: (0, i, 0, 0))],
        out_shape=[jax.ShapeDtypeStruct((MLA_HEADS, s // tm, MLA_QK, tm), BF16),
                   jax.ShapeDtypeStruct((MLA_HEADS, s, MLA_QK), BF16),
                   jax.ShapeDtypeStruct((MLA_HEADS, s // tm, MLA_V + SUM_ROWS, tm), BF16)],
        compiler_params=_params("parallel"),
        name="mla_in_proj",
    )(x, nw, sc, sh, wcq, wckv, wkr, qn, kvn, wqn, wqr, wkn, wv, cos, sin)


def _mla_attn_kernel(q_ref, k_ref, v_ref, o_ref, s_ref, mx_ref, p_ref, acc_ref):
    i = pl.program_id(1)

    def causal(s):
        key = lax.broadcasted_iota(jnp.int32, s.shape, 0)
        qry = lax.broadcasted_iota(jnp.int32, s.shape, 1)
        return [(0, jnp.where(key <= qry, s, NEG_INF))]

    per_q = MLA_Q_BLOCK // KEY_BLOCK
    q_t = jnp.concatenate([q_ref[0, b] for b in range(per_q)], axis=1)
    out = _flash_steps(
        n_steps=i, first_blocks=[2 * i, 2 * i + 1], first_keys=2,
        step_blocks=lambda t: [2 * (t - 1), 2 * (t - 1) + 1],
        k_at=lambda j, n: k_ref[0, pl.ds(pl.multiple_of(j * KEY_BLOCK, KEY_BLOCK), n * KEY_BLOCK), :],
        v_at=lambda j: v_ref[0, j], q_t=q_t, mask_first=causal, mask_step=lambda s, t: [(0, s)],
        s_ref=s_ref, mx_ref=mx_ref, p_ref=p_ref, acc_ref=acc_ref)
    o_ref[...] = jnp.transpose(out).astype(o_ref.dtype)


def _mla_attention(q_t, k, v_t):
    _, s, _ = k.shape
    bq = MLA_Q_BLOCK
    per_q = bq // KEY_BLOCK
    assert per_q == 2
    return pl.pallas_call(
        _mla_attn_kernel,
        grid=(MLA_HEADS, s // bq),
        in_specs=[pl.BlockSpec((1, per_q, MLA_QK, KEY_BLOCK), lambda h, i: (h, i, 0, 0)),
                  pl.BlockSpec((1, s, MLA_QK), lambda h, i: (h, 0, 0)),
                  pl.BlockSpec((1, s // KEY_BLOCK, MLA_V + SUM_ROWS, KEY_BLOCK), lambda h, i: (h, 0, 0, 0))],
        out_specs=pl.BlockSpec((bq, MLA_V), lambda h, i: (i, h)),
        out_shape=jax.ShapeDtypeStruct((s, MLA_HEADS * MLA_V), BF16),
        scratch_shapes=[pltpu.VMEM((2, 2 * KEY_BLOCK, bq), F32), pltpu.VMEM((2, 1, bq), F32),
                        pltpu.VMEM((2, 2 * KEY_BLOCK, bq), BF16), pltpu.VMEM((MLA_V + SUM_ROWS, bq), F32)],
        compiler_params=_params("parallel", "arbitrary"),
        name="mla_attention",
    )(q_t, k, v_t)


def _compare_exchange(vs, i, l):
    vs[i], vs[l] = jnp.maximum(vs[i], vs[l]), jnp.minimum(vs[i], vs[l])


def _sort_desc(vs):
    vs = list(vs)
    n = len(vs)
    k = 2
    while k <= n:
        j = k // 2
        while j >= 1:
            for i in range(n):
                l = i ^ j
                if l > i:
                    if i & k == 0:
                        _compare_exchange(vs, i, l)
                    else:
                        _compare_exchange(vs, l, i)
            j //= 2
        k *= 2
    return vs


def _merge_top(a, b):
    n = len(a)
    vs = [jnp.maximum(a[i], b[n - 1 - i]) for i in range(n)]
    j = n // 2
    while j >= 1:
        for i in range(n):
            if i ^ j > i:
                _compare_exchange(vs, i, i ^ j)
        j //= 2
    return vs


def _top16_of_keys(groups):
    vs = _sort_desc(groups)
    for shift in (4, 2, 1):
        vs = _merge_top(vs, [pltpu.roll(v, shift, 0) for v in vs])
    return vs


def _peer_topk_kernel(s_ref, a_ref, n_ref, b_ref, rank_ref):
    sub = SUBLANES
    groups = PEER_NKEYS // sub

    def head_body(h, carry):
        g1 = [s_ref[2 * h, g * sub:(g + 1) * sub, :] for g in range(groups)]
        g2 = [s_ref[2 * h + 1, g * sub:(g + 1) * sub, :] for g in range(groups)]
        v1 = _top16_of_keys(g1)
        v2 = _top16_of_keys(g2)

        ninf = jnp.full_like(v1[0], NEG_INF)
        pad = lambda vs: vs + [ninf] * (PEER_TOPK - len(vs))
        row = lambda r1: [v1[r1] + v2[r2] for r2 in range(PEER_TOPK // (r1 + 1))]
        top = _merge_top(
            _merge_top(row(0), _merge_top(pad(row(1)), pad(row(2)))),
            _merge_top(_sort_desc(pad(row(3) + row(4) + row(5) + row(6) + row(7))),
                       pad([row(r1)[0] for r1 in range(8, PEER_TOPK)])))
        thr = top[PEER_TOPK - 1]
        z = jnp.ones_like(thr)
        for r in range(1, PEER_TOPK):
            z = z + jnp.exp(top[r] - top[0])
        half_inv_z = 0.5 / z

        split = 5
        hit = lambda s1, r2s: sum(jnp.where(s1 + v2[r2] >= thr, 1.0, 0.0) for r2 in r2s)
        n_top = [hit(v1[r1], range(split, PEER_TOPK // (r1 + 1))) for r1 in range(2)]

        b_all, rank_all = [], []
        for g in range(groups):
            rows = slice(g * sub, (g + 1) * sub)
            n = hit(g1[g], range(split)) + jnp.where(g1[g] == v1[0], n_top[0],
                                                     jnp.where(g1[g] == v1[1], n_top[1], 0.0))
            a_ref[h, rows, :] = jnp.exp(g1[g] - v1[0]) * half_inv_z
            n_ref[h, rows, :] = n
            b_all.append(jnp.exp(g2[g] - v2[0]))
            rank_all.append(sum(jnp.where(v2[r] > g2[g], 1.0, 0.0) for r in range(PEER_TOPK)))
        for pair in range(groups // 2):
            b_ref[h, pair] = jnp.concatenate(b_all[2 * pair:2 * pair + 2], axis=0).astype(BF16)
            rank_ref[h, pair] = jnp.concatenate(rank_all[2 * pair:2 * pair + 2], axis=0).astype(BF16)
        return carry

    lax.fori_loop(0, PEER_HEADS, head_body, 0)


def _peer_topk(scores):
    _, nk, s = scores.shape
    tt = min(s, PEER_TOPK_TOK)
    big = pl.BlockSpec((PEER_HEADS, nk, tt), lambda i: (0, 0, i))
    tiled = pl.BlockSpec((PEER_HEADS, nk // BF16_ROWS, BF16_ROWS, tt), lambda i: (0, 0, 0, i))
    shape = (PEER_HEADS, nk, s)
    shape16 = (PEER_HEADS, nk // BF16_ROWS, BF16_ROWS, s)
    return pl.pallas_call(
        _peer_topk_kernel,
        grid=(s // tt,),
        in_specs=[pl.BlockSpec((2 * PEER_HEADS, nk, tt), lambda i: (0, 0, i))],
        out_specs=[big, big, tiled, tiled],
        out_shape=[jax.ShapeDtypeStruct(shape, F32), jax.ShapeDtypeStruct(shape, F32),
                   jax.ShapeDtypeStruct(shape16, BF16), jax.ShapeDtypeStruct(shape16, BF16)],
        compiler_params=_params("parallel"),
        name="peer_topk_threshold",
    )(scores)


def _peer_dense_kernel(h_ref, a_ref, n_ref, b_ref, rank_ref, u_ref, vt_ref, x_ref, g_ref, o_ref,
                       acc_ref, p_ref):
    j = pl.program_id(1)
    nk = PEER_NKEYS

    @pl.when(j == 0)
    def _():
        acc_ref[...] = jnp.zeros_like(acc_ref)

    act_t = _dot_nt(u_ref[...], h_ref[...])
    for blk in range(PEER_EXP // nk):
        i1 = j * (PEER_EXP // nk) + blk
        wgt = jnp.zeros((nk // BF16_ROWS, BF16_ROWS, PEER_TOK), BF16)
        for h in range(PEER_HEADS):
            row = lambda ref: jnp.broadcast_to(ref[h, pl.ds(i1, 1), :], (BF16_ROWS, PEER_TOK)).astype(BF16)[None]
            chosen = rank_ref[h] < row(n_ref)
            wgt = wgt + jnp.where(chosen, b_ref[h], 0.0) * row(a_ref)
        act = act_t[blk * nk:(blk + 1) * nk]
        gelu2 = act * (1.0 + lax.erf(act * (2.0 ** -0.5)))
        p_ref[blk * nk:(blk + 1) * nk, :] = wgt.reshape(nk, PEER_TOK) * gelu2.astype(BF16)
    acc_ref[...] += _dot(vt_ref[...], p_ref[...])

    @pl.when(j == pl.num_programs(1) - 1)
    def _():
        o_ref[...] = x_ref[...] + g_ref[...] * jnp.transpose(acc_ref[...])


def _peer_dense(h2, a, n, b, rank, u16, vt16, layer, x, gate):
    s, d = x.shape
    n_exp = u16.shape[1]
    tt = min(s, PEER_TOK)
    assert tt == PEER_TOK
    tok3 = pl.BlockSpec((PEER_HEADS, PEER_NKEYS, tt), lambda i, j: (0, 0, i))
    tok4 = pl.BlockSpec((PEER_HEADS, PEER_NKEYS // BF16_ROWS, BF16_ROWS, tt), lambda i, j: (0, 0, 0, i))
    return pl.pallas_call(
        _peer_dense_kernel,
        grid=(s // tt, n_exp // PEER_EXP),
        in_specs=[pl.BlockSpec((tt, d), lambda i, j: (i, 0)),
                  tok3, tok3, tok4, tok4,
                  pl.BlockSpec((None, PEER_EXP, d), lambda i, j: (layer, j, 0)),
                  pl.BlockSpec((None, d, PEER_EXP), lambda i, j: (layer, 0, j)),
                  pl.BlockSpec((tt, d), lambda i, j: (i, 0)),
                  pl.BlockSpec((1, d), lambda i, j: (0, 0))],
        out_specs=pl.BlockSpec((tt, d), lambda i, j: (i, 0)),
        out_shape=jax.ShapeDtypeStruct((s, d), F32),
        scratch_shapes=[pltpu.VMEM((d, tt), F32), pltpu.VMEM((PEER_EXP, tt), BF16)],
        compiler_params=_params("parallel", "arbitrary"),
        name="peer_dense_experts",
    )(h2, a, n, b, rank, u16, vt16, x, gate)


def _final_norm_kernel(x_ref, w_ref, o_ref):
    x = x_ref[...]
    o_ref[...] = x * lax.rsqrt(jnp.mean(x * x, axis=-1, keepdims=True) + EPS) * w_ref[...]


def _final_norm(x, w):
    s, d = x.shape
    tm = min(s, 1024)
    return pl.pallas_call(
        _final_norm_kernel,
        grid=(s // tm,),
        in_specs=[pl.BlockSpec((tm, d), lambda i: (i, 0)), _full((1, d))],
        out_specs=pl.BlockSpec((tm, d), lambda i: (i, 0)),
        out_shape=jax.ShapeDtypeStruct((s, d), F32),
        compiler_params=_params("parallel"),
        name="final_rmsnorm",
    )(x, w)


def _lane_row(values, offset):
    return jnp.zeros((1, LANES), F32).at[0, offset:offset + values.shape[0]].set(values.astype(F32))


def _hybrid_mixer(x, nw, sc, sh, cos, sin, w_in, conv_w, a_log, dt_bias, o_norm, w_out):
    w16 = w_in.astype(BF16)
    c0 = GDN_CONV_CH
    c1 = c0 + GDN_V
    c2 = c1 + 2 * GDN_HEADS
    wba = jnp.pad(w16[:, c1:c2], ((0, 0), (0, LANES - 2 * GDN_HEADS)))
    qkva, z, ba, qb, kb, vb = _hy_proj(x, nw, sc, sh, w16[:, :c0], w16[:, c0:c1], wba, w16[:, c2:], cos, sin)
    o_a = _gdn(qkva, ba, z, conv_w, _lane_row(a_log, GDN_HEADS), _lane_row(dt_bias, GDN_HEADS),
               o_norm.reshape(1, -1))
    o_b = _moba(qb, kb, vb)
    wo16 = w_out.astype(BF16)
    return [o_a, o_b], [wo16[:GDN_V], wo16[GDN_V:]]


def _mla_mixer(x, nw, sc, sh, cos, sin, w_in, q_norm, kv_norm, w_uq, w_ukv, w_out):
    w16 = w_in.astype(BF16)
    r0 = MLA_Q_RANK
    r1 = r0 + MLA_KV_RANK
    wkr = jnp.pad(w16[:, r1:], ((0, 0), (0, LANES - MLA_ROPE)))
    wuq = w_uq.astype(BF16).reshape(MLA_Q_RANK, MLA_HEADS, MLA_QK)
    wqn = wuq[:, :, :MLA_NOPE].reshape(MLA_Q_RANK, MLA_HEADS * MLA_NOPE)
    wqr = wuq[:, :, MLA_NOPE:].reshape(MLA_Q_RANK, MLA_HEADS * MLA_ROPE)
    wukv = w_ukv.astype(BF16).reshape(MLA_KV_RANK, MLA_HEADS, MLA_NOPE + MLA_V)
    wkn = wukv[:, :, :MLA_NOPE].reshape(MLA_KV_RANK, MLA_HEADS * MLA_NOPE)
    wv = wukv[:, :, MLA_NOPE:].reshape(MLA_KV_RANK, MLA_HEADS * MLA_V)
    q, k, v = _mla_proj(x, nw, sc, sh, w16[:, :r0], w16[:, r0:r1], wkr, q_norm.reshape(1, -1),
                        kv_norm.reshape(1, -1), wqn, wqr, wkn, wv, cos, sin)
    o = _mla_attention(q, k, v)
    return [o], [w_out.astype(BF16)]


def kernel(x, c, positions, mod_w, mod_b, norm_mix, norm_ffn, hy_w_in, gdn_conv, gdn_a_log, gdn_dt_bias, gdn_o_norm, hy_w_out, mla_w_in, mla_q_norm, mla_kv_norm, mla_w_uq, mla_w_ukv, mla_w_out, peer_w_q, peer_sub_keys, peer_u, peer_v, final_norm):
    bsz, s, d = x.shape
    assert bsz == 1 and d == D_MODEL
    depth = mod_w.shape[0]
    xs = x.reshape(s, d)
    mod = _modulation(c.reshape(d, 1), mod_w, mod_b)
    cos_b, sin_b, cos_c, sin_c = _rope_tables(positions.reshape(s, 1))
    u16 = peer_u.astype(BF16)
    vt16 = jnp.swapaxes(peer_v.astype(BF16), 1, 2)
    wq16 = peer_w_q.astype(BF16)
    keys16 = peer_sub_keys.astype(BF16).reshape(depth, 2 * PEER_HEADS, PEER_NKEYS, PEER_DSUB)
    for layer in range(depth):
        sh1, sc1, g1, sh2, sc2, g2 = (mod[layer, :, n * d:(n + 1) * d] for n in range(6))
        nw = norm_mix[layer].reshape(1, d)
        i = layer // 2
        if layer % 2 == 0:
            outs, w_outs = _hybrid_mixer(xs, nw, sc1, sh1, cos_b, sin_b, hy_w_in[i], gdn_conv[i], gdn_a_log[i],
                                         gdn_dt_bias[i], gdn_o_norm[i], hy_w_out[i])
        else:
            outs, w_outs = _mla_mixer(xs, nw, sc1, sh1, cos_c, sin_c, mla_w_in[i], mla_q_norm[i], mla_kv_norm[i],
                                      mla_w_uq[i], mla_w_ukv[i], mla_w_out[i])
        xs, h2, scores = _outproj_peer_proj(outs, w_outs, xs, g1, norm_ffn[layer].reshape(1, d), sc2, sh2,
                                            wq16[layer], keys16[layer])
        a, n, b, rank = _peer_topk(scores)
        xs = _peer_dense(h2, a, n, b, rank, u16, vt16, layer, xs, g2)
    return _final_norm(xs, final_norm.reshape(1, d)).reshape(bsz, s, d)
```

```python
import functools
import math

import jax
import jax.numpy as jnp
from jax import lax
from jax.experimental import pallas as pl
from jax.experimental.pallas import tpu as pltpu

F32 = jnp.float32
BF16 = jnp.bfloat16
NEG_INF = float("-inf")
HIGHEST = lax.Precision.HIGHEST

D_MODEL = 1024
MOD_COLS = 1536
EPS = 1e-6
ROPE_THETA = 10000.0
LOG2_E = math.log2(math.e)

GDN_HEADS = 4
GDN_DK = 128
GDN_CHUNK = 64
GDN_QK = GDN_HEADS * GDN_DK
GDN_DV = 128
GDN_V = GDN_HEADS * GDN_DV
GDN_CONV = 4
GDN_CONV_CH = 2 * GDN_QK + GDN_V
GDN_ROWS = 256

MOBA_HEADS = 4
MOBA_DH = 128
MOBA_BLOCK = 256
MOBA_TOPK = 3
MOBA_W = MOBA_HEADS * MOBA_DH

MLA_HEADS = 8
MLA_Q_RANK = 512
MLA_KV_RANK = 256
MLA_NOPE = 128
MLA_ROPE = 64
MLA_V = 128
MLA_QK = MLA_NOPE + MLA_ROPE
KEY_BLOCK = 256
MLA_Q_BLOCK = 512

PEER_HEADS = 8
PEER_NKEYS = 128
PEER_TOPK = 16
PEER_DSUB = 128
PEER_TOK = 512
PEER_EXP = 2048
PEER_TOPK_TOK = 256

LANES = 128
SUBLANES = 8
BF16_ROWS = 16
SUM_ROWS = BF16_ROWS
VMEM_LIMIT = 56 * 1024 * 1024


def _params(*sem):
    return pltpu.CompilerParams(dimension_semantics=sem, vmem_limit_bytes=VMEM_LIMIT)


def _dot(a, b, precision=None):
    return jnp.dot(a, b, preferred_element_type=F32, precision=precision)


def _dot_nt(a, b):
    return lax.dot_general(a, b, (((1,), (1,)), ((), ())), preferred_element_type=F32)


def _sigmoid(x):
    return jax.nn.sigmoid(x)


def _norm_mod(x, nw, sc, sh):
    y = x * lax.rsqrt(jnp.mean(x * x, axis=-1, keepdims=True) + EPS) * nw
    return y * (1.0 + sc) + sh


def _full(shape):
    n = len(shape)
    return pl.BlockSpec(shape, lambda *_: (0,) * n)


def _mod_kernel(c_ref, w_ref, b_ref, o_ref):
    c = c_ref[...]
    cond = c * _sigmoid(c)
    o_ref[0] = jnp.sum(cond * w_ref[0], axis=0, keepdims=True) + b_ref[0]


def _modulation(c_col, mod_w, mod_b):
    depth, d, n = mod_w.shape
    tn = MOD_COLS
    return pl.pallas_call(
        _mod_kernel,
        grid=(depth, n // tn),
        in_specs=[pl.BlockSpec((d, 1), lambda l, j: (0, 0)),
                  pl.BlockSpec((1, d, tn), lambda l, j: (l, 0, j)),
                  pl.BlockSpec((1, 1, tn), lambda l, j: (l, 0, j))],
        out_specs=pl.BlockSpec((1, 1, tn), lambda l, j: (l, 0, j)),
        out_shape=jax.ShapeDtypeStruct((depth, 1, n), F32),
        compiler_params=_params("parallel", "parallel"),
        name="adaln_modulation",
    )(c_col, mod_w, mod_b.reshape(depth, 1, n))


def _rope_kernel(pos_ref, fb_ref, fc_ref, cb_ref, sb_ref, cc_ref, sc_ref):
    p = pos_ref[...].astype(F32)
    lane = lax.broadcasted_iota(jnp.int32, cb_ref.shape, 1)
    ab = p * fb_ref[...]
    sb = jnp.sin(ab)
    cb_ref[...] = jnp.cos(ab)
    sb_ref[...] = jnp.where(lane < MOBA_DH // 2, -sb, sb)
    ac = p * fc_ref[...]
    sc = jnp.sin(ac)
    cc_ref[...] = jnp.cos(ac)
    sc_ref[...] = jnp.where(lane % MLA_ROPE < MLA_ROPE // 2, -sc, sc)


def _rope_tables(pos_col):
    s = pos_col.shape[0]
    tm = min(s, 1024)
    fb = ROPE_THETA ** (-jnp.arange(0, MOBA_DH, 2, dtype=F32) / MOBA_DH)
    fc = ROPE_THETA ** (-jnp.arange(0, MLA_ROPE, 2, dtype=F32) / MLA_ROPE)
    fb = jnp.tile(fb, 2).reshape(1, LANES)
    fc = jnp.tile(fc, 4).reshape(1, LANES)
    tab = pl.BlockSpec((tm, LANES), lambda i: (i, 0))
    return pl.pallas_call(
        _rope_kernel,
        grid=(s // tm,),
        in_specs=[pl.BlockSpec((tm, 1), lambda i: (i, 0)), _full((1, LANES)), _full((1, LANES))],
        out_specs=[tab] * 4,
        out_shape=[jax.ShapeDtypeStruct((s, LANES), F32)] * 4,
        compiler_params=_params("parallel"),
        name="rope_tables",
    )(pos_col, fb, fc)


def _rope_head128(x, cos, sin):
    return x * cos + pltpu.roll(x, MOBA_DH // 2, 1) * sin


def _rope_heads64(x, cos, sin):
    lane = lax.broadcasted_iota(jnp.int32, x.shape, 1)
    first = lane % MLA_ROPE < MLA_ROPE // 2
    partner = jnp.where(first, pltpu.roll(x, LANES - MLA_ROPE // 2, 1), pltpu.roll(x, MLA_ROPE // 2, 1))
    return x * cos + partner * sin


def _hy_proj_kernel(x_ref, nw_ref, sc_ref, sh_ref, wa_ref, wz_ref, wba_ref, wb_ref, cos_ref, sin_ref,
                    qkva_ref, z_ref, ba_ref, qb_ref, kb_ref, vb_ref):
    h = _norm_mod(x_ref[...], nw_ref[...], sc_ref[...], sh_ref[...]).astype(BF16)
    qkva_ref[...] = _dot(h, wa_ref[...])
    z_ref[...] = _dot(h, wz_ref[...])
    ba_ref[...] = _dot(h, wba_ref[...])
    qkvb = _dot(h, wb_ref[...])
    cos = cos_ref[...]
    sin = sin_ref[...]
    scale = MOBA_DH ** -0.5 * LOG2_E
    for hh in range(MOBA_HEADS):
        lo = hh * MOBA_DH
        q = qkvb[:, lo:lo + MOBA_DH]
        k = qkvb[:, MOBA_W + lo:MOBA_W + lo + MOBA_DH]
        qb_ref[hh, 0] = jnp.transpose(_rope_head128(q, cos, sin) * scale).astype(BF16)
        kb_ref[:, lo:lo + MOBA_DH] = _rope_head128(k, cos, sin).astype(BF16)
        v = qkvb[:, 2 * MOBA_W + lo:2 * MOBA_W + lo + MOBA_DH]
        vb_ref[hh, 0] = _with_sum_rows(jnp.transpose(v))


def _hy_proj(x, nw, sc, sh, wa, wz, wba, wb, cos, sin):
    s, d = x.shape
    tm = KEY_BLOCK
    row = lambda n: pl.BlockSpec((tm, n), lambda i: (i, 0))
    vec = _full((1, d))
    return pl.pallas_call(
        _hy_proj_kernel,
        grid=(s // tm,),
        in_specs=[row(d), vec, vec, vec, _full(wa.shape), _full(wz.shape), _full(wba.shape),
                  _full(wb.shape), row(LANES), row(LANES)],
        out_specs=[row(GDN_CONV_CH), row(GDN_V), row(LANES),
                   pl.BlockSpec((MOBA_HEADS, 1, MOBA_DH, tm), lambda i: (0, i, 0, 0)), row(MOBA_W),
                   pl.BlockSpec((MOBA_HEADS, 1, MOBA_DH + SUM_ROWS, tm), lambda i: (0, i, 0, 0))],
        out_shape=[jax.ShapeDtypeStruct((s, GDN_CONV_CH), F32),
                   jax.ShapeDtypeStruct((s, GDN_V), F32),
                   jax.ShapeDtypeStruct((s, LANES), F32),
                   jax.ShapeDtypeStruct((MOBA_HEADS, s // tm, MOBA_DH, tm), BF16),
                   jax.ShapeDtypeStruct((s, MOBA_W), BF16),
                   jax.ShapeDtypeStruct((MOBA_HEADS, s // tm, MOBA_DH + SUM_ROWS, tm), BF16)],
        compiler_params=_params("parallel"),
        name="hybrid_in_proj",
    )(x, nw, sc, sh, wa, wz, wba, wb, cos, sin)


def _unit_lower_inverses(l_stricts, eye):
    powers = [-l for l in l_stricts]
    invs = [eye + p for p in powers]
    p = 2
    while p < GDN_CHUNK:
        powers = [_dot(x.astype(BF16), x.astype(BF16)) for x in powers]
        invs = [_dot(inv.astype(BF16), (eye + x).astype(BF16)) for inv, x in zip(invs, powers)]
        p *= 2
    return invs


def _gdn_kernel(qkv_ref, ba_ref, z_ref, conv_ref, alog_ref, dtb_ref, onorm_ref, o_ref, ext_ref, state_ref):
    tb = GDN_ROWS
    cs = GDN_CHUNK
    pad = SUBLANES

    @pl.when(pl.program_id(0) == 0)
    def _():
        ext_ref[0:pad, :] = jnp.zeros((pad, GDN_CONV_CH), F32)
        state_ref[...] = jnp.zeros_like(state_ref)

    ext_ref[pad:pad + tb, :] = qkv_ref[...]
    cw = conv_ref[...]
    first = pad - (GDN_CONV - 1)
    y = ext_ref[first:first + tb, :] * cw[0:1]
    for j in range(1, GDN_CONV):
        y = y + ext_ref[first + j:first + j + tb, :] * cw[j:j + 1]
    ext_ref[0:pad, :] = ext_ref[tb:tb + pad, :]
    y = y * _sigmoid(y)

    ba = ba_ref[...]
    beta_all = _sigmoid(ba)
    t = ba + dtb_ref[...]
    softplus = jnp.maximum(t, 0.0) + jnp.log1p(jnp.exp(-jnp.abs(t)))
    g_all = -jnp.exp(alog_ref[...]) * softplus

    row = lax.broadcasted_iota(jnp.int32, (tb, tb), 0)
    col = lax.broadcasted_iota(jnp.int32, (tb, tb), 1)
    same = (row // cs) == (col // cs)
    causal = same & (col <= row)
    strict = same & (col < row)
    eye = (row == col).astype(F32)
    g_cum = _dot(causal.astype(F32), g_all, HIGHEST)
    g_tot = _dot(same.astype(F32), g_all, HIGHEST)
    g_cum_t = jnp.transpose(g_cum)
    z = z_ref[...]
    onorm = onorm_ref[...]

    heads = range(GDN_HEADS)
    rhs, l_strict, gtot, qk, qdec, kdec = [], [], [], [], [], []
    for h in heads:
        lo = h * GDN_DK
        q = y[:, lo:lo + GDN_DK]
        k = y[:, GDN_QK + lo:GDN_QK + lo + GDN_DK]
        v = y[:, 2 * GDN_QK + lo:2 * GDN_QK + lo + GDN_DK]
        q = q * lax.rsqrt(jnp.sum(q * q, axis=-1, keepdims=True) + EPS) * (GDN_DK ** -0.5)
        k = k * lax.rsqrt(jnp.sum(k * k, axis=-1, keepdims=True) + EPS)
        beta = beta_all[:, h:h + 1]
        gcol = g_cum[:, GDN_HEADS + h:GDN_HEADS + h + 1]
        grow = g_cum_t[GDN_HEADS + h:GDN_HEADS + h + 1, :]
        gtot.append(g_tot[:, GDN_HEADS + h:GDN_HEADS + h + 1])
        decay = jnp.exp(jnp.where(causal, gcol - grow, NEG_INF))
        kb = k * beta
        k16 = k.astype(BF16)
        l_strict.append(jnp.where(strict, _dot_nt(kb.astype(BF16), k16) * decay, 0.0))
        eg = jnp.exp(gcol)
        rhs.append(jnp.concatenate([v * beta, kb * eg], axis=1).astype(BF16))
        qk.append((_dot_nt(q.astype(BF16), k16) * decay).astype(BF16))
        qdec.append((q * eg).astype(BF16))
        kdec.append(k * jnp.exp(gtot[h] - gcol))

    uw = [_dot(t_inv.astype(BF16), rhs[h]) for h, t_inv in enumerate(_unit_lower_inverses(l_strict, eye))]
    u = [x[:, :GDN_DK] for x in uw]
    w = [x[:, GDN_DK:].astype(BF16) for x in uw]

    state = [state_ref[h] for h in heads]
    v_new = [[] for _ in heads]
    o_state = [[] for _ in heads]
    for c in range(tb // cs):
        r0 = c * cs
        for h in heads:
            s16 = state[h].astype(BF16)
            vn = u[h][r0:r0 + cs] - _dot(w[h][r0:r0 + cs], s16)
            o_state[h].append(_dot(qdec[h][r0:r0 + cs], s16))
            kd_t = jnp.transpose(kdec[h][r0:r0 + cs]).astype(BF16)
            state[h] = state[h] * jnp.exp(gtot[h][r0:r0 + 1]) + _dot(kd_t, vn.astype(BF16))
            v_new[h].append(vn)

    for h in heads:
        lo = h * GDN_DK
        state_ref[h] = state[h]
        o = jnp.concatenate(o_state[h], axis=0) + _dot(qk[h], jnp.concatenate(v_new[h], axis=0).astype(BF16))
        o = o * lax.rsqrt(jnp.mean(o * o, axis=-1, keepdims=True) + EPS) * onorm
        zh = z[:, lo:lo + GDN_DK]
        o_ref[:, lo:lo + GDN_DK] = (o * (zh * _sigmoid(zh))).astype(BF16)


def _gdn(qkva, ba, z, conv_w, alog, dtb, onorm):
    s = qkva.shape[0]
    tb = GDN_ROWS
    row = lambda n: pl.BlockSpec((tb, n), lambda i: (i, 0))
    return pl.pallas_call(
        _gdn_kernel,
        grid=(s // tb,),
        in_specs=[row(GDN_CONV_CH), row(LANES), row(GDN_V), _full(conv_w.shape),
                  _full((1, LANES)), _full((1, LANES)), _full((1, LANES))],
        out_specs=row(GDN_V),
        out_shape=jax.ShapeDtypeStruct((s, GDN_V), BF16),
        scratch_shapes=[pltpu.VMEM((tb + SUBLANES, GDN_CONV_CH), F32),
                        pltpu.VMEM((GDN_HEADS, GDN_DK, GDN_DV), F32)],
        compiler_params=_params("arbitrary"),
        name="gated_delta_rule",
    )(qkva, ba, z, conv_w, alog, dtb, onorm)


def _scores_into(s_ref, mx_ref, slot, k_tile, q_t, mask):
    s = _dot(k_tile, q_t)
    rows = s.shape[0]
    top = None
    for lo, piece in mask(s):
        s_ref[slot, lo:lo + piece.shape[0], :] = piece
        pmax = piece.max(axis=0, keepdims=True)
        top = pmax if top is None else jnp.maximum(top, pmax)
    if rows < s_ref.shape[1]:
        s_ref[slot, rows:, :] = jnp.full((s_ref.shape[1] - rows, s.shape[1]), NEG_INF, F32)
    mx_ref[slot] = top


def _values(p_ref, slot, v_at, blocks):
    v_t = jnp.concatenate([v_at(j) for j in blocks], axis=1)
    return _dot(v_t, p_ref[slot])


def _flash_steps(n_steps, first_blocks, first_keys, step_blocks, k_at, v_at, q_t, mask_first, mask_step,
                 s_ref, mx_ref, p_ref, acc_ref):
    per_step = len(first_blocks)
    _scores_into(s_ref, mx_ref, 0, k_at(first_blocks[0], first_keys), q_t, mask_first)
    m = mx_ref[0]
    p_ref[0] = jnp.exp2(s_ref[0] - m).astype(BF16)
    acc_ref[...] = jnp.zeros_like(acc_ref)
    _scores_into(s_ref, mx_ref, 1, k_at(step_blocks(1)[0], per_step), q_t, lambda s: mask_step(s, 1))

    def step(t, carry, cur):
        m, alpha_prev = carry
        nxt = jnp.minimum(t + 1, jnp.maximum(n_steps, 1))
        _scores_into(s_ref, mx_ref, 1 - cur, k_at(step_blocks(nxt)[0], per_step), q_t, lambda s: mask_step(s, nxt))
        m_new = jnp.maximum(m, mx_ref[cur])
        alpha = jnp.exp2(m - m_new)
        p = jnp.exp2(s_ref[cur] - m_new)
        prev = [jnp.where(t == 1, a, b) for a, b in zip(first_blocks, step_blocks(jnp.maximum(t - 1, 1)))]
        acc_ref[...] = alpha_prev * acc_ref[...] + _values(p_ref, 1 - cur, v_at, prev)
        p_ref[cur] = p.astype(BF16)
        return m_new, alpha

    def body(t, carry):
        return lax.cond(t % 2 == 0, lambda c: step(t, c, 0), lambda c: step(t, c, 1), carry)

    m, alpha = lax.fori_loop(1, n_steps + 1, body, (m, jnp.ones_like(m)))
    last = [jnp.where(n_steps == 0, a, b) for a, b in zip(first_blocks, step_blocks(jnp.maximum(n_steps, 1)))]
    acc = alpha * acc_ref[...] + _values(p_ref, n_steps % 2, v_at, last)
    dh = acc.shape[0] - SUM_ROWS
    return acc[:dh] / acc[dh:dh + 1]


def _with_sum_rows(v_t):
    extra = lax.broadcasted_iota(jnp.int32, (SUM_ROWS, v_t.shape[1]), 0) == 0
    return jnp.concatenate([v_t, extra.astype(F32)], axis=0).astype(BF16)


def _moba_kernel(q_ref, k_ref, v_ref, o_ref, kmean_ref, sel_ref, s_ref, mx_ref, p_ref, acc_ref, *, n_blocks):
    i = pl.program_id(1)
    bs = MOBA_BLOCK

    @pl.when(i == 0)
    def _():
        kmean_ref[...] = jnp.zeros_like(kmean_ref)

        def mean_body(n, carry):
            blk = k_ref[pl.ds(pl.multiple_of(n * bs, bs), bs), :].astype(F32)
            kmean_ref[pl.ds(n, 1), :] = jnp.mean(blk, axis=0, keepdims=True)
            return carry

        lax.fori_loop(0, n_blocks, mean_body, 0)

    q_t = jnp.concatenate([q_ref[0, 0], q_ref[0, 1]], axis=1)
    gate = _dot(kmean_ref[...].astype(BF16), q_t)
    blk_id = lax.broadcasted_iota(jnp.int32, gate.shape, 0)
    second = lax.broadcasted_iota(jnp.int32, gate.shape, 1) >= bs
    gate = jnp.where(blk_id < 2 * i + second.astype(jnp.int32), gate, NEG_INF)
    sel = jnp.zeros(gate.shape, F32)
    for _ in range(MOBA_TOPK):
        gmax = jnp.max(gate, axis=0, keepdims=True)
        first = jnp.min(jnp.where(gate == gmax, blk_id, LANES), axis=0, keepdims=True)
        hit = blk_id == first
        sel = jnp.where(hit & (gmax > NEG_INF), 1.0, sel)
        gate = jnp.where(hit, NEG_INF, gate)
    sel_ref[...] = sel

    def own_mask(s):
        key = lax.broadcasted_iota(jnp.int32, (bs, 2 * bs), 0)
        qry = lax.broadcasted_iota(jnp.int32, (bs, 2 * bs), 1)
        picked = sel_ref[pl.ds(2 * i, 1), :] > 0.0
        first_ok = (key <= qry) & ((qry < bs) | picked)
        return [(0, jnp.where(first_ok, s[:bs], NEG_INF)), (bs, jnp.where(key + bs <= qry, s[bs:], NEG_INF))]

    def sel_mask(s, t):
        return [(sub * bs, jnp.where(sel_ref[pl.ds(2 * (t - 1) + sub, 1), :] > 0.0, s[sub * bs:(sub + 1) * bs], NEG_INF))
                for sub in range(2)]

    out = _flash_steps(
        n_steps=i, first_blocks=[2 * i, 2 * i + 1], first_keys=2,
        step_blocks=lambda t: [2 * (t - 1), 2 * (t - 1) + 1],
        k_at=lambda j, n: k_ref[pl.ds(pl.multiple_of(j * bs, bs), n * bs), :], v_at=lambda j: v_ref[0, j],
        q_t=q_t, mask_first=own_mask, mask_step=sel_mask,
        s_ref=s_ref, mx_ref=mx_ref, p_ref=p_ref, acc_ref=acc_ref)
    o_ref[...] = jnp.transpose(out).astype(o_ref.dtype)


def _moba(q_t, k, v_t):
    s = k.shape[0]
    bs = MOBA_BLOCK
    n_blocks = s // bs
    assert n_blocks <= LANES and bs == KEY_BLOCK and n_blocks % 2 == 0
    bq = 2 * bs
    kv = pl.BlockSpec((s, MOBA_DH), lambda h, i: (0, h))
    return pl.pallas_call(
        functools.partial(_moba_kernel, n_blocks=n_blocks),
        grid=(MOBA_HEADS, n_blocks // 2),
        in_specs=[pl.BlockSpec((1, 2, MOBA_DH, bs), lambda h, i: (h, i, 0, 0)), kv,
                  pl.BlockSpec((1, n_blocks, MOBA_DH + SUM_ROWS, bs), lambda h, i: (h, 0, 0, 0))],
        out_specs=pl.BlockSpec((bq, MOBA_DH), lambda h, i: (i, h)),
        out_shape=jax.ShapeDtypeStruct((s, MOBA_W), BF16),
        scratch_shapes=[pltpu.VMEM((LANES, MOBA_DH), F32),
                        pltpu.VMEM((LANES, bq), F32),
                        pltpu.VMEM((2, 2 * KEY_BLOCK, bq), F32),
                        pltpu.VMEM((2, 1, bq), F32),
                        pltpu.VMEM((2, 2 * KEY_BLOCK, bq), BF16),
                        pltpu.VMEM((MOBA_DH + SUM_ROWS, bq), F32)],
        compiler_params=_params("parallel", "arbitrary"),
        name="moba_attention",
    )(q_t, k, v_t)


def _outproj_peer_kernel(*refs, n_in):
    o_refs = refs[:n_in]
    w_refs = refs[n_in:2 * n_in]
    x_ref, g_ref, nw_ref, sc_ref, sh_ref, wq_ref, keys_ref, xo_ref, h_ref, s_ref = refs[2 * n_in:]
    y = _dot(o_refs[0][...], w_refs[0][...])
    for o_ref, w_ref in zip(o_refs[1:], w_refs[1:]):
        y = y + _dot(o_ref[...], w_ref[...])
    x = x_ref[...] + g_ref[...] * y
    xo_ref[...] = x
    h = _norm_mod(x, nw_ref[...], sc_ref[...], sh_ref[...]).astype(BF16)
    h_ref[...] = h
    q = _dot(h, wq_ref[...]).astype(BF16)
    for hp in range(2 * PEER_HEADS):
        s_ref[hp] = _dot_nt(keys_ref[hp], q[:, hp * PEER_DSUB:(hp + 1) * PEER_DSUB])


def _outproj_peer_proj(os_, ws, x, gate, nw, sc, sh, wq, keys):
    s, d = x.shape
    tm = min(s, 256)
    row = lambda n: pl.BlockSpec((tm, n), lambda i: (i, 0))
    vec = _full((1, d))
    return pl.pallas_call(
        functools.partial(_outproj_peer_kernel, n_in=len(os_)),
        grid=(s // tm,),
        in_specs=[row(o.shape[1]) for o in os_] + [_full(w.shape) for w in ws]
                 + [row(d), vec, vec, vec, vec, _full(wq.shape), _full(keys.shape)],
        out_specs=[row(d), row(d), pl.BlockSpec((2 * PEER_HEADS, PEER_NKEYS, tm), lambda i: (0, 0, i))],
        out_shape=[jax.ShapeDtypeStruct((s, d), F32), jax.ShapeDtypeStruct((s, d), BF16),
                   jax.ShapeDtypeStruct((2 * PEER_HEADS, PEER_NKEYS, s), F32)],
        compiler_params=_params("parallel"),
        name="out_proj_peer_scores",
    )(*os_, *ws, x, gate, nw, sc, sh, wq, keys)


def _mla_proj_kernel(x_ref, nw_ref, sc_ref, sh_ref, wcq_ref, wckv_ref, wkr_ref, qn_ref, kvn_ref,
                     wqn_ref, wqr_ref, wkn_ref, wv_ref, cos_ref, sin_ref, q_ref, k_ref, v_ref):
    h = _norm_mod(x_ref[...], nw_ref[...], sc_ref[...], sh_ref[...]).astype(BF16)
    cq = _dot(h, wcq_ref[...])
    ckv = _dot(h, wckv_ref[...])
    kr = _dot(h, wkr_ref[...])
    cq = (cq * lax.rsqrt(jnp.mean(cq * cq, axis=-1, keepdims=True) + EPS) * qn_ref[...]).astype(BF16)
    ckv = (ckv * lax.rsqrt(jnp.mean(ckv * ckv, axis=-1, keepdims=True) + EPS) * kvn_ref[...]).astype(BF16)
    cos = cos_ref[...]
    sin = sin_ref[...]
    scale = MLA_QK ** -0.5 * LOG2_E
    q_nope = _dot(cq, wqn_ref[...]) * scale
    q_rope = _dot(cq, wqr_ref[...])
    k_nope = _dot(ckv, wkn_ref[...])
    val = _dot(ckv, wv_ref[...])
    k_rope = _rope_heads64(kr, cos, sin)[:, :MLA_ROPE].astype(BF16)
    for pair in range(MLA_HEADS // 2):
        slab_t = jnp.transpose(_rope_heads64(q_rope[:, pair * LANES:(pair + 1) * LANES], cos, sin) * scale)
        for sub in range(2):
            hh = 2 * pair + sub
            q_ref[hh, 0, MLA_NOPE:, :] = slab_t[sub * MLA_ROPE:(sub + 1) * MLA_ROPE].astype(BF16)
    for hh in range(MLA_HEADS):
        q_ref[hh, 0, :MLA_NOPE, :] = jnp.transpose(q_nope[:, hh * MLA_NOPE:(hh + 1) * MLA_NOPE]).astype(BF16)
        k_ref[hh, :, :MLA_NOPE] = k_nope[:, hh * MLA_NOPE:(hh + 1) * MLA_NOPE].astype(BF16)
        k_ref[hh, :, MLA_NOPE:] = k_rope
        v_ref[hh, 0] = _with_sum_rows(jnp.transpose(val[:, hh * MLA_V:(hh + 1) * MLA_V]))


def _mla_proj(x, nw, sc, sh, wcq, wckv, wkr, qn, kvn, wqn, wqr, wkn, wv, cos, sin):
    s, d = x.shape
    tm = KEY_BLOCK
    row = lambda n: pl.BlockSpec((tm, n), lambda i: (i, 0))
    vec = _full((1, d))
    heads = lambda n: pl.BlockSpec((MLA_HEADS, tm, n), lambda i: (0, i, 0))
    return pl.pallas_call(
        _mla_proj_kernel,
        grid=(s // tm,),
        in_specs=[row(d), vec, vec, vec, _full(wcq.shape), _full(wckv.shape), _full(wkr.shape),
                  _full(qn.shape), _full(kvn.shape), _full(wqn.shape), _full(wqr.shape),
                  _full(wkn.shape), _full(wv.shape), row(LANES), row(LANES)],
        out_specs=[pl.BlockSpec((MLA_HEADS, 1, MLA_QK, tm), lambda i: (0, i, 0, 0)), heads(MLA_QK),
                   pl.BlockSpec((MLA_HEADS, 1, MLA_V + SUM_ROWS, tm), lambda i: (0, i, 0, 0))],
        out_shape=[jax.ShapeDtypeStruct((MLA_HEADS, s // tm, MLA_QK, tm), BF16),
                   jax.ShapeDtypeStruct((MLA_HEADS, s, MLA_QK), BF16),
                   jax.ShapeDtypeStruct((MLA_HEADS, s // tm, MLA_V + SUM_ROWS, tm), BF16)],
        compiler_params=_params("parallel"),
        name="mla_in_proj",
    )(x, nw, sc, sh, wcq, wckv, wkr, qn, kvn, wqn, wqr, wkn, wv, cos, sin)


def _mla_attn_kernel(q_ref, k_ref, v_ref, o_ref, s_ref, mx_ref, p_ref, acc_ref):
    i = pl.program_id(1)

    def causal(s):
        key = lax.broadcasted_iota(jnp.int32, s.shape, 0)
        qry = lax.broadcasted_iota(jnp.int32, s.shape, 1)
        return [(0, jnp.where(key <= qry, s, NEG_INF))]

    per_q = MLA_Q_BLOCK // KEY_BLOCK
    q_t = jnp.concatenate([q_ref[0, b] for b in range(per_q)], axis=1)
    out = _flash_steps(
        n_steps=i, first_blocks=[2 * i, 2 * i + 1], first_keys=2,
        step_blocks=lambda t: [2 * (t - 1), 2 * (t - 1) + 1],
        k_at=lambda j, n: k_ref[0, pl.ds(pl.multiple_of(j * KEY_BLOCK, KEY_BLOCK), n * KEY_BLOCK), :],
        v_at=lambda j: v_ref[0, j], q_t=q_t, mask_first=causal, mask_step=lambda s, t: [(0, s)],
        s_ref=s_ref, mx_ref=mx_ref, p_ref=p_ref, acc_ref=acc_ref)
    o_ref[...] = jnp.transpose(out).astype(o_ref.dtype)


def _mla_attention(q_t, k, v_t):
    _, s, _ = k.shape
    bq = MLA_Q_BLOCK
    per_q = bq // KEY_BLOCK
    assert per_q == 2
    return pl.pallas_call(
        _mla_attn_kernel,
        grid=(MLA_HEADS, s // bq),
        in_specs=[pl.BlockSpec((1, per_q, MLA_QK, KEY_BLOCK), lambda h, i: (h, i, 0, 0)),
                  pl.BlockSpec((1, s, MLA_QK), lambda h, i: (h, 0, 0)),
                  pl.BlockSpec((1, s // KEY_BLOCK, MLA_V + SUM_ROWS, KEY_BLOCK), lambda h, i: (h, 0, 0, 0))],
        out_specs=pl.BlockSpec((bq, MLA_V), lambda h, i: (i, h)),
        out_shape=jax.ShapeDtypeStruct((s, MLA_HEADS * MLA_V), BF16),
        scratch_shapes=[pltpu.VMEM((2, 2 * KEY_BLOCK, bq), F32), pltpu.VMEM((2, 1, bq), F32),
                        pltpu.VMEM((2, 2 * KEY_BLOCK, bq), BF16), pltpu.VMEM((MLA_V + SUM_ROWS, bq), F32)],
        compiler_params=_params("parallel", "arbitrary"),
        name="mla_attention",
    )(q_t, k, v_t)


def _compare_exchange(vs, i, l):
    vs[i], vs[l] = jnp.maximum(vs[i], vs[l]), jnp.minimum(vs[i], vs[l])


def _sort_desc(vs):
    vs = list(vs)
    n = len(vs)
    k = 2
    while k <= n:
        j = k // 2
        while j >= 1:
            for i in range(n):
                l = i ^ j
                if l > i:
                    if i & k == 0:
                        _compare_exchange(vs, i, l)
                    else:
                        _compare_exchange(vs, l, i)
            j //= 2
        k *= 2
    return vs


def _merge_top(a, b):
    n = len(a)
    vs = [jnp.maximum(a[i], b[n - 1 - i]) for i in range(n)]
    j = n // 2
    while j >= 1:
        for i in range(n):
            if i ^ j > i:
                _compare_exchange(vs, i, i ^ j)
        j //= 2
    return vs


def _top16_of_keys(groups):
    vs = _sort_desc(groups)
    for shift in (4, 2, 1):
        vs = _merge_top(vs, [pltpu.roll(v, shift, 0) for v in vs])
    return vs


def _peer_topk_kernel(s_ref, a_ref, n_ref, b_ref, rank_ref):
    sub = SUBLANES
    groups = PEER_NKEYS // sub

    def head_body(h, carry):
        g1 = [s_ref[2 * h, g * sub:(g + 1) * sub, :] for g in range(groups)]
        g2 = [s_ref[2 * h + 1, g * sub:(g + 1) * sub, :] for g in range(groups)]
        v1 = _top16_of_keys(g1)
        v2 = _top16_of_keys(g2)

        ninf = jnp.full_like(v1[0], NEG_INF)
        pad = lambda vs: vs + [ninf] * (PEER_TOPK - len(vs))
        row = lambda r1: [v1[r1] + v2[r2] for r2 in range(PEER_TOPK // (r1 + 1))]
        top = _merge_top(
            _merge_top(row(0), _merge_top(pad(row(1)), pad(row(2)))),
            _merge_top(_sort_desc(pad(row(3) + row(4) + row(5) + row(6) + row(7))),
                       pad([row(r1)[0] for r1 in range(8, PEER_TOPK)])))
        thr = top[PEER_TOPK - 1]
        z = jnp.ones_like(thr)
        for r in range(1, PEER_TOPK):
            z = z + jnp.exp(top[r] - top[0])
        half_inv_z = 0.5 / z

        split = 5
        hit = lambda s1, r2s: sum(jnp.where(s1 + v2[r2] >= thr, 1.0, 0.0) for r2 in r2s)
        n_top = [hit(v1[r1], range(split, PEER_TOPK // (r1 + 1))) for r1 in range(2)]

        b_all, rank_all = [], []
        for g in range(groups):
            rows = slice(g * sub, (g + 1) * sub)
            n = hit(g1[g], range(split)) + jnp.where(g1[g] == v1[0], n_top[0],
                                                     jnp.where(g1[g] == v1[1], n_top[1], 0.0))
            a_ref[h, rows, :] = jnp.exp(g1[g] - v1[0]) * half_inv_z
            n_ref[h, rows, :] = n
            b_all.append(jnp.exp(g2[g] - v2[0]))
            rank_all.append(sum(jnp.where(v2[r] > g2[g], 1.0, 0.0) for r in range(PEER_TOPK)))
        for pair in range(groups // 2):
            b_ref[h, pair] = jnp.concatenate(b_all[2 * pair:2 * pair + 2], axis=0).astype(BF16)
            rank_ref[h, pair] = jnp.concatenate(rank_all[2 * pair:2 * pair + 2], axis=0).astype(BF16)
        return carry

    lax.fori_loop(0, PEER_HEADS, head_body, 0)


def _peer_topk(scores):
    _, nk, s = scores.shape
    tt = min(s, PEER_TOPK_TOK)
    big = pl.BlockSpec((PEER_HEADS, nk, tt), lambda i: (0, 0, i))
    tiled = pl.BlockSpec((PEER_HEADS, nk // BF16_ROWS, BF16_ROWS, tt), lambda i: (0, 0, 0, i))
    shape = (PEER_HEADS, nk, s)
    shape16 = (PEER_HEADS, nk // BF16_ROWS, BF16_ROWS, s)
    return pl.pallas_call(
        _peer_topk_kernel,
        grid=(s // tt,),
        in_specs=[pl.BlockSpec((2 * PEER_HEADS, nk, tt), lambda i: (0, 0, i))],
        out_specs=[big, big, tiled, tiled],
        out_shape=[jax.ShapeDtypeStruct(shape, F32), jax.ShapeDtypeStruct(shape, F32),
                   jax.ShapeDtypeStruct(shape16, BF16), jax.ShapeDtypeStruct(shape16, BF16)],
        compiler_params=_params("parallel"),
        name="peer_topk_threshold",
    )(scores)


def _peer_dense_kernel(h_ref, a_ref, n_ref, b_ref, rank_ref, u_ref, vt_ref, x_ref, g_ref, o_ref,
                       acc_ref, p_ref):
    j = pl.program_id(1)
    nk = PEER_NKEYS

    @pl.when(j == 0)
    def _():
        acc_ref[...] = jnp.zeros_like(acc_ref)

    act_t = _dot_nt(u_ref[...], h_ref[...])
    for blk in range(PEER_EXP // nk):
        i1 = j * (PEER_EXP // nk) + blk
        wgt = jnp.zeros((nk // BF16_ROWS, BF16_ROWS, PEER_TOK), BF16)
        for h in range(PEER_HEADS):
            row = lambda ref: jnp.broadcast_to(ref[h, pl.ds(i1, 1), :], (BF16_ROWS, PEER_TOK)).astype(BF16)[None]
            chosen = rank_ref[h] < row(n_ref)
            wgt = wgt + jnp.where(chosen, b_ref[h], 0.0) * row(a_ref)
        act = act_t[blk * nk:(blk + 1) * nk].astype(BF16)
        gelu2 = act * (1.0 + lax.erf(act * (2.0 ** -0.5)))
        p_ref[blk * nk:(blk + 1) * nk, :] = wgt.reshape(nk, PEER_TOK) * gelu2
    acc_ref[...] += _dot(vt_ref[...], p_ref[...])

    @pl.when(j == pl.num_programs(1) - 1)
    def _():
        o_ref[...] = x_ref[...] + g_ref[...] * jnp.transpose(acc_ref[...])


def _peer_dense(h2, a, n, b, rank, u16, vt16, layer, x, gate):
    s, d = x.shape
    n_exp = u16.shape[1]
    tt = min(s, PEER_TOK)
    assert tt == PEER_TOK
    tok3 = pl.BlockSpec((PEER_HEADS, PEER_NKEYS, tt), lambda i, j: (0, 0, i))
    tok4 = pl.BlockSpec((PEER_HEADS, PEER_NKEYS // BF16_ROWS, BF16_ROWS, tt), lambda i, j: (0, 0, 0, i))
    return pl.pallas_call(
        _peer_dense_kernel,
        grid=(s // tt, n_exp // PEER_EXP),
        in_specs=[pl.BlockSpec((tt, d), lambda i, j: (i, 0)),
                  tok3, tok3, tok4, tok4,
                  pl.BlockSpec((None, PEER_EXP, d), lambda i, j: (layer, j, 0)),
                  pl.BlockSpec((None, d, PEER_EXP), lambda i, j: (layer, 0, j)),
                  pl.BlockSpec((tt, d), lambda i, j: (i, 0)),
                  pl.BlockSpec((1, d), lambda i, j: (0, 0))],
        out_specs=pl.BlockSpec((tt, d), lambda i, j: (i, 0)),
        out_shape=jax.ShapeDtypeStruct((s, d), F32),
        scratch_shapes=[pltpu.VMEM((d, tt), F32), pltpu.VMEM((PEER_EXP, tt), BF16)],
        compiler_params=_params("parallel", "arbitrary"),
        name="peer_dense_experts",
    )(h2, a, n, b, rank, u16, vt16, x, gate)


def _final_norm_kernel(x_ref, w_ref, o_ref):
    x = x_ref[...]
    o_ref[...] = x * lax.rsqrt(jnp.mean(x * x, axis=-1, keepdims=True) + EPS) * w_ref[...]


def _final_norm(x, w):
    s, d = x.shape
    tm = min(s, 1024)
    return pl.pallas_call(
        _final_norm_kernel,
        grid=(s // tm,),
        in_specs=[pl.BlockSpec((tm, d), lambda i: (i, 0)), _full((1, d))],
        out_specs=pl.BlockSpec((tm, d), lambda i: (i, 0)),
        out_shape=jax.ShapeDtypeStruct((s, d), F32),
        compiler_params=_params("parallel"),
        name="final_rmsnorm",
    )(x, w)


def _lane_row(values, offset):
    return jnp.zeros((1, LANES), F32).at[0, offset:offset + values.shape[0]].set(values.astype(F32))


def _hybrid_mixer(x, nw, sc, sh, cos, sin, w_in, conv_w, a_log, dt_bias, o_norm, w_out):
    w16 = w_in.astype(BF16)
    c0 = GDN_CONV_CH
    c1 = c0 + GDN_V
    c2 = c1 + 2 * GDN_HEADS
    wba = jnp.pad(w16[:, c1:c2], ((0, 0), (0, LANES - 2 * GDN_HEADS)))
    qkva, z, ba, qb, kb, vb = _hy_proj(x, nw, sc, sh, w16[:, :c0], w16[:, c0:c1], wba, w16[:, c2:], cos, sin)
    o_a = _gdn(qkva, ba, z, conv_w, _lane_row(a_log, GDN_HEADS), _lane_row(dt_bias, GDN_HEADS),
               o_norm.reshape(1, -1))
    o_b = _moba(qb, kb, vb)
    wo16 = w_out.astype(BF16)
    return [o_a, o_b], [wo16[:GDN_V], wo16[GDN_V:]]


def _mla_mixer(x, nw, sc, sh, cos, sin, w_in, q_norm, kv_norm, w_uq, w_ukv, w_out):
    w16 = w_in.astype(BF16)
    r0 = MLA_Q_RANK
    r1 = r0 + MLA_KV_RANK
    wkr = jnp.pad(w16[:, r1:], ((0, 0), (0, LANES - MLA_ROPE)))
    wuq = w_uq.astype(BF16).reshape(MLA_Q_RANK, MLA_HEADS, MLA_QK)
    wqn = wuq[:, :, :MLA_NOPE].reshape(MLA_Q_RANK, MLA_HEADS * MLA_NOPE)
    wqr = wuq[:, :, MLA_NOPE:].reshape(MLA_Q_RANK, MLA_HEADS * MLA_ROPE)
    wukv = w_ukv.astype(BF16).reshape(MLA_KV_RANK, MLA_HEADS, MLA_NOPE + MLA_V)
    wkn = wukv[:, :, :MLA_NOPE].reshape(MLA_KV_RANK, MLA_HEADS * MLA_NOPE)
    wv = wukv[:, :, MLA_NOPE:].reshape(MLA_KV_RANK, MLA_HEADS * MLA_V)
    q, k, v = _mla_proj(x, nw, sc, sh, w16[:, :r0], w16[:, r0:r1], wkr, q_norm.reshape(1, -1),
                        kv_norm.reshape(1, -1), wqn, wqr, wkn, wv, cos, sin)
    o = _mla_attention(q, k, v)
    return [o], [w_out.astype(BF16)]


def kernel(x, c, positions, mod_w, mod_b, norm_mix, norm_ffn, hy_w_in, gdn_conv, gdn_a_log, gdn_dt_bias, gdn_o_norm, hy_w_out, mla_w_in, mla_q_norm, mla_kv_norm, mla_w_uq, mla_w_ukv, mla_w_out, peer_w_q, peer_sub_keys, peer_u, peer_v, final_norm):
    bsz, s, d = x.shape
    assert bsz == 1 and d == D_MODEL
    depth = mod_w.shape[0]
    xs = x.reshape(s, d)
    mod = _modulation(c.reshape(d, 1), mod_w, mod_b)
    cos_b, sin_b, cos_c, sin_c = _rope_tables(positions.reshape(s, 1))
    u16 = peer_u.astype(BF16)
    vt16 = jnp.swapaxes(peer_v.astype(BF16), 1, 2)
    wq16 = peer_w_q.astype(BF16)
    keys16 = peer_sub_keys.astype(BF16).reshape(depth, 2 * PEER_HEADS, PEER_NKEYS, PEER_DSUB)
    for layer in range(depth):
        sh1, sc1, g1, sh2, sc2, g2 = (mod[layer, :, n * d:(n + 1) * d] for n in range(6))
        nw = norm_mix[layer].reshape(1, d)
        i = layer // 2
        if layer % 2 == 0:
            outs, w_outs = _hybrid_mixer(xs, nw, sc1, sh1, cos_b, sin_b, hy_w_in[i], gdn_conv[i], gdn_a_log[i],
                                         gdn_dt_bias[i], gdn_o_norm[i], hy_w_out[i])
        else:
            outs, w_outs = _mla_mixer(xs, nw, sc1, sh1, cos_c, sin_c, mla_w_in[i], mla_q_norm[i], mla_kv_norm[i],
                                      mla_w_uq[i], mla_w_ukv[i], mla_w_out[i])
        xs, h2, scores = _outproj_peer_proj(outs, w_outs, xs, g1, norm_ffn[layer].reshape(1, d), sc2, sh2,
                                            wq16[layer], keys16[layer])
        a, n, b, rank = _peer_topk(scores)
        xs = _peer_dense(h2, a, n, b, rank, u16, vt16, layer, xs, g2)
    return _final_norm(xs, final_norm.reshape(1, d)).reshape(bsz, s, d)
```

```python
import functools
import math

import jax
import jax.numpy as jnp
from jax import lax
from jax.experimental import pallas as pl
from jax.experimental.pallas import tpu as pltpu

F32 = jnp.float32
BF16 = jnp.bfloat16
NEG_INF = float("-inf")
HIGHEST = lax.Precision.HIGHEST

D_MODEL = 1024
MOD_COLS = 1536
EPS = 1e-6
ROPE_THETA = 10000.0
LOG2_E = math.log2(math.e)

GDN_HEADS = 4
GDN_DK = 128
GDN_CHUNK = 64
GDN_QK = GDN_HEADS * GDN_DK
GDN_DV = 128
GDN_V = GDN_HEADS * GDN_DV
GDN_CONV = 4
GDN_CONV_CH = 2 * GDN_QK + GDN_V
GDN_ROWS = 256

MOBA_HEADS = 4
MOBA_DH = 128
MOBA_BLOCK = 256
MOBA_TOPK = 3
MOBA_W = MOBA_HEADS * MOBA_DH

MLA_HEADS = 8
MLA_Q_RANK = 512
MLA_KV_RANK = 256
MLA_NOPE = 128
MLA_ROPE = 64
MLA_V = 128
MLA_QK = MLA_NOPE + MLA_ROPE
KEY_BLOCK = 256
MLA_Q_BLOCK = 512

PEER_HEADS = 8
PEER_NKEYS = 128
PEER_TOPK = 16
PEER_DSUB = 128
PEER_TOK = 512
PEER_EXP = 2048
PEER_TOPK_TOK = 256

LANES = 128
SUBLANES = 8
BF16_ROWS = 16
SUM_ROWS = BF16_ROWS
VMEM_LIMIT = 56 * 1024 * 1024


def _params(*sem):
    return pltpu.CompilerParams(dimension_semantics=sem, vmem_limit_bytes=VMEM_LIMIT)


def _dot(a, b, precision=None):
    return jnp.dot(a, b, preferred_element_type=F32, precision=precision)


def _dot_nt(a, b):
    return lax.dot_general(a, b, (((1,), (1,)), ((), ())), preferred_element_type=F32)


def _sigmoid(x):
    return jax.nn.sigmoid(x)


def _norm_mod(x, nw, sc, sh):
    y = x * lax.rsqrt(jnp.mean(x * x, axis=-1, keepdims=True) + EPS) * nw
    return y * (1.0 + sc) + sh


def _full(shape):
    n = len(shape)
    return pl.BlockSpec(shape, lambda *_: (0,) * n)


def _mod_kernel(c_ref, w_ref, b_ref, o_ref):
    c = c_ref[...]
    cond = c * _sigmoid(c)
    o_ref[0] = jnp.sum(cond * w_ref[0], axis=0, keepdims=True) + b_ref[0]


def _modulation(c_col, mod_w, mod_b):
    depth, d, n = mod_w.shape
    tn = MOD_COLS
    return pl.pallas_call(
        _mod_kernel,
        grid=(depth, n // tn),
        in_specs=[pl.BlockSpec((d, 1), lambda l, j: (0, 0)),
                  pl.BlockSpec((1, d, tn), lambda l, j: (l, 0, j)),
                  pl.BlockSpec((1, 1, tn), lambda l, j: (l, 0, j))],
        out_specs=pl.BlockSpec((1, 1, tn), lambda l, j: (l, 0, j)),
        out_shape=jax.ShapeDtypeStruct((depth, 1, n), F32),
        compiler_params=_params("parallel", "parallel"),
        name="adaln_modulation",
    )(c_col, mod_w, mod_b.reshape(depth, 1, n))


def _rope_kernel(pos_ref, fb_ref, fc_ref, cb_ref, sb_ref, cc_ref, sc_ref):
    p = pos_ref[...].astype(F32)
    lane = lax.broadcasted_iota(jnp.int32, cb_ref.shape, 1)
    ab = p * fb_ref[...]
    sb = jnp.sin(ab)
    cb_ref[...] = jnp.cos(ab)
    sb_ref[...] = jnp.where(lane < MOBA_DH // 2, -sb, sb)
    ac = p * fc_ref[...]
    sc = jnp.sin(ac)
    cc_ref[...] = jnp.cos(ac)
    sc_ref[...] = jnp.where(lane % MLA_ROPE < MLA_ROPE // 2, -sc, sc)


def _rope_tables(pos_col):
    s = pos_col.shape[0]
    tm = min(s, 1024)
    fb = ROPE_THETA ** (-jnp.arange(0, MOBA_DH, 2, dtype=F32) / MOBA_DH)
    fc = ROPE_THETA ** (-jnp.arange(0, MLA_ROPE, 2, dtype=F32) / MLA_ROPE)
    fb = jnp.tile(fb, 2).reshape(1, LANES)
    fc = jnp.tile(fc, 4).reshape(1, LANES)
    tab = pl.BlockSpec((tm, LANES), lambda i: (i, 0))
    return pl.pallas_call(
        _rope_kernel,
        grid=(s // tm,),
        in_specs=[pl.BlockSpec((tm, 1), lambda i: (i, 0)), _full((1, LANES)), _full((1, LANES))],
        out_specs=[tab] * 4,
        out_shape=[jax.ShapeDtypeStruct((s, LANES), F32)] * 4,
        compiler_params=_params("parallel"),
        name="rope_tables",
    )(pos_col, fb, fc)


def _rope_head128(x, cos, sin):
    return x * cos + pltpu.roll(x, MOBA_DH // 2, 1) * sin


def _rope_heads64(x, cos, sin):
    lane = lax.broadcasted_iota(jnp.int32, x.shape, 1)
    first = lane % MLA_ROPE < MLA_ROPE // 2
    partner = jnp.where(first, pltpu.roll(x, LANES - MLA_ROPE // 2, 1), pltpu.roll(x, MLA_ROPE // 2, 1))
    return x * cos + partner * sin


def _hy_proj_kernel(x_ref, nw_ref, sc_ref, sh_ref, wa_ref, wz_ref, wba_ref, wb_ref, cos_ref, sin_ref,
                    qkva_ref, z_ref, ba_ref, qb_ref, kb_ref, vb_ref):
    h = _norm_mod(x_ref[...], nw_ref[...], sc_ref[...], sh_ref[...]).astype(BF16)
    qkva_ref[...] = _dot(h, wa_ref[...])
    z_ref[...] = _dot(h, wz_ref[...])
    ba_ref[...] = _dot(h, wba_ref[...])
    qkvb = _dot(h, wb_ref[...])
    cos = cos_ref[...]
    sin = sin_ref[...]
    scale = MOBA_DH ** -0.5 * LOG2_E
    for hh in range(MOBA_HEADS):
        lo = hh * MOBA_DH
        q = qkvb[:, lo:lo + MOBA_DH]
        k = qkvb[:, MOBA_W + lo:MOBA_W + lo + MOBA_DH]
        qb_ref[hh, 0] = jnp.transpose(_rope_head128(q, cos, sin) * scale).astype(BF16)
        kb_ref[:, lo:lo + MOBA_DH] = _rope_head128(k, cos, sin).astype(BF16)
        v = qkvb[:, 2 * MOBA_W + lo:2 * MOBA_W + lo + MOBA_DH]
        vb_ref[hh, 0] = _with_sum_rows(jnp.transpose(v))


def _hy_proj(x, nw, sc, sh, wa, wz, wba, wb, cos, sin):
    s, d = x.shape
    tm = KEY_BLOCK
    row = lambda n: pl.BlockSpec((tm, n), lambda i: (i, 0))
    vec = _full((1, d))
    return pl.pallas_call(
        _hy_proj_kernel,
        grid=(s // tm,),
        in_specs=[row(d), vec, vec, vec, _full(wa.shape), _full(wz.shape), _full(wba.shape),
                  _full(wb.shape), row(LANES), row(LANES)],
        out_specs=[row(GDN_CONV_CH), row(GDN_V), row(LANES),
                   pl.BlockSpec((MOBA_HEADS, 1, MOBA_DH, tm), lambda i: (0, i, 0, 0)), row(MOBA_W),
                   pl.BlockSpec((MOBA_HEADS, 1, MOBA_DH + SUM_ROWS, tm), lambda i: (0, i, 0, 0))],
        out_shape=[jax.ShapeDtypeStruct((s, GDN_CONV_CH), F32),
                   jax.ShapeDtypeStruct((s, GDN_V), F32),
                   jax.ShapeDtypeStruct((s, LANES), F32),
                   jax.ShapeDtypeStruct((MOBA_HEADS, s // tm, MOBA_DH, tm), BF16),
                   jax.ShapeDtypeStruct((s, MOBA_W), BF16),
                   jax.ShapeDtypeStruct((MOBA_HEADS, s // tm, MOBA_DH + SUM_ROWS, tm), BF16)],
        compiler_params=_params("parallel"),
        name="hybrid_in_proj",
    )(x, nw, sc, sh, wa, wz, wba, wb, cos, sin)


def _unit_lower_inverses(l_stricts, eye):
    powers = [-l for l in l_stricts]
    invs = [eye + p for p in powers]
    p = 2
    while p < GDN_CHUNK:
        powers = [_dot(x.astype(BF16), x.astype(BF16)) for x in powers]
        invs = [_dot(inv.astype(BF16), (eye + x).astype(BF16)) for inv, x in zip(invs, powers)]
        p *= 2
    return invs


def _gdn_kernel(qkv_ref, ba_ref, z_ref, conv_ref, alog_ref, dtb_ref, onorm_ref, o_ref, ext_ref, state_ref):
    tb = GDN_ROWS
    cs = GDN_CHUNK
    pad = SUBLANES

    @pl.when(pl.program_id(0) == 0)
    def _():
        ext_ref[0:pad, :] = jnp.zeros((pad, GDN_CONV_CH), F32)
        state_ref[...] = jnp.zeros_like(state_ref)

    ext_ref[pad:pad + tb, :] = qkv_ref[...]
    cw = conv_ref[...]
    first = pad - (GDN_CONV - 1)
    y = ext_ref[first:first + tb, :] * cw[0:1]
    for j in range(1, GDN_CONV):
        y = y + ext_ref[first + j:first + j + tb, :] * cw[j:j + 1]
    ext_ref[0:pad, :] = ext_ref[tb:tb + pad, :]
    y = y * _sigmoid(y)

    ba = ba_ref[...]
    beta_all = _sigmoid(ba)
    t = ba + dtb_ref[...]
    softplus = jnp.maximum(t, 0.0) + jnp.log1p(jnp.exp(-jnp.abs(t)))
    g_all = -jnp.exp(alog_ref[...]) * softplus

    row = lax.broadcasted_iota(jnp.int32, (tb, tb), 0)
    col = lax.broadcasted_iota(jnp.int32, (tb, tb), 1)
    same = (row // cs) == (col // cs)
    causal = same & (col <= row)
    strict = same & (col < row)
    eye = (row == col).astype(F32)
    g_cum = _dot(causal.astype(F32), g_all, HIGHEST)
    g_tot = _dot(same.astype(F32), g_all, HIGHEST)
    g_cum_t = jnp.transpose(g_cum)
    z = z_ref[...]
    onorm = onorm_ref[...]

    heads = range(GDN_HEADS)
    rhs, l_strict, gtot, qk, qdec, kdec = [], [], [], [], [], []
    for h in heads:
        lo = h * GDN_DK
        q = y[:, lo:lo + GDN_DK]
        k = y[:, GDN_QK + lo:GDN_QK + lo + GDN_DK]
        v = y[:, 2 * GDN_QK + lo:2 * GDN_QK + lo + GDN_DK]
        q = q * lax.rsqrt(jnp.sum(q * q, axis=-1, keepdims=True) + EPS) * (GDN_DK ** -0.5)
        k = k * lax.rsqrt(jnp.sum(k * k, axis=-1, keepdims=True) + EPS)
        beta = beta_all[:, h:h + 1]
        gcol = g_cum[:, GDN_HEADS + h:GDN_HEADS + h + 1]
        grow = g_cum_t[GDN_HEADS + h:GDN_HEADS + h + 1, :]
        gtot.append(g_tot[:, GDN_HEADS + h:GDN_HEADS + h + 1])
        decay = jnp.exp(jnp.where(causal, gcol - grow, NEG_INF))
        kb = k * beta
        k16 = k.astype(BF16)
        l_strict.append(jnp.where(strict, _dot_nt(kb.astype(BF16), k16) * decay, 0.0))
        eg = jnp.exp(gcol)
        rhs.append(jnp.concatenate([v * beta, kb * eg], axis=1).astype(BF16))
        qk.append((_dot_nt(q.astype(BF16), k16) * decay).astype(BF16))
        qdec.append((q * eg).astype(BF16))
        kdec.append(k * jnp.exp(gtot[h] - gcol))

    uw = [_dot(t_inv.astype(BF16), rhs[h]) for h, t_inv in enumerate(_unit_lower_inverses(l_strict, eye))]
    u = [x[:, :GDN_DK] for x in uw]
    w = [x[:, GDN_DK:].astype(BF16) for x in uw]

    state = [state_ref[h] for h in heads]
    v_new = [[] for _ in heads]
    o_state = [[] for _ in heads]
    for c in range(tb // cs):
        r0 = c * cs
        for h in heads:
            s16 = state[h].astype(BF16)
            vn = u[h][r0:r0 + cs] - _dot(w[h][r0:r0 + cs], s16)
            o_state[h].append(_dot(qdec[h][r0:r0 + cs], s16))
            kd_t = jnp.transpose(kdec[h][r0:r0 + cs]).astype(BF16)
            state[h] = state[h] * jnp.exp(gtot[h][r0:r0 + 1]) + _dot(kd_t, vn.astype(BF16))
            v_new[h].append(vn)

    for h in heads:
        lo = h * GDN_DK
        state_ref[h] = state[h]
        o = jnp.concatenate(o_state[h], axis=0) + _dot(qk[h], jnp.concatenate(v_new[h], axis=0).astype(BF16))
        o = o * lax.rsqrt(jnp.mean(o * o, axis=-1, keepdims=True) + EPS) * onorm
        zh = z[:, lo:lo + GDN_DK]
        o_ref[:, lo:lo + GDN_DK] = (o * (zh * _sigmoid(zh))).astype(BF16)


def _gdn(qkva, ba, z, conv_w, alog, dtb, onorm):
    s = qkva.shape[0]
    tb = GDN_ROWS
    row = lambda n: pl.BlockSpec((tb, n), lambda i: (i, 0))
    return pl.pallas_call(
        _gdn_kernel,
        grid=(s // tb,),
        in_specs=[row(GDN_CONV_CH), row(LANES), row(GDN_V), _full(conv_w.shape),
                  _full((1, LANES)), _full((1, LANES)), _full((1, LANES))],
        out_specs=row(GDN_V),
        out_shape=jax.ShapeDtypeStruct((s, GDN_V), BF16),
        scratch_shapes=[pltpu.VMEM((tb + SUBLANES, GDN_CONV_CH), F32),
                        pltpu.VMEM((GDN_HEADS, GDN_DK, GDN_DV), F32)],
        compiler_params=_params("arbitrary"),
        name="gated_delta_rule",
    )(qkva, ba, z, conv_w, alog, dtb, onorm)


def _scores_into(s_ref, mx_ref, slot, k_tile, q_t, mask):
    s = _dot(k_tile, q_t)
    rows = s.shape[0]
    top = None
    for lo, piece in mask(s):
        s_ref[slot, lo:lo + piece.shape[0], :] = piece
        pmax = piece.max(axis=0, keepdims=True)
        top = pmax if top is None else jnp.maximum(top, pmax)
    if rows < s_ref.shape[1]:
        s_ref[slot, rows:, :] = jnp.full((s_ref.shape[1] - rows, s.shape[1]), NEG_INF, F32)
    mx_ref[slot] = top


def _values(p_ref, slot, v_at, blocks):
    v_t = jnp.concatenate([v_at(j) for j in blocks], axis=1)
    return _dot(v_t, p_ref[slot])


def _flash_steps(n_steps, first_blocks, first_keys, step_blocks, k_at, v_at, q_t, mask_first, mask_step,
                 s_ref, mx_ref, p_ref, acc_ref):
    per_step = len(first_blocks)
    _scores_into(s_ref, mx_ref, 0, k_at(first_blocks[0], first_keys), q_t, mask_first)
    m = mx_ref[0]
    p_ref[0] = jnp.exp2(s_ref[0] - m).astype(BF16)
    acc_ref[...] = jnp.zeros_like(acc_ref)
    _scores_into(s_ref, mx_ref, 1, k_at(step_blocks(1)[0], per_step), q_t, lambda s: mask_step(s, 1))

    def step(t, carry, cur):
        m, alpha_prev = carry
        nxt = jnp.minimum(t + 1, jnp.maximum(n_steps, 1))
        _scores_into(s_ref, mx_ref, 1 - cur, k_at(step_blocks(nxt)[0], per_step), q_t, lambda s: mask_step(s, nxt))
        m_new = jnp.maximum(m, mx_ref[cur])
        alpha = jnp.exp2(m - m_new)
        p = jnp.exp2(s_ref[cur] - m_new)
        prev = [jnp.where(t == 1, a, b) for a, b in zip(first_blocks, step_blocks(jnp.maximum(t - 1, 1)))]
        acc_ref[...] = alpha_prev * acc_ref[...] + _values(p_ref, 1 - cur, v_at, prev)
        p_ref[cur] = p.astype(BF16)
        return m_new, alpha

    def body(t, carry):
        return lax.cond(t % 2 == 0, lambda c: step(t, c, 0), lambda c: step(t, c, 1), carry)

    m, alpha = lax.fori_loop(1, n_steps + 1, body, (m, jnp.ones_like(m)))
    last = [jnp.where(n_steps == 0, a, b) for a, b in zip(first_blocks, step_blocks(jnp.maximum(n_steps, 1)))]
    acc = alpha * acc_ref[...] + _values(p_ref, n_steps % 2, v_at, last)
    dh = acc.shape[0] - SUM_ROWS
    return acc[:dh] / acc[dh:dh + 1]


def _with_sum_rows(v_t):
    extra = lax.broadcasted_iota(jnp.int32, (SUM_ROWS, v_t.shape[1]), 0) == 0
    return jnp.concatenate([v_t, extra.astype(F32)], axis=0).astype(BF16)


def _moba_kernel(q_ref, k_ref, v_ref, o_ref, kmean_ref, sel_ref, s_ref, mx_ref, p_ref, acc_ref, *, n_blocks):
    i = pl.program_id(1)
    bs = MOBA_BLOCK

    @pl.when(i == 0)
    def _():
        kmean_ref[...] = jnp.zeros_like(kmean_ref)

        def mean_body(n, carry):
            blk = k_ref[pl.ds(pl.multiple_of(n * bs, bs), bs), :].astype(F32)
            kmean_ref[pl.ds(n, 1), :] = jnp.mean(blk, axis=0, keepdims=True)
            return carry

        lax.fori_loop(0, n_blocks, mean_body, 0)

    q_t = jnp.concatenate([q_ref[0, 0], q_ref[0, 1]], axis=1)
    gate = _dot(kmean_ref[...].astype(BF16), q_t)
    blk_id = lax.broadcasted_iota(jnp.int32, gate.shape, 0)
    second = lax.broadcasted_iota(jnp.int32, gate.shape, 1) >= bs
    gate = jnp.where(blk_id < 2 * i + second.astype(jnp.int32), gate, NEG_INF)
    sel = jnp.zeros(gate.shape, F32)
    for _ in range(MOBA_TOPK):
        gmax = jnp.max(gate, axis=0, keepdims=True)
        first = jnp.min(jnp.where(gate == gmax, blk_id, LANES), axis=0, keepdims=True)
        hit = blk_id == first
        sel = jnp.where(hit & (gmax > NEG_INF), 1.0, sel)
        gate = jnp.where(hit, NEG_INF, gate)
    sel_ref[...] = sel

    def own_mask(s):
        key = lax.broadcasted_iota(jnp.int32, (bs, 2 * bs), 0)
        qry = lax.broadcasted_iota(jnp.int32, (bs, 2 * bs), 1)
        picked = sel_ref[pl.ds(2 * i, 1), :] > 0.0
        first_ok = (key <= qry) & ((qry < bs) | picked)
        return [(0, jnp.where(first_ok, s[:bs], NEG_INF)), (bs, jnp.where(key + bs <= qry, s[bs:], NEG_INF))]

    def sel_mask(s, t):
        return [(sub * bs, jnp.where(sel_ref[pl.ds(2 * (t - 1) + sub, 1), :] > 0.0, s[sub * bs:(sub + 1) * bs], NEG_INF))
                for sub in range(2)]

    out = _flash_steps(
        n_steps=i, first_blocks=[2 * i, 2 * i + 1], first_keys=2,
        step_blocks=lambda t: [2 * (t - 1), 2 * (t - 1) + 1],
        k_at=lambda j, n: k_ref[pl.ds(pl.multiple_of(j * bs, bs), n * bs), :], v_at=lambda j: v_ref[0, j],
        q_t=q_t, mask_first=own_mask, mask_step=sel_mask,
        s_ref=s_ref, mx_ref=mx_ref, p_ref=p_ref, acc_ref=acc_ref)
    o_ref[...] = jnp.transpose(out).astype(o_ref.dtype)


def _moba(q_t, k, v_t):
    s = k.shape[0]
    bs = MOBA_BLOCK
    n_blocks = s // bs
    assert n_blocks <= LANES and bs == KEY_BLOCK and n_blocks % 2 == 0
    bq = 2 * bs
    kv = pl.BlockSpec((s, MOBA_DH), lambda h, i: (0, h))
    return pl.pallas_call(
        functools.partial(_moba_kernel, n_blocks=n_blocks),
        grid=(MOBA_HEADS, n_blocks // 2),
        in_specs=[pl.BlockSpec((1, 2, MOBA_DH, bs), lambda h, i: (h, i, 0, 0)), kv,
                  pl.BlockSpec((1, n_blocks, MOBA_DH + SUM_ROWS, bs), lambda h, i: (h, 0, 0, 0))],
        out_specs=pl.BlockSpec((bq, MOBA_DH), lambda h, i: (i, h)),
        out_shape=jax.ShapeDtypeStruct((s, MOBA_W), BF16),
        scratch_shapes=[pltpu.VMEM((LANES, MOBA_DH), F32),
                        pltpu.VMEM((LANES, bq), F32),
                        pltpu.VMEM((2, 2 * KEY_BLOCK, bq), F32),
                        pltpu.VMEM((2, 1, bq), F32),
                        pltpu.VMEM((2, 2 * KEY_BLOCK, bq), BF16),
                        pltpu.VMEM((MOBA_DH + SUM_ROWS, bq), F32)],
        compiler_params=_params("parallel", "arbitrary"),
        name="moba_attention",
    )(q_t, k, v_t)


def _outproj_peer_kernel(*refs, n_in):
    o_refs = refs[:n_in]
    w_refs = refs[n_in:2 * n_in]
    x_ref, g_ref, nw_ref, sc_ref, sh_ref, wq_ref, keys_ref, xo_ref, h_ref, s_ref = refs[2 * n_in:]
    y = _dot(o_refs[0][...], w_refs[0][...])
    for o_ref, w_ref in zip(o_refs[1:], w_refs[1:]):
        y = y + _dot(o_ref[...], w_ref[...])
    x = x_ref[...] + g_ref[...] * y
    xo_ref[...] = x
    h = _norm_mod(x, nw_ref[...], sc_ref[...], sh_ref[...]).astype(BF16)
    h_ref[...] = h
    q = _dot(h, wq_ref[...]).astype(BF16)
    for hp in range(2 * PEER_HEADS):
        s_ref[hp] = _dot_nt(keys_ref[hp], q[:, hp * PEER_DSUB:(hp + 1) * PEER_DSUB])


def _outproj_peer_proj(os_, ws, x, gate, nw, sc, sh, wq, keys):
    s, d = x.shape
    tm = min(s, 256)
    row = lambda n: pl.BlockSpec((tm, n), lambda i: (i, 0))
    vec = _full((1, d))
    return pl.pallas_call(
        functools.partial(_outproj_peer_kernel, n_in=len(os_)),
        grid=(s // tm,),
        in_specs=[row(o.shape[1]) for o in os_] + [_full(w.shape) for w in ws]
                 + [row(d), vec, vec, vec, vec, _full(wq.shape), _full(keys.shape)],
        out_specs=[row(d), row(d), pl.BlockSpec((2 * PEER_HEADS, PEER_NKEYS, tm), lambda i: (0, 0, i))],
        out_shape=[jax.ShapeDtypeStruct((s, d), F32), jax.ShapeDtypeStruct((s, d), BF16),
                   jax.ShapeDtypeStruct((2 * PEER_HEADS, PEER_NKEYS, s), F32)],
        compiler_params=_params("parallel"),
        name="out_proj_peer_scores",
    )(*os_, *ws, x, gate, nw, sc, sh, wq, keys)


def _mla_proj_kernel(x_ref, nw_ref, sc_ref, sh_ref, wcq_ref, wckv_ref, wkr_ref, qn_ref, kvn_ref,
                     wqn_ref, wqr_ref, wkn_ref, wv_ref, cos_ref, sin_ref, q_ref, k_ref, v_ref):
    h = _norm_mod(x_ref[...], nw_ref[...], sc_ref[...], sh_ref[...]).astype(BF16)
    cq = _dot(h, wcq_ref[...])
    ckv = _dot(h, wckv_ref[...])
    kr = _dot(h, wkr_ref[...])
    cq = (cq * lax.rsqrt(jnp.mean(cq * cq, axis=-1, keepdims=True) + EPS) * qn_ref[...]).astype(BF16)
    ckv = (ckv * lax.rsqrt(jnp.mean(ckv * ckv, axis=-1, keepdims=True) + EPS) * kvn_ref[...]).astype(BF16)
    cos = cos_ref[...]
    sin = sin_ref[...]
    scale = MLA_QK ** -0.5 * LOG2_E
    q_nope = _dot(cq, wqn_ref[...]) * scale
    q_rope = _dot(cq, wqr_ref[...])
    k_nope = _dot(ckv, wkn_ref[...])
    val = _dot(ckv, wv_ref[...])
    k_rope = _rope_heads64(kr, cos, sin)[:, :MLA_ROPE].astype(BF16)
    for pair in range(MLA_HEADS // 2):
        slab_t = jnp.transpose(_rope_heads64(q_rope[:, pair * LANES:(pair + 1) * LANES], cos, sin) * scale)
        for sub in range(2):
            hh = 2 * pair + sub
            q_ref[hh, 0, MLA_NOPE:, :] = slab_t[sub * MLA_ROPE:(sub + 1) * MLA_ROPE].astype(BF16)
    for hh in range(MLA_HEADS):
        q_ref[hh, 0, :MLA_NOPE, :] = jnp.transpose(q_nope[:, hh * MLA_NOPE:(hh + 1) * MLA_NOPE]).astype(BF16)
        k_ref[hh, :, :MLA_NOPE] = k_nope[:, hh * MLA_NOPE:(hh + 1) * MLA_NOPE].astype(BF16)
        k_ref[hh, :, MLA_NOPE:] = k_rope
        v_ref[hh, 0] = _with_sum_rows(jnp.transpose(val[:, hh * MLA_V:(hh + 1) * MLA_V]))


def _mla_proj(x, nw, sc, sh, wcq, wckv, wkr, qn, kvn, wqn, wqr, wkn, wv, cos, sin):
    s, d = x.shape
    tm = KEY_BLOCK
    row = lambda n: pl.BlockSpec((tm, n), lambda i: (i, 0))
    vec = _full((1, d))
    heads = lambda n: pl.BlockSpec((MLA_HEADS, tm, n), lambda i: (0, i, 0))
    return pl.pallas_call(
        _mla_proj_kernel,
        grid=(s // tm,),
        in_specs=[row(d), vec, vec, vec, _full(wcq.shape), _full(wckv.shape), _full(wkr.shape),
                  _full(qn.shape), _full(kvn.shape), _full(wqn.shape), _full(wqr.shape),
                  _full(wkn.shape), _full(wv.shape), row(LANES), row(LANES)],
        out_specs=[pl.BlockSpec((MLA_HEADS, 1, MLA_QK, tm), lambda i: (0, i, 0, 0)), heads(MLA_QK),
                   pl.BlockSpec((MLA_HEADS, 1, MLA_V + SUM_ROWS, tm), lambda i: (0, i, 0, 0))],
        out_shape=[jax.ShapeDtypeStruct((MLA_HEADS, s // tm, MLA_QK, tm), BF16),
                   jax.ShapeDtypeStruct((MLA_HEADS, s, MLA_QK), BF16),
                   jax.ShapeDtypeStruct((MLA_HEADS, s // tm, MLA_V + SUM_ROWS, tm), BF16)],
        compiler_params=_params("parallel"),
        name="mla_in_proj",
    )(x, nw, sc, sh, wcq, wckv, wkr, qn, kvn, wqn, wqr, wkn, wv, cos, sin)


def _mla_attn_kernel(q_ref, k_ref, v_ref, o_ref, s_ref, mx_ref, p_ref, acc_ref):
    i = pl.program_id(1)

    def causal(s):
        key = lax.broadcasted_iota(jnp.int32, s.shape, 0)
        qry = lax.broadcasted_iota(jnp.int32, s.shape, 1)
        return [(0, jnp.where(key <= qry, s, NEG_INF))]

    per_q = MLA_Q_BLOCK // KEY_BLOCK
    q_t = jnp.concatenate([q_ref[0, b] for b in range(per_q)], axis=1)
    out = _flash_steps(
        n_steps=i, first_blocks=[2 * i, 2 * i + 1], first_keys=2,
        step_blocks=lambda t: [2 * (t - 1), 2 * (t - 1) + 1],
        k_at=lambda j, n: k_ref[0, pl.ds(pl.multiple_of(j * KEY_BLOCK, KEY_BLOCK), n * KEY_BLOCK), :],
        v_at=lambda j: v_ref[0, j], q_t=q_t, mask_first=causal, mask_step=lambda s, t: [(0, s)],
        s_ref=s_ref, mx_ref=mx_ref, p_ref=p_ref, acc_ref=acc_ref)
    o_ref[...] = jnp.transpose(out).astype(o_ref.dtype)


def _mla_attention(q_t, k, v_t):
    _, s, _ = k.shape
    bq = MLA_Q_BLOCK
    per_q = bq // KEY_BLOCK
    assert per_q == 2
    return pl.pallas_call(
        _mla_attn_kernel,
        grid=(MLA_HEADS, s // bq),
        in_specs=[pl.BlockSpec((1, per_q, MLA_QK, KEY_BLOCK), lambda h, i: (h, i, 0, 0)),
                  pl.BlockSpec((1, s, MLA_QK), lambda h, i: (h, 0, 0)),
                  pl.BlockSpec((1, s // KEY_BLOCK, MLA_V + SUM_ROWS, KEY_BLOCK), lambda h, i: (h, 0, 0, 0))],
        out_specs=pl.BlockSpec((bq, MLA_V), lambda h, i: (i, h)),
        out_shape=jax.ShapeDtypeStruct((s, MLA_HEADS * MLA_V), BF16),
        scratch_shapes=[pltpu.VMEM((2, 2 * KEY_BLOCK, bq), F32), pltpu.VMEM((2, 1, bq), F32),
                        pltpu.VMEM((2, 2 * KEY_BLOCK, bq), BF16), pltpu.VMEM((MLA_V + SUM_ROWS, bq), F32)],
        compiler_params=_params("parallel", "arbitrary"),
        name="mla_attention",
    )(q_t, k, v_t)


def _compare_exchange(vs, i, l):
    vs[i], vs[l] = jnp.maximum(vs[i], vs[l]), jnp.minimum(vs[i], vs[l])


def _sort_desc(vs):
    vs = list(vs)
    n = len(vs)
    k = 2
    while k <= n:
        j = k // 2
        while j >= 1:
            for i in range(n):
                l = i ^ j
                if l > i:
                    if i & k == 0:
                        _compare_exchange(vs, i, l)
                    else:
                        _compare_exchange(vs, l, i)
            j //= 2
        k *= 2
    return vs


def _merge_top(a, b):
    n = len(a)
    vs = [jnp.maximum(a[i], b[n - 1 - i]) for i in range(n)]
    j = n // 2
    while j >= 1:
        for i in range(n):
            if i ^ j > i:
                _compare_exchange(vs, i, i ^ j)
        j //= 2
    return vs


def _top16_of_keys(groups):
    vs = _sort_desc(groups)
    for shift in (4, 2, 1):
        vs = _merge_top(vs, [pltpu.roll(v, shift, 0) for v in vs])
    return vs


def _peer_topk_kernel(s_ref, a_ref, n_ref, b_ref, rank_ref):
    sub = SUBLANES
    groups = PEER_NKEYS // sub

    def head_body(h, carry):
        g1 = [s_ref[2 * h, g * sub:(g + 1) * sub, :] for g in range(groups)]
        g2 = [s_ref[2 * h + 1, g * sub:(g + 1) * sub, :] for g in range(groups)]
        v1 = _top16_of_keys(g1)
        v2 = _top16_of_keys(g2)

        ninf = jnp.full_like(v1[0], NEG_INF)
        pad = lambda vs: vs + [ninf] * (PEER_TOPK - len(vs))
        row = lambda r1: [v1[r1] + v2[r2] for r2 in range(PEER_TOPK // (r1 + 1))]
        top = _merge_top(
            _merge_top(row(0), _merge_top(pad(row(1)), pad(row(2)))),
            _merge_top(_sort_desc(pad(row(3) + row(4) + row(5) + row(6) + row(7))),
                       pad([row(r1)[0] for r1 in range(8, PEER_TOPK)])))
        thr = top[PEER_TOPK - 1]
        z = jnp.ones_like(thr)
        for r in range(1, PEER_TOPK):
            z = z + jnp.exp(top[r] - top[0])
        half_inv_z = 0.5 / z

        split = 5
        hit = lambda s1, r2s: sum(jnp.where(s1 + v2[r2] >= thr, 1.0, 0.0) for r2 in r2s)
        n_top = [hit(v1[r1], range(split, PEER_TOPK // (r1 + 1))) for r1 in range(2)]

        b_all, rank_all = [], []
        for g in range(groups):
            rows = slice(g * sub, (g + 1) * sub)
            n = hit(g1[g], range(split)) + jnp.where(g1[g] == v1[0], n_top[0],
                                                     jnp.where(g1[g] == v1[1], n_top[1], 0.0))
            a_ref[h, rows, :] = jnp.exp(g1[g] - v1[0]) * half_inv_z
            n_ref[h, rows, :] = n
            b_all.append(jnp.exp(g2[g] - v2[0]))
            rank_all.append(sum(jnp.where(v2[r] > g2[g], 1.0, 0.0) for r in range(PEER_TOPK)))
        for pair in range(groups // 2):
            b_ref[h, pair] = jnp.concatenate(b_all[2 * pair:2 * pair + 2], axis=0).astype(BF16)
            rank_ref[h, pair] = jnp.concatenate(rank_all[2 * pair:2 * pair + 2], axis=0).astype(BF16)
        return carry

    lax.fori_loop(0, PEER_HEADS, head_body, 0)


def _peer_topk(scores):
    _, nk, s = scores.shape
    tt = min(s, PEER_TOPK_TOK)
    big = pl.BlockSpec((PEER_HEADS, nk, tt), lambda i: (0, 0, i))
    tiled = pl.BlockSpec((PEER_HEADS, nk // BF16_ROWS, BF16_ROWS, tt), lambda i: (0, 0, 0, i))
    shape = (PEER_HEADS, nk, s)
    shape16 = (PEER_HEADS, nk // BF16_ROWS, BF16_ROWS, s)
    return pl.pallas_call(
        _peer_topk_kernel,
        grid=(s // tt,),
        in_specs=[pl.BlockSpec((2 * PEER_HEADS, nk, tt), lambda i: (0, 0, i))],
        out_specs=[big, big, tiled, tiled],
        out_shape=[jax.ShapeDtypeStruct(shape, F32), jax.ShapeDtypeStruct(shape, F32),
                   jax.ShapeDtypeStruct(shape16, BF16), jax.ShapeDtypeStruct(shape16, BF16)],
        compiler_params=_params("parallel"),
        name="peer_topk_threshold",
    )(scores)


def _peer_dense_kernel(h_ref, a_ref, n_ref, b_ref, rank_ref, u_ref, vt_ref, x_ref, g_ref, o_ref,
                       acc_ref, p_ref):
    j = pl.program_id(1)
    nk = PEER_NKEYS

    @pl.when(j == 0)
    def _():
        acc_ref[...] = jnp.zeros_like(acc_ref)

    slab = PEER_EXP // 2
    act_t = jnp.concatenate([_dot_nt(u_ref[e0:e0 + slab, :], h_ref[...]) for e0 in range(0, PEER_EXP, slab)], axis=0)
    for blk in range(PEER_EXP // nk):
        i1 = j * (PEER_EXP // nk) + blk
        wgt = jnp.zeros((nk // BF16_ROWS, BF16_ROWS, PEER_TOK), BF16)
        for h in range(PEER_HEADS):
            row = lambda ref: jnp.broadcast_to(ref[h, pl.ds(i1, 1), :], (BF16_ROWS, PEER_TOK)).astype(BF16)[None]
            chosen = rank_ref[h] < row(n_ref)
            wgt = wgt + jnp.where(chosen, b_ref[h], 0.0) * row(a_ref)
        act = act_t[blk * nk:(blk + 1) * nk].astype(BF16)
        gelu2 = act * (1.0 + lax.erf(act * (2.0 ** -0.5)))
        p_ref[blk * nk:(blk + 1) * nk, :] = wgt.reshape(nk, PEER_TOK) * gelu2
    acc_ref[...] += _dot(vt_ref[...], p_ref[...])

    @pl.when(j == pl.num_programs(1) - 1)
    def _():
        o_ref[...] = x_ref[...] + g_ref[...] * jnp.transpose(acc_ref[...])


def _peer_dense(h2, a, n, b, rank, u16, vt16, layer, x, gate):
    s, d = x.shape
    n_exp = u16.shape[1]
    tt = min(s, PEER_TOK)
    assert tt == PEER_TOK
    tok3 = pl.BlockSpec((PEER_HEADS, PEER_NKEYS, tt), lambda i, j: (0, 0, i))
    tok4 = pl.BlockSpec((PEER_HEADS, PEER_NKEYS // BF16_ROWS, BF16_ROWS, tt), lambda i, j: (0, 0, 0, i))
    return pl.pallas_call(
        _peer_dense_kernel,
        grid=(s // tt, n_exp // PEER_EXP),
        in_specs=[pl.BlockSpec((tt, d), lambda i, j: (i, 0)),
                  tok3, tok3, tok4, tok4,
                  pl.BlockSpec((None, PEER_EXP, d), lambda i, j: (layer, j, 0)),
                  pl.BlockSpec((None, d, PEER_EXP), lambda i, j: (layer, 0, j)),
                  pl.BlockSpec((tt, d), lambda i, j: (i, 0)),
                  pl.BlockSpec((1, d), lambda i, j: (0, 0))],
        out_specs=pl.BlockSpec((tt, d), lambda i, j: (i, 0)),
        out_shape=jax.ShapeDtypeStruct((s, d), F32),
        scratch_shapes=[pltpu.VMEM((d, tt), F32), pltpu.VMEM((PEER_EXP, tt), BF16)],
        compiler_params=_params("parallel", "arbitrary"),
        name="peer_dense_experts",
    )(h2, a, n, b, rank, u16, vt16, x, gate)


def _final_norm_kernel(x_ref, w_ref, o_ref):
    x = x_ref[...]
    o_ref[...] = x * lax.rsqrt(jnp.mean(x * x, axis=-1, keepdims=True) + EPS) * w_ref[...]


def _final_norm(x, w):
    s, d = x.shape
    tm = min(s, 1024)
    return pl.pallas_call(
        _final_norm_kernel,
        grid=(s // tm,),
        in_specs=[pl.BlockSpec((tm, d), lambda i: (i, 0)), _full((1, d))],
        out_specs=pl.BlockSpec((tm, d), lambda i: (i, 0)),
        out_shape=jax.ShapeDtypeStruct((s, d), F32),
        compiler_params=_params("parallel"),
        name="final_rmsnorm",
    )(x, w)


def _lane_row(values, offset):
    return jnp.zeros((1, LANES), F32).at[0, offset:offset + values.shape[0]].set(values.astype(F32))


def _hybrid_mixer(x, nw, sc, sh, cos, sin, w_in, conv_w, a_log, dt_bias, o_norm, w_out):
    w16 = w_in.astype(BF16)
    c0 = GDN_CONV_CH
    c1 = c0 + GDN_V
    c2 = c1 + 2 * GDN_HEADS
    wba = jnp.pad(w16[:, c1:c2], ((0, 0), (0, LANES - 2 * GDN_HEADS)))
    qkva, z, ba, qb, kb, vb = _hy_proj(x, nw, sc, sh, w16[:, :c0], w16[:, c0:c1], wba, w16[:, c2:], cos, sin)
    o_a = _gdn(qkva, ba, z, conv_w, _lane_row(a_log, GDN_HEADS), _lane_row(dt_bias, GDN_HEADS),
               o_norm.reshape(1, -1))
    o_b = _moba(qb, kb, vb)
    wo16 = w_out.astype(BF16)
    return [o_a, o_b], [wo16[:GDN_V], wo16[GDN_V:]]


def _mla_mixer(x, nw, sc, sh, cos, sin, w_in, q_norm, kv_norm, w_uq, w_ukv, w_out):
    w16 = w_in.astype(BF16)
    r0 = MLA_Q_RANK
    r1 = r0 + MLA_KV_RANK
    wkr = jnp.pad(w16[:, r1:], ((0, 0), (0, LANES - MLA_ROPE)))
    wuq = w_uq.astype(BF16).reshape(MLA_Q_RANK, MLA_HEADS, MLA_QK)
    wqn = wuq[:, :, :MLA_NOPE].reshape(MLA_Q_RANK, MLA_HEADS * MLA_NOPE)
    wqr = wuq[:, :, MLA_NOPE:].reshape(MLA_Q_RANK, MLA_HEADS * MLA_ROPE)
    wukv = w_ukv.astype(BF16).reshape(MLA_KV_RANK, MLA_HEADS, MLA_NOPE + MLA_V)
    wkn = wukv[:, :, :MLA_NOPE].reshape(MLA_KV_RANK, MLA_HEADS * MLA_NOPE)
    wv = wukv[:, :, MLA_NOPE:].reshape(MLA_KV_RANK, MLA_HEADS * MLA_V)
    q, k, v = _mla_proj(x, nw, sc, sh, w16[:, :r0], w16[:, r0:r1], wkr, q_norm.reshape(1, -1),
                        kv_norm.reshape(1, -1), wqn, wqr, wkn, wv, cos, sin)
    o = _mla_attention(q, k, v)
    return [o], [w_out.astype(BF16)]


def kernel(x, c, positions, mod_w, mod_b, norm_mix, norm_ffn, hy_w_in, gdn_conv, gdn_a_log, gdn_dt_bias, gdn_o_norm, hy_w_out, mla_w_in, mla_q_norm, mla_kv_norm, mla_w_uq, mla_w_ukv, mla_w_out, peer_w_q, peer_sub_keys, peer_u, peer_v, final_norm):
    bsz, s, d = x.shape
    assert bsz == 1 and d == D_MODEL
    depth = mod_w.shape[0]
    xs = x.reshape(s, d)
    mod = _modulation(c.reshape(d, 1), mod_w, mod_b)
    cos_b, sin_b, cos_c, sin_c = _rope_tables(positions.reshape(s, 1))
    u16 = peer_u.astype(BF16)
    vt16 = jnp.swapaxes(peer_v.astype(BF16), 1, 2)
    wq16 = peer_w_q.astype(BF16)
    keys16 = peer_sub_keys.astype(BF16).reshape(depth, 2 * PEER_HEADS, PEER_NKEYS, PEER_DSUB)
    for layer in range(depth):
        sh1, sc1, g1, sh2, sc2, g2 = (mod[layer, :, n * d:(n + 1) * d] for n in range(6))
        nw = norm_mix[layer].reshape(1, d)
        i = layer // 2
        if layer % 2 == 0:
            outs, w_outs = _hybrid_mixer(xs, nw, sc1, sh1, cos_b, sin_b, hy_w_in[i], gdn_conv[i], gdn_a_log[i],
                                         gdn_dt_bias[i], gdn_o_norm[i], hy_w_out[i])
        else:
            outs, w_outs = _mla_mixer(xs, nw, sc1, sh1, cos_c, sin_c, mla_w_in[i], mla_q_norm[i], mla_kv_norm[i],
                                      mla_w_uq[i], mla_w_ukv[i], mla_w_out[i])
        xs, h2, scores = _outproj_peer_proj(outs, w_outs, xs, g1, norm_ffn[layer].reshape(1, d), sc2, sh2,
                                            wq16[layer], keys16[layer])
        a, n, b, rank = _peer_topk(scores)
        xs = _peer_dense(h2, a, n, b, rank, u16, vt16, layer, xs, g2)
    return _final_norm(xs, final_norm.reshape(1, d)).reshape(bsz, s, d)
```

```python
import functools
import math

import jax
import jax.numpy as jnp
from jax import lax
from jax.experimental import pallas as pl
from jax.experimental.pallas import tpu as pltpu

F32 = jnp.float32
BF16 = jnp.bfloat16
NEG_INF = float("-inf")
HIGHEST = lax.Precision.HIGHEST

D_MODEL = 1024
MOD_COLS = 1536
EPS = 1e-6
ROPE_THETA = 10000.0
LOG2_E = math.log2(math.e)

GDN_HEADS = 4
GDN_DK = 128
GDN_CHUNK = 64
GDN_QK = GDN_HEADS * GDN_DK
GDN_DV = 128
GDN_V = GDN_HEADS * GDN_DV
GDN_CONV = 4
GDN_CONV_CH = 2 * GDN_QK + GDN_V
GDN_ROWS = 256

MOBA_HEADS = 4
MOBA_DH = 128
MOBA_BLOCK = 256
MOBA_TOPK = 3
MOBA_W = MOBA_HEADS * MOBA_DH

MLA_HEADS = 8
MLA_Q_RANK = 512
MLA_KV_RANK = 256
MLA_NOPE = 128
MLA_ROPE = 64
MLA_V = 128
MLA_QK = MLA_NOPE + MLA_ROPE
KEY_BLOCK = 256
MLA_Q_BLOCK = 512

PEER_HEADS = 8
PEER_NKEYS = 128
PEER_TOPK = 16
PEER_DSUB = 128
PEER_TOK = 512
PEER_EXP = 2048
PEER_TOPK_TOK = 256

LANES = 128
SUBLANES = 8
BF16_ROWS = 16
SUM_ROWS = BF16_ROWS
VMEM_LIMIT = 56 * 1024 * 1024


def _params(*sem):
    return pltpu.CompilerParams(dimension_semantics=sem, vmem_limit_bytes=VMEM_LIMIT)


def _dot(a, b, precision=None):
    return jnp.dot(a, b, preferred_element_type=F32, precision=precision)


def _dot_nt(a, b):
    return lax.dot_general(a, b, (((1,), (1,)), ((), ())), preferred_element_type=F32)


def _sigmoid(x):
    return jax.nn.sigmoid(x)


def _norm_mod(x, nw, sc, sh):
    y = x * lax.rsqrt(jnp.mean(x * x, axis=-1, keepdims=True) + EPS) * nw
    return y * (1.0 + sc) + sh


def _full(shape):
    n = len(shape)
    return pl.BlockSpec(shape, lambda *_: (0,) * n)


def _mod_kernel(c_ref, w_ref, b_ref, o_ref):
    c = c_ref[...]
    cond = c * _sigmoid(c)
    o_ref[0] = jnp.sum(cond * w_ref[0], axis=0, keepdims=True) + b_ref[0]


def _modulation(c_col, mod_w, mod_b):
    depth, d, n = mod_w.shape
    tn = MOD_COLS
    return pl.pallas_call(
        _mod_kernel,
        grid=(depth, n // tn),
        in_specs=[pl.BlockSpec((d, 1), lambda l, j: (0, 0)),
                  pl.BlockSpec((1, d, tn), lambda l, j: (l, 0, j)),
                  pl.BlockSpec((1, 1, tn), lambda l, j: (l, 0, j))],
        out_specs=pl.BlockSpec((1, 1, tn), lambda l, j: (l, 0, j)),
        out_shape=jax.ShapeDtypeStruct((depth, 1, n), F32),
        compiler_params=_params("parallel", "parallel"),
        name="adaln_modulation",
    )(c_col, mod_w, mod_b.reshape(depth, 1, n))


def _rope_kernel(pos_ref, fb_ref, fc_ref, cb_ref, sb_ref, cc_ref, sc_ref):
    p = pos_ref[...].astype(F32)
    lane = lax.broadcasted_iota(jnp.int32, cb_ref.shape, 1)
    ab = p * fb_ref[...]
    sb = jnp.sin(ab)
    cb_ref[...] = jnp.cos(ab)
    sb_ref[...] = jnp.where(lane < MOBA_DH // 2, -sb, sb)
    ac = p * fc_ref[...]
    sc = jnp.sin(ac)
    cc_ref[...] = jnp.cos(ac)
    sc_ref[...] = jnp.where(lane % MLA_ROPE < MLA_ROPE // 2, -sc, sc)


def _rope_tables(pos_col):
    s = pos_col.shape[0]
    tm = min(s, 1024)
    fb = ROPE_THETA ** (-jnp.arange(0, MOBA_DH, 2, dtype=F32) / MOBA_DH)
    fc = ROPE_THETA ** (-jnp.arange(0, MLA_ROPE, 2, dtype=F32) / MLA_ROPE)
    fb = jnp.tile(fb, 2).reshape(1, LANES)
    fc = jnp.tile(fc, 4).reshape(1, LANES)
    tab = pl.BlockSpec((tm, LANES), lambda i: (i, 0))
    return pl.pallas_call(
        _rope_kernel,
        grid=(s // tm,),
        in_specs=[pl.BlockSpec((tm, 1), lambda i: (i, 0)), _full((1, LANES)), _full((1, LANES))],
        out_specs=[tab] * 4,
        out_shape=[jax.ShapeDtypeStruct((s, LANES), F32)] * 4,
        compiler_params=_params("parallel"),
        name="rope_tables",
    )(pos_col, fb, fc)


def _rope_head128(x, cos, sin):
    return x * cos + pltpu.roll(x, MOBA_DH // 2, 1) * sin


def _rope_heads64(x, cos, sin):
    lane = lax.broadcasted_iota(jnp.int32, x.shape, 1)
    first = lane % MLA_ROPE < MLA_ROPE // 2
    partner = jnp.where(first, pltpu.roll(x, LANES - MLA_ROPE // 2, 1), pltpu.roll(x, MLA_ROPE // 2, 1))
    return x * cos + partner * sin


def _hy_proj_kernel(x_ref, nw_ref, sc_ref, sh_ref, wa_ref, wz_ref, wba_ref, wb_ref, cos_ref, sin_ref,
                    qkva_ref, z_ref, ba_ref, qb_ref, kb_ref, vb_ref):
    h = _norm_mod(x_ref[...], nw_ref[...], sc_ref[...], sh_ref[...]).astype(BF16)
    qkva_ref[...] = _dot(h, wa_ref[...])
    z_ref[...] = _dot(h, wz_ref[...])
    ba_ref[...] = _dot(h, wba_ref[...])
    qkvb = _dot(h, wb_ref[...])
    cos = cos_ref[...]
    sin = sin_ref[...]
    scale = MOBA_DH ** -0.5 * LOG2_E
    for hh in range(MOBA_HEADS):
        lo = hh * MOBA_DH
        q = qkvb[:, lo:lo + MOBA_DH]
        k = qkvb[:, MOBA_W + lo:MOBA_W + lo + MOBA_DH]
        qb_ref[hh, 0] = jnp.transpose(_rope_head128(q, cos, sin) * scale).astype(BF16)
        kb_ref[:, lo:lo + MOBA_DH] = _rope_head128(k, cos, sin).astype(BF16)
        v = qkvb[:, 2 * MOBA_W + lo:2 * MOBA_W + lo + MOBA_DH]
        vb_ref[hh, 0] = _with_sum_rows(jnp.transpose(v))


def _hy_proj(x, nw, sc, sh, wa, wz, wba, wb, cos, sin):
    s, d = x.shape
    tm = KEY_BLOCK
    row = lambda n: pl.BlockSpec((tm, n), lambda i: (i, 0))
    vec = _full((1, d))
    return pl.pallas_call(
        _hy_proj_kernel,
        grid=(s // tm,),
        in_specs=[row(d), vec, vec, vec, _full(wa.shape), _full(wz.shape), _full(wba.shape),
                  _full(wb.shape), row(LANES), row(LANES)],
        out_specs=[row(GDN_CONV_CH), row(GDN_V), row(LANES),
                   pl.BlockSpec((MOBA_HEADS, 1, MOBA_DH, tm), lambda i: (0, i, 0, 0)), row(MOBA_W),
                   pl.BlockSpec((MOBA_HEADS, 1, MOBA_DH + SUM_ROWS, tm), lambda i: (0, i, 0, 0))],
        out_shape=[jax.ShapeDtypeStruct((s, GDN_CONV_CH), F32),
                   jax.ShapeDtypeStruct((s, GDN_V), F32),
                   jax.ShapeDtypeStruct((s, LANES), F32),
                   jax.ShapeDtypeStruct((MOBA_HEADS, s // tm, MOBA_DH, tm), BF16),
                   jax.ShapeDtypeStruct((s, MOBA_W), BF16),
                   jax.ShapeDtypeStruct((MOBA_HEADS, s // tm, MOBA_DH + SUM_ROWS, tm), BF16)],
        compiler_params=_params("parallel"),
        name="hybrid_in_proj",
    )(x, nw, sc, sh, wa, wz, wba, wb, cos, sin)


def _unit_lower_inverses(l_stricts, eye):
    powers = [-l for l in l_stricts]
    invs = [eye + p for p in powers]
    p = 2
    while p < GDN_CHUNK:
        powers = [_dot(x.astype(BF16), x.astype(BF16)) for x in powers]
        invs = [_dot(inv.astype(BF16), (eye + x).astype(BF16)) for inv, x in zip(invs, powers)]
        p *= 2
    return invs


def _gdn_kernel(qkv_ref, ba_ref, z_ref, conv_ref, alog_ref, dtb_ref, onorm_ref, o_ref, ext_ref, state_ref):
    tb = GDN_ROWS
    cs = GDN_CHUNK
    pad = SUBLANES

    @pl.when(pl.program_id(0) == 0)
    def _():
        ext_ref[0:pad, :] = jnp.zeros((pad, GDN_CONV_CH), F32)
        state_ref[...] = jnp.zeros_like(state_ref)

    ext_ref[pad:pad + tb, :] = qkv_ref[...]
    cw = conv_ref[...]
    first = pad - (GDN_CONV - 1)
    y = ext_ref[first:first + tb, :] * cw[0:1]
    for j in range(1, GDN_CONV):
        y = y + ext_ref[first + j:first + j + tb, :] * cw[j:j + 1]
    ext_ref[0:pad, :] = ext_ref[tb:tb + pad, :]
    y = y * _sigmoid(y)

    ba = ba_ref[...]
    beta_all = _sigmoid(ba)
    t = ba + dtb_ref[...]
    softplus = jnp.maximum(t, 0.0) + jnp.log1p(jnp.exp(-jnp.abs(t)))
    g_all = -jnp.exp(alog_ref[...]) * softplus

    row = lax.broadcasted_iota(jnp.int32, (tb, tb), 0)
    col = lax.broadcasted_iota(jnp.int32, (tb, tb), 1)
    same = (row // cs) == (col // cs)
    causal = same & (col <= row)
    strict = same & (col < row)
    eye = (row == col).astype(F32)
    g_cum = _dot(causal.astype(F32), g_all, HIGHEST)
    g_tot = _dot(same.astype(F32), g_all, HIGHEST)
    g_cum_t = jnp.transpose(g_cum)
    z = z_ref[...]
    onorm = onorm_ref[...]

    heads = range(GDN_HEADS)
    rhs, l_strict, gtot, qk, qdec, kdec = [], [], [], [], [], []
    for h in heads:
        lo = h * GDN_DK
        q = y[:, lo:lo + GDN_DK]
        k = y[:, GDN_QK + lo:GDN_QK + lo + GDN_DK]
        v = y[:, 2 * GDN_QK + lo:2 * GDN_QK + lo + GDN_DK]
        q = q * lax.rsqrt(jnp.sum(q * q, axis=-1, keepdims=True) + EPS) * (GDN_DK ** -0.5)
        k = k * lax.rsqrt(jnp.sum(k * k, axis=-1, keepdims=True) + EPS)
        beta = beta_all[:, h:h + 1]
        gcol = g_cum[:, GDN_HEADS + h:GDN_HEADS + h + 1]
        grow = g_cum_t[GDN_HEADS + h:GDN_HEADS + h + 1, :]
        gtot.append(g_tot[:, GDN_HEADS + h:GDN_HEADS + h + 1])
        decay = jnp.exp(jnp.where(causal, gcol - grow, NEG_INF))
        kb = k * beta
        k16 = k.astype(BF16)
        l_strict.append(jnp.where(strict, _dot_nt(kb.astype(BF16), k16) * decay, 0.0))
        eg = jnp.exp(gcol)
        rhs.append(jnp.concatenate([v * beta, kb * eg], axis=1).astype(BF16))
        qk.append((_dot_nt(q.astype(BF16), k16) * decay).astype(BF16))
        qdec.append((q * eg).astype(BF16))
        kdec.append(k * jnp.exp(gtot[h] - gcol))

    uw = [_dot(t_inv.astype(BF16), rhs[h]) for h, t_inv in enumerate(_unit_lower_inverses(l_strict, eye))]
    u = [x[:, :GDN_DK] for x in uw]
    w = [x[:, GDN_DK:].astype(BF16) for x in uw]

    state = [state_ref[h] for h in heads]
    v_new = [[] for _ in heads]
    o_state = [[] for _ in heads]
    for c in range(tb // cs):
        r0 = c * cs
        for h in heads:
            s16 = state[h].astype(BF16)
            vn = u[h][r0:r0 + cs] - _dot(w[h][r0:r0 + cs], s16)
            o_state[h].append(_dot(qdec[h][r0:r0 + cs], s16))
            kd_t = jnp.transpose(kdec[h][r0:r0 + cs]).astype(BF16)
            state[h] = state[h] * jnp.exp(gtot[h][r0:r0 + 1]) + _dot(kd_t, vn.astype(BF16))
            v_new[h].append(vn)

    for h in heads:
        lo = h * GDN_DK
        state_ref[h] = state[h]
        o = jnp.concatenate(o_state[h], axis=0) + _dot(qk[h], jnp.concatenate(v_new[h], axis=0).astype(BF16))
        o = o * lax.rsqrt(jnp.mean(o * o, axis=-1, keepdims=True) + EPS) * onorm
        zh = z[:, lo:lo + GDN_DK]
        o_ref[:, lo:lo + GDN_DK] = (o * (zh * _sigmoid(zh))).astype(BF16)


def _gdn(qkva, ba, z, conv_w, alog, dtb, onorm):
    s = qkva.shape[0]
    tb = GDN_ROWS
    row = lambda n: pl.BlockSpec((tb, n), lambda i: (i, 0))
    return pl.pallas_call(
        _gdn_kernel,
        grid=(s // tb,),
        in_specs=[row(GDN_CONV_CH), row(LANES), row(GDN_V), _full(conv_w.shape),
                  _full((1, LANES)), _full((1, LANES)), _full((1, LANES))],
        out_specs=row(GDN_V),
        out_shape=jax.ShapeDtypeStruct((s, GDN_V), BF16),
        scratch_shapes=[pltpu.VMEM((tb + SUBLANES, GDN_CONV_CH), F32),
                        pltpu.VMEM((GDN_HEADS, GDN_DK, GDN_DV), F32)],
        compiler_params=_params("arbitrary"),
        name="gated_delta_rule",
    )(qkva, ba, z, conv_w, alog, dtb, onorm)


def _scores_into(s_ref, mx_ref, slot, k_tile, q_t, mask):
    half = k_tile.shape[0] // 2
    s = jnp.concatenate([_dot(k_tile[:half], q_t), _dot(k_tile[half:], q_t)], axis=0)
    rows = s.shape[0]
    top = None
    for lo, piece in mask(s):
        s_ref[slot, lo:lo + piece.shape[0], :] = piece
        pmax = piece.max(axis=0, keepdims=True)
        top = pmax if top is None else jnp.maximum(top, pmax)
    if rows < s_ref.shape[1]:
        s_ref[slot, rows:, :] = jnp.full((s_ref.shape[1] - rows, s.shape[1]), NEG_INF, F32)
    mx_ref[slot] = top


def _values(p_ref, slot, v_at, blocks):
    v_t = jnp.concatenate([v_at(j) for j in blocks], axis=1)
    return _dot(v_t, p_ref[slot])


def _flash_steps(n_steps, first_blocks, first_keys, step_blocks, k_at, v_at, q_t, mask_first, mask_step,
                 s_ref, mx_ref, p_ref, acc_ref):
    per_step = len(first_blocks)
    _scores_into(s_ref, mx_ref, 0, k_at(first_blocks[0], first_keys), q_t, mask_first)
    m = mx_ref[0]
    p_ref[0] = jnp.exp2(s_ref[0] - m).astype(BF16)
    acc_ref[...] = jnp.zeros_like(acc_ref)
    _scores_into(s_ref, mx_ref, 1, k_at(step_blocks(1)[0], per_step), q_t, lambda s: mask_step(s, 1))

    def step(t, carry, cur):
        m, alpha_prev = carry
        nxt = jnp.minimum(t + 1, jnp.maximum(n_steps, 1))
        _scores_into(s_ref, mx_ref, 1 - cur, k_at(step_blocks(nxt)[0], per_step), q_t, lambda s: mask_step(s, nxt))
        m_new = jnp.maximum(m, mx_ref[cur])
        alpha = jnp.exp2(m - m_new)
        p = jnp.exp2(s_ref[cur] - m_new)
        prev = [jnp.where(t == 1, a, b) for a, b in zip(first_blocks, step_blocks(jnp.maximum(t - 1, 1)))]
        acc_ref[...] = alpha_prev * acc_ref[...] + _values(p_ref, 1 - cur, v_at, prev)
        p_ref[cur] = p.astype(BF16)
        return m_new, alpha

    def body(t, carry):
        return lax.cond(t % 2 == 0, lambda c: step(t, c, 0), lambda c: step(t, c, 1), carry)

    m, alpha = lax.fori_loop(1, n_steps + 1, body, (m, jnp.ones_like(m)))
    last = [jnp.where(n_steps == 0, a, b) for a, b in zip(first_blocks, step_blocks(jnp.maximum(n_steps, 1)))]
    acc = alpha * acc_ref[...] + _values(p_ref, n_steps % 2, v_at, last)
    dh = acc.shape[0] - SUM_ROWS
    return acc[:dh] / acc[dh:dh + 1]


def _with_sum_rows(v_t):
    extra = lax.broadcasted_iota(jnp.int32, (SUM_ROWS, v_t.shape[1]), 0) == 0
    return jnp.concatenate([v_t, extra.astype(F32)], axis=0).astype(BF16)


def _moba_kernel(q_ref, k_ref, v_ref, o_ref, kmean_ref, sel_ref, s_ref, mx_ref, p_ref, acc_ref, *, n_blocks):
    i = pl.program_id(1)
    bs = MOBA_BLOCK

    @pl.when(i == 0)
    def _():
        kmean_ref[...] = jnp.zeros_like(kmean_ref)

        def mean_body(n, carry):
            blk = k_ref[pl.ds(pl.multiple_of(n * bs, bs), bs), :].astype(F32)
            kmean_ref[pl.ds(n, 1), :] = jnp.mean(blk, axis=0, keepdims=True)
            return carry

        lax.fori_loop(0, n_blocks, mean_body, 0)

    q_t = jnp.concatenate([q_ref[0, 0], q_ref[0, 1]], axis=1)
    gate = _dot(kmean_ref[...].astype(BF16), q_t)
    blk_id = lax.broadcasted_iota(jnp.int32, gate.shape, 0)
    second = lax.broadcasted_iota(jnp.int32, gate.shape, 1) >= bs
    gate = jnp.where(blk_id < 2 * i + second.astype(jnp.int32), gate, NEG_INF)
    sel = jnp.zeros(gate.shape, F32)
    for _ in range(MOBA_TOPK):
        gmax = jnp.max(gate, axis=0, keepdims=True)
        first = jnp.min(jnp.where(gate == gmax, blk_id, LANES), axis=0, keepdims=True)
        hit = blk_id == first
        sel = jnp.where(hit & (gmax > NEG_INF), 1.0, sel)
        gate = jnp.where(hit, NEG_INF, gate)
    sel_ref[...] = sel

    def own_mask(s):
        key = lax.broadcasted_iota(jnp.int32, (bs, 2 * bs), 0)
        qry = lax.broadcasted_iota(jnp.int32, (bs, 2 * bs), 1)
        picked = sel_ref[pl.ds(2 * i, 1), :] > 0.0
        first_ok = (key <= qry) & ((qry < bs) | picked)
        return [(0, jnp.where(first_ok, s[:bs], NEG_INF)), (bs, jnp.where(key + bs <= qry, s[bs:], NEG_INF))]

    def sel_mask(s, t):
        return [(sub * bs, jnp.where(sel_ref[pl.ds(2 * (t - 1) + sub, 1), :] > 0.0, s[sub * bs:(sub + 1) * bs], NEG_INF))
                for sub in range(2)]

    out = _flash_steps(
        n_steps=i, first_blocks=[2 * i, 2 * i + 1], first_keys=2,
        step_blocks=lambda t: [2 * (t - 1), 2 * (t - 1) + 1],
        k_at=lambda j, n: k_ref[pl.ds(pl.multiple_of(j * bs, bs), n * bs), :], v_at=lambda j: v_ref[0, j],
        q_t=q_t, mask_first=own_mask, mask_step=sel_mask,
        s_ref=s_ref, mx_ref=mx_ref, p_ref=p_ref, acc_ref=acc_ref)
    o_ref[...] = jnp.transpose(out).astype(o_ref.dtype)


def _moba(q_t, k, v_t):
    s = k.shape[0]
    bs = MOBA_BLOCK
    n_blocks = s // bs
    assert n_blocks <= LANES and bs == KEY_BLOCK and n_blocks % 2 == 0
    bq = 2 * bs
    kv = pl.BlockSpec((s, MOBA_DH), lambda h, i: (0, h))
    return pl.pallas_call(
        functools.partial(_moba_kernel, n_blocks=n_blocks),
        grid=(MOBA_HEADS, n_blocks // 2),
        in_specs=[pl.BlockSpec((1, 2, MOBA_DH, bs), lambda h, i: (h, i, 0, 0)), kv,
                  pl.BlockSpec((1, n_blocks, MOBA_DH + SUM_ROWS, bs), lambda h, i: (h, 0, 0, 0))],
        out_specs=pl.BlockSpec((bq, MOBA_DH), lambda h, i: (i, h)),
        out_shape=jax.ShapeDtypeStruct((s, MOBA_W), BF16),
        scratch_shapes=[pltpu.VMEM((LANES, MOBA_DH), F32),
                        pltpu.VMEM((LANES, bq), F32),
                        pltpu.VMEM((2, 2 * KEY_BLOCK, bq), F32),
                        pltpu.VMEM((2, 1, bq), F32),
                        pltpu.VMEM((2, 2 * KEY_BLOCK, bq), BF16),
                        pltpu.VMEM((MOBA_DH + SUM_ROWS, bq), F32)],
        compiler_params=_params("parallel", "arbitrary"),
        name="moba_attention",
    )(q_t, k, v_t)


def _outproj_peer_kernel(*refs, n_in):
    o_refs = refs[:n_in]
    w_refs = refs[n_in:2 * n_in]
    x_ref, g_ref, nw_ref, sc_ref, sh_ref, wq_ref, keys_ref, xo_ref, h_ref, s_ref = refs[2 * n_in:]
    y = _dot(o_refs[0][...], w_refs[0][...])
    for o_ref, w_ref in zip(o_refs[1:], w_refs[1:]):
        y = y + _dot(o_ref[...], w_ref[...])
    x = x_ref[...] + g_ref[...] * y
    xo_ref[...] = x
    h = _norm_mod(x, nw_ref[...], sc_ref[...], sh_ref[...]).astype(BF16)
    h_ref[...] = h
    q = _dot(h, wq_ref[...]).astype(BF16)
    for hp in range(2 * PEER_HEADS):
        s_ref[hp] = _dot_nt(keys_ref[hp], q[:, hp * PEER_DSUB:(hp + 1) * PEER_DSUB])


def _outproj_peer_proj(os_, ws, x, gate, nw, sc, sh, wq, keys):
    s, d = x.shape
    tm = min(s, 256)
    row = lambda n: pl.BlockSpec((tm, n), lambda i: (i, 0))
    vec = _full((1, d))
    return pl.pallas_call(
        functools.partial(_outproj_peer_kernel, n_in=len(os_)),
        grid=(s // tm,),
        in_specs=[row(o.shape[1]) for o in os_] + [_full(w.shape) for w in ws]
                 + [row(d), vec, vec, vec, vec, _full(wq.shape), _full(keys.shape)],
        out_specs=[row(d), row(d), pl.BlockSpec((2 * PEER_HEADS, PEER_NKEYS, tm), lambda i: (0, 0, i))],
        out_shape=[jax.ShapeDtypeStruct((s, d), F32), jax.ShapeDtypeStruct((s, d), BF16),
                   jax.ShapeDtypeStruct((2 * PEER_HEADS, PEER_NKEYS, s), F32)],
        compiler_params=_params("parallel"),
        name="out_proj_peer_scores",
    )(*os_, *ws, x, gate, nw, sc, sh, wq, keys)


def _mla_proj_kernel(x_ref, nw_ref, sc_ref, sh_ref, wcq_ref, wckv_ref, wkr_ref, qn_ref, kvn_ref,
                     wqn_ref, wqr_ref, wkn_ref, wv_ref, cos_ref, sin_ref, q_ref, k_ref, v_ref):
    h = _norm_mod(x_ref[...], nw_ref[...], sc_ref[...], sh_ref[...]).astype(BF16)
    cq = _dot(h, wcq_ref[...])
    ckv = _dot(h, wckv_ref[...])
    kr = _dot(h, wkr_ref[...])
    cq = (cq * lax.rsqrt(jnp.mean(cq * cq, axis=-1, keepdims=True) + EPS) * qn_ref[...]).astype(BF16)
    ckv = (ckv * lax.rsqrt(jnp.mean(ckv * ckv, axis=-1, keepdims=True) + EPS) * kvn_ref[...]).astype(BF16)
    cos = cos_ref[...]
    sin = sin_ref[...]
    scale = MLA_QK ** -0.5 * LOG2_E
    q_nope = _dot(cq, wqn_ref[...]) * scale
    q_rope = _dot(cq, wqr_ref[...])
    k_nope = _dot(ckv, wkn_ref[...])
    val = _dot(ckv, wv_ref[...])
    k_rope = _rope_heads64(kr, cos, sin)[:, :MLA_ROPE].astype(BF16)
    for pair in range(MLA_HEADS // 2):
        slab_t = jnp.transpose(_rope_heads64(q_rope[:, pair * LANES:(pair + 1) * LANES], cos, sin) * scale)
        for sub in range(2):
            hh = 2 * pair + sub
            q_ref[hh, 0, MLA_NOPE:, :] = slab_t[sub * MLA_ROPE:(sub + 1) * MLA_ROPE].astype(BF16)
    for hh in range(MLA_HEADS):
        q_ref[hh, 0, :MLA_NOPE, :] = jnp.transpose(q_nope[:, hh * MLA_NOPE:(hh + 1) * MLA_NOPE]).astype(BF16)
        k_ref[hh, :, :MLA_NOPE] = k_nope[:, hh * MLA_NOPE:(hh + 1) * MLA_NOPE].astype(BF16)
        k_ref[hh, :, MLA_NOPE:] = k_rope
        v_ref[hh, 0] = _with_sum_rows(jnp.transpose(val[:, hh * MLA_V:(hh + 1) * MLA_V]))


def _mla_proj(x, nw, sc, sh, wcq, wckv, wkr, qn, kvn, wqn, wqr, wkn, wv, cos, sin):
    s, d = x.shape
    tm = KEY_BLOCK
    row = lambda n: pl.BlockSpec((tm, n), lambda i: (i, 0))
    vec = _full((1, d))
    heads = lambda n: pl.BlockSpec((MLA_HEADS, tm, n), lambda i: (0, i, 0))
    return pl.pallas_call(
        _mla_proj_kernel,
        grid=(s // tm,),
        in_specs=[row(d), vec, vec, vec, _full(wcq.shape), _full(wckv.shape), _full(wkr.shape),
                  _full(qn.shape), _full(kvn.shape), _full(wqn.shape), _full(wqr.shape),
                  _full(wkn.shape), _full(wv.shape), row(LANES), row(LANES)],
        out_specs=[pl.BlockSpec((MLA_HEADS, 1, MLA_QK, tm), lambda i: (0, i, 0, 0)), heads(MLA_QK),
                   pl.BlockSpec((MLA_HEADS, 1, MLA_V + SUM_ROWS, tm), lambda i: (0, i, 0, 0))],
        out_shape=[jax.ShapeDtypeStruct((MLA_HEADS, s // tm, MLA_QK, tm), BF16),
                   jax.ShapeDtypeStruct((MLA_HEADS, s, MLA_QK), BF16),
                   jax.ShapeDtypeStruct((MLA_HEADS, s // tm, MLA_V + SUM_ROWS, tm), BF16)],
        compiler_params=_params("parallel"),
        name="mla_in_proj",
    )(x, nw, sc, sh, wcq, wckv, wkr, qn, kvn, wqn, wqr, wkn, wv, cos, sin)


def _mla_attn_kernel(q_ref, k_ref, v_ref, o_ref, s_ref, mx_ref, p_ref, acc_ref):
    i = pl.program_id(1)

    def causal(s):
        key = lax.broadcasted_iota(jnp.int32, s.shape, 0)
        qry = lax.broadcasted_iota(jnp.int32, s.shape, 1)
        return [(0, jnp.where(key <= qry, s, NEG_INF))]

    per_q = MLA_Q_BLOCK // KEY_BLOCK
    q_t = jnp.concatenate([q_ref[0, b] for b in range(per_q)], axis=1)
    out = _flash_steps(
        n_steps=i, first_blocks=[2 * i, 2 * i + 1], first_keys=2,
        step_blocks=lambda t: [2 * (t - 1), 2 * (t - 1) + 1],
        k_at=lambda j, n: k_ref[0, pl.ds(pl.multiple_of(j * KEY_BLOCK, KEY_BLOCK), n * KEY_BLOCK), :],
        v_at=lambda j: v_ref[0, j], q_t=q_t, mask_first=causal, mask_step=lambda s, t: [(0, s)],
        s_ref=s_ref, mx_ref=mx_ref, p_ref=p_ref, acc_ref=acc_ref)
    o_ref[...] = jnp.transpose(out).astype(o_ref.dtype)


def _mla_attention(q_t, k, v_t):
    _, s, _ = k.shape
    bq = MLA_Q_BLOCK
    per_q = bq // KEY_BLOCK
    assert per_q == 2
    return pl.pallas_call(
        _mla_attn_kernel,
        grid=(MLA_HEADS, s // bq),
        in_specs=[pl.BlockSpec((1, per_q, MLA_QK, KEY_BLOCK), lambda h, i: (h, i, 0, 0)),
                  pl.BlockSpec((1, s, MLA_QK), lambda h, i: (h, 0, 0)),
                  pl.BlockSpec((1, s // KEY_BLOCK, MLA_V + SUM_ROWS, KEY_BLOCK), lambda h, i: (h, 0, 0, 0))],
        out_specs=pl.BlockSpec((bq, MLA_V), lambda h, i: (i, h)),
        out_shape=jax.ShapeDtypeStruct((s, MLA_HEADS * MLA_V), BF16),
        scratch_shapes=[pltpu.VMEM((2, 2 * KEY_BLOCK, bq), F32), pltpu.VMEM((2, 1, bq), F32),
                        pltpu.VMEM((2, 2 * KEY_BLOCK, bq), BF16), pltpu.VMEM((MLA_V + SUM_ROWS, bq), F32)],
        compiler_params=_params("parallel", "arbitrary"),
        name="mla_attention",
    )(q_t, k, v_t)


def _compare_exchange(vs, i, l):
    vs[i], vs[l] = jnp.maximum(vs[i], vs[l]), jnp.minimum(vs[i], vs[l])


def _sort_desc(vs):
    vs = list(vs)
    n = len(vs)
    k = 2
    while k <= n:
        j = k // 2
        while j >= 1:
            for i in range(n):
                l = i ^ j
                if l > i:
                    if i & k == 0:
                        _compare_exchange(vs, i, l)
                    else:
                        _compare_exchange(vs, l, i)
            j //= 2
        k *= 2
    return vs


def _merge_top(a, b):
    n = len(a)
    vs = [jnp.maximum(a[i], b[n - 1 - i]) for i in range(n)]
    j = n // 2
    while j >= 1:
        for i in range(n):
            if i ^ j > i:
                _compare_exchange(vs, i, i ^ j)
        j //= 2
    return vs


def _top16_of_keys(groups):
    vs = _sort_desc(groups)
    for shift in (4, 2, 1):
        vs = _merge_top(vs, [pltpu.roll(v, shift, 0) for v in vs])
    return vs


def _peer_topk_kernel(s_ref, a_ref, n_ref, b_ref, rank_ref):
    sub = SUBLANES
    groups = PEER_NKEYS // sub

    def head_body(h, carry):
        g1 = [s_ref[2 * h, g * sub:(g + 1) * sub, :] for g in range(groups)]
        g2 = [s_ref[2 * h + 1, g * sub:(g + 1) * sub, :] for g in range(groups)]
        v1 = _top16_of_keys(g1)
        v2 = _top16_of_keys(g2)

        ninf = jnp.full_like(v1[0], NEG_INF)
        pad = lambda vs: vs + [ninf] * (PEER_TOPK - len(vs))
        row = lambda r1: [v1[r1] + v2[r2] for r2 in range(PEER_TOPK // (r1 + 1))]
        top = _merge_top(
            _merge_top(row(0), _merge_top(pad(row(1)), pad(row(2)))),
            _merge_top(_sort_desc(pad(row(3) + row(4) + row(5) + row(6) + row(7))),
                       pad([row(r1)[0] for r1 in range(8, PEER_TOPK)])))
        thr = top[PEER_TOPK - 1]
        z = jnp.ones_like(thr)
        for r in range(1, PEER_TOPK):
            z = z + jnp.exp(top[r] - top[0])
        half_inv_z = 0.5 / z

        split = 5
        hit = lambda s1, r2s: sum(jnp.where(s1 + v2[r2] >= thr, 1.0, 0.0) for r2 in r2s)
        n_top = [hit(v1[r1], range(split, PEER_TOPK // (r1 + 1))) for r1 in range(2)]

        b_all, rank_all = [], []
        for g in range(groups):
            rows = slice(g * sub, (g + 1) * sub)
            n = hit(g1[g], range(split)) + jnp.where(g1[g] == v1[0], n_top[0],
                                                     jnp.where(g1[g] == v1[1], n_top[1], 0.0))
            a_ref[h, rows, :] = jnp.exp(g1[g] - v1[0]) * half_inv_z
            n_ref[h, rows, :] = n
            b_all.append(jnp.exp(g2[g] - v2[0]))
            rank_all.append(sum(jnp.where(v2[r] > g2[g], 1.0, 0.0) for r in range(PEER_TOPK)))
        for pair in range(groups // 2):
            b_ref[h, pair] = jnp.concatenate(b_all[2 * pair:2 * pair + 2], axis=0).astype(BF16)
            rank_ref[h, pair] = jnp.concatenate(rank_all[2 * pair:2 * pair + 2], axis=0).astype(BF16)
        return carry

    lax.fori_loop(0, PEER_HEADS, head_body, 0)


def _peer_topk(scores):
    _, nk, s = scores.shape
    tt = min(s, PEER_TOPK_TOK)
    big = pl.BlockSpec((PEER_HEADS, nk, tt), lambda i: (0, 0, i))
    tiled = pl.BlockSpec((PEER_HEADS, nk // BF16_ROWS, BF16_ROWS, tt), lambda i: (0, 0, 0, i))
    shape = (PEER_HEADS, nk, s)
    shape16 = (PEER_HEADS, nk // BF16_ROWS, BF16_ROWS, s)
    return pl.pallas_call(
        _peer_topk_kernel,
        grid=(s // tt,),
        in_specs=[pl.BlockSpec((2 * PEER_HEADS, nk, tt), lambda i: (0, 0, i))],
        out_specs=[big, big, tiled, tiled],
        out_shape=[jax.ShapeDtypeStruct(shape, F32), jax.ShapeDtypeStruct(shape, F32),
                   jax.ShapeDtypeStruct(shape16, BF16), jax.ShapeDtypeStruct(shape16, BF16)],
        compiler_params=_params("parallel"),
        name="peer_topk_threshold",
    )(scores)


def _peer_dense_kernel(h_ref, a_ref, n_ref, b_ref, rank_ref, u_ref, vt_ref, x_ref, g_ref, o_ref,
                       acc_ref, p_ref):
    j = pl.program_id(1)
    nk = PEER_NKEYS

    @pl.when(j == 0)
    def _():
        acc_ref[...] = jnp.zeros_like(acc_ref)

    slab = PEER_EXP // 2
    act_t = jnp.concatenate([_dot_nt(u_ref[e0:e0 + slab, :], h_ref[...]) for e0 in range(0, PEER_EXP, slab)], axis=0)
    for blk in range(PEER_EXP // nk):
        i1 = j * (PEER_EXP // nk) + blk
        wgt = jnp.zeros((nk // BF16_ROWS, BF16_ROWS, PEER_TOK), BF16)
        for h in range(PEER_HEADS):
            row = lambda ref: jnp.broadcast_to(ref[h, pl.ds(i1, 1), :], (BF16_ROWS, PEER_TOK)).astype(BF16)[None]
            chosen = rank_ref[h] < row(n_ref)
            wgt = wgt + jnp.where(chosen, b_ref[h], 0.0) * row(a_ref)
        act = act_t[blk * nk:(blk + 1) * nk].astype(BF16)
        gelu2 = act * (1.0 + lax.erf(act * (2.0 ** -0.5)))
        p_ref[blk * nk:(blk + 1) * nk, :] = wgt.reshape(nk, PEER_TOK) * gelu2
    acc_ref[...] += _dot(vt_ref[...], p_ref[...])

    @pl.when(j == pl.num_programs(1) - 1)
    def _():
        o_ref[...] = x_ref[...] + g_ref[...] * jnp.transpose(acc_ref[...])


def _peer_dense(h2, a, n, b, rank, u16, vt16, layer, x, gate):
    s, d = x.shape
    n_exp = u16.shape[1]
    tt = min(s, PEER_TOK)
    assert tt == PEER_TOK
    tok3 = pl.BlockSpec((PEER_HEADS, PEER_NKEYS, tt), lambda i, j: (0, 0, i))
    tok4 = pl.BlockSpec((PEER_HEADS, PEER_NKEYS // BF16_ROWS, BF16_ROWS, tt), lambda i, j: (0, 0, 0, i))
    return pl.pallas_call(
        _peer_dense_kernel,
        grid=(s // tt, n_exp // PEER_EXP),
        in_specs=[pl.BlockSpec((tt, d), lambda i, j: (i, 0)),
                  tok3, tok3, tok4, tok4,
                  pl.BlockSpec((None, PEER_EXP, d), lambda i, j: (layer, j, 0)),
                  pl.BlockSpec((None, d, PEER_EXP), lambda i, j: (layer, 0, j)),
                  pl.BlockSpec((tt, d), lambda i, j: (i, 0)),
                  pl.BlockSpec((1, d), lambda i, j: (0, 0))],
        out_specs=pl.BlockSpec((tt, d), lambda i, j: (i, 0)),
        out_shape=jax.ShapeDtypeStruct((s, d), F32),
        scratch_shapes=[pltpu.VMEM((d, tt), F32), pltpu.VMEM((PEER_EXP, tt), BF16)],
        compiler_params=_params("parallel", "arbitrary"),
        name="peer_dense_experts",
    )(h2, a, n, b, rank, u16, vt16, x, gate)


def _final_norm_kernel(x_ref, w_ref, o_ref):
    x = x_ref[...]
    o_ref[...] = x * lax.rsqrt(jnp.mean(x * x, axis=-1, keepdims=True) + EPS) * w_ref[...]


def _final_norm(x, w):
    s, d = x.shape
    tm = min(s, 1024)
    return pl.pallas_call(
        _final_norm_kernel,
        grid=(s // tm,),
        in_specs=[pl.BlockSpec((tm, d), lambda i: (i, 0)), _full((1, d))],
        out_specs=pl.BlockSpec((tm, d), lambda i: (i, 0)),
        out_shape=jax.ShapeDtypeStruct((s, d), F32),
        compiler_params=_params("parallel"),
        name="final_rmsnorm",
    )(x, w)


def _lane_row(values, offset):
    return jnp.zeros((1, LANES), F32).at[0, offset:offset + values.shape[0]].set(values.astype(F32))


def _hybrid_mixer(x, nw, sc, sh, cos, sin, w_in, conv_w, a_log, dt_bias, o_norm, w_out):
    w16 = w_in.astype(BF16)
    c0 = GDN_CONV_CH
    c1 = c0 + GDN_V
    c2 = c1 + 2 * GDN_HEADS
    wba = jnp.pad(w16[:, c1:c2], ((0, 0), (0, LANES - 2 * GDN_HEADS)))
    qkva, z, ba, qb, kb, vb = _hy_proj(x, nw, sc, sh, w16[:, :c0], w16[:, c0:c1], wba, w16[:, c2:], cos, sin)
    o_a = _gdn(qkva, ba, z, conv_w, _lane_row(a_log, GDN_HEADS), _lane_row(dt_bias, GDN_HEADS),
               o_norm.reshape(1, -1))
    o_b = _moba(qb, kb, vb)
    wo16 = w_out.astype(BF16)
    return [o_a, o_b], [wo16[:GDN_V], wo16[GDN_V:]]


def _mla_mixer(x, nw, sc, sh, cos, sin, w_in, q_norm, kv_norm, w_uq, w_ukv, w_out):
    w16 = w_in.astype(BF16)
    r0 = MLA_Q_RANK
    r1 = r0 + MLA_KV_RANK
    wkr = jnp.pad(w16[:, r1:], ((0, 0), (0, LANES - MLA_ROPE)))
    wuq = w_uq.astype(BF16).reshape(MLA_Q_RANK, MLA_HEADS, MLA_QK)
    wqn = wuq[:, :, :MLA_NOPE].reshape(MLA_Q_RANK, MLA_HEADS * MLA_NOPE)
    wqr = wuq[:, :, MLA_NOPE:].reshape(MLA_Q_RANK, MLA_HEADS * MLA_ROPE)
    wukv = w_ukv.astype(BF16).reshape(MLA_KV_RANK, MLA_HEADS, MLA_NOPE + MLA_V)
    wkn = wukv[:, :, :MLA_NOPE].reshape(MLA_KV_RANK, MLA_HEADS * MLA_NOPE)
    wv = wukv[:, :, MLA_NOPE:].reshape(MLA_KV_RANK, MLA_HEADS * MLA_V)
    q, k, v = _mla_proj(x, nw, sc, sh, w16[:, :r0], w16[:, r0:r1], wkr, q_norm.reshape(1, -1),
                        kv_norm.reshape(1, -1), wqn, wqr, wkn, wv, cos, sin)
    o = _mla_attention(q, k, v)
    return [o], [w_out.astype(BF16)]


def kernel(x, c, positions, mod_w, mod_b, norm_mix, norm_ffn, hy_w_in, gdn_conv, gdn_a_log, gdn_dt_bias, gdn_o_norm, hy_w_out, mla_w_in, mla_q_norm, mla_kv_norm, mla_w_uq, mla_w_ukv, mla_w_out, peer_w_q, peer_sub_keys, peer_u, peer_v, final_norm):
    bsz, s, d = x.shape
    assert bsz == 1 and d == D_MODEL
    depth = mod_w.shape[0]
    xs = x.reshape(s, d)
    mod = _modulation(c.reshape(d, 1), mod_w, mod_b)
    cos_b, sin_b, cos_c, sin_c = _rope_tables(positions.reshape(s, 1))
    u16 = peer_u.astype(BF16)
    vt16 = jnp.swapaxes(peer_v.astype(BF16), 1, 2)
    wq16 = peer_w_q.astype(BF16)
    keys16 = peer_sub_keys.astype(BF16).reshape(depth, 2 * PEER_HEADS, PEER_NKEYS, PEER_DSUB)
    for layer in range(depth):
        sh1, sc1, g1, sh2, sc2, g2 = (mod[layer, :, n * d:(n + 1) * d] for n in range(6))
        nw = norm_mix[layer].reshape(1, d)
        i = layer // 2
        if layer % 2 == 0:
            outs, w_outs = _hybrid_mixer(xs, nw, sc1, sh1, cos_b, sin_b, hy_w_in[i], gdn_conv[i], gdn_a_log[i],
                                         gdn_dt_bias[i], gdn_o_norm[i], hy_w_out[i])
        else:
            outs, w_outs = _mla_mixer(xs, nw, sc1, sh1, cos_c, sin_c, mla_w_in[i], mla_q_norm[i], mla_kv_norm[i],
                                      mla_w_uq[i], mla_w_ukv[i], mla_w_out[i])
        xs, h2, scores = _outproj_peer_proj(outs, w_outs, xs, g1, norm_ffn[layer].reshape(1, d), sc2, sh2,
                                            wq16[layer], keys16[layer])
        a, n, b, rank = _peer_topk(scores)
        xs = _peer_dense(h2, a, n, b, rank, u16, vt16, layer, xs, g2)
    return _final_norm(xs, final_norm.reshape(1, d)).reshape(bsz, s, d)
```

```python
import functools
import math

import jax
import jax.numpy as jnp
from jax import lax
from jax.experimental import pallas as pl
from jax.experimental.pallas import tpu as pltpu

F32 = jnp.float32
BF16 = jnp.bfloat16
NEG_INF = float("-inf")
HIGHEST = lax.Precision.HIGHEST

D_MODEL = 1024
MOD_COLS = 1536
EPS = 1e-6
ROPE_THETA = 10000.0
LOG2_E = math.log2(math.e)

GDN_HEADS = 4
GDN_DK = 128
GDN_CHUNK = 64
GDN_QK = GDN_HEADS * GDN_DK
GDN_DV = 128
GDN_V = GDN_HEADS * GDN_DV
GDN_CONV = 4
GDN_CONV_CH = 2 * GDN_QK + GDN_V
GDN_ROWS = 256

MOBA_HEADS = 4
MOBA_DH = 128
MOBA_BLOCK = 256
MOBA_TOPK = 3
MOBA_W = MOBA_HEADS * MOBA_DH

MLA_HEADS = 8
MLA_Q_RANK = 512
MLA_KV_RANK = 256
MLA_NOPE = 128
MLA_ROPE = 64
MLA_V = 128
MLA_QK = MLA_NOPE + MLA_ROPE
KEY_BLOCK = 256
MLA_Q_BLOCK = 512

PEER_HEADS = 8
PEER_NKEYS = 128
PEER_TOPK = 16
PEER_DSUB = 128
PEER_TOK = 512
PEER_EXP = 2048
PEER_TOPK_TOK = 256

LANES = 128
SUBLANES = 8
BF16_ROWS = 16
SUM_ROWS = BF16_ROWS
VMEM_LIMIT = 56 * 1024 * 1024


def _params(*sem):
    return pltpu.CompilerParams(dimension_semantics=sem, vmem_limit_bytes=VMEM_LIMIT)


def _dot(a, b, precision=None):
    return jnp.dot(a, b, preferred_element_type=F32, precision=precision)


def _dot_nt(a, b):
    return lax.dot_general(a, b, (((1,), (1,)), ((), ())), preferred_element_type=F32)


def _sigmoid(x):
    return jax.nn.sigmoid(x)


def _norm_mod(x, nw, sc, sh):
    y = x * lax.rsqrt(jnp.mean(x * x, axis=-1, keepdims=True) + EPS) * nw
    return y * (1.0 + sc) + sh


def _full(shape):
    n = len(shape)
    return pl.BlockSpec(shape, lambda *_: (0,) * n)


def _mod_kernel(c_ref, w_ref, b_ref, o_ref):
    c = c_ref[...]
    cond = c * _sigmoid(c)
    o_ref[0] = jnp.sum(cond * w_ref[0], axis=0, keepdims=True) + b_ref[0]


def _modulation(c_col, mod_w, mod_b):
    depth, d, n = mod_w.shape
    tn = MOD_COLS
    return pl.pallas_call(
        _mod_kernel,
        grid=(depth, n // tn),
        in_specs=[pl.BlockSpec((d, 1), lambda l, j: (0, 0)),
                  pl.BlockSpec((1, d, tn), lambda l, j: (l, 0, j)),
                  pl.BlockSpec((1, 1, tn), lambda l, j: (l, 0, j))],
        out_specs=pl.BlockSpec((1, 1, tn), lambda l, j: (l, 0, j)),
        out_shape=jax.ShapeDtypeStruct((depth, 1, n), F32),
        compiler_params=_params("parallel", "parallel"),
        name="adaln_modulation",
    )(c_col, mod_w, mod_b.reshape(depth, 1, n))


def _rope_kernel(pos_ref, fb_ref, fc_ref, cb_ref, sb_ref, cc_ref, sc_ref):
    p = pos_ref[...].astype(F32)
    lane = lax.broadcasted_iota(jnp.int32, cb_ref.shape, 1)
    ab = p * fb_ref[...]
    sb = jnp.sin(ab)
    cb_ref[...] = jnp.cos(ab)
    sb_ref[...] = jnp.where(lane < MOBA_DH // 2, -sb, sb)
    ac = p * fc_ref[...]
    sc = jnp.sin(ac)
    cc_ref[...] = jnp.cos(ac)
    sc_ref[...] = jnp.where(lane % MLA_ROPE < MLA_ROPE // 2, -sc, sc)


def _rope_tables(pos_col):
    s = pos_col.shape[0]
    tm = min(s, 1024)
    fb = ROPE_THETA ** (-jnp.arange(0, MOBA_DH, 2, dtype=F32) / MOBA_DH)
    fc = ROPE_THETA ** (-jnp.arange(0, MLA_ROPE, 2, dtype=F32) / MLA_ROPE)
    fb = jnp.tile(fb, 2).reshape(1, LANES)
    fc = jnp.tile(fc, 4).reshape(1, LANES)
    tab = pl.BlockSpec((tm, LANES), lambda i: (i, 0))
    return pl.pallas_call(
        _rope_kernel,
        grid=(s // tm,),
        in_specs=[pl.BlockSpec((tm, 1), lambda i: (i, 0)), _full((1, LANES)), _full((1, LANES))],
        out_specs=[tab] * 4,
        out_shape=[jax.ShapeDtypeStruct((s, LANES), F32)] * 4,
        compiler_params=_params("parallel"),
        name="rope_tables",
    )(pos_col, fb, fc)


def _rope_head128(x, cos, sin):
    return x * cos + pltpu.roll(x, MOBA_DH // 2, 1) * sin


def _rope_heads64(x, cos, sin):
    lane = lax.broadcasted_iota(jnp.int32, x.shape, 1)
    first = lane % MLA_ROPE < MLA_ROPE // 2
    partner = jnp.where(first, pltpu.roll(x, LANES - MLA_ROPE // 2, 1), pltpu.roll(x, MLA_ROPE // 2, 1))
    return x * cos + partner * sin


def _hy_proj_kernel(x_ref, nw_ref, sc_ref, sh_ref, wa_ref, wz_ref, wba_ref, wb_ref, cos_ref, sin_ref,
                    qkva_ref, z_ref, ba_ref, qb_ref, kb_ref, vb_ref):
    h = _norm_mod(x_ref[...], nw_ref[...], sc_ref[...], sh_ref[...]).astype(BF16)
    qkva_ref[...] = _dot(h, wa_ref[...])
    z_ref[...] = _dot(h, wz_ref[...])
    ba_ref[...] = _dot(h, wba_ref[...])
    qkvb = _dot(h, wb_ref[...])
    cos = cos_ref[...]
    sin = sin_ref[...]
    scale = MOBA_DH ** -0.5 * LOG2_E
    for hh in range(MOBA_HEADS):
        lo = hh * MOBA_DH
        q = qkvb[:, lo:lo + MOBA_DH]
        k = qkvb[:, MOBA_W + lo:MOBA_W + lo + MOBA_DH]
        qb_ref[hh, 0] = jnp.transpose(_rope_head128(q, cos, sin) * scale).astype(BF16)
        kb_ref[:, lo:lo + MOBA_DH] = _rope_head128(k, cos, sin).astype(BF16)
        v = qkvb[:, 2 * MOBA_W + lo:2 * MOBA_W + lo + MOBA_DH]
        vb_ref[hh, 0] = _with_sum_rows(jnp.transpose(v))


def _hy_proj(x, nw, sc, sh, wa, wz, wba, wb, cos, sin):
    s, d = x.shape
    tm = KEY_BLOCK
    row = lambda n: pl.BlockSpec((tm, n), lambda i: (i, 0))
    vec = _full((1, d))
    return pl.pallas_call(
        _hy_proj_kernel,
        grid=(s // tm,),
        in_specs=[row(d), vec, vec, vec, _full(wa.shape), _full(wz.shape), _full(wba.shape),
                  _full(wb.shape), row(LANES), row(LANES)],
        out_specs=[row(GDN_CONV_CH), row(GDN_V), row(LANES),
                   pl.BlockSpec((MOBA_HEADS, 1, MOBA_DH, tm), lambda i: (0, i, 0, 0)), row(MOBA_W),
                   pl.BlockSpec((MOBA_HEADS, 1, MOBA_DH + SUM_ROWS, tm), lambda i: (0, i, 0, 0))],
        out_shape=[jax.ShapeDtypeStruct((s, GDN_CONV_CH), F32),
                   jax.ShapeDtypeStruct((s, GDN_V), F32),
                   jax.ShapeDtypeStruct((s, LANES), F32),
                   jax.ShapeDtypeStruct((MOBA_HEADS, s // tm, MOBA_DH, tm), BF16),
                   jax.ShapeDtypeStruct((s, MOBA_W), BF16),
                   jax.ShapeDtypeStruct((MOBA_HEADS, s // tm, MOBA_DH + SUM_ROWS, tm), BF16)],
        compiler_params=_params("parallel"),
        name="hybrid_in_proj",
    )(x, nw, sc, sh, wa, wz, wba, wb, cos, sin)


def _unit_lower_inverses(l_stricts, eye):
    powers = [-l for l in l_stricts]
    invs = [eye + p for p in powers]
    p = 2
    while p < GDN_CHUNK:
        powers = [_dot(x.astype(BF16), x.astype(BF16)) for x in powers]
        invs = [_dot(inv.astype(BF16), (eye + x).astype(BF16)) for inv, x in zip(invs, powers)]
        p *= 2
    return invs


def _gdn_kernel(qkv_ref, ba_ref, z_ref, conv_ref, alog_ref, dtb_ref, onorm_ref, o_ref, ext_ref, state_ref):
    tb = GDN_ROWS
    cs = GDN_CHUNK
    pad = SUBLANES

    @pl.when(pl.program_id(0) == 0)
    def _():
        ext_ref[0:pad, :] = jnp.zeros((pad, GDN_CONV_CH), F32)
        state_ref[...] = jnp.zeros_like(state_ref)

    ext_ref[pad:pad + tb, :] = qkv_ref[...]
    cw = conv_ref[...]
    first = pad - (GDN_CONV - 1)
    y = ext_ref[first:first + tb, :] * cw[0:1]
    for j in range(1, GDN_CONV):
        y = y + ext_ref[first + j:first + j + tb, :] * cw[j:j + 1]
    ext_ref[0:pad, :] = ext_ref[tb:tb + pad, :]
    y = y * _sigmoid(y)

    ba = ba_ref[...]
    beta_all = _sigmoid(ba)
    t = ba + dtb_ref[...]
    softplus = jnp.maximum(t, 0.0) + jnp.log1p(jnp.exp(-jnp.abs(t)))
    g_all = -jnp.exp(alog_ref[...]) * softplus

    row = lax.broadcasted_iota(jnp.int32, (tb, tb), 0)
    col = lax.broadcasted_iota(jnp.int32, (tb, tb), 1)
    same = (row // cs) == (col // cs)
    causal = same & (col <= row)
    strict = same & (col < row)
    eye = (row == col).astype(F32)
    g_cum = _dot(causal.astype(F32), g_all, HIGHEST)
    g_tot = _dot(same.astype(F32), g_all, HIGHEST)
    g_cum_t = jnp.transpose(g_cum)
    z = z_ref[...]
    onorm = onorm_ref[...]

    heads = range(GDN_HEADS)
    rhs, l_strict, gtot, qk, qdec, kdec = [], [], [], [], [], []
    for h in heads:
        lo = h * GDN_DK
        q = y[:, lo:lo + GDN_DK]
        k = y[:, GDN_QK + lo:GDN_QK + lo + GDN_DK]
        v = y[:, 2 * GDN_QK + lo:2 * GDN_QK + lo + GDN_DK]
        q = q * lax.rsqrt(jnp.sum(q * q, axis=-1, keepdims=True) + EPS) * (GDN_DK ** -0.5)
        k = k * lax.rsqrt(jnp.sum(k * k, axis=-1, keepdims=True) + EPS)
        beta = beta_all[:, h:h + 1]
        gcol = g_cum[:, GDN_HEADS + h:GDN_HEADS + h + 1]
        grow = g_cum_t[GDN_HEADS + h:GDN_HEADS + h + 1, :]
        gtot.append(g_tot[:, GDN_HEADS + h:GDN_HEADS + h + 1])
        decay = jnp.exp(jnp.where(causal, gcol - grow, NEG_INF))
        kb = k * beta
        k16 = k.astype(BF16)
        l_strict.append(jnp.where(strict, _dot_nt(kb.astype(BF16), k16) * decay, 0.0))
        eg = jnp.exp(gcol)
        rhs.append(jnp.concatenate([v * beta, kb * eg], axis=1).astype(BF16))
        qk.append((_dot_nt(q.astype(BF16), k16) * decay).astype(BF16))
        qdec.append((q * eg).astype(BF16))
        kdec.append(k * jnp.exp(gtot[h] - gcol))

    uw = [_dot(t_inv.astype(BF16), rhs[h]) for h, t_inv in enumerate(_unit_lower_inverses(l_strict, eye))]
    u = [x[:, :GDN_DK] for x in uw]
    w = [x[:, GDN_DK:].astype(BF16) for x in uw]

    state = [state_ref[h] for h in heads]
    v_new = [[] for _ in heads]
    o_state = [[] for _ in heads]
    for c in range(tb // cs):
        r0 = c * cs
        for h in heads:
            s16 = state[h].astype(BF16)
            vn = u[h][r0:r0 + cs] - _dot(w[h][r0:r0 + cs], s16)
            o_state[h].append(_dot(qdec[h][r0:r0 + cs], s16))
            kd_t = jnp.transpose(kdec[h][r0:r0 + cs]).astype(BF16)
            state[h] = state[h] * jnp.exp(gtot[h][r0:r0 + 1]) + _dot(kd_t, vn.astype(BF16))
            v_new[h].append(vn)

    for h in heads:
        lo = h * GDN_DK
        state_ref[h] = state[h]
        o = jnp.concatenate(o_state[h], axis=0) + _dot(qk[h], jnp.concatenate(v_new[h], axis=0).astype(BF16))
        o = o * lax.rsqrt(jnp.mean(o * o, axis=-1, keepdims=True) + EPS) * onorm
        zh = z[:, lo:lo + GDN_DK]
        o_ref[:, lo:lo + GDN_DK] = (o * (zh * _sigmoid(zh))).astype(BF16)


def _gdn(qkva, ba, z, conv_w, alog, dtb, onorm):
    s = qkva.shape[0]
    tb = GDN_ROWS
    row = lambda n: pl.BlockSpec((tb, n), lambda i: (i, 0))
    return pl.pallas_call(
        _gdn_kernel,
        grid=(s // tb,),
        in_specs=[row(GDN_CONV_CH), row(LANES), row(GDN_V), _full(conv_w.shape),
                  _full((1, LANES)), _full((1, LANES)), _full((1, LANES))],
        out_specs=row(GDN_V),
        out_shape=jax.ShapeDtypeStruct((s, GDN_V), BF16),
        scratch_shapes=[pltpu.VMEM((tb + SUBLANES, GDN_CONV_CH), F32),
                        pltpu.VMEM((GDN_HEADS, GDN_DK, GDN_DV), F32)],
        compiler_params=_params("arbitrary"),
        name="gated_delta_rule",
    )(qkva, ba, z, conv_w, alog, dtb, onorm)


def _scores_into(s_ref, mx_ref, slot, k_tile, q_t, mask):
    s = _dot(k_tile, q_t)
    rows = s.shape[0]
    top = None
    for lo, piece in mask(s):
        s_ref[slot, lo:lo + piece.shape[0], :] = piece
        pmax = piece.max(axis=0, keepdims=True)
        top = pmax if top is None else jnp.maximum(top, pmax)
    if rows < s_ref.shape[1]:
        s_ref[slot, rows:, :] = jnp.full((s_ref.shape[1] - rows, s.shape[1]), NEG_INF, F32)
    mx_ref[slot] = top


def _values(p_ref, slot, v_at, blocks):
    v_t = jnp.concatenate([v_at(j) for j in blocks], axis=1)
    return _dot(v_t, p_ref[slot])


def _flash_steps(n_steps, first_blocks, first_keys, step_blocks, k_at, v_at, q_t, mask_first, mask_step,
                 s_ref, mx_ref, p_ref, acc_ref):
    per_step = len(first_blocks)
    _scores_into(s_ref, mx_ref, 0, k_at(first_blocks[0], first_keys), q_t, mask_first)
    m = mx_ref[0]
    p_ref[0] = jnp.exp2(s_ref[0] - m).astype(BF16)
    acc_ref[...] = jnp.zeros_like(acc_ref)
    _scores_into(s_ref, mx_ref, 1, k_at(step_blocks(1)[0], per_step), q_t, lambda s: mask_step(s, 1))

    def step(t, carry, cur):
        m, alpha_prev = carry
        nxt = jnp.minimum(t + 1, jnp.maximum(n_steps, 1))
        _scores_into(s_ref, mx_ref, 1 - cur, k_at(step_blocks(nxt)[0], per_step), q_t, lambda s: mask_step(s, nxt))
        m_new = jnp.maximum(m, mx_ref[cur])
        alpha = jnp.exp2(m - m_new)
        p = jnp.exp2(s_ref[cur] - m_new)
        prev = [jnp.where(t == 1, a, b) for a, b in zip(first_blocks, step_blocks(jnp.maximum(t - 1, 1)))]
        acc_ref[...] = alpha_prev * acc_ref[...] + _values(p_ref, 1 - cur, v_at, prev)
        p_ref[cur] = p.astype(BF16)
        return m_new, alpha

    def body(t, carry):
        return lax.cond(t % 2 == 0, lambda c: step(t, c, 0), lambda c: step(t, c, 1), carry)

    m, alpha = lax.fori_loop(1, n_steps + 1, body, (m, jnp.ones_like(m)))
    last = [jnp.where(n_steps == 0, a, b) for a, b in zip(first_blocks, step_blocks(jnp.maximum(n_steps, 1)))]
    acc = alpha * acc_ref[...] + _values(p_ref, n_steps % 2, v_at, last)
    dh = acc.shape[0] - SUM_ROWS
    return acc[:dh] / acc[dh:dh + 1]


def _with_sum_rows(v_t):
    extra = lax.broadcasted_iota(jnp.int32, (SUM_ROWS, v_t.shape[1]), 0) == 0
    return jnp.concatenate([v_t, extra.astype(F32)], axis=0).astype(BF16)


def _moba_kernel(q_ref, k_ref, v_ref, o_ref, kmean_ref, sel_ref, s_ref, mx_ref, p_ref, acc_ref, *, n_blocks):
    i = pl.program_id(1)
    bs = MOBA_BLOCK

    @pl.when(i == 0)
    def _():
        kmean_ref[...] = jnp.zeros_like(kmean_ref)

        def mean_body(n, carry):
            blk = k_ref[pl.ds(pl.multiple_of(n * bs, bs), bs), :].astype(F32)
            kmean_ref[pl.ds(n, 1), :] = jnp.mean(blk, axis=0, keepdims=True)
            return carry

        lax.fori_loop(0, n_blocks, mean_body, 0)

    q_t = jnp.concatenate([q_ref[0, 0], q_ref[0, 1]], axis=1)
    gate = _dot(kmean_ref[...].astype(BF16), q_t)
    blk_id = lax.broadcasted_iota(jnp.int32, gate.shape, 0)
    second = lax.broadcasted_iota(jnp.int32, gate.shape, 1) >= bs
    gate = jnp.where(blk_id < 2 * i + second.astype(jnp.int32), gate, NEG_INF)
    sel = jnp.zeros(gate.shape, F32)
    for _ in range(MOBA_TOPK):
        gmax = jnp.max(gate, axis=0, keepdims=True)
        first = jnp.min(jnp.where(gate == gmax, blk_id, LANES), axis=0, keepdims=True)
        hit = blk_id == first
        sel = jnp.where(hit & (gmax > NEG_INF), 1.0, sel)
        gate = jnp.where(hit, NEG_INF, gate)
    sel_ref[...] = sel

    def own_mask(s):
        key = lax.broadcasted_iota(jnp.int32, (bs, 2 * bs), 0)
        qry = lax.broadcasted_iota(jnp.int32, (bs, 2 * bs), 1)
        picked = sel_ref[pl.ds(2 * i, 1), :] > 0.0
        first_ok = (key <= qry) & ((qry < bs) | picked)
        return [(0, jnp.where(first_ok, s[:bs], NEG_INF)), (bs, jnp.where(key + bs <= qry, s[bs:], NEG_INF))]

    def sel_mask(s, t):
        return [(sub * bs, jnp.where(sel_ref[pl.ds(2 * (t - 1) + sub, 1), :] > 0.0, s[sub * bs:(sub + 1) * bs], NEG_INF))
                for sub in range(2)]

    out = _flash_steps(
        n_steps=i, first_blocks=[2 * i, 2 * i + 1], first_keys=2,
        step_blocks=lambda t: [2 * (t - 1), 2 * (t - 1) + 1],
        k_at=lambda j, n: k_ref[pl.ds(pl.multiple_of(j * bs, bs), n * bs), :], v_at=lambda j: v_ref[0, j],
        q_t=q_t, mask_first=own_mask, mask_step=sel_mask,
        s_ref=s_ref, mx_ref=mx_ref, p_ref=p_ref, acc_ref=acc_ref)
    o_ref[...] = jnp.transpose(out).astype(o_ref.dtype)


def _moba(q_t, k, v_t):
    s = k.shape[0]
    bs = MOBA_BLOCK
    n_blocks = s // bs
    assert n_blocks <= LANES and bs == KEY_BLOCK and n_blocks % 2 == 0
    bq = 2 * bs
    kv = pl.BlockSpec((s, MOBA_DH), lambda h, i: (0, h))
    return pl.pallas_call(
        functools.partial(_moba_kernel, n_blocks=n_blocks),
        grid=(MOBA_HEADS, n_blocks // 2),
        in_specs=[pl.BlockSpec((1, 2, MOBA_DH, bs), lambda h, i: (h, i, 0, 0)), kv,
                  pl.BlockSpec((1, n_blocks, MOBA_DH + SUM_ROWS, bs), lambda h, i: (h, 0, 0, 0))],
        out_specs=pl.BlockSpec((bq, MOBA_DH), lambda h, i: (i, h)),
        out_shape=jax.ShapeDtypeStruct((s, MOBA_W), BF16),
        scratch_shapes=[pltpu.VMEM((LANES, MOBA_DH), F32),
                        pltpu.VMEM((LANES, bq), F32),
                        pltpu.VMEM((2, 2 * KEY_BLOCK, bq), F32),
                        pltpu.VMEM((2, 1, bq), F32),
                        pltpu.VMEM((2, 2 * KEY_BLOCK, bq), BF16),
                        pltpu.VMEM((MOBA_DH + SUM_ROWS, bq), F32)],
        compiler_params=_params("parallel", "arbitrary"),
        name="moba_attention",
    )(q_t, k, v_t)


def _outproj_peer_kernel(*refs, n_in):
    o_refs = refs[:n_in]
    w_refs = refs[n_in:2 * n_in]
    x_ref, g_ref, nw_ref, sc_ref, sh_ref, wq_ref, keys_ref, xo_ref, h_ref, s_ref = refs[2 * n_in:]
    y = _dot(o_refs[0][...], w_refs[0][...])
    for o_ref, w_ref in zip(o_refs[1:], w_refs[1:]):
        y = y + _dot(o_ref[...], w_ref[...])
    x = x_ref[...] + g_ref[...] * y
    xo_ref[...] = x
    h = _norm_mod(x, nw_ref[...], sc_ref[...], sh_ref[...]).astype(BF16)
    h_ref[...] = h
    q = _dot(h, wq_ref[...]).astype(BF16)
    for hp in range(2 * PEER_HEADS):
        s_ref[hp] = _dot_nt(keys_ref[hp], q[:, hp * PEER_DSUB:(hp + 1) * PEER_DSUB])


def _outproj_peer_proj(os_, ws, x, gate, nw, sc, sh, wq, keys):
    s, d = x.shape
    tm = min(s, 256)
    row = lambda n: pl.BlockSpec((tm, n), lambda i: (i, 0))
    vec = _full((1, d))
    return pl.pallas_call(
        functools.partial(_outproj_peer_kernel, n_in=len(os_)),
        grid=(s // tm,),
        in_specs=[row(o.shape[1]) for o in os_] + [_full(w.shape) for w in ws]
                 + [row(d), vec, vec, vec, vec, _full(wq.shape), _full(keys.shape)],
        out_specs=[row(d), row(d), pl.BlockSpec((2 * PEER_HEADS, PEER_NKEYS, tm), lambda i: (0, 0, i))],
        out_shape=[jax.ShapeDtypeStruct((s, d), F32), jax.ShapeDtypeStruct((s, d), BF16),
                   jax.ShapeDtypeStruct((2 * PEER_HEADS, PEER_NKEYS, s), F32)],
        compiler_params=_params("parallel"),
        name="out_proj_peer_scores",
    )(*os_, *ws, x, gate, nw, sc, sh, wq, keys)


def _mla_proj_kernel(x_ref, nw_ref, sc_ref, sh_ref, wcq_ref, wckv_ref, wkr_ref, qn_ref, kvn_ref,
                     wqn_ref, wqr_ref, wkn_ref, wv_ref, cos_ref, sin_ref, q_ref, k_ref, v_ref):
    h = _norm_mod(x_ref[...], nw_ref[...], sc_ref[...], sh_ref[...]).astype(BF16)
    cq = _dot(h, wcq_ref[...])
    ckv = _dot(h, wckv_ref[...])
    kr = _dot(h, wkr_ref[...])
    cq = (cq * lax.rsqrt(jnp.mean(cq * cq, axis=-1, keepdims=True) + EPS) * qn_ref[...]).astype(BF16)
    ckv = (ckv * lax.rsqrt(jnp.mean(ckv * ckv, axis=-1, keepdims=True) + EPS) * kvn_ref[...]).astype(BF16)
    cos = cos_ref[...]
    sin = sin_ref[...]
    scale = MLA_QK ** -0.5 * LOG2_E
    q_nope = _dot(cq, wqn_ref[...]) * scale
    q_rope = _dot(cq, wqr_ref[...])
    k_nope = _dot(ckv, wkn_ref[...])
    val = _dot(ckv, wv_ref[...])
    k_rope = _rope_heads64(kr, cos, sin)[:, :MLA_ROPE].astype(BF16)
    for pair in range(MLA_HEADS // 2):
        slab_t = jnp.transpose(_rope_heads64(q_rope[:, pair * LANES:(pair + 1) * LANES], cos, sin) * scale)
        for sub in range(2):
            hh = 2 * pair + sub
            q_ref[hh, 0, MLA_NOPE:, :] = slab_t[sub * MLA_ROPE:(sub + 1) * MLA_ROPE].astype(BF16)
    for hh in range(MLA_HEADS):
        q_ref[hh, 0, :MLA_NOPE, :] = jnp.transpose(q_nope[:, hh * MLA_NOPE:(hh + 1) * MLA_NOPE]).astype(BF16)
        k_ref[hh, :, :MLA_NOPE] = k_nope[:, hh * MLA_NOPE:(hh + 1) * MLA_NOPE].astype(BF16)
        k_ref[hh, :, MLA_NOPE:] = k_rope
        v_ref[hh, 0] = _with_sum_rows(jnp.transpose(val[:, hh * MLA_V:(hh + 1) * MLA_V]))


def _mla_proj(x, nw, sc, sh, wcq, wckv, wkr, qn, kvn, wqn, wqr, wkn, wv, cos, sin):
    s, d = x.shape
    tm = KEY_BLOCK
    row = lambda n: pl.BlockSpec((tm, n), lambda i: (i, 0))
    vec = _full((1, d))
    heads = lambda n: pl.BlockSpec((MLA_HEADS, tm, n), lambda i: (0, i, 0))
    return pl.pallas_call(
        _mla_proj_kernel,
        grid=(s // tm,),
        in_specs=[row(d), vec, vec, vec, _full(wcq.shape), _full(wckv.shape), _full(wkr.shape),
                  _full(qn.shape), _full(kvn.shape), _full(wqn.shape), _full(wqr.shape),
                  _full(wkn.shape), _full(wv.shape), row(LANES), row(LANES)],
        out_specs=[pl.BlockSpec((MLA_HEADS, 1, MLA_QK, tm), lambda i: (0, i, 0, 0)), heads(MLA_QK),
                   pl.BlockSpec((MLA_HEADS, 1, MLA_V + SUM_ROWS, tm), lambda i: (0, i, 0, 0))],
        out_shape=[jax.ShapeDtypeStruct((MLA_HEADS, s // tm, MLA_QK, tm), BF16),
                   jax.ShapeDtypeStruct((MLA_HEADS, s, MLA_QK), BF16),
                   jax.ShapeDtypeStruct((MLA_HEADS, s // tm, MLA_V + SUM_ROWS, tm), BF16)],
        compiler_params=_params("parallel"),
        name="mla_in_proj",
    )(x, nw, sc, sh, wcq, wckv, wkr, qn, kvn, wqn, wqr, wkn, wv, cos, sin)


def _mla_attn_kernel(q_ref, k_ref, v_ref, o_ref, s_ref, mx_ref, p_ref, acc_ref):
    i = pl.program_id(1)

    def causal(s):
        key = lax.broadcasted_iota(jnp.int32, s.shape, 0)
        qry = lax.broadcasted_iota(jnp.int32, s.shape, 1)
        return [(0, jnp.where(key <= qry, s, NEG_INF))]

    per_q = MLA_Q_BLOCK // KEY_BLOCK
    q_t = jnp.concatenate([q_ref[0, b] for b in range(per_q)], axis=1)
    out = _flash_steps(
        n_steps=i, first_blocks=[2 * i, 2 * i + 1], first_keys=2,
        step_blocks=lambda t: [2 * (t - 1), 2 * (t - 1) + 1],
        k_at=lambda j, n: k_ref[0, pl.ds(pl.multiple_of(j * KEY_BLOCK, KEY_BLOCK), n * KEY_BLOCK), :],
        v_at=lambda j: v_ref[0, j], q_t=q_t, mask_first=causal, mask_step=lambda s, t: [(0, s)],
        s_ref=s_ref, mx_ref=mx_ref, p_ref=p_ref, acc_ref=acc_ref)
    o_ref[...] = jnp.transpose(out).astype(o_ref.dtype)


def _mla_attention(q_t, k, v_t):
    _, s, _ = k.shape
    bq = MLA_Q_BLOCK
    per_q = bq // KEY_BLOCK
    assert per_q == 2
    return pl.pallas_call(
        _mla_attn_kernel,
        grid=(MLA_HEADS, s // bq),
        in_specs=[pl.BlockSpec((1, per_q, MLA_QK, KEY_BLOCK), lambda h, i: (h, i, 0, 0)),
                  pl.BlockSpec((1, s, MLA_QK), lambda h, i: (h, 0, 0)),
                  pl.BlockSpec((1, s // KEY_BLOCK, MLA_V + SUM_ROWS, KEY_BLOCK), lambda h, i: (h, 0, 0, 0))],
        out_specs=pl.BlockSpec((bq, MLA_V), lambda h, i: (i, h)),
        out_shape=jax.ShapeDtypeStruct((s, MLA_HEADS * MLA_V), BF16),
        scratch_shapes=[pltpu.VMEM((2, 2 * KEY_BLOCK, bq), F32), pltpu.VMEM((2, 1, bq), F32),
                        pltpu.VMEM((2, 2 * KEY_BLOCK, bq), BF16), pltpu.VMEM((MLA_V + SUM_ROWS, bq), F32)],
        compiler_params=_params("parallel", "arbitrary"),
        name="mla_attention",
    )(q_t, k, v_t)


def _compare_exchange(vs, i, l):
    vs[i], vs[l] = jnp.maximum(vs[i], vs[l]), jnp.minimum(vs[i], vs[l])


def _sort_desc(vs):
    vs = list(vs)
    n = len(vs)
    k = 2
    while k <= n:
        j = k // 2
        while j >= 1:
            for i in range(n):
                l = i ^ j
                if l > i:
                    if i & k == 0:
                        _compare_exchange(vs, i, l)
                    else:
                        _compare_exchange(vs, l, i)
            j //= 2
        k *= 2
    return vs


def _merge_top(a, b):
    n = len(a)
    vs = [jnp.maximum(a[i], b[n - 1 - i]) for i in range(n)]
    j = n // 2
    while j >= 1:
        for i in range(n):
            if i ^ j > i:
                _compare_exchange(vs, i, i ^ j)
        j //= 2
    return vs


def _top16_of_keys(groups):
    vs = _sort_desc(groups)
    for shift in (4, 2, 1):
        vs = _merge_top(vs, [pltpu.roll(v, shift, 0) for v in vs])
    return vs


def _peer_topk_kernel(s_ref, a_ref, n_ref, b_ref, rank_ref):
    sub = SUBLANES
    groups = PEER_NKEYS // sub

    def head_body(h, carry):
        g1 = [s_ref[2 * h, g * sub:(g + 1) * sub, :] for g in range(groups)]
        g2 = [s_ref[2 * h + 1, g * sub:(g + 1) * sub, :] for g in range(groups)]
        v1 = _top16_of_keys(g1)
        v2 = _top16_of_keys(g2)

        ninf = jnp.full_like(v1[0], NEG_INF)
        pad = lambda vs: vs + [ninf] * (PEER_TOPK - len(vs))
        row = lambda r1: [v1[r1] + v2[r2] for r2 in range(PEER_TOPK // (r1 + 1))]
        top = _merge_top(
            _merge_top(row(0), _merge_top(pad(row(1)), pad(row(2)))),
            _merge_top(_sort_desc(pad(row(3) + row(4) + row(5) + row(6) + row(7))),
                       pad([row(r1)[0] for r1 in range(8, PEER_TOPK)])))
        thr = top[PEER_TOPK - 1]
        z = jnp.ones_like(thr)
        for r in range(1, PEER_TOPK):
            z = z + jnp.exp(top[r] - top[0])
        half_inv_z = 0.5 / z

        split = 5
        hit = lambda s1, r2s: sum(jnp.where(s1 + v2[r2] >= thr, 1.0, 0.0) for r2 in r2s)
        n_top = [hit(v1[r1], range(split, PEER_TOPK // (r1 + 1))) for r1 in range(2)]

        b_all, rank_all = [], []
        for g in range(groups):
            rows = slice(g * sub, (g + 1) * sub)
            n = hit(g1[g], range(split)) + jnp.where(g1[g] == v1[0], n_top[0],
                                                     jnp.where(g1[g] == v1[1], n_top[1], 0.0))
            a_ref[h, rows, :] = jnp.exp(g1[g] - v1[0]) * half_inv_z
            n_ref[h, rows, :] = n
            b_all.append(jnp.exp(g2[g] - v2[0]))
            rank_all.append(sum(jnp.where(v2[r] > g2[g], 1.0, 0.0) for r in range(PEER_TOPK)))
        for pair in range(groups // 2):
            b_ref[h, pair] = jnp.concatenate(b_all[2 * pair:2 * pair + 2], axis=0).astype(BF16)
            rank_ref[h, pair] = jnp.concatenate(rank_all[2 * pair:2 * pair + 2], axis=0).astype(BF16)
        return carry

    lax.fori_loop(0, PEER_HEADS, head_body, 0)


def _peer_topk(scores):
    _, nk, s = scores.shape
    tt = min(s, PEER_TOPK_TOK)
    big = pl.BlockSpec((PEER_HEADS, nk, tt), lambda i: (0, 0, i))
    tiled = pl.BlockSpec((PEER_HEADS, nk // BF16_ROWS, BF16_ROWS, tt), lambda i: (0, 0, 0, i))
    shape = (PEER_HEADS, nk, s)
    shape16 = (PEER_HEADS, nk // BF16_ROWS, BF16_ROWS, s)
    return pl.pallas_call(
        _peer_topk_kernel,
        grid=(s // tt,),
        in_specs=[pl.BlockSpec((2 * PEER_HEADS, nk, tt), lambda i: (0, 0, i))],
        out_specs=[big, big, tiled, tiled],
        out_shape=[jax.ShapeDtypeStruct(shape, F32), jax.ShapeDtypeStruct(shape, F32),
                   jax.ShapeDtypeStruct(shape16, BF16), jax.ShapeDtypeStruct(shape16, BF16)],
        compiler_params=_params("parallel"),
        name="peer_topk_threshold",
    )(scores)


def _peer_dense_kernel(h_ref, a_ref, n_ref, b_ref, rank_ref, u_ref, vt_ref, x_ref, g_ref, o_ref,
                       acc_ref, p_ref):
    j = pl.program_id(1)
    nk = PEER_NKEYS

    @pl.when(j == 0)
    def _():
        acc_ref[...] = jnp.zeros_like(acc_ref)

    slab = PEER_EXP // 4
    act_t = jnp.concatenate([_dot_nt(u_ref[e0:e0 + slab, :], h_ref[...]) for e0 in range(0, PEER_EXP, slab)], axis=0)
    for blk in range(PEER_EXP // nk):
        i1 = j * (PEER_EXP // nk) + blk
        wgt = jnp.zeros((nk // BF16_ROWS, BF16_ROWS, PEER_TOK), BF16)
        for h in range(PEER_HEADS):
            row = lambda ref: jnp.broadcast_to(ref[h, pl.ds(i1, 1), :], (BF16_ROWS, PEER_TOK)).astype(BF16)[None]
            chosen = rank_ref[h] < row(n_ref)
            wgt = wgt + jnp.where(chosen, b_ref[h], 0.0) * row(a_ref)
        act = act_t[blk * nk:(blk + 1) * nk].astype(BF16)
        gelu2 = act * (1.0 + lax.erf(act * (2.0 ** -0.5)))
        p_ref[blk * nk:(blk + 1) * nk, :] = wgt.reshape(nk, PEER_TOK) * gelu2
    acc_ref[...] += _dot(vt_ref[...], p_ref[...])

    @pl.when(j == pl.num_programs(1) - 1)
    def _():
        o_ref[...] = x_ref[...] + g_ref[...] * jnp.transpose(acc_ref[...])


def _peer_dense(h2, a, n, b, rank, u16, vt16, layer, x, gate):
    s, d = x.shape
    n_exp = u16.shape[1]
    tt = min(s, PEER_TOK)
    assert tt == PEER_TOK
    tok3 = pl.BlockSpec((PEER_HEADS, PEER_NKEYS, tt), lambda i, j: (0, 0, i))
    tok4 = pl.BlockSpec((PEER_HEADS, PEER_NKEYS // BF16_ROWS, BF16_ROWS, tt), lambda i, j: (0, 0, 0, i))
    return pl.pallas_call(
        _peer_dense_kernel,
        grid=(s // tt, n_exp // PEER_EXP),
        in_specs=[pl.BlockSpec((tt, d), lambda i, j: (i, 0)),
                  tok3, tok3, tok4, tok4,
                  pl.BlockSpec((None, PEER_EXP, d), lambda i, j: (layer, j, 0)),
                  pl.BlockSpec((None, d, PEER_EXP), lambda i, j: (layer, 0, j)),
                  pl.BlockSpec((tt, d), lambda i, j: (i, 0)),
                  pl.BlockSpec((1, d), lambda i, j: (0, 0))],
        out_specs=pl.BlockSpec((tt, d), lambda i, j: (i, 0)),
        out_shape=jax.ShapeDtypeStruct((s, d), F32),
        scratch_shapes=[pltpu.VMEM((d, tt), F32), pltpu.VMEM((PEER_EXP, tt), BF16)],
        compiler_params=_params("parallel", "arbitrary"),
        name="peer_dense_experts",
    )(h2, a, n, b, rank, u16, vt16, x, gate)


def _final_norm_kernel(x_ref, w_ref, o_ref):
    x = x_ref[...]
    o_ref[...] = x * lax.rsqrt(jnp.mean(x * x, axis=-1, keepdims=True) + EPS) * w_ref[...]


def _final_norm(x, w):
    s, d = x.shape
    tm = min(s, 1024)
    return pl.pallas_call(
        _final_norm_kernel,
        grid=(s // tm,),
        in_specs=[pl.BlockSpec((tm, d), lambda i: (i, 0)), _full((1, d))],
        out_specs=pl.BlockSpec((tm, d), lambda i: (i, 0)),
        out_shape=jax.ShapeDtypeStruct((s, d), F32),
        compiler_params=_params("parallel"),
        name="final_rmsnorm",
    )(x, w)


def _lane_row(values, offset):
    return jnp.zeros((1, LANES), F32).at[0, offset:offset + values.shape[0]].set(values.astype(F32))


def _hybrid_mixer(x, nw, sc, sh, cos, sin, w_in, conv_w, a_log, dt_bias, o_norm, w_out):
    w16 = w_in.astype(BF16)
    c0 = GDN_CONV_CH
    c1 = c0 + GDN_V
    c2 = c1 + 2 * GDN_HEADS
    wba = jnp.pad(w16[:, c1:c2], ((0, 0), (0, LANES - 2 * GDN_HEADS)))
    qkva, z, ba, qb, kb, vb = _hy_proj(x, nw, sc, sh, w16[:, :c0], w16[:, c0:c1], wba, w16[:, c2:], cos, sin)
    o_a = _gdn(qkva, ba, z, conv_w, _lane_row(a_log, GDN_HEADS), _lane_row(dt_bias, GDN_HEADS),
               o_norm.reshape(1, -1))
    o_b = _moba(qb, kb, vb)
    wo16 = w_out.astype(BF16)
    return [o_a, o_b], [wo16[:GDN_V], wo16[GDN_V:]]


def _mla_mixer(x, nw, sc, sh, cos, sin, w_in, q_norm, kv_norm, w_uq, w_ukv, w_out):
    w16 = w_in.astype(BF16)
    r0 = MLA_Q_RANK
    r1 = r0 + MLA_KV_RANK
    wkr = jnp.pad(w16[:, r1:], ((0, 0), (0, LANES - MLA_ROPE)))
    wuq = w_uq.astype(BF16).reshape(MLA_Q_RANK, MLA_HEADS, MLA_QK)
    wqn = wuq[:, :, :MLA_NOPE].reshape(MLA_Q_RANK, MLA_HEADS * MLA_NOPE)
    wqr = wuq[:, :, MLA_NOPE:].reshape(MLA_Q_RANK, MLA_HEADS * MLA_ROPE)
    wukv = w_ukv.astype(BF16).reshape(MLA_KV_RANK, MLA_HEADS, MLA_NOPE + MLA_V)
    wkn = wukv[:, :, :MLA_NOPE].reshape(MLA_KV_RANK, MLA_HEADS * MLA_NOPE)
    wv = wukv[:, :, MLA_NOPE:].reshape(MLA_KV_RANK, MLA_HEADS * MLA_V)
    q, k, v = _mla_proj(x, nw, sc, sh, w16[:, :r0], w16[:, r0:r1], wkr, q_norm.reshape(1, -1),
                        kv_norm.reshape(1, -1), wqn, wqr, wkn, wv, cos, sin)
    o = _mla_attention(q, k, v)
    return [o], [w_out.astype(BF16)]


def kernel(x, c, positions, mod_w, mod_b, norm_mix, norm_ffn, hy_w_in, gdn_conv, gdn_a_log, gdn_dt_bias, gdn_o_norm, hy_w_out, mla_w_in, mla_q_norm, mla_kv_norm, mla_w_uq, mla_w_ukv, mla_w_out, peer_w_q, peer_sub_keys, peer_u, peer_v, final_norm):
    bsz, s, d = x.shape
    assert bsz == 1 and d == D_MODEL
    depth = mod_w.shape[0]
    xs = x.reshape(s, d)
    mod = _modulation(c.reshape(d, 1), mod_w, mod_b)
    cos_b, sin_b, cos_c, sin_c = _rope_tables(positions.reshape(s, 1))
    u16 = peer_u.astype(BF16)
    vt16 = jnp.swapaxes(peer_v.astype(BF16), 1, 2)
    wq16 = peer_w_q.astype(BF16)
    keys16 = peer_sub_keys.astype(BF16).reshape(depth, 2 * PEER_HEADS, PEER_NKEYS, PEER_DSUB)
    for layer in range(depth):
        sh1, sc1, g1, sh2, sc2, g2 = (mod[layer, :, n * d:(n + 1) * d] for n in range(6))
        nw = norm_mix[layer].reshape(1, d)
        i = layer // 2
        if layer % 2 == 0:
            outs, w_outs = _hybrid_mixer(xs, nw, sc1, sh1, cos_b, sin_b, hy_w_in[i], gdn_conv[i], gdn_a_log[i],
                                         gdn_dt_bias[i], gdn_o_norm[i], hy_w_out[i])
        else:
            outs, w_outs = _mla_mixer(xs, nw, sc1, sh1, cos_c, sin_c, mla_w_in[i], mla_q_norm[i], mla_kv_norm[i],
                                      mla_w_uq[i], mla_w_ukv[i], mla_w_out[i])
        xs, h2, scores = _outproj_peer_proj(outs, w_outs, xs, g1, norm_ffn[layer].reshape(1, d), sc2, sh2,
                                            wq16[layer], keys16[layer])
        a, n, b, rank = _peer_topk(scores)
        xs = _peer_dense(h2, a, n, b, rank, u16, vt16, layer, xs, g2)
    return _final_norm(xs, final_norm.reshape(1, d)).reshape(bsz, s, d)
```
